```python
import math
import jax, jax.numpy as jnp
from jax import lax
import numpy as np

D_MODEL = 1024
BATCH = 4
SEQ = 4096
DEPTH = 2
DEC_BATCH = 32
DEC_SEQ = 1
PAST_LEN = 16384
PAGE_SIZE = 128

N_A_LAYERS = DEPTH // 2
N_B_LAYERS = DEPTH - N_A_LAYERS
N_DENSE = (DEPTH + 1) // 2
N_MOE = DEPTH // 2

DK_A = 128
DV_A = 128
H_A = D_MODEL // DK_A
CONV_W = 4
CHUNK = 64
QKV_A = H_A * (2 * DK_A + DV_A)
A_IN = QKV_A + H_A * DV_A + 2 * H_A

GROUPS = ((128, 1), (512, 4), (2048, 16))
N_GROUPS = len(GROUPS)
HG = 8
HD_B = 64
ROT_DIM = HD_B // 4
ROPE_THETA = 500000.0
Q_B = N_GROUPS * HG * HD_B
KV_B = 2 * Q_B
O_B = HG * HD_B

D_FF = 3584
N_EXPERTS = 8
TOP_K = 2
PLE_DIM = 256
EPS = 1e-6

kernel_name = "yoco_gdn_dilated_window_step"


def _rms(x, g):
    xf = x.astype(jnp.float32)
    y = xf * lax.rsqrt(jnp.mean(xf * xf, -1, keepdims=True) + EPS)
    return (y * g.astype(jnp.float32)).astype(x.dtype)


def _l2n(x):
    xf = x.astype(jnp.float32)
    return xf * lax.rsqrt(jnp.sum(xf * xf, -1, keepdims=True) + EPS)


def _rope(x, pos):
    half = ROT_DIM // 2
    inv = ROPE_THETA ** (-jnp.arange(half, dtype=jnp.float32) * 2.0 / ROT_DIM)
    ang = pos.astype(jnp.float32)[:, None] * inv[None]
    shape = (pos.shape[0],) + (1,) * (x.ndim - 3) + (half,)
    c = jnp.cos(ang).reshape(shape)
    s = jnp.sin(ang).reshape(shape)
    xf = x.astype(jnp.float32)
    x1, x2, rest = xf[..., :half], xf[..., half:ROT_DIM], xf[..., ROT_DIM:]
    return jnp.concatenate([x1 * c - x2 * s, x2 * c + x1 * s, rest], -1).astype(x.dtype)


def _swiglu(u, w_gu, w_down):
    gu = u @ w_gu
    return (jax.nn.silu(gu[..., :D_FF]) * gu[..., D_FF:]) @ w_down


def _gated_delta_rule(q, k, v, g, beta, S0):
    f32 = jnp.float32
    B_, L, H, _ = q.shape
    C = min(CHUNK, L)
    N = -(-L // C)
    pad = N * C - L

    def prep(t):
        t = t.astype(f32)
        t = jnp.pad(t, [(0, 0), (0, pad)] + [(0, 0)] * (t.ndim - 2))
        t = t.reshape((B_, N, C) + t.shape[2:])
        return jnp.moveaxis(t, 3, 1)

    q, k, v, g, beta = prep(q), prep(k), prep(v), prep(g), prep(beta)
    G = jnp.cumsum(g, axis=-1)
    tri = jnp.tril(jnp.ones((C, C), bool))
    strict = jnp.tril(jnp.ones((C, C), bool), -1)
    decay = jnp.exp(jnp.where(tri, G[..., :, None] - G[..., None, :], -jnp.inf))
    kb = k * beta[..., None]
    a_mat = jnp.eye(C, dtype=f32) + jnp.where(
        strict, jnp.einsum('bhncd,bhnjd->bhncj', kb, k) * decay, 0.0)
    rhs = jnp.concatenate([v * beta[..., None], kb * jnp.exp(G)[..., None]], -1)
    sol = lax.linalg.triangular_solve(a_mat, rhs, left_side=True, lower=True,
                                      unit_diagonal=True)
    u_c, w_c = sol[..., :DV_A], sol[..., DV_A:]
    attn_qk = jnp.einsum('bhncd,bhnjd->bhncj', q, k) * decay
    q_dec = q * jnp.exp(G)[..., None]
    k_dec = k * jnp.exp(G[..., -1:] - G)[..., None]
    g_last = jnp.exp(G[..., -1])

    def step(S, xs):
        qd, kd, uu, ww, aqk, gl = xs
        v_new = uu - jnp.einsum('bhcd,bhde->bhce', ww, S)
        o = jnp.einsum('bhcd,bhde->bhce', qd, S) + jnp.einsum('bhcj,bhje->bhce', aqk, v_new)
        S = S * gl[..., None, None] + jnp.einsum('bhcd,bhce->bhde', kd, v_new)
        return S, o

    xs = tuple(jnp.moveaxis(t, 2, 0) for t in (q_dec, k_dec, u_c, w_c, attn_qk, g_last))
    S, o = lax.scan(step, S0.astype(f32), xs)
    o = jnp.moveaxis(jnp.moveaxis(o, 0, 2), 1, 3).reshape(B_, N * C, H, DV_A)[:, :L]
    return o, S


def _delta_mixer(h, W, l, conv_buf, S0):
    B_, L, _ = h.shape
    u = _rms(h, W['a_norm'][l])
    proj = u @ W['a_w_in'][l]
    o0 = QKV_A
    o1 = o0 + H_A * DV_A
    qkv = proj[..., :o0]
    z = proj[..., o0:o1].reshape(B_, L, H_A, DV_A)
    b_raw = proj[..., o1:o1 + H_A]
    a_raw = proj[..., o1 + H_A:]
    xp = jnp.concatenate([conv_buf.astype(qkv.dtype), qkv], 1)
    new_buf = xp[:, L:]
    cw = W['a_conv_w'][l]
    conv = jax.nn.silu(sum(xp[:, j:j + L] * cw[j] for j in range(CONV_W)))
    q = _l2n(conv[..., :H_A * DK_A].reshape(B_, L, H_A, DK_A)) * (DK_A ** -0.5)
    k = _l2n(conv[..., H_A * DK_A:2 * H_A * DK_A].reshape(B_, L, H_A, DK_A))
    v = conv[..., 2 * H_A * DK_A:].reshape(B_, L, H_A, DV_A)
    beta = jax.nn.sigmoid(b_raw.astype(jnp.float32))
    g = -jnp.exp(W['a_A_log'][l].astype(jnp.float32)) * jax.nn.softplus(
        a_raw.astype(jnp.float32) + W['a_dt_bias'][l].astype(jnp.float32))
    o, S = _gated_delta_rule(q, k, v, g, beta, S0)
    o = _rms(o, W['a_out_norm'][l]) * jax.nn.silu(z.astype(jnp.float32))
    y = o.reshape(B_, L, H_A * DV_A).astype(h.dtype) @ W['a_w_out'][l]
    return h + y, new_buf, S


def _shared_kv(h, W, pos):
    B_, L, _ = h.shape
    kv = (_rms(h, W['kv_norm']) @ W['w_kv']).reshape(B_, L, 2, N_GROUPS, HG, HD_B)
    k = _rope(_rms(kv[:, :, 0], W['k_norm']), pos)
    kv = jnp.stack([k, kv[:, :, 1]], 2)
    return [kv[:, :, :, gi] for gi in range(N_GROUPS)]


def _dilated_band(q, kv, dil, span):
    f32 = jnp.float32
    B_, S, H, D = q.shape
    n = S // dil
    nb = -(-n // span)
    npad = nb * span - n

    def blocks(t):
        t = jnp.moveaxis(t.astype(f32).reshape((B_, n, dil) + t.shape[2:]), 2, 1)
        t = jnp.pad(t, [(0, 0), (0, 0), (0, npad)] + [(0, 0)] * (t.ndim - 3))
        return t.reshape((B_, dil, nb, span) + t.shape[3:])

    def unblock(t):
        t = t.reshape((B_, dil, nb * span) + t.shape[4:])[:, :, :n]
        t = jnp.moveaxis(t, 1, 2)
        return t.reshape((B_, S) + t.shape[3:])

    qb = blocks(q)
    kvb = blocks(kv)
    prev = jnp.pad(kvb, [(0, 0), (0, 0), (1, 0)] + [(0, 0)] * (kvb.ndim - 3))[:, :, :-1]
    kk = jnp.concatenate([prev, kvb], axis=3)
    s = jnp.einsum('brnqhd,brnkhd->brnhqk', qb, kk[..., 0, :, :]) * (HD_B ** -0.5)
    qi = jnp.arange(span)[:, None]
    ki = jnp.arange(2 * span)[None, :]
    dist = qi + span - ki
    band = (dist >= 0) & (dist <= span)
    mask = band[None] & ((jnp.arange(nb) > 0)[:, None, None] | (ki >= span)[None])
    s = jnp.where(mask[:, None], s, -jnp.inf)
    m = jnp.max(s, -1, keepdims=True)
    e = jnp.exp(s - m)
    den = jnp.sum(e, -1, keepdims=True)
    o = jnp.einsum('brnhqk,brnkhd->brnqhd', e / den, kk[..., 1, :, :])
    lse = jnp.moveaxis((m + jnp.log(den))[..., 0], 3, 4)
    return unblock(o), unblock(lse)


def _dilated_gather(q, kv_all, dil, span):
    f32 = jnp.float32
    Ls = q.shape[1]
    Lb = kv_all.shape[1] - Ls
    idx = Lb + jnp.arange(Ls)[:, None] - dil * jnp.arange(span + 1)[None, :]
    valid = idx >= 0
    gk = jnp.take(kv_all.astype(f32), jnp.maximum(idx, 0), axis=1)
    s = jnp.einsum('bqhd,bqjhd->bhqj', q.astype(f32), gk[..., 0, :, :]) * (HD_B ** -0.5)
    s = jnp.where(valid[None, None], s, -jnp.inf)
    m = jnp.max(s, -1, keepdims=True)
    e = jnp.exp(s - m)
    den = jnp.sum(e, -1, keepdims=True)
    o = jnp.einsum('bhqj,bqjhd->bqhd', e / den, gk[..., 1, :, :])
    lse = jnp.moveaxis((m + jnp.log(den))[..., 0], 1, 2)
    return o, lse


def _dilated_mixer(h, W, l, kv_groups, pos, banded):
    B_, L, _ = h.shape
    q = (_rms(h, W['b_norm'][l]) @ W['b_w_q'][l]).reshape(B_, L, N_GROUPS, HG, HD_B)
    q = _rope(_rms(q, W['b_q_norm'][l]), pos)
    outs, lses = [], []
    for gi, (win, dil) in enumerate(GROUPS):
        attend = _dilated_band if banded else _dilated_gather
        o, lse = attend(q[:, :, gi], kv_groups[gi], dil, win // dil)
        outs.append(o)
        lses.append(lse)
    wts = jax.nn.softmax(jnp.stack(lses, 0), axis=0)
    o = jnp.sum(wts[..., None] * jnp.stack(outs, 0), 0)
    return h + o.reshape(B_, L, O_B).astype(h.dtype) @ W['b_w_out'][l]


def _channel_mixer(h, W, i):
    u = _rms(h, W['ffn_norm'][i])
    if i % 2 == 0:
        y = _swiglu(u, W['dense_w_gu'][i // 2], W['dense_w_down'][i // 2])
    else:
        j = i // 2
        probs = jax.nn.softmax((u @ W['moe_router'][j]).astype(jnp.float32), -1)
        top_p, top_i = lax.top_k(probs, TOP_K)
        top_p = top_p / jnp.sum(top_p, -1, keepdims=True)
        gates = jnp.sum(jax.nn.one_hot(top_i, N_EXPERTS, dtype=jnp.float32) * top_p[..., None], -2)
        y = jnp.zeros(u.shape, jnp.float32)
        for e in range(N_EXPERTS):
            y = y + gates[..., e:e + 1] * _swiglu(
                u, W['moe_w_gu'][j, e], W['moe_w_down'][j, e]).astype(jnp.float32)
    return h + y.astype(h.dtype)


def _ple(h, p_i, W, i):
    gate = jax.nn.sigmoid((_rms(h, W['ple_norm'][i]) @ W['ple_gate_w'][i]).astype(jnp.float32))
    return h + ((p_i @ W['ple_w'][i]).astype(jnp.float32) * gate).astype(h.dtype)


def _trunk(x, p, pos, conv_in, delta_in, kv_bufs, W):
    h = x
    convs, deltas = [], []
    kv_new, kv_all = None, None
    for i in range(DEPTH):
        if i < N_A_LAYERS:
            h, cb, S = _delta_mixer(h, W, i, conv_in[i], delta_in[i])
            convs.append(cb)
            deltas.append(S)
        else:
            if i == N_A_LAYERS:
                kv_new = _shared_kv(h, W, pos)
                if kv_bufs is None:
                    kv_all = kv_new
                else:
                    kv_all = [jnp.concatenate([b.astype(nw.dtype), nw], 1)
                              for b, nw in zip(kv_bufs, kv_new)]
            h = _dilated_mixer(h, W, i - N_A_LAYERS, kv_all, pos, kv_bufs is None)
        h = _channel_mixer(h, W, i)
        h = _ple(h, p[i], W, i)
    return h, jnp.stack(convs), jnp.stack(deltas), kv_new


def setup_inputs(seed: int = 0) -> dict:
    key = jax.random.key(seed)
    ks = list(jax.random.split(key, 40))
    f32 = jnp.float32

    def nrm(i, shape, scale=1.0):
        return jax.random.normal(ks[i], shape, f32) * scale

    def gain(i, shape):
        return 1.0 + 0.1 * jax.random.normal(ks[i], shape, f32)

    lb = [min(w, PAST_LEN) for w, _ in GROUPS]
    dt = jnp.exp(jax.random.uniform(ks[12], (N_A_LAYERS, H_A), f32,
                                    math.log(1e-3), math.log(1e-1)))
    return {
        'x_prompt': nrm(0, (BATCH, SEQ, D_MODEL)),
        'x_sample': nrm(1, (DEC_BATCH, DEC_SEQ, D_MODEL)),
        'p_prompt': nrm(2, (DEPTH, BATCH, SEQ, PLE_DIM)),
        'p_sample': nrm(3, (DEPTH, DEC_BATCH, DEC_SEQ, PLE_DIM)),
        'state_conv': nrm(4, (N_A_LAYERS, DEC_BATCH, CONV_W - 1, QKV_A)),
        'state_delta': nrm(5, (N_A_LAYERS, DEC_BATCH, H_A, DK_A, DV_A), 0.1),
        'cache_kv_w128': nrm(6, (DEC_BATCH, lb[0], 2, HG, HD_B)),
        'cache_kv_w512': nrm(7, (DEC_BATCH, lb[1], 2, HG, HD_B)),
        'cache_kv_w2048': nrm(8, (DEC_BATCH, lb[2], 2, HG, HD_B)),
        'a_norm': gain(9, (N_A_LAYERS, D_MODEL)),
        'a_w_in': nrm(10, (N_A_LAYERS, D_MODEL, A_IN), D_MODEL ** -0.5),
        'a_conv_w': nrm(11, (N_A_LAYERS, CONV_W, QKV_A), CONV_W ** -0.5),
        'a_A_log': jnp.log(jax.random.uniform(ks[13], (N_A_LAYERS, H_A), f32, 1.0, 16.0)),
        'a_dt_bias': jnp.log(jnp.expm1(dt)),
        'a_out_norm': gain(14, (N_A_LAYERS, DV_A)),
        'a_w_out': nrm(15, (N_A_LAYERS, H_A * DV_A, D_MODEL), (H_A * DV_A) ** -0.5),
        'kv_norm': gain(16, (D_MODEL,)),
        'w_kv': nrm(17, (D_MODEL, KV_B), D_MODEL ** -0.5),
        'k_norm': gain(18, (HD_B,)),
        'b_norm': gain(19, (N_B_LAYERS, D_MODEL)),
        'b_w_q': nrm(20, (N_B_LAYERS, D_MODEL, Q_B), D_MODEL ** -0.5),
        'b_q_norm': gain(21, (N_B_LAYERS, HD_B)),
        'b_w_out': nrm(22, (N_B_LAYERS, O_B, D_MODEL), O_B ** -0.5),
        'ffn_norm': gain(23, (DEPTH, D_MODEL)),
        'dense_w_gu': nrm(24, (N_DENSE, D_MODEL, 2 * D_FF), D_MODEL ** -0.5),
        'dense_w_down': nrm(25, (N_DENSE, D_FF, D_MODEL), D_FF ** -0.5),
        'moe_router': nrm(26, (N_MOE, D_MODEL, N_EXPERTS), D_MODEL ** -0.5),
        'moe_w_gu': nrm(27, (N_MOE, N_EXPERTS, D_MODEL, 2 * D_FF), D_MODEL ** -0.5),
        'moe_w_down': nrm(28, (N_MOE, N_EXPERTS, D_FF, D_MODEL), D_FF ** -0.5),
        'ple_w': nrm(29, (DEPTH, PLE_DIM, D_MODEL), PLE_DIM ** -0.5),
        'ple_norm': gain(30, (DEPTH, D_MODEL)),
        'ple_gate_w': nrm(31, (DEPTH, D_MODEL, D_MODEL), D_MODEL ** -0.5),
    }


def reference(x_prompt, x_sample, p_prompt, p_sample, state_conv, state_delta,
              cache_kv_w128, cache_kv_w512, cache_kv_w2048,
              a_norm, a_w_in, a_conv_w, a_A_log, a_dt_bias, a_out_norm, a_w_out,
              kv_norm, w_kv, k_norm, b_norm, b_w_q, b_q_norm, b_w_out,
              ffn_norm, dense_w_gu, dense_w_down, moe_router, moe_w_gu, moe_w_down,
              ple_w, ple_norm, ple_gate_w):
    W = dict(a_norm=a_norm, a_w_in=a_w_in, a_conv_w=a_conv_w, a_A_log=a_A_log,
             a_dt_bias=a_dt_bias, a_out_norm=a_out_norm, a_w_out=a_w_out,
             kv_norm=kv_norm, w_kv=w_kv, k_norm=k_norm, b_norm=b_norm, b_w_q=b_w_q,
             b_q_norm=b_q_norm, b_w_out=b_w_out, ffn_norm=ffn_norm,
             dense_w_gu=dense_w_gu, dense_w_down=dense_w_down, moe_router=moe_router,
             moe_w_gu=moe_w_gu, moe_w_down=moe_w_down, ple_w=ple_w, ple_norm=ple_norm,
             ple_gate_w=ple_gate_w)
    Bp, Sp, _ = x_prompt.shape
    Ls = x_sample.shape[1]
    conv0 = jnp.zeros((N_A_LAYERS, Bp, CONV_W - 1, QKV_A), x_prompt.dtype)
    delta0 = jnp.zeros((N_A_LAYERS, Bp, H_A, DK_A, DV_A), jnp.float32)
    y_prompt, conv_p, delta_p, kv_p = _trunk(
        x_prompt, p_prompt, jnp.arange(Sp, dtype=jnp.int32), conv0, delta0, None, W)
    y_sample, conv_s, delta_s, kv_s = _trunk(
        x_sample, p_sample, PAST_LEN + jnp.arange(Ls, dtype=jnp.int32), state_conv, state_delta,
        (cache_kv_w128, cache_kv_w512, cache_kv_w2048), W)
    kv_p = [kv[:, -min(win, Sp):] for kv, (win, _) in zip(kv_p, GROUPS)]
    return (y_prompt, y_sample, conv_p, conv_s, delta_p, delta_s,
            kv_p[0], kv_s[0], kv_p[1], kv_s[1], kv_p[2], kv_s[2])
```

```python
import functools

import jax
import jax.numpy as jnp
from jax import lax
from jax.experimental import pallas as pl
from jax.experimental.pallas import tpu as pltpu

F32 = jnp.float32
BF16 = jnp.bfloat16

EPS = 1e-6
PAST_LEN = 16384
GROUPS = ((128, 1), (512, 4), (2048, 16))
N_GROUPS = len(GROUPS)
HG = 8
HD_B = 64
ROT_DIM = HD_B // 4
ROPE_THETA = 500000.0
GW = HG * HD_B
H_A = 8
DK_A = 128
CONV_W = 4
CHUNK = 64
LANES = 128
VMEM_LIMIT = 52 * 1024 * 1024
NEG = -1e30


def _cparams(*sem):
    return pltpu.CompilerParams(dimension_semantics=sem, vmem_limit_bytes=VMEM_LIMIT)


def _rms_rows(x, gain):
    return x * lax.rsqrt(jnp.mean(x * x, -1, keepdims=True) + EPS) * gain


def _silu(x):
    return x * (1.0 / (1.0 + jnp.exp(-x)))


def _sigmoid(x):
    return 1.0 / (1.0 + jnp.exp(-x))


def _dot(a, b):
    return jnp.dot(a.astype(BF16), b.astype(BF16), preferred_element_type=F32)


def _dot_nt(a, b):
    return lax.dot_general(a.astype(BF16), b.astype(BF16), (((1,), (1,)), ((), ())),
                           preferred_element_type=F32)


def _dot_tn(a, b):
    return lax.dot_general(a.astype(BF16), b.astype(BF16), (((0,), (0,)), ((), ())),
                           preferred_element_type=F32)


def _head_norm_rope(x, hgain, cos, sin):
    t = x.shape[0]
    lane = lax.broadcasted_iota(jnp.int32, (t, LANES), 1)
    lo = lane < HD_B
    d = lane & (HD_B - 1)
    outs = []
    for c in range(GW // LANES):
        sl = slice(c * LANES, (c + 1) * LANES)
        xb = x[:, sl]
        sq = xb * xb
        s_lo = jnp.sum(jnp.where(lo, sq, 0.0), -1, keepdims=True)
        s_hi = jnp.sum(jnp.where(lo, 0.0, sq), -1, keepdims=True)
        scale = jnp.where(lo, lax.rsqrt(s_lo * (1.0 / HD_B) + EPS),
                          lax.rsqrt(s_hi * (1.0 / HD_B) + EPS))
        yb = xb * scale * hgain[:, sl]
        half = ROT_DIM // 2
        rot = jnp.where(d < half, pltpu.roll(yb, LANES - half, 1), pltpu.roll(yb, half, 1))
        outs.append(yb * cos[:, sl] + rot * sin[:, sl])
    return jnp.concatenate(outs, axis=1)


def _norm_mm_kernel(x_ref, g_ref, w_ref, o_ref, u_ref):
    @pl.when(pl.program_id(1) == 0)
    def _():
        u_ref[...] = _rms_rows(x_ref[...], g_ref[...]).astype(BF16)

    o_ref[...] = jnp.dot(u_ref[...], w_ref[...], preferred_element_type=F32)


def _norm_mm(x, gain, w, *, tm, tn):
    m, k = x.shape
    n = w.shape[1]
    return pl.pallas_call(
        _norm_mm_kernel,
        grid=(m // tm, n // tn),
        in_specs=[pl.BlockSpec((tm, k), lambda i, j: (i, 0)),
                  pl.BlockSpec((1, k), lambda i, j: (0, 0)),
                  pl.BlockSpec((k, tn), lambda i, j: (0, j))],
        out_specs=pl.BlockSpec((tm, tn), lambda i, j: (i, j)),
        out_shape=jax.ShapeDtypeStruct((m, n), F32),
        scratch_shapes=[pltpu.VMEM((tm, k), BF16)],
        compiler_params=_cparams("parallel", "arbitrary"),
        name="norm_mm",
    )(x, gain.reshape(1, k), w)


def _proj_rope_kernel(x_ref, g_ref, w_ref, hg_ref, cos_ref, sin_ref, o_ref, u_ref, *, n_rope):
    j = pl.program_id(1)

    @pl.when(j == 0)
    def _():
        u_ref[...] = _rms_rows(x_ref[...], g_ref[...]).astype(BF16)

    acc = jnp.dot(u_ref[...], w_ref[...], preferred_element_type=F32)

    @pl.when(j < n_rope)
    def _():
        o_ref[...] = _head_norm_rope(acc, hg_ref[...], cos_ref[...], sin_ref[...])

    @pl.when(j >= n_rope)
    def _():
        o_ref[...] = acc


def _proj_rope(x, gain, w, hgain, cos, sin, *, tm, n_rope):
    m, k = x.shape
    n = w.shape[1]
    pos_blocks = cos.shape[0] // tm
    return pl.pallas_call(
        functools.partial(_proj_rope_kernel, n_rope=n_rope),
        grid=(m // tm, n // GW),
        in_specs=[pl.BlockSpec((tm, k), lambda i, j: (i, 0)),
                  pl.BlockSpec((1, k), lambda i, j: (0, 0)),
                  pl.BlockSpec((k, GW), lambda i, j: (0, j)),
                  pl.BlockSpec((1, GW), lambda i, j: (0, 0)),
                  pl.BlockSpec((tm, GW), lambda i, j: (i % pos_blocks, 0)),
                  pl.BlockSpec((tm, GW), lambda i, j: (i % pos_blocks, 0))],
        out_specs=pl.BlockSpec((tm, GW), lambda i, j: (i, j)),
        out_shape=jax.ShapeDtypeStruct((m, n), F32),
        scratch_shapes=[pltpu.VMEM((tm, k), BF16)],
        compiler_params=_cparams("parallel", "arbitrary"),
        name="proj_rope",
    )(x, gain.reshape(1, k), w, hgain, cos, sin)


def _rope_tables(pos):
    half = ROT_DIM // 2
    inv = ROPE_THETA ** (-jnp.arange(half, dtype=F32) * 2.0 / ROT_DIM)
    ang = pos.astype(F32)[:, None] * inv[None]
    c, s = jnp.cos(ang), jnp.sin(ang)
    n = pos.shape[0]
    cos_h = jnp.concatenate([c, c, jnp.ones((n, HD_B - ROT_DIM), F32)], 1)
    sin_h = jnp.concatenate([-s, s, jnp.zeros((n, HD_B - ROT_DIM), F32)], 1)
    return jnp.tile(cos_h, (1, HG)), jnp.tile(sin_h, (1, HG))


def _mm_res_kernel(x_ref, w_ref, r_ref, o_ref):
    o_ref[...] = r_ref[...] + jnp.dot(x_ref[...].astype(BF16), w_ref[...],
                                      preferred_element_type=F32)


def _mm_res(x, w, res, *, tm):
    m, k = x.shape
    n = w.shape[1]
    return pl.pallas_call(
        _mm_res_kernel,
        grid=(m // tm,),
        in_specs=[pl.BlockSpec((tm, k), lambda i: (i, 0)),
                  pl.BlockSpec((k, n), lambda i: (0, 0)),
                  pl.BlockSpec((tm, n), lambda i: (i, 0))],
        out_specs=pl.BlockSpec((tm, n), lambda i: (i, 0)),
        out_shape=jax.ShapeDtypeStruct((m, n), F32),
        compiler_params=_cparams("parallel"),
        name="mm_res",
    )(x, w, res)


def _unit_lower_inverse(a):
    c = a.shape[0]
    row = lax.broadcasted_iota(jnp.int32, (c, c), 0)
    col = lax.broadcasted_iota(jnp.int32, (c, c), 1)
    t = jnp.where(row == col, 1.0, 0.0).astype(F32)
    b = 1
    while b < c:
        sel = ((row ^ col) < 2 * b) & ((row & b) != 0) & ((col & b) == 0)
        low = jnp.where(sel, a, 0.0)
        if b == 1:
            t = t - low
        else:
            t = t - _dot(_dot(t, low), t)
        b *= 2
    return t


def _gdn_head_params(ba, hp):
    beta = _sigmoid(ba)
    x = ba + hp[1:2, :]
    softplus = jnp.maximum(x, 0.0) + jnp.log(1.0 + jnp.exp(-jnp.abs(x)))
    g = -jnp.exp(hp[0:1, :]) * softplus
    return beta, g


def _gdn_kernel(qkv_ref, z_ref, ba_ref, cw_ref, hp_ref, on_ref, conv0_ref, s0_ref,
                og_ref, sout_ref, xbuf, s_scr, *, C):
    n = pl.program_id(1)
    pad = 8

    @pl.when(n == 0)
    def _():
        xbuf[pad - (CONV_W - 1):pad, :] = conv0_ref[...]
        s_scr[...] = s0_ref[...]

    xbuf[pad:pad + C, :] = qkv_ref[...]

    def conv_cols(c0):
        acc = None
        for j in range(CONV_W):
            r0 = pad - (CONV_W - 1) + j
            term = xbuf[r0:r0 + C, c0:c0 + DK_A] * cw_ref[j:j + 1, c0:c0 + DK_A]
            acc = term if acc is None else acc + term
        return _silu(acc)

    beta, g = _gdn_head_params(ba_ref[...], hp_ref[...])
    row = lax.broadcasted_iota(jnp.int32, (C, C), 0)
    col = lax.broadcasted_iota(jnp.int32, (C, C), 1)
    tri = (row >= col).astype(F32)
    gcum = jnp.dot(tri, g, preferred_element_type=F32, precision=lax.Precision.HIGHEST)
    gcum_t = gcum.T
    incl = row >= col
    strict = row > col

    for h in range(H_A):
        q = conv_cols(h * DK_A)
        k = conv_cols((H_A + h) * DK_A)
        v = conv_cols((2 * H_A + h) * DK_A)
        q = q * lax.rsqrt(jnp.sum(q * q, -1, keepdims=True) + EPS) * (DK_A ** -0.5)
        k = k * lax.rsqrt(jnp.sum(k * k, -1, keepdims=True) + EPS)
        bc = beta[:, h:h + 1]
        gc = gcum[:, H_A + h:H_A + h + 1]
        gr = gcum_t[H_A + h:H_A + h + 1, :]
        diff = gc - gr
        decay = jnp.exp(jnp.where(incl, diff, NEG))
        kb = k * bc
        a = jnp.where(strict, _dot_nt(kb, k) * decay, 0.0)
        t = _unit_lower_inverse(a)
        eg = jnp.exp(gc)
        sol = _dot(t, jnp.concatenate([v * bc, kb * eg], axis=1))
        u, w = sol[:, :DK_A], sol[:, DK_A:]
        aqk = _dot_nt(q, k) * decay
        g_last = gc[C - 1:C, :]
        kd = k * jnp.exp(g_last - gc)
        s = s_scr[h]
        v_new = u - _dot(w, s)
        o = _dot(q * eg, s) + _dot(aqk, v_new)
        s_scr[h] = s * jnp.exp(g_last) + _dot_tn(kd, v_new)
        o = _rms_rows(o, on_ref[...]) * _silu(z_ref[:, h * DK_A:(h + 1) * DK_A])
        og_ref[:, h * DK_A:(h + 1) * DK_A] = o

    xbuf[pad - (CONV_W - 1):pad, :] = xbuf[pad + C - (CONV_W - 1):pad + C, :]

    @pl.when(n == pl.num_programs(1) - 1)
    def _():
        sout_ref[...] = s_scr[...]


def _gdn_prompt(proj, conv_w, hp, out_norm, conv0, s0, *, batch, seq):
    C = min(CHUNK, seq)
    nc = seq // C
    qkv_w = 3 * H_A * DK_A
    z_w = H_A * DK_A
    return pl.pallas_call(
        functools.partial(_gdn_kernel, C=C),
        grid=(batch, nc),
        in_specs=[pl.BlockSpec((C, qkv_w), lambda b, n: (b * nc + n, 0)),
                  pl.BlockSpec((C, z_w), lambda b, n: (b * nc + n, qkv_w // z_w)),
                  pl.BlockSpec((C, LANES), lambda b, n: (b * nc + n, (qkv_w + z_w) // LANES)),
                  pl.BlockSpec((CONV_W, qkv_w), lambda b, n: (0, 0)),
                  pl.BlockSpec((2, LANES), lambda b, n: (0, 0)),
                  pl.BlockSpec((1, DK_A), lambda b, n: (0, 0)),
                  pl.BlockSpec((None, CONV_W - 1, qkv_w), lambda b, n: (b, 0, 0)),
                  pl.BlockSpec((None, H_A, DK_A, DK_A), lambda b, n: (b, 0, 0, 0))],
        out_specs=[pl.BlockSpec((C, z_w), lambda b, n: (b * nc + n, 0)),
                   pl.BlockSpec((None, H_A, DK_A, DK_A), lambda b, n: (b, 0, 0, 0))],
        out_shape=[jax.ShapeDtypeStruct((batch * seq, z_w), F32),
                   jax.ShapeDtypeStruct((batch, H_A, DK_A, DK_A), F32)],
        scratch_shapes=[pltpu.VMEM((C + 8, qkv_w), F32),
                        pltpu.VMEM((H_A, DK_A, DK_A), F32)],
        compiler_params=_cparams("parallel", "arbitrary"),
        name="gdn_chunked",
    )(proj, proj, proj, conv_w, hp, out_norm, conv0, s0)


def _gdn_step_kernel(proj_ref, conv_ref, cw_ref, hp_ref, on_ref, s0_ref, og_ref, sout_ref, qk_scr):
    qkv_w = 3 * H_A * DK_A
    z_w = H_A * DK_A

    def conv_cols(c0):
        sl = slice(c0, c0 + DK_A)
        acc = proj_ref[:, sl] * cw_ref[CONV_W - 1:CONV_W, sl]
        for j in range(CONV_W - 1):
            acc = acc + conv_ref[j:j + 1, sl] * cw_ref[j:j + 1, sl]
        return _silu(acc)

    beta, g = _gdn_head_params(proj_ref[:, qkv_w + z_w:qkv_w + z_w + LANES], hp_ref[...])
    qk_scr[...] = jnp.zeros_like(qk_scr)
    vs = []
    for h in range(H_A):
        q = conv_cols(h * DK_A)
        k = conv_cols((H_A + h) * DK_A)
        vs.append(conv_cols((2 * H_A + h) * DK_A))
        qk_scr[H_A + h:H_A + h + 1, :] = (
            q * lax.rsqrt(jnp.sum(q * q, -1, keepdims=True) + EPS) * (DK_A ** -0.5))
        qk_scr[h:h + 1, :] = k * lax.rsqrt(jnp.sum(k * k, -1, keepdims=True) + EPS)
    qk = qk_scr[...]
    qk_t = qk.T
    for h in range(H_A):
        k_row = qk[h:h + 1, :]
        q_row = qk[H_A + h:H_A + h + 1, :]
        k_col = qk_t[:, h:h + 1]
        q_col = qk_t[:, H_A + h:H_A + h + 1]
        bh = beta[:, h:h + 1]
        eg = jnp.exp(g[:, H_A + h:H_A + h + 1])
        s = s0_ref[h]
        k_s = jnp.sum(s * k_col, 0, keepdims=True)
        q_s = jnp.sum(s * q_col, 0, keepdims=True)
        v_new = bh * (vs[h] - eg * k_s)
        o = eg * q_s + jnp.sum(q_row * k_row, -1, keepdims=True) * v_new
        sout_ref[h] = s * eg + k_col * v_new
        o = _rms_rows(o, on_ref[...]) * _silu(proj_ref[:, qkv_w + h * DK_A:qkv_w + (h + 1) * DK_A])
        og_ref[:, h * DK_A:(h + 1) * DK_A] = o


def _gdn_step(proj, conv_state, conv_w, hp, out_norm, s0):
    nb, pw = proj.shape
    qkv_w = 3 * H_A * DK_A
    z_w = H_A * DK_A
    og, s_new = pl.pallas_call(
        _gdn_step_kernel,
        grid=(nb,),
        in_specs=[pl.BlockSpec((None, 1, pw), lambda b: (b, 0, 0)),
                  pl.BlockSpec((None, CONV_W - 1, qkv_w), lambda b: (b, 0, 0)),
                  pl.BlockSpec((CONV_W, qkv_w), lambda b: (0, 0)),
                  pl.BlockSpec((2, LANES), lambda b: (0, 0)),
                  pl.BlockSpec((1, DK_A), lambda b: (0, 0)),
                  pl.BlockSpec((None, H_A, DK_A, DK_A), lambda b: (b, 0, 0, 0))],
        out_specs=[pl.BlockSpec((None, 1, z_w), lambda b: (b, 0, 0)),
                   pl.BlockSpec((None, H_A, DK_A, DK_A), lambda b: (b, 0, 0, 0))],
        out_shape=[jax.ShapeDtypeStruct((nb, 1, z_w), F32),
                   jax.ShapeDtypeStruct((nb, H_A, DK_A, DK_A), F32)],
        scratch_shapes=[pltpu.VMEM((LANES, DK_A), F32)],
        compiler_params=_cparams("parallel"),
        name="gdn_step",
    )(proj.reshape(nb, 1, pw), conv_state, conv_w, hp, out_norm, s0)
    return og.reshape(nb, z_w), s_new


def _top2_gates(logits):
    t = logits.shape[0]
    lane = lax.broadcasted_iota(jnp.int32, (t, LANES), 1)
    valid = lane < 8
    lg = jnp.where(valid, logits, NEG)
    mx = jnp.max(lg, -1, keepdims=True)
    e = jnp.where(valid, jnp.exp(lg - mx), 0.0)
    probs = e / jnp.sum(e, -1, keepdims=True)
    p1 = jnp.max(probs, -1, keepdims=True)
    i1 = jnp.min(jnp.where((probs == p1) & valid, lane, LANES), -1, keepdims=True)
    rest = jnp.where((lane == i1) | ~valid, -1.0, probs)
    p2 = jnp.max(rest, -1, keepdims=True)
    i2 = jnp.min(jnp.where(rest == p2, lane, LANES), -1, keepdims=True)
    tot = p1 + p2
    return jnp.where(lane == i1, p1 / tot, jnp.where(lane == i2, p2 / tot, 0.0))


def _ffn_kernel(h_ref, g_ref, r_ref, wg_ref, wu_ref, wd_ref, o_ref, u_ref, gate_ref, acc_ref,
                *, routed):
    e = pl.program_id(1)
    f = pl.program_id(2)

    @pl.when((e == 0) & (f == 0))
    def _():
        u = _rms_rows(h_ref[...], g_ref[...]).astype(BF16)
        u_ref[...] = u
        acc_ref[...] = jnp.zeros_like(acc_ref)
        if routed:
            gate_ref[...] = _top2_gates(jnp.dot(u, r_ref[...], preferred_element_type=F32))

    u = u_ref[...]
    gate = jnp.dot(u, wg_ref[...], preferred_element_type=F32)
    up = jnp.dot(u, wu_ref[...], preferred_element_type=F32)
    act = (_silu(gate) * up).astype(BF16)
    d = jnp.dot(act, wd_ref[...], preferred_element_type=F32)
    if routed:
        lane = lax.broadcasted_iota(jnp.int32, gate_ref.shape, 1)
        d = d * jnp.sum(jnp.where(lane == e, gate_ref[...], 0.0), -1, keepdims=True)
    acc_ref[...] += d

    @pl.when((e == pl.num_programs(1) - 1) & (f == pl.num_programs(2) - 1))
    def _():
        o_ref[...] = h_ref[...] + acc_ref[...]


def _ffn(h, gain, router, w_gu, w_down, *, tm, tf, routed):
    m, dm = h.shape
    ne, ff, _ = w_down.shape
    nf = ff // tf
    return pl.pallas_call(
        functools.partial(_ffn_kernel, routed=routed),
        grid=(m // tm, ne, nf),
        in_specs=[pl.BlockSpec((tm, dm), lambda i, e, f: (i, 0)),
                  pl.BlockSpec((1, dm), lambda i, e, f: (0, 0)),
                  pl.BlockSpec((dm, LANES), lambda i, e, f: (0, 0)),
                  pl.BlockSpec((None, dm, tf), lambda i, e, f: (e, 0, f)),
                  pl.BlockSpec((None, dm, tf), lambda i, e, f: (e, 0, nf + f)),
                  pl.BlockSpec((None, tf, dm), lambda i, e, f: (e, f, 0))],
        out_specs=pl.BlockSpec((tm, dm), lambda i, e, f: (i, 0)),
        out_shape=jax.ShapeDtypeStruct((m, dm), F32),
        scratch_shapes=[pltpu.VMEM((tm, dm), BF16),
                        pltpu.VMEM((tm, LANES), F32),
                        pltpu.VMEM((tm, dm), F32)],
        compiler_params=_cparams("parallel", "arbitrary", "arbitrary"),
        name="ffn_routed" if routed else "ffn_dense",
    )(h, gain.reshape(1, dm), router, w_gu, w_gu, w_down)


def _ple_kernel(h_ref, g_ref, gw_ref, p_ref, pw_ref, o_ref):
    h = h_ref[...]
    u = _rms_rows(h, g_ref[...]).astype(BF16)
    gate = _sigmoid(jnp.dot(u, gw_ref[...], preferred_element_type=F32))
    emb = jnp.dot(p_ref[...].astype(BF16), pw_ref[...], preferred_element_type=F32)
    o_ref[...] = h + emb * gate


def _ple(h, gain, gate_w, p, ple_w, *, tm):
    m, dm = h.shape
    pd = p.shape[1]
    return pl.pallas_call(
        _ple_kernel,
        grid=(m // tm,),
        in_specs=[pl.BlockSpec((tm, dm), lambda i: (i, 0)),
                  pl.BlockSpec((1, dm), lambda i: (0, 0)),
                  pl.BlockSpec((dm, dm), lambda i: (0, 0)),
                  pl.BlockSpec((tm, pd), lambda i: (i, 0)),
                  pl.BlockSpec((pd, dm), lambda i: (0, 0))],
        out_specs=pl.BlockSpec((tm, dm), lambda i: (i, 0)),
        out_shape=jax.ShapeDtypeStruct((m, dm), F32),
        compiler_params=_cparams("parallel"),
        name="ple",
    )(h, gain.reshape(1, dm), gate_w, p, ple_w)


def _band_attn_kernel(q_ref, kp_ref, kc_ref, vp_ref, vc_ref, o_ref, l_ref, *, span, tq):
    j = pl.program_id(2)
    q = (q_ref[...] * (HD_B ** -0.5)).astype(BF16)
    kk = jnp.concatenate([kp_ref[...], kc_ref[...]], axis=0).astype(BF16)
    vv = jnp.concatenate([vp_ref[...], vc_ref[...]], axis=0).astype(BF16)
    qi = lax.broadcasted_iota(jnp.int32, (span, 2 * span), 0)
    ki = lax.broadcasted_iota(jnp.int32, (span, 2 * span), 1)
    dist = qi + span - ki
    band = (dist >= 0) & (dist <= span)
    lane = lax.broadcasted_iota(jnp.int32, (span, LANES), 1)
    for sb in range(tq // span):
        r0 = sb * span
        mask = band & (ki >= jnp.where(j > 0, 0, span)) if sb == 0 else band
        outs = []
        lse_tile = jnp.zeros((span, LANES), F32)
        for h in range(HG):
            hs = slice(h * HD_B, (h + 1) * HD_B)
            s = _dot_nt(q[r0:r0 + span, hs], kk[r0:r0 + 2 * span, hs])
            s = jnp.where(mask, s, NEG)
            m = jnp.max(s, -1, keepdims=True)
            e = jnp.exp(s - m)
            den = jnp.sum(e, -1, keepdims=True)
            outs.append(_dot(e * (1.0 / den), vv[r0:r0 + 2 * span, hs]))
            lse_tile = jnp.where(lane == h, m + jnp.log(den), lse_tile)
        o_ref[r0:r0 + span, :] = jnp.concatenate(outs, axis=1)
        l_ref[r0:r0 + span, :] = lse_tile


def _band_attn(q, kv, gi, *, batch, seq):
    win, dil = GROUPS[gi]
    span = win // dil
    n = seq // dil
    tq = min(4 * span, n)
    nb = n // tq
    sub = tq // span
    qb, kb = N_GROUPS, 2 * N_GROUPS
    q3 = q.reshape(batch, n, dil * qb * GW)
    kv3 = kv.reshape(batch, n, dil * kb * GW)
    prev = lambda j: jnp.maximum(j * sub - 1, 0)
    o, lse = pl.pallas_call(
        functools.partial(_band_attn_kernel, span=span, tq=tq),
        grid=(batch, dil, nb),
        in_specs=[pl.BlockSpec((None, tq, GW), lambda b, r, j: (b, j, r * qb + gi)),
                  pl.BlockSpec((None, span, GW), lambda b, r, j: (b, prev(j), r * kb + gi)),
                  pl.BlockSpec((None, tq, GW), lambda b, r, j: (b, j, r * kb + gi)),
                  pl.BlockSpec((None, span, GW),
                               lambda b, r, j: (b, prev(j), r * kb + N_GROUPS + gi)),
                  pl.BlockSpec((None, tq, GW), lambda b, r, j: (b, j, r * kb + N_GROUPS + gi))],
        out_specs=[pl.BlockSpec((None, tq, GW), lambda b, r, j: (b, j, r)),
                   pl.BlockSpec((None, tq, LANES), lambda b, r, j: (b, j, r))],
        out_shape=[jax.ShapeDtypeStruct((batch, n, dil * GW), F32),
                   jax.ShapeDtypeStruct((batch, n, dil * LANES), F32)],
        compiler_params=_cparams("parallel", "parallel", "arbitrary"),
        name=f"band_attn_g{gi}",
    )(q3, kv3, kv3, kv3, kv3)
    return o.reshape(batch * seq, GW), lse.reshape(batch * seq, LANES)


def _merge_out_kernel(o0_ref, o1_ref, o2_ref, l0_ref, l1_ref, l2_ref, w_ref, r_ref, o_ref):
    ls = [l0_ref[...], l1_ref[...], l2_ref[...]]
    os_ = [o0_ref, o1_ref, o2_ref]
    m = jnp.maximum(jnp.maximum(ls[0], ls[1]), ls[2])
    es = [jnp.exp(l - m) for l in ls]
    inv = 1.0 / (es[0] + es[1] + es[2])
    t = ls[0].shape[0]
    lo = lax.broadcasted_iota(jnp.int32, (t, LANES), 1) < HD_B
    cols = []
    for c in range(GW // LANES):
        acc = None
        for g in range(N_GROUPS):
            wt = es[g] * inv
            wexp = jnp.where(lo, wt[:, 2 * c:2 * c + 1], wt[:, 2 * c + 1:2 * c + 2])
            term = wexp * os_[g][:, c * LANES:(c + 1) * LANES]
            acc = term if acc is None else acc + term
        cols.append(acc)
    o = jnp.concatenate(cols, axis=1).astype(BF16)
    o_ref[...] = r_ref[...] + jnp.dot(o, w_ref[...], preferred_element_type=F32)


def _merge_out(outs, lses, w, res, *, tm):
    m, dm = res.shape
    o_spec = pl.BlockSpec((tm, GW), lambda i: (i, 0))
    l_spec = pl.BlockSpec((tm, LANES), lambda i: (i, 0))
    return pl.pallas_call(
        _merge_out_kernel,
        grid=(m // tm,),
        in_specs=[o_spec, o_spec, o_spec, l_spec, l_spec, l_spec,
                  pl.BlockSpec((GW, dm), lambda i: (0, 0)),
                  pl.BlockSpec((tm, dm), lambda i: (i, 0))],
        out_specs=pl.BlockSpec((tm, dm), lambda i: (i, 0)),
        out_shape=jax.ShapeDtypeStruct((m, dm), F32),
        compiler_params=_cparams("parallel"),
        name="merge_out",
    )(*outs, *lses, w, res)


def _gather_attn_kernel(q_ref, kvn_ref, c0_ref, c1_ref, c2_ref, o_ref):
    caches = [c0_ref, c1_ref, c2_ref]
    rows = c0_ref.shape[0]
    lo = lax.broadcasted_iota(jnp.int32, (rows, LANES), 1) < HD_B
    lo1 = lo[0:1, :]

    def head_sums(x, mask):
        s_lo = jnp.sum(jnp.where(mask, x, 0.0), -1, keepdims=True)
        s_hi = jnp.sum(jnp.where(mask, 0.0, x), -1, keepdims=True)
        return jnp.where(mask, s_lo, s_hi)

    for c in range(GW // LANES):
        outs, lses = [], []
        for g in range(N_GROUPS):
            sl = slice(g * GW + c * LANES, g * GW + (c + 1) * LANES)
            q = q_ref[:, sl] * (HD_B ** -0.5)
            kn = kvn_ref[:, sl]
            vn = kvn_ref[:, N_GROUPS * GW + g * GW + c * LANES:
                         N_GROUPS * GW + g * GW + (c + 1) * LANES]
            kc = caches[g][:, c * LANES:(c + 1) * LANES]
            vc = caches[g][:, GW + c * LANES:GW + (c + 1) * LANES]
            s = head_sums(kc * q, lo)
            s_new = head_sums(kn * q, lo1)
            m = jnp.maximum(jnp.max(s, 0, keepdims=True), s_new)
            e = jnp.exp(s - m)
            e_new = jnp.exp(s_new - m)
            den = jnp.sum(e, 0, keepdims=True) + e_new
            outs.append((jnp.sum(e * vc, 0, keepdims=True) + e_new * vn) / den)
            lses.append(m + jnp.log(den))
        m = jnp.maximum(jnp.maximum(lses[0], lses[1]), lses[2])
        es = [jnp.exp(l - m) for l in lses]
        tot = es[0] + es[1] + es[2]
        o_ref[:, c * LANES:(c + 1) * LANES] = (
            es[0] * outs[0] + es[1] * outs[1] + es[2] * outs[2]) / tot


def _gather_attn(q, kv_new, caches):
    nb = q.shape[0]
    span = GROUPS[0][0] // GROUPS[0][1]
    c3 = []
    for (win, dil), c in zip(GROUPS, caches):
        lb = c.shape[1]
        assert lb == win and lb // dil == span, "window buffer must hold the full window"
        c3.append(c.reshape(nb, lb // dil, dil * 2 * GW))
    o = pl.pallas_call(
        _gather_attn_kernel,
        grid=(nb,),
        in_specs=[pl.BlockSpec((None, 1, q.shape[1]), lambda b: (b, 0, 0)),
                  pl.BlockSpec((None, 1, kv_new.shape[1]), lambda b: (b, 0, 0))] +
                 [pl.BlockSpec((None, span, 2 * GW), lambda b: (b, 0, 0)) for _ in c3],
        out_specs=pl.BlockSpec((None, 1, GW), lambda b: (b, 0, 0)),
        out_shape=jax.ShapeDtypeStruct((nb, 1, GW), F32),
        compiler_params=_cparams("parallel"),
        name="gather_attn",
    )(q.reshape(nb, 1, -1), kv_new.reshape(nb, 1, -1), *c3)
    return o.reshape(nb, GW)


def _prep_weights(a_w_in, a_A_log, a_dt_bias, a_w_out, w_kv, b_w_q, b_w_out, dense_w_gu,
                  dense_w_down, moe_router, moe_w_gu, moe_w_down, ple_w, ple_gate_w, k_norm,
                  b_q_norm):
    d_model, a_in = a_w_in.shape[1:]
    a_in_pad = -(-a_in // LANES) * LANES
    w = {}
    w['a_w_in'] = jnp.pad(a_w_in[0], ((0, 0), (0, a_in_pad - a_in))).astype(BF16)
    hp = jnp.stack([a_A_log[0], a_dt_bias[0]])
    w['a_hp'] = jnp.pad(hp, ((0, 0), (H_A, LANES - 2 * H_A)))
    w['a_w_out'] = a_w_out[0].astype(BF16)
    w['w_kv'] = w_kv.astype(BF16)
    w['b_w_q'] = b_w_q[0].astype(BF16)
    w['b_w_out'] = b_w_out[0].astype(BF16)
    w['dense_w_gu'] = dense_w_gu.astype(BF16)
    w['dense_w_down'] = dense_w_down.astype(BF16)
    w['router'] = jnp.pad(moe_router[0], ((0, 0), (0, LANES - moe_router.shape[2]))).astype(BF16)
    w['moe_w_gu'] = moe_w_gu[0].astype(BF16)
    w['moe_w_down'] = moe_w_down[0].astype(BF16)
    w['ple_w'] = ple_w.astype(BF16)
    w['ple_gate_w'] = ple_gate_w.astype(BF16)
    w['k_gain'] = jnp.tile(k_norm, HG).reshape(1, GW)
    w['q_gain'] = jnp.tile(b_q_norm[0], HG).reshape(1, GW)
    return w


def _layer0(x, p0, w, P, *, tm, in_tn, mixer):
    proj = _norm_mm(x, P['a_norm'][0], w['a_w_in'], tm=tm, tn=in_tn)
    og, s_new = mixer(proj)
    h = _mm_res(og, w['a_w_out'], x, tm=tm)
    h = _ffn(h, P['ffn_norm'][0], w['router'], w['dense_w_gu'], w['dense_w_down'],
             tm=tm, tf=512, routed=False)
    h = _ple(h, P['ple_norm'][0], w['ple_gate_w'][0], p0, w['ple_w'][0], tm=tm)
    return h, proj, s_new


def _layer1_tail(h, p1, w, P, *, tm):
    h = _ffn(h, P['ffn_norm'][1], w['router'], w['moe_w_gu'], w['moe_w_down'],
             tm=tm, tf=512, routed=True)
    return _ple(h, P['ple_norm'][1], w['ple_gate_w'][1], p1, w['ple_w'][1], tm=tm)


def kernel(x_prompt, x_sample, p_prompt, p_sample, state_conv, state_delta, cache_kv_w128, cache_kv_w512, cache_kv_w2048, a_norm, a_w_in, a_conv_w, a_A_log, a_dt_bias, a_out_norm, a_w_out, kv_norm, w_kv, k_norm, b_norm, b_w_q, b_q_norm, b_w_out, ffn_norm, dense_w_gu, dense_w_down, moe_router, moe_w_gu, moe_w_down, ple_w, ple_norm, ple_gate_w):
    assert a_w_in.shape[0] == 1 and b_w_q.shape[0] == 1, "one mixer of each kind"
    bp, sp, dm = x_prompt.shape
    bs, ls, _ = x_sample.shape
    assert ls == 1, "sample group decodes one token per sequence"
    qkv_w = 3 * H_A * DK_A
    P = dict(a_norm=a_norm, ffn_norm=ffn_norm, ple_norm=ple_norm)
    w = _prep_weights(a_w_in, a_A_log, a_dt_bias, a_w_out, w_kv, b_w_q, b_w_out, dense_w_gu,
                      dense_w_down, moe_router, moe_w_gu, moe_w_down, ple_w, ple_gate_w, k_norm,
                      b_q_norm)
    a_in_pad = w['a_w_in'].shape[1]
    in_tn = a_in_pad // 3 if a_in_pad % (3 * LANES) == 0 else LANES
    conv_w = a_conv_w[0]
    out_norm = a_out_norm[0].reshape(1, DK_A)

    mp = bp * sp
    tm = min(1024, sp)
    xp = x_prompt.reshape(mp, dm)
    conv0 = jnp.zeros((bp, CONV_W - 1, qkv_w), F32)
    s0 = jnp.zeros((bp, H_A, DK_A, DK_A), F32)
    h, proj, delta_p = _layer0(
        xp, p_prompt[0].reshape(mp, -1), w, P, tm=tm, in_tn=in_tn,
        mixer=lambda pr: _gdn_prompt(pr, conv_w, w['a_hp'], out_norm, conv0, s0, batch=bp, seq=sp))
    conv_p = proj.reshape(bp, sp, -1)[:, sp - (CONV_W - 1):, :qkv_w][None]

    cos, sin = _rope_tables(jnp.arange(sp, dtype=jnp.int32))
    kv = _proj_rope(h, kv_norm, w['w_kv'], w['k_gain'], cos, sin, tm=tm, n_rope=N_GROUPS)
    q = _proj_rope(h, b_norm[0], w['b_w_q'], w['q_gain'], cos, sin, tm=tm, n_rope=N_GROUPS)
    outs, lses = [], []
    for gi in range(N_GROUPS):
        o, lse = _band_attn(q, kv, gi, batch=bp, seq=sp)
        outs.append(o)
        lses.append(lse)
    h = _merge_out(outs, lses, w['b_w_out'], h, tm=tm)
    y_prompt = _layer1_tail(h, p_prompt[1].reshape(mp, -1), w, P, tm=tm).reshape(bp, sp, dm)
    kv5 = kv.reshape(bp, sp, 2, N_GROUPS, HG, HD_B)
    kv_p = [kv5[:, sp - min(win, sp):, :, gi] for gi, (win, _) in enumerate(GROUPS)]

    xs = x_sample.reshape(bs, dm)
    hs, proj_s, delta_s = _layer0(
        xs, p_sample[0].reshape(bs, -1), w, P, tm=bs, in_tn=in_tn,
        mixer=lambda pr: _gdn_step(pr, state_conv[0], conv_w, w['a_hp'], out_norm, state_delta[0]))
    conv_s = jnp.concatenate([state_conv[0][:, 1:], proj_s[:, None, :qkv_w]], axis=1)[None]
    cos_s, sin_s = _rope_tables(jnp.full((bs,), PAST_LEN, jnp.int32))
    kv_s = _proj_rope(hs, kv_norm, w['w_kv'], w['k_gain'], cos_s, sin_s, tm=bs, n_rope=N_GROUPS)
    q_s = _proj_rope(hs, b_norm[0], w['b_w_q'], w['q_gain'], cos_s, sin_s, tm=bs, n_rope=N_GROUPS)
    o_s = _gather_attn(q_s, kv_s, (cache_kv_w128, cache_kv_w512, cache_kv_w2048))
    hs = _mm_res(o_s, w['b_w_out'], hs, tm=bs)
    y_sample = _layer1_tail(hs, p_sample[1].reshape(bs, -1), w, P, tm=bs).reshape(bs, 1, dm)
    kvs5 = kv_s.reshape(bs, 1, 2, N_GROUPS, HG, HD_B)
    kv_sn = [kvs5[:, :, :, gi] for gi in range(N_GROUPS)]

    return (y_prompt, y_sample, conv_p, conv_s, delta_p[None], delta_s[None],
            kv_p[0], kv_sn[0], kv_p[1], kv_sn[1], kv_p[2], kv_sn[2])
```

```python
import functools

import jax
import jax.numpy as jnp
from jax import lax
from jax.experimental import pallas as pl
from jax.experimental.pallas import tpu as pltpu

F32 = jnp.float32
BF16 = jnp.bfloat16

EPS = 1e-6
PAST_LEN = 16384
GROUPS = ((128, 1), (512, 4), (2048, 16))
N_GROUPS = len(GROUPS)
HG = 8
HD_B = 64
ROT_DIM = HD_B // 4
ROPE_THETA = 500000.0
GW = HG * HD_B
H_A = 8
DK_A = 128
CONV_W = 4
CHUNK = 64
LANES = 128
VMEM_LIMIT = 52 * 1024 * 1024
NEG = -1e30


def _cparams(*sem):
    return pltpu.CompilerParams(dimension_semantics=sem, vmem_limit_bytes=VMEM_LIMIT)


def _rms_rows(x, gain):
    return x * lax.rsqrt(jnp.mean(x * x, -1, keepdims=True) + EPS) * gain


def _silu(x):
    return x * (1.0 / (1.0 + jnp.exp(-x)))


def _sigmoid(x):
    return 1.0 / (1.0 + jnp.exp(-x))


def _dot(a, b):
    return jnp.dot(a.astype(BF16), b.astype(BF16), preferred_element_type=F32)


def _dot_nt(a, b):
    return lax.dot_general(a.astype(BF16), b.astype(BF16), (((1,), (1,)), ((), ())),
                           preferred_element_type=F32)


def _dot_tn(a, b):
    return lax.dot_general(a.astype(BF16), b.astype(BF16), (((0,), (0,)), ((), ())),
                           preferred_element_type=F32)


def _head_norm_rope(x, hgain, cos, sin):
    t = x.shape[0]
    lane = lax.broadcasted_iota(jnp.int32, (t, LANES), 1)
    lo = lane < HD_B
    d = lane & (HD_B - 1)
    outs = []
    for c in range(GW // LANES):
        sl = slice(c * LANES, (c + 1) * LANES)
        xb = x[:, sl]
        sq = xb * xb
        s_lo = jnp.sum(jnp.where(lo, sq, 0.0), -1, keepdims=True)
        s_hi = jnp.sum(jnp.where(lo, 0.0, sq), -1, keepdims=True)
        scale = jnp.where(lo, lax.rsqrt(s_lo * (1.0 / HD_B) + EPS),
                          lax.rsqrt(s_hi * (1.0 / HD_B) + EPS))
        yb = xb * scale * hgain[:, sl]
        half = ROT_DIM // 2
        rot = jnp.where(d < half, pltpu.roll(yb, LANES - half, 1), pltpu.roll(yb, half, 1))
        outs.append(yb * cos[:, sl] + rot * sin[:, sl])
    return jnp.concatenate(outs, axis=1)


def _norm_mm_kernel(x_ref, g_ref, w_ref, o_ref, u_ref):
    @pl.when(pl.program_id(1) == 0)
    def _():
        u_ref[...] = _rms_rows(x_ref[...], g_ref[...]).astype(BF16)

    o_ref[...] = jnp.dot(u_ref[...], w_ref[...], preferred_element_type=F32)


def _norm_mm(x, gain, w, *, tm, tn):
    m, k = x.shape
    n = w.shape[1]
    return pl.pallas_call(
        _norm_mm_kernel,
        grid=(m // tm, n // tn),
        in_specs=[pl.BlockSpec((tm, k), lambda i, j: (i, 0)),
                  pl.BlockSpec((1, k), lambda i, j: (0, 0)),
                  pl.BlockSpec((k, tn), lambda i, j: (0, j))],
        out_specs=pl.BlockSpec((tm, tn), lambda i, j: (i, j)),
        out_shape=jax.ShapeDtypeStruct((m, n), F32),
        scratch_shapes=[pltpu.VMEM((tm, k), BF16)],
        compiler_params=_cparams("parallel", "arbitrary"),
        name="norm_mm",
    )(x, gain.reshape(1, k), w)


def _proj_rope_kernel(x_ref, g_ref, w_ref, hg_ref, cos_ref, sin_ref, *rest,
                      n_rope, natural, dils, tm):
    n_out = int(natural) + len(dils)
    outs, (u_ref, slab_ref) = rest[:n_out], rest[n_out:]
    nat_ref = outs[0] if natural else None
    ph_refs = outs[int(natural):]
    j = pl.program_id(1)

    @pl.when(j == 0)
    def _():
        u_ref[...] = _rms_rows(x_ref[...], g_ref[...]).astype(BF16)

    acc = jnp.dot(u_ref[...], w_ref[...], preferred_element_type=F32)

    def emit(jj, y):
        if natural:
            nat_ref[...] = y
        if jj >= len(dils):
            return
        d = dils[jj]
        if d == 1:
            ph_refs[jj][0] = y.astype(BF16)
            return
        for c in range(GW // LANES):
            slab_ref[c] = y[:, c * LANES:(c + 1) * LANES]
        for r in range(d):
            for c in range(GW // LANES):
                ph_refs[jj][r, :, c * LANES:(c + 1) * LANES] = (
                    slab_ref[c, pl.ds(r, tm // d, stride=d), :].astype(BF16))

    for jj in range(max(n_rope, len(dils))):
        @pl.when(j == jj)
        def _(jj=jj):
            y = acc
            if jj < n_rope:
                y = _head_norm_rope(acc, hg_ref[...], cos_ref[...], sin_ref[...])
            emit(jj, y)

    if natural:
        @pl.when(j >= max(n_rope, len(dils)))
        def _():
            nat_ref[...] = acc


def _proj_rope(x, gain, w, hgain, cos, sin, *, tm, n_rope, natural, dils):
    m, k = x.shape
    n = w.shape[1]
    pos_blocks = cos.shape[0] // tm
    out_specs, out_shape = [], []
    if natural:
        out_specs.append(pl.BlockSpec((tm, GW), lambda i, j: (i, j)))
        out_shape.append(jax.ShapeDtypeStruct((m, n), F32))
    for d in dils:
        out_specs.append(pl.BlockSpec((d, tm // d, GW), lambda i, j: (0, i, 0)))
        out_shape.append(jax.ShapeDtypeStruct((d, m // d, GW), BF16))
    return pl.pallas_call(
        functools.partial(_proj_rope_kernel, n_rope=n_rope, natural=natural, dils=tuple(dils),
                          tm=tm),
        grid=(m // tm, n // GW),
        in_specs=[pl.BlockSpec((tm, k), lambda i, j: (i, 0)),
                  pl.BlockSpec((1, k), lambda i, j: (0, 0)),
                  pl.BlockSpec((k, GW), lambda i, j: (0, j)),
                  pl.BlockSpec((1, GW), lambda i, j: (0, 0)),
                  pl.BlockSpec((tm, GW), lambda i, j: (i % pos_blocks, 0)),
                  pl.BlockSpec((tm, GW), lambda i, j: (i % pos_blocks, 0))],
        out_specs=out_specs,
        out_shape=out_shape,
        scratch_shapes=[pltpu.VMEM((tm, k), BF16),
                        pltpu.VMEM((GW // LANES, tm, LANES), F32)],
        compiler_params=_cparams("parallel", "arbitrary"),
        name="proj_rope",
    )(x, gain.reshape(1, k), w, hgain, cos, sin)


def _rope_tables(pos):
    half = ROT_DIM // 2
    inv = ROPE_THETA ** (-jnp.arange(half, dtype=F32) * 2.0 / ROT_DIM)
    ang = pos.astype(F32)[:, None] * inv[None]
    c, s = jnp.cos(ang), jnp.sin(ang)
    n = pos.shape[0]
    cos_h = jnp.concatenate([c, c, jnp.ones((n, HD_B - ROT_DIM), F32)], 1)
    sin_h = jnp.concatenate([-s, s, jnp.zeros((n, HD_B - ROT_DIM), F32)], 1)
    return jnp.tile(cos_h, (1, HG)), jnp.tile(sin_h, (1, HG))


def _mm_res_kernel(x_ref, w_ref, r_ref, o_ref):
    o_ref[...] = r_ref[...] + jnp.dot(x_ref[...].astype(BF16), w_ref[...],
                                      preferred_element_type=F32)


def _mm_res(x, w, res, *, tm):
    m, k = x.shape
    n = w.shape[1]
    return pl.pallas_call(
        _mm_res_kernel,
        grid=(m // tm,),
        in_specs=[pl.BlockSpec((tm, k), lambda i: (i, 0)),
                  pl.BlockSpec((k, n), lambda i: (0, 0)),
                  pl.BlockSpec((tm, n), lambda i: (i, 0))],
        out_specs=pl.BlockSpec((tm, n), lambda i: (i, 0)),
        out_shape=jax.ShapeDtypeStruct((m, n), F32),
        compiler_params=_cparams("parallel"),
        name="mm_res",
    )(x, w, res)


def _unit_lower_inverses(mats):
    c = mats[0].shape[0]
    row = lax.broadcasted_iota(jnp.int32, (c, c), 0)
    col = lax.broadcasted_iota(jnp.int32, (c, c), 1)
    eye = jnp.where(row == col, 1.0, 0.0).astype(F32)
    ts = None
    b = 1
    while b < c:
        sel = ((row ^ col) < 2 * b) & ((row & b) != 0) & ((col & b) == 0)
        lows = [jnp.where(sel, a, 0.0) for a in mats]
        if ts is None:
            ts = [eye - low for low in lows]
        else:
            tl = [_dot(t, low) for t, low in zip(ts, lows)]
            ts = [t - _dot(x, t) for t, x in zip(ts, tl)]
        b *= 2
    return ts


def _gdn_head_params(ba, hp):
    beta = _sigmoid(ba)
    x = ba + hp[1:2, :]
    softplus = jnp.maximum(x, 0.0) + jnp.log(1.0 + jnp.exp(-jnp.abs(x)))
    g = -jnp.exp(hp[0:1, :]) * softplus
    return beta, g


def _gdn_kernel(qkv_ref, z_ref, ba_ref, cw_ref, hp_ref, on_ref, conv0_ref, s0_ref,
                og_ref, sout_ref, xbuf, s_scr, *, C, nch):
    n = pl.program_id(1)
    R = nch * C
    pad = 8

    @pl.when(n == 0)
    def _():
        xbuf[pad - (CONV_W - 1):pad, :] = conv0_ref[...]
        s_scr[...] = s0_ref[...]

    xbuf[pad:pad + R, :] = qkv_ref[...]

    def conv_cols(c0):
        acc = None
        for j in range(CONV_W):
            r0 = pad - (CONV_W - 1) + j
            term = xbuf[r0:r0 + R, c0:c0 + DK_A] * cw_ref[j:j + 1, c0:c0 + DK_A]
            acc = term if acc is None else acc + term
        return _silu(acc)

    beta, g = _gdn_head_params(ba_ref[...], hp_ref[...])
    rr = lax.broadcasted_iota(jnp.int32, (R, R), 0)
    rc = lax.broadcasted_iota(jnp.int32, (R, R), 1)
    blocktri = ((rr >= rc) & ((rr ^ rc) < C)).astype(F32)
    gcum = jnp.dot(blocktri, g, preferred_element_type=F32, precision=lax.Precision.HIGHEST)
    gcum_t = gcum.T
    row = lax.broadcasted_iota(jnp.int32, (C, C), 0)
    col = lax.broadcasted_iota(jnp.int32, (C, C), 1)
    incl = row >= col
    strict = row > col

    units = [(c, h) for c in range(nch) for h in range(H_A)]
    qs, ks, vs = {}, {}, {}
    for h in range(H_A):
        q = conv_cols(h * DK_A)
        k = conv_cols((H_A + h) * DK_A)
        v = conv_cols((2 * H_A + h) * DK_A)
        q = q * lax.rsqrt(jnp.sum(q * q, -1, keepdims=True) + EPS) * (DK_A ** -0.5)
        k = k * lax.rsqrt(jnp.sum(k * k, -1, keepdims=True) + EPS)
        for c in range(nch):
            rs = slice(c * C, (c + 1) * C)
            qs[c, h], ks[c, h], vs[c, h] = q[rs], k[rs], v[rs]

    bcs, gcs, decays, kbs = {}, {}, {}, {}
    for c, h in units:
        rs = slice(c * C, (c + 1) * C)
        bcs[c, h] = beta[rs, h:h + 1]
        gcs[c, h] = gcum[rs, H_A + h:H_A + h + 1]
        gr = gcum_t[H_A + h:H_A + h + 1, rs]
        decays[c, h] = jnp.exp(jnp.where(incl, gcs[c, h] - gr, NEG))
        kbs[c, h] = ks[c, h] * bcs[c, h]
    grams = {u: _dot_nt(jnp.concatenate([kbs[u], qs[u]], axis=0), ks[u]) for u in units}
    a_mats = [jnp.where(strict, grams[u][:C] * decays[u], 0.0) for u in units]
    aqks = {u: grams[u][C:] * decays[u] for u in units}
    t_mats = dict(zip(units, _unit_lower_inverses(a_mats)))
    egs = {u: jnp.exp(gcs[u]) for u in units}
    sols = {u: _dot(t_mats[u], jnp.concatenate([vs[u] * bcs[u], kbs[u] * egs[u]], axis=1))
            for u in units}

    states = [s_scr[h] for h in range(H_A)]
    for c in range(nch):
        rs = slice(c * C, (c + 1) * C)
        for h in range(H_A):
            u = (c, h)
            g_last = gcs[u][C - 1:C, :]
            ws = _dot(jnp.concatenate([sols[u][:, DK_A:], qs[u] * egs[u]], axis=0), states[h])
            v_new = sols[u][:, :DK_A] - ws[:C]
            o = ws[C:] + _dot(aqks[u], v_new)
            kd = ks[u] * jnp.exp(g_last - gcs[u])
            states[h] = states[h] * jnp.exp(g_last) + _dot_tn(kd, v_new)
            o = _rms_rows(o, on_ref[...]) * _silu(z_ref[rs, h * DK_A:(h + 1) * DK_A])
            og_ref[rs, h * DK_A:(h + 1) * DK_A] = o
    for h in range(H_A):
        s_scr[h] = states[h]

    xbuf[pad - (CONV_W - 1):pad, :] = xbuf[pad + R - (CONV_W - 1):pad + R, :]

    @pl.when(n == pl.num_programs(1) - 1)
    def _():
        sout_ref[...] = s_scr[...]


def _gdn_prompt(proj, conv_w, hp, out_norm, conv0, s0, *, batch, seq):
    C = min(CHUNK, seq)
    nch = 2 if seq % (2 * C) == 0 else 1
    R = nch * C
    nc = seq // R
    qkv_w = 3 * H_A * DK_A
    z_w = H_A * DK_A
    return pl.pallas_call(
        functools.partial(_gdn_kernel, C=C, nch=nch),
        grid=(batch, nc),
        in_specs=[pl.BlockSpec((R, qkv_w), lambda b, n: (b * nc + n, 0)),
                  pl.BlockSpec((R, z_w), lambda b, n: (b * nc + n, qkv_w // z_w)),
                  pl.BlockSpec((R, LANES), lambda b, n: (b * nc + n, (qkv_w + z_w) // LANES)),
                  pl.BlockSpec((CONV_W, qkv_w), lambda b, n: (0, 0)),
                  pl.BlockSpec((2, LANES), lambda b, n: (0, 0)),
                  pl.BlockSpec((1, DK_A), lambda b, n: (0, 0)),
                  pl.BlockSpec((None, CONV_W - 1, qkv_w), lambda b, n: (b, 0, 0)),
                  pl.BlockSpec((None, H_A, DK_A, DK_A), lambda b, n: (b, 0, 0, 0))],
        out_specs=[pl.BlockSpec((R, z_w), lambda b, n: (b * nc + n, 0)),
                   pl.BlockSpec((None, H_A, DK_A, DK_A), lambda b, n: (b, 0, 0, 0))],
        out_shape=[jax.ShapeDtypeStruct((batch * seq, z_w), F32),
                   jax.ShapeDtypeStruct((batch, H_A, DK_A, DK_A), F32)],
        scratch_shapes=[pltpu.VMEM((R + 8, qkv_w), F32),
                        pltpu.VMEM((H_A, DK_A, DK_A), F32)],
        compiler_params=_cparams("parallel", "arbitrary"),
        name="gdn_chunked",
    )(proj, proj, proj, conv_w, hp, out_norm, conv0, s0)


def _gdn_step_kernel(proj_ref, conv_ref, cw_ref, hp_ref, on_ref, s0_ref, og_ref, sout_ref, qk_scr):
    qkv_w = 3 * H_A * DK_A
    z_w = H_A * DK_A

    def conv_cols(c0):
        sl = slice(c0, c0 + DK_A)
        acc = proj_ref[:, sl] * cw_ref[CONV_W - 1:CONV_W, sl]
        for j in range(CONV_W - 1):
            acc = acc + conv_ref[j:j + 1, sl] * cw_ref[j:j + 1, sl]
        return _silu(acc)

    beta, g = _gdn_head_params(proj_ref[:, qkv_w + z_w:qkv_w + z_w + LANES], hp_ref[...])
    qk_scr[...] = jnp.zeros_like(qk_scr)
    vs = []
    for h in range(H_A):
        q = conv_cols(h * DK_A)
        k = conv_cols((H_A + h) * DK_A)
        vs.append(conv_cols((2 * H_A + h) * DK_A))
        qk_scr[H_A + h:H_A + h + 1, :] = (
            q * lax.rsqrt(jnp.sum(q * q, -1, keepdims=True) + EPS) * (DK_A ** -0.5))
        qk_scr[h:h + 1, :] = k * lax.rsqrt(jnp.sum(k * k, -1, keepdims=True) + EPS)
    qk = qk_scr[...]
    qk_t = qk.T
    for h in range(H_A):
        k_row = qk[h:h + 1, :]
        q_row = qk[H_A + h:H_A + h + 1, :]
        k_col = qk_t[:, h:h + 1]
        q_col = qk_t[:, H_A + h:H_A + h + 1]
        bh = beta[:, h:h + 1]
        eg = jnp.exp(g[:, H_A + h:H_A + h + 1])
        s = s0_ref[h]
        k_s = jnp.sum(s * k_col, 0, keepdims=True)
        q_s = jnp.sum(s * q_col, 0, keepdims=True)
        v_new = bh * (vs[h] - eg * k_s)
        o = eg * q_s + jnp.sum(q_row * k_row, -1, keepdims=True) * v_new
        sout_ref[h] = s * eg + k_col * v_new
        o = _rms_rows(o, on_ref[...]) * _silu(proj_ref[:, qkv_w + h * DK_A:qkv_w + (h + 1) * DK_A])
        og_ref[:, h * DK_A:(h + 1) * DK_A] = o


def _gdn_step(proj, conv_state, conv_w, hp, out_norm, s0):
    nb, pw = proj.shape
    qkv_w = 3 * H_A * DK_A
    z_w = H_A * DK_A
    og, s_new = pl.pallas_call(
        _gdn_step_kernel,
        grid=(nb,),
        in_specs=[pl.BlockSpec((None, 1, pw), lambda b: (b, 0, 0)),
                  pl.BlockSpec((None, CONV_W - 1, qkv_w), lambda b: (b, 0, 0)),
                  pl.BlockSpec((CONV_W, qkv_w), lambda b: (0, 0)),
                  pl.BlockSpec((2, LANES), lambda b: (0, 0)),
                  pl.BlockSpec((1, DK_A), lambda b: (0, 0)),
                  pl.BlockSpec((None, H_A, DK_A, DK_A), lambda b: (b, 0, 0, 0))],
        out_specs=[pl.BlockSpec((None, 1, z_w), lambda b: (b, 0, 0)),
                   pl.BlockSpec((None, H_A, DK_A, DK_A), lambda b: (b, 0, 0, 0))],
        out_shape=[jax.ShapeDtypeStruct((nb, 1, z_w), F32),
                   jax.ShapeDtypeStruct((nb, H_A, DK_A, DK_A), F32)],
        scratch_shapes=[pltpu.VMEM((LANES, DK_A), F32)],
        compiler_params=_cparams("parallel"),
        name="gdn_step",
    )(proj.reshape(nb, 1, pw), conv_state, conv_w, hp, out_norm, s0)
    return og.reshape(nb, z_w), s_new


def _top2_gates(logits):
    t = logits.shape[0]
    lane = lax.broadcasted_iota(jnp.int32, (t, LANES), 1)
    valid = lane < 8
    lg = jnp.where(valid, logits, NEG)
    mx = jnp.max(lg, -1, keepdims=True)
    e = jnp.where(valid, jnp.exp(lg - mx), 0.0)
    probs = e / jnp.sum(e, -1, keepdims=True)
    p1 = jnp.max(probs, -1, keepdims=True)
    i1 = jnp.min(jnp.where((probs == p1) & valid, lane, LANES), -1, keepdims=True)
    rest = jnp.where((lane == i1) | ~valid, -1.0, probs)
    p2 = jnp.max(rest, -1, keepdims=True)
    i2 = jnp.min(jnp.where(rest == p2, lane, LANES), -1, keepdims=True)
    tot = p1 + p2
    return jnp.where(lane == i1, p1 / tot, jnp.where(lane == i2, p2 / tot, 0.0))


def _ffn_kernel(h_ref, g_ref, r_ref, wg_ref, wu_ref, wd_ref, o_ref, u_ref, gate_ref, acc_ref,
                *, routed):
    e = pl.program_id(1)
    f = pl.program_id(2)

    @pl.when((e == 0) & (f == 0))
    def _():
        u = _rms_rows(h_ref[...], g_ref[...]).astype(BF16)
        u_ref[...] = u
        acc_ref[...] = jnp.zeros_like(acc_ref)
        if routed:
            gate_ref[...] = _top2_gates(jnp.dot(u, r_ref[...], preferred_element_type=F32))

    u = u_ref[...]
    gate = jnp.dot(u, wg_ref[...], preferred_element_type=F32)
    up = jnp.dot(u, wu_ref[...], preferred_element_type=F32)
    act = (_silu(gate) * up).astype(BF16)
    d = jnp.dot(act, wd_ref[...], preferred_element_type=F32)
    if routed:
        lane = lax.broadcasted_iota(jnp.int32, gate_ref.shape, 1)
        d = d * jnp.sum(jnp.where(lane == e, gate_ref[...], 0.0), -1, keepdims=True)
    acc_ref[...] += d

    @pl.when((e == pl.num_programs(1) - 1) & (f == pl.num_programs(2) - 1))
    def _():
        o_ref[...] = h_ref[...] + acc_ref[...]


def _ffn(h, gain, router, w_gu, w_down, *, tm, tf, routed):
    m, dm = h.shape
    ne, ff, _ = w_down.shape
    nf = ff // tf
    return pl.pallas_call(
        functools.partial(_ffn_kernel, routed=routed),
        grid=(m // tm, ne, nf),
        in_specs=[pl.BlockSpec((tm, dm), lambda i, e, f: (i, 0)),
                  pl.BlockSpec((1, dm), lambda i, e, f: (0, 0)),
                  pl.BlockSpec((dm, LANES), lambda i, e, f: (0, 0)),
                  pl.BlockSpec((None, dm, tf), lambda i, e, f: (e, 0, f)),
                  pl.BlockSpec((None, dm, tf), lambda i, e, f: (e, 0, nf + f)),
                  pl.BlockSpec((None, tf, dm), lambda i, e, f: (e, f, 0))],
        out_specs=pl.BlockSpec((tm, dm), lambda i, e, f: (i, 0)),
        out_shape=jax.ShapeDtypeStruct((m, dm), F32),
        scratch_shapes=[pltpu.VMEM((tm, dm), BF16),
                        pltpu.VMEM((tm, LANES), F32),
                        pltpu.VMEM((tm, dm), F32)],
        compiler_params=_cparams("parallel", "arbitrary", "arbitrary"),
        name="ffn_routed" if routed else "ffn_dense",
    )(h, gain.reshape(1, dm), router, w_gu, w_gu, w_down)


def _ple_kernel(h_ref, g_ref, gw_ref, p_ref, pw_ref, o_ref):
    h = h_ref[...]
    u = _rms_rows(h, g_ref[...]).astype(BF16)
    gate = _sigmoid(jnp.dot(u, gw_ref[...], preferred_element_type=F32))
    emb = jnp.dot(p_ref[...].astype(BF16), pw_ref[...], preferred_element_type=F32)
    o_ref[...] = h + emb * gate


def _ple(h, gain, gate_w, p, ple_w, *, tm):
    m, dm = h.shape
    pd = p.shape[1]
    return pl.pallas_call(
        _ple_kernel,
        grid=(m // tm,),
        in_specs=[pl.BlockSpec((tm, dm), lambda i: (i, 0)),
                  pl.BlockSpec((1, dm), lambda i: (0, 0)),
                  pl.BlockSpec((dm, dm), lambda i: (0, 0)),
                  pl.BlockSpec((tm, pd), lambda i: (i, 0)),
                  pl.BlockSpec((pd, dm), lambda i: (0, 0))],
        out_specs=pl.BlockSpec((tm, dm), lambda i: (i, 0)),
        out_shape=jax.ShapeDtypeStruct((m, dm), F32),
        compiler_params=_cparams("parallel"),
        name="ple",
    )(h, gain.reshape(1, dm), gate_w, p, ple_w)


def _band_attn_kernel(q_ref, kp_ref, kc_ref, vp_ref, vc_ref, o_ref, l_ref, *, span, tq):
    j = pl.program_id(2)
    q = q_ref[...] * (HD_B ** -0.5)
    kk = jnp.concatenate([kp_ref[...], kc_ref[...]], axis=0)
    vv = jnp.concatenate([vp_ref[...], vc_ref[...]], axis=0)
    qi = lax.broadcasted_iota(jnp.int32, (span, 2 * span), 0)
    ki = lax.broadcasted_iota(jnp.int32, (span, 2 * span), 1)
    dist = qi + span - ki
    band = (dist >= 0) & (dist <= span)
    lane = lax.broadcasted_iota(jnp.int32, (span, LANES), 1)
    for sb in range(tq // span):
        r0 = sb * span
        mask = band & (ki >= jnp.where(j > 0, 0, span)) if sb == 0 else band
        outs = []
        lse_tile = jnp.zeros((span, LANES), F32)
        for h in range(HG):
            hs = slice(h * HD_B, (h + 1) * HD_B)
            s = _dot_nt(q[r0:r0 + span, hs], kk[r0:r0 + 2 * span, hs])
            s = jnp.where(mask, s, NEG)
            m = jnp.max(s, -1, keepdims=True)
            e = jnp.exp(s - m)
            den = jnp.sum(e, -1, keepdims=True)
            outs.append(_dot(e * (1.0 / den), vv[r0:r0 + 2 * span, hs]))
            lse_tile = jnp.where(lane == h, m + jnp.log(den), lse_tile)
        o_ref[r0:r0 + span, :] = jnp.concatenate(outs, axis=1)
        l_ref[r0:r0 + span, :] = lse_tile


def _band_attn(q, k, v, gi, *, batch, seq):
    win, dil = GROUPS[gi]
    span = win // dil
    n = seq // dil
    tq = min(4 * span, n)
    nb = n // tq
    sub = tq // span
    cur = lambda b, r, j: (r, b * nb + j, 0)
    prev = lambda b, r, j: (r, b * nb * sub + jnp.maximum(j * sub - 1, 0), 0)
    return pl.pallas_call(
        functools.partial(_band_attn_kernel, span=span, tq=tq),
        grid=(batch, dil, nb),
        in_specs=[pl.BlockSpec((None, tq, GW), cur),
                  pl.BlockSpec((None, span, GW), prev),
                  pl.BlockSpec((None, tq, GW), cur),
                  pl.BlockSpec((None, span, GW), prev),
                  pl.BlockSpec((None, tq, GW), cur)],
        out_specs=[pl.BlockSpec((None, tq, GW), cur),
                   pl.BlockSpec((None, tq, LANES), cur)],
        out_shape=[jax.ShapeDtypeStruct((dil, batch * n, GW), F32),
                   jax.ShapeDtypeStruct((dil, batch * n, LANES), F32)],
        compiler_params=_cparams("parallel", "parallel", "arbitrary"),
        name=f"band_attn_g{gi}",
    )(q, k, k, v, v)


def _merge_out_kernel(o0_ref, o1_ref, o2_ref, l0_ref, l1_ref, l2_ref, w_ref, r_ref, o_ref,
                      o_scr, l_scr, *, tm):
    for gi, (o_ph, l_ph) in enumerate(((o1_ref, l1_ref), (o2_ref, l2_ref))):
        d = o_ph.shape[0]
        for r in range(d):
            rows = pl.ds(r, tm // d, stride=d)
            l_scr[gi, rows, :] = l_ph[r]
            for c in range(GW // LANES):
                o_scr[gi, c, rows, :] = o_ph[r, :, c * LANES:(c + 1) * LANES]
    ls = [l0_ref[0], l_scr[0], l_scr[1]]

    def o_cols(g, c):
        if g == 0:
            return o0_ref[0, :, c * LANES:(c + 1) * LANES]
        return o_scr[g - 1, c]

    m = jnp.maximum(jnp.maximum(ls[0], ls[1]), ls[2])
    es = [jnp.exp(l - m) for l in ls]
    inv = 1.0 / (es[0] + es[1] + es[2])
    t = ls[0].shape[0]
    lo = lax.broadcasted_iota(jnp.int32, (t, LANES), 1) < HD_B
    cols = []
    for c in range(GW // LANES):
        acc = None
        for g in range(N_GROUPS):
            wt = es[g] * inv
            wexp = jnp.where(lo, wt[:, 2 * c:2 * c + 1], wt[:, 2 * c + 1:2 * c + 2])
            term = wexp * o_cols(g, c)
            acc = term if acc is None else acc + term
        cols.append(acc)
    o = jnp.concatenate(cols, axis=1).astype(BF16)
    o_ref[...] = r_ref[...] + jnp.dot(o, w_ref[...], preferred_element_type=F32)


def _merge_out(outs, lses, w, res, *, tm):
    m, dm = res.shape
    ph_spec = lambda a: pl.BlockSpec((a.shape[0], tm // a.shape[0], a.shape[2]),
                                     lambda i: (0, i, 0))
    return pl.pallas_call(
        functools.partial(_merge_out_kernel, tm=tm),
        grid=(m // tm,),
        in_specs=[ph_spec(a) for a in outs] + [ph_spec(a) for a in lses] +
                 [pl.BlockSpec((GW, dm), lambda i: (0, 0)),
                  pl.BlockSpec((tm, dm), lambda i: (i, 0))],
        out_specs=pl.BlockSpec((tm, dm), lambda i: (i, 0)),
        out_shape=jax.ShapeDtypeStruct((m, dm), F32),
        scratch_shapes=[pltpu.VMEM((N_GROUPS - 1, GW // LANES, tm, LANES), F32),
                        pltpu.VMEM((N_GROUPS - 1, tm, LANES), F32)],
        compiler_params=_cparams("parallel"),
        name="merge_out",
    )(*outs, *lses, w, res)


def _gather_attn_kernel(q_ref, kvn_ref, c0_ref, c1_ref, c2_ref, o_ref):
    caches = [c0_ref, c1_ref, c2_ref]
    rows = c0_ref.shape[0]
    lo = lax.broadcasted_iota(jnp.int32, (rows, LANES), 1) < HD_B
    lo1 = lo[0:1, :]

    def head_sums(x, mask):
        s_lo = jnp.sum(jnp.where(mask, x, 0.0), -1, keepdims=True)
        s_hi = jnp.sum(jnp.where(mask, 0.0, x), -1, keepdims=True)
        return jnp.where(mask, s_lo, s_hi)

    for c in range(GW // LANES):
        outs, lses = [], []
        for g in range(N_GROUPS):
            sl = slice(g * GW + c * LANES, g * GW + (c + 1) * LANES)
            q = q_ref[:, sl] * (HD_B ** -0.5)
            kn = kvn_ref[:, sl]
            vn = kvn_ref[:, N_GROUPS * GW + g * GW + c * LANES:
                         N_GROUPS * GW + g * GW + (c + 1) * LANES]
            kc = caches[g][:, c * LANES:(c + 1) * LANES]
            vc = caches[g][:, GW + c * LANES:GW + (c + 1) * LANES]
            s = head_sums(kc * q, lo)
            s_new = head_sums(kn * q, lo1)
            m = jnp.maximum(jnp.max(s, 0, keepdims=True), s_new)
            e = jnp.exp(s - m)
            e_new = jnp.exp(s_new - m)
            den = jnp.sum(e, 0, keepdims=True) + e_new
            outs.append((jnp.sum(e * vc, 0, keepdims=True) + e_new * vn) / den)
            lses.append(m + jnp.log(den))
        m = jnp.maximum(jnp.maximum(lses[0], lses[1]), lses[2])
        es = [jnp.exp(l - m) for l in lses]
        tot = es[0] + es[1] + es[2]
        o_ref[:, c * LANES:(c + 1) * LANES] = (
            es[0] * outs[0] + es[1] * outs[1] + es[2] * outs[2]) / tot


def _gather_attn(q, kv_new, caches):
    nb = q.shape[0]
    span = GROUPS[0][0] // GROUPS[0][1]
    c3 = []
    for (win, dil), c in zip(GROUPS, caches):
        lb = c.shape[1]
        assert lb == win and lb // dil == span, "window buffer must hold the full window"
        c3.append(c.reshape(nb, lb // dil, dil * 2 * GW))
    o = pl.pallas_call(
        _gather_attn_kernel,
        grid=(nb,),
        in_specs=[pl.BlockSpec((None, 1, q.shape[1]), lambda b: (b, 0, 0)),
                  pl.BlockSpec((None, 1, kv_new.shape[1]), lambda b: (b, 0, 0))] +
                 [pl.BlockSpec((None, span, 2 * GW), lambda b: (b, 0, 0)) for _ in c3],
        out_specs=pl.BlockSpec((None, 1, GW), lambda b: (b, 0, 0)),
        out_shape=jax.ShapeDtypeStruct((nb, 1, GW), F32),
        compiler_params=_cparams("parallel"),
        name="gather_attn",
    )(q.reshape(nb, 1, -1), kv_new.reshape(nb, 1, -1), *c3)
    return o.reshape(nb, GW)


def _prep_weights(a_w_in, a_A_log, a_dt_bias, a_w_out, w_kv, b_w_q, b_w_out, dense_w_gu,
                  dense_w_down, moe_router, moe_w_gu, moe_w_down, ple_w, ple_gate_w, k_norm,
                  b_q_norm):
    d_model, a_in = a_w_in.shape[1:]
    a_in_pad = -(-a_in // LANES) * LANES
    w = {}
    w['a_w_in'] = jnp.pad(a_w_in[0], ((0, 0), (0, a_in_pad - a_in))).astype(BF16)
    hp = jnp.stack([a_A_log[0], a_dt_bias[0]])
    w['a_hp'] = jnp.pad(hp, ((0, 0), (H_A, LANES - 2 * H_A)))
    w['a_w_out'] = a_w_out[0].astype(BF16)
    w['w_kv'] = w_kv.astype(BF16)
    w['b_w_q'] = b_w_q[0].astype(BF16)
    w['b_w_out'] = b_w_out[0].astype(BF16)
    w['dense_w_gu'] = dense_w_gu.astype(BF16)
    w['dense_w_down'] = dense_w_down.astype(BF16)
    w['router'] = jnp.pad(moe_router[0], ((0, 0), (0, LANES - moe_router.shape[2]))).astype(BF16)
    w['moe_w_gu'] = moe_w_gu[0].astype(BF16)
    w['moe_w_down'] = moe_w_down[0].astype(BF16)
    w['ple_w'] = ple_w.astype(BF16)
    w['ple_gate_w'] = ple_gate_w.astype(BF16)
    w['k_gain'] = jnp.tile(k_norm, HG).reshape(1, GW)
    w['q_gain'] = jnp.tile(b_q_norm[0], HG).reshape(1, GW)
    return w


def _layer0(x, p0, w, P, *, tm, in_tn, mixer):
    proj = _norm_mm(x, P['a_norm'][0], w['a_w_in'], tm=tm, tn=in_tn)
    og, s_new = mixer(proj)
    h = _mm_res(og, w['a_w_out'], x, tm=tm)
    h = _ffn(h, P['ffn_norm'][0], w['router'], w['dense_w_gu'], w['dense_w_down'],
             tm=tm, tf=512, routed=False)
    h = _ple(h, P['ple_norm'][0], w['ple_gate_w'][0], p0, w['ple_w'][0], tm=tm)
    return h, proj, s_new


def _layer1_tail(h, p1, w, P, *, tm):
    h = _ffn(h, P['ffn_norm'][1], w['router'], w['moe_w_gu'], w['moe_w_down'],
             tm=tm, tf=512, routed=True)
    return _ple(h, P['ple_norm'][1], w['ple_gate_w'][1], p1, w['ple_w'][1], tm=tm)


def kernel(x_prompt, x_sample, p_prompt, p_sample, state_conv, state_delta, cache_kv_w128, cache_kv_w512, cache_kv_w2048, a_norm, a_w_in, a_conv_w, a_A_log, a_dt_bias, a_out_norm, a_w_out, kv_norm, w_kv, k_norm, b_norm, b_w_q, b_q_norm, b_w_out, ffn_norm, dense_w_gu, dense_w_down, moe_router, moe_w_gu, moe_w_down, ple_w, ple_norm, ple_gate_w):
    assert a_w_in.shape[0] == 1 and b_w_q.shape[0] == 1, "one mixer of each kind"
    bp, sp, dm = x_prompt.shape
    bs, ls, _ = x_sample.shape
    assert ls == 1, "sample group decodes one token per sequence"
    qkv_w = 3 * H_A * DK_A
    P = dict(a_norm=a_norm, ffn_norm=ffn_norm, ple_norm=ple_norm)
    w = _prep_weights(a_w_in, a_A_log, a_dt_bias, a_w_out, w_kv, b_w_q, b_w_out, dense_w_gu,
                      dense_w_down, moe_router, moe_w_gu, moe_w_down, ple_w, ple_gate_w, k_norm,
                      b_q_norm)
    a_in_pad = w['a_w_in'].shape[1]
    in_tn = a_in_pad // 3 if a_in_pad % (3 * LANES) == 0 else LANES
    conv_w = a_conv_w[0]
    out_norm = a_out_norm[0].reshape(1, DK_A)

    mp = bp * sp
    tm = min(1024, sp)
    xp = x_prompt.reshape(mp, dm)
    conv0 = jnp.zeros((bp, CONV_W - 1, qkv_w), F32)
    s0 = jnp.zeros((bp, H_A, DK_A, DK_A), F32)
    h, proj, delta_p = _layer0(
        xp, p_prompt[0].reshape(mp, -1), w, P, tm=tm, in_tn=in_tn,
        mixer=lambda pr: _gdn_prompt(pr, conv_w, w['a_hp'], out_norm, conv0, s0, batch=bp, seq=sp))
    conv_p = proj.reshape(bp, sp, -1)[:, sp - (CONV_W - 1):, :qkv_w][None]

    cos, sin = _rope_tables(jnp.arange(sp, dtype=jnp.int32))
    dils = [d for _, d in GROUPS]
    kv, *kv_ph = _proj_rope(h, kv_norm, w['w_kv'], w['k_gain'], cos, sin, tm=tm, n_rope=N_GROUPS,
                            natural=True, dils=dils + dils)
    q_ph = _proj_rope(h, b_norm[0], w['b_w_q'], w['q_gain'], cos, sin, tm=tm, n_rope=N_GROUPS,
                      natural=False, dils=dils)
    outs, lses = [], []
    for gi in range(N_GROUPS):
        o, lse = _band_attn(q_ph[gi], kv_ph[gi], kv_ph[N_GROUPS + gi], gi, batch=bp, seq=sp)
        outs.append(o)
        lses.append(lse)
    h = _merge_out(outs, lses, w['b_w_out'], h, tm=tm)
    y_prompt = _layer1_tail(h, p_prompt[1].reshape(mp, -1), w, P, tm=tm).reshape(bp, sp, dm)
    kv5 = kv.reshape(bp, sp, 2, N_GROUPS, HG, HD_B)
    kv_p = [kv5[:, sp - min(win, sp):, :, gi] for gi, (win, _) in enumerate(GROUPS)]

    xs = x_sample.reshape(bs, dm)
    hs, proj_s, delta_s = _layer0(
        xs, p_sample[0].reshape(bs, -1), w, P, tm=bs, in_tn=in_tn,
        mixer=lambda pr: _gdn_step(pr, state_conv[0], conv_w, w['a_hp'], out_norm, state_delta[0]))
    conv_s = jnp.concatenate([state_conv[0][:, 1:], proj_s[:, None, :qkv_w]], axis=1)[None]
    cos_s, sin_s = _rope_tables(jnp.full((bs,), PAST_LEN, jnp.int32))
    kv_s, = _proj_rope(hs, kv_norm, w['w_kv'], w['k_gain'], cos_s, sin_s, tm=bs, n_rope=N_GROUPS,
                       natural=True, dils=())
    q_s, = _proj_rope(hs, b_norm[0], w['b_w_q'], w['q_gain'], cos_s, sin_s, tm=bs, n_rope=N_GROUPS,
                      natural=True, dils=())
    o_s = _gather_attn(q_s, kv_s, (cache_kv_w128, cache_kv_w512, cache_kv_w2048))
    hs = _mm_res(o_s, w['b_w_out'], hs, tm=bs)
    y_sample = _layer1_tail(hs, p_sample[1].reshape(bs, -1), w, P, tm=bs).reshape(bs, 1, dm)
    kvs5 = kv_s.reshape(bs, 1, 2, N_GROUPS, HG, HD_B)
    kv_sn = [kvs5[:, :, :, gi] for gi in range(N_GROUPS)]

    return (y_prompt, y_sample, conv_p, conv_s, delta_p[None], delta_s[None],
            kv_p[0], kv_sn[0], kv_p[1], kv_sn[1], kv_p[2], kv_sn[2])
```

```python
import functools

import jax
import jax.numpy as jnp
from jax import lax
from jax.experimental import pallas as pl
from jax.experimental.pallas import tpu as pltpu

F32 = jnp.float32
BF16 = jnp.bfloat16

EPS = 1e-6
PAST_LEN = 16384
GROUPS = ((128, 1), (512, 4), (2048, 16))
N_GROUPS = len(GROUPS)
HG = 8
HD_B = 64
ROT_DIM = HD_B // 4
ROPE_THETA = 500000.0
GW = HG * HD_B
H_A = 8
DK_A = 128
CONV_W = 4
CHUNK = 64
LANES = 128
VMEM_LIMIT = 52 * 1024 * 1024
NEG = -1e30


def _cparams(*sem):
    return pltpu.CompilerParams(dimension_semantics=sem, vmem_limit_bytes=VMEM_LIMIT)


def _rms_rows(x, gain):
    return x * lax.rsqrt(jnp.mean(x * x, -1, keepdims=True) + EPS) * gain


def _silu(x):
    return x * (1.0 / (1.0 + jnp.exp(-x)))


def _sigmoid(x):
    return 1.0 / (1.0 + jnp.exp(-x))


def _dot(a, b):
    return jnp.dot(a.astype(BF16), b.astype(BF16), preferred_element_type=F32)


def _dot_nt(a, b):
    return lax.dot_general(a.astype(BF16), b.astype(BF16), (((1,), (1,)), ((), ())),
                           preferred_element_type=F32)


def _dot_tn(a, b):
    return lax.dot_general(a.astype(BF16), b.astype(BF16), (((0,), (0,)), ((), ())),
                           preferred_element_type=F32)


def _head_norm_rope(x, hgain, cos, sin):
    t = x.shape[0]
    lane = lax.broadcasted_iota(jnp.int32, (t, LANES), 1)
    lo = lane < HD_B
    d = lane & (HD_B - 1)
    outs = []
    for c in range(GW // LANES):
        sl = slice(c * LANES, (c + 1) * LANES)
        xb = x[:, sl]
        sq = xb * xb
        s_lo = jnp.sum(jnp.where(lo, sq, 0.0), -1, keepdims=True)
        s_hi = jnp.sum(jnp.where(lo, 0.0, sq), -1, keepdims=True)
        scale = jnp.where(lo, lax.rsqrt(s_lo * (1.0 / HD_B) + EPS),
                          lax.rsqrt(s_hi * (1.0 / HD_B) + EPS))
        yb = xb * scale * hgain[:, sl]
        half = ROT_DIM // 2
        rot = jnp.where(d < half, pltpu.roll(yb, LANES - half, 1), pltpu.roll(yb, half, 1))
        outs.append(yb * cos[:, sl] + rot * sin[:, sl])
    return jnp.concatenate(outs, axis=1)


def _norm_mm_kernel(x_ref, g_ref, w_ref, o_ref, u_ref):
    @pl.when(pl.program_id(1) == 0)
    def _():
        u_ref[...] = _rms_rows(x_ref[...], g_ref[...]).astype(BF16)

    o_ref[...] = jnp.dot(u_ref[...], w_ref[...], preferred_element_type=F32)


def _norm_mm(x, gain, w, *, tm, tn):
    m, k = x.shape
    n = w.shape[1]
    return pl.pallas_call(
        _norm_mm_kernel,
        grid=(m // tm, n // tn),
        in_specs=[pl.BlockSpec((tm, k), lambda i, j: (i, 0)),
                  pl.BlockSpec((1, k), lambda i, j: (0, 0)),
                  pl.BlockSpec((k, tn), lambda i, j: (0, j))],
        out_specs=pl.BlockSpec((tm, tn), lambda i, j: (i, j)),
        out_shape=jax.ShapeDtypeStruct((m, n), F32),
        scratch_shapes=[pltpu.VMEM((tm, k), BF16)],
        compiler_params=_cparams("parallel", "arbitrary"),
        name="norm_mm",
    )(x, gain.reshape(1, k), w)


def _proj_rope_kernel(x_ref, g_ref, w_ref, hg_ref, cos_ref, sin_ref, *rest,
                      n_rope, natural, dils, tm):
    n_out = int(natural) + len(dils)
    outs, (u_ref, slab_ref) = rest[:n_out], rest[n_out:]
    nat_ref = outs[0] if natural else None
    ph_refs = outs[int(natural):]
    j = pl.program_id(1)

    @pl.when(j == 0)
    def _():
        u_ref[...] = _rms_rows(x_ref[...], g_ref[...]).astype(BF16)

    acc = jnp.dot(u_ref[...], w_ref[...], preferred_element_type=F32)

    def emit(jj, y):
        if natural:
            nat_ref[...] = y
        if jj >= len(dils):
            return
        d = dils[jj]
        if d == 1:
            ph_refs[jj][0] = y.astype(BF16)
            return
        for c in range(GW // LANES):
            slab_ref[c] = y[:, c * LANES:(c + 1) * LANES]
        for r in range(d):
            for c in range(GW // LANES):
                ph_refs[jj][r, :, c * LANES:(c + 1) * LANES] = (
                    slab_ref[c, pl.ds(r, tm // d, stride=d), :].astype(BF16))

    for jj in range(max(n_rope, len(dils))):
        @pl.when(j == jj)
        def _(jj=jj):
            y = acc
            if jj < n_rope:
                y = _head_norm_rope(acc, hg_ref[...], cos_ref[...], sin_ref[...])
            emit(jj, y)

    if natural:
        @pl.when(j >= max(n_rope, len(dils)))
        def _():
            nat_ref[...] = acc


def _proj_rope(x, gain, w, hgain, cos, sin, *, tm, n_rope, natural, dils):
    m, k = x.shape
    n = w.shape[1]
    pos_blocks = cos.shape[0] // tm
    out_specs, out_shape = [], []
    if natural:
        out_specs.append(pl.BlockSpec((tm, GW), lambda i, j: (i, j)))
        out_shape.append(jax.ShapeDtypeStruct((m, n), F32))
    for d in dils:
        out_specs.append(pl.BlockSpec((d, tm // d, GW), lambda i, j: (0, i, 0)))
        out_shape.append(jax.ShapeDtypeStruct((d, m // d, GW), BF16))
    return pl.pallas_call(
        functools.partial(_proj_rope_kernel, n_rope=n_rope, natural=natural, dils=tuple(dils),
                          tm=tm),
        grid=(m // tm, n // GW),
        in_specs=[pl.BlockSpec((tm, k), lambda i, j: (i, 0)),
                  pl.BlockSpec((1, k), lambda i, j: (0, 0)),
                  pl.BlockSpec((k, GW), lambda i, j: (0, j)),
                  pl.BlockSpec((1, GW), lambda i, j: (0, 0)),
                  pl.BlockSpec((tm, GW), lambda i, j: (i % pos_blocks, 0)),
                  pl.BlockSpec((tm, GW), lambda i, j: (i % pos_blocks, 0))],
        out_specs=out_specs,
        out_shape=out_shape,
        scratch_shapes=[pltpu.VMEM((tm, k), BF16),
                        pltpu.VMEM((GW // LANES, tm, LANES), F32)],
        compiler_params=_cparams("parallel", "arbitrary"),
        name="proj_rope",
    )(x, gain.reshape(1, k), w, hgain, cos, sin)


def _rope_tables(pos):
    half = ROT_DIM // 2
    inv = ROPE_THETA ** (-jnp.arange(half, dtype=F32) * 2.0 / ROT_DIM)
    ang = pos.astype(F32)[:, None] * inv[None]
    c, s = jnp.cos(ang), jnp.sin(ang)
    n = pos.shape[0]
    cos_h = jnp.concatenate([c, c, jnp.ones((n, HD_B - ROT_DIM), F32)], 1)
    sin_h = jnp.concatenate([-s, s, jnp.zeros((n, HD_B - ROT_DIM), F32)], 1)
    return jnp.tile(cos_h, (1, HG)), jnp.tile(sin_h, (1, HG))


def _mm_res_kernel(x_ref, w_ref, r_ref, o_ref):
    o_ref[...] = r_ref[...] + jnp.dot(x_ref[...].astype(BF16), w_ref[...],
                                      preferred_element_type=F32)


def _mm_res(x, w, res, *, tm):
    m, k = x.shape
    n = w.shape[1]
    return pl.pallas_call(
        _mm_res_kernel,
        grid=(m // tm,),
        in_specs=[pl.BlockSpec((tm, k), lambda i: (i, 0)),
                  pl.BlockSpec((k, n), lambda i: (0, 0)),
                  pl.BlockSpec((tm, n), lambda i: (i, 0))],
        out_specs=pl.BlockSpec((tm, n), lambda i: (i, 0)),
        out_shape=jax.ShapeDtypeStruct((m, n), F32),
        compiler_params=_cparams("parallel"),
        name="mm_res",
    )(x, w, res)


def _unit_lower_inverses(mats):
    c = mats[0].shape[0]
    row = lax.broadcasted_iota(jnp.int32, (c, c), 0)
    col = lax.broadcasted_iota(jnp.int32, (c, c), 1)
    eye = jnp.where(row == col, 1.0, 0.0).astype(F32)
    ts = None
    b = 1
    while b < c:
        sel = ((row ^ col) < 2 * b) & ((row & b) != 0) & ((col & b) == 0)
        lows = [jnp.where(sel, a, 0.0) for a in mats]
        if ts is None:
            ts = [eye - low for low in lows]
        else:
            tl = [_dot(t, low) for t, low in zip(ts, lows)]
            ts = [t - _dot(x, t) for t, x in zip(ts, tl)]
        b *= 2
    return ts


def _gdn_head_params(ba, hp):
    beta = _sigmoid(ba)
    x = ba + hp[1:2, :]
    softplus = jnp.maximum(x, 0.0) + jnp.log(1.0 + jnp.exp(-jnp.abs(x)))
    g = -jnp.exp(hp[0:1, :]) * softplus
    return beta, g


def _gdn_kernel(qkv_ref, z_ref, ba_ref, cw_ref, hp_ref, on_ref, conv0_ref, s0_ref,
                og_ref, sout_ref, xbuf, s_scr, *, C, nch):
    n = pl.program_id(1)
    R = nch * C
    pad = 8

    @pl.when(n == 0)
    def _():
        xbuf[pad - (CONV_W - 1):pad, :] = conv0_ref[...]
        s_scr[...] = s0_ref[...]

    xbuf[pad:pad + R, :] = qkv_ref[...]

    def conv_cols(c0):
        acc = None
        for j in range(CONV_W):
            r0 = pad - (CONV_W - 1) + j
            term = xbuf[r0:r0 + R, c0:c0 + DK_A] * cw_ref[j:j + 1, c0:c0 + DK_A]
            acc = term if acc is None else acc + term
        return _silu(acc)

    beta, g = _gdn_head_params(ba_ref[...], hp_ref[...])
    rr = lax.broadcasted_iota(jnp.int32, (R, R), 0)
    rc = lax.broadcasted_iota(jnp.int32, (R, R), 1)
    blocktri = ((rr >= rc) & ((rr ^ rc) < C)).astype(F32)
    gcum = jnp.dot(blocktri, g, preferred_element_type=F32, precision=lax.Precision.HIGHEST)
    gcum_t = gcum.T
    row = lax.broadcasted_iota(jnp.int32, (C, C), 0)
    col = lax.broadcasted_iota(jnp.int32, (C, C), 1)
    incl = row >= col
    strict = row > col

    units = [(c, h) for c in range(nch) for h in range(H_A)]
    qs, ks, vs = {}, {}, {}
    for h in range(H_A):
        q = conv_cols(h * DK_A)
        k = conv_cols((H_A + h) * DK_A)
        v = conv_cols((2 * H_A + h) * DK_A)
        q = q * lax.rsqrt(jnp.sum(q * q, -1, keepdims=True) + EPS) * (DK_A ** -0.5)
        k = k * lax.rsqrt(jnp.sum(k * k, -1, keepdims=True) + EPS)
        for c in range(nch):
            rs = slice(c * C, (c + 1) * C)
            qs[c, h], ks[c, h], vs[c, h] = q[rs], k[rs], v[rs]

    bcs, gcs, decays, kbs = {}, {}, {}, {}
    for c, h in units:
        rs = slice(c * C, (c + 1) * C)
        bcs[c, h] = beta[rs, h:h + 1]
        gcs[c, h] = gcum[rs, H_A + h:H_A + h + 1]
        gr = gcum_t[H_A + h:H_A + h + 1, rs]
        decays[c, h] = jnp.exp(jnp.where(incl, gcs[c, h] - gr, NEG))
        kbs[c, h] = ks[c, h] * bcs[c, h]
    grams = {u: _dot_nt(jnp.concatenate([kbs[u], qs[u]], axis=0), ks[u]) for u in units}
    a_mats = [jnp.where(strict, grams[u][:C] * decays[u], 0.0) for u in units]
    aqks = {u: grams[u][C:] * decays[u] for u in units}
    t_mats = dict(zip(units, _unit_lower_inverses(a_mats)))
    egs = {u: jnp.exp(gcs[u]) for u in units}
    sols = {u: _dot(t_mats[u], jnp.concatenate([vs[u] * bcs[u], kbs[u] * egs[u]], axis=1))
            for u in units}

    states = [s_scr[h] for h in range(H_A)]
    for c in range(nch):
        rs = slice(c * C, (c + 1) * C)
        for h in range(H_A):
            u = (c, h)
            g_last = gcs[u][C - 1:C, :]
            ws = _dot(jnp.concatenate([sols[u][:, DK_A:], qs[u] * egs[u]], axis=0), states[h])
            v_new = sols[u][:, :DK_A] - ws[:C]
            o = ws[C:] + _dot(aqks[u], v_new)
            kd = ks[u] * jnp.exp(g_last - gcs[u])
            states[h] = states[h] * jnp.exp(g_last) + _dot_tn(kd, v_new)
            o = _rms_rows(o, on_ref[...]) * _silu(z_ref[rs, h * DK_A:(h + 1) * DK_A])
            og_ref[rs, h * DK_A:(h + 1) * DK_A] = o
    for h in range(H_A):
        s_scr[h] = states[h]

    xbuf[pad - (CONV_W - 1):pad, :] = xbuf[pad + R - (CONV_W - 1):pad + R, :]

    @pl.when(n == pl.num_programs(1) - 1)
    def _():
        sout_ref[...] = s_scr[...]


def _gdn_prompt(proj, conv_w, hp, out_norm, conv0, s0, *, batch, seq):
    C = min(CHUNK, seq)
    nch = 2 if seq % (2 * C) == 0 else 1
    R = nch * C
    nc = seq // R
    qkv_w = 3 * H_A * DK_A
    z_w = H_A * DK_A
    return pl.pallas_call(
        functools.partial(_gdn_kernel, C=C, nch=nch),
        grid=(batch, nc),
        in_specs=[pl.BlockSpec((R, qkv_w), lambda b, n: (b * nc + n, 0)),
                  pl.BlockSpec((R, z_w), lambda b, n: (b * nc + n, qkv_w // z_w)),
                  pl.BlockSpec((R, LANES), lambda b, n: (b * nc + n, (qkv_w + z_w) // LANES)),
                  pl.BlockSpec((CONV_W, qkv_w), lambda b, n: (0, 0)),
                  pl.BlockSpec((2, LANES), lambda b, n: (0, 0)),
                  pl.BlockSpec((1, DK_A), lambda b, n: (0, 0)),
                  pl.BlockSpec((None, CONV_W - 1, qkv_w), lambda b, n: (b, 0, 0)),
                  pl.BlockSpec((None, H_A, DK_A, DK_A), lambda b, n: (b, 0, 0, 0))],
        out_specs=[pl.BlockSpec((R, z_w), lambda b, n: (b * nc + n, 0)),
                   pl.BlockSpec((None, H_A, DK_A, DK_A), lambda b, n: (b, 0, 0, 0))],
        out_shape=[jax.ShapeDtypeStruct((batch * seq, z_w), F32),
                   jax.ShapeDtypeStruct((batch, H_A, DK_A, DK_A), F32)],
        scratch_shapes=[pltpu.VMEM((R + 8, qkv_w), F32),
                        pltpu.VMEM((H_A, DK_A, DK_A), F32)],
        compiler_params=_cparams("parallel", "arbitrary"),
        name="gdn_chunked",
    )(proj, proj, proj, conv_w, hp, out_norm, conv0, s0)


def _gdn_step_kernel(proj_ref, conv_ref, cw_ref, hp_ref, on_ref, s0_ref, og_ref, sout_ref, qk_scr):
    qkv_w = 3 * H_A * DK_A
    z_w = H_A * DK_A

    def conv_cols(c0):
        sl = slice(c0, c0 + DK_A)
        acc = proj_ref[:, sl] * cw_ref[CONV_W - 1:CONV_W, sl]
        for j in range(CONV_W - 1):
            acc = acc + conv_ref[j:j + 1, sl] * cw_ref[j:j + 1, sl]
        return _silu(acc)

    beta, g = _gdn_head_params(proj_ref[:, qkv_w + z_w:qkv_w + z_w + LANES], hp_ref[...])
    qk_scr[...] = jnp.zeros_like(qk_scr)
    vs = []
    for h in range(H_A):
        q = conv_cols(h * DK_A)
        k = conv_cols((H_A + h) * DK_A)
        vs.append(conv_cols((2 * H_A + h) * DK_A))
        qk_scr[H_A + h:H_A + h + 1, :] = (
            q * lax.rsqrt(jnp.sum(q * q, -1, keepdims=True) + EPS) * (DK_A ** -0.5))
        qk_scr[h:h + 1, :] = k * lax.rsqrt(jnp.sum(k * k, -1, keepdims=True) + EPS)
    qk = qk_scr[...]
    qk_t = qk.T
    for h in range(H_A):
        k_row = qk[h:h + 1, :]
        q_row = qk[H_A + h:H_A + h + 1, :]
        k_col = qk_t[:, h:h + 1]
        q_col = qk_t[:, H_A + h:H_A + h + 1]
        bh = beta[:, h:h + 1]
        eg = jnp.exp(g[:, H_A + h:H_A + h + 1])
        s = s0_ref[h]
        k_s = jnp.sum(s * k_col, 0, keepdims=True)
        q_s = jnp.sum(s * q_col, 0, keepdims=True)
        v_new = bh * (vs[h] - eg * k_s)
        o = eg * q_s + jnp.sum(q_row * k_row, -1, keepdims=True) * v_new
        sout_ref[h] = s * eg + k_col * v_new
        o = _rms_rows(o, on_ref[...]) * _silu(proj_ref[:, qkv_w + h * DK_A:qkv_w + (h + 1) * DK_A])
        og_ref[:, h * DK_A:(h + 1) * DK_A] = o


def _gdn_step(proj, conv_state, conv_w, hp, out_norm, s0):
    nb, pw = proj.shape
    qkv_w = 3 * H_A * DK_A
    z_w = H_A * DK_A
    og, s_new = pl.pallas_call(
        _gdn_step_kernel,
        grid=(nb,),
        in_specs=[pl.BlockSpec((None, 1, pw), lambda b: (b, 0, 0)),
                  pl.BlockSpec((None, CONV_W - 1, qkv_w), lambda b: (b, 0, 0)),
                  pl.BlockSpec((CONV_W, qkv_w), lambda b: (0, 0)),
                  pl.BlockSpec((2, LANES), lambda b: (0, 0)),
                  pl.BlockSpec((1, DK_A), lambda b: (0, 0)),
                  pl.BlockSpec((None, H_A, DK_A, DK_A), lambda b: (b, 0, 0, 0))],
        out_specs=[pl.BlockSpec((None, 1, z_w), lambda b: (b, 0, 0)),
                   pl.BlockSpec((None, H_A, DK_A, DK_A), lambda b: (b, 0, 0, 0))],
        out_shape=[jax.ShapeDtypeStruct((nb, 1, z_w), F32),
                   jax.ShapeDtypeStruct((nb, H_A, DK_A, DK_A), F32)],
        scratch_shapes=[pltpu.VMEM((LANES, DK_A), F32)],
        compiler_params=_cparams("parallel"),
        name="gdn_step",
    )(proj.reshape(nb, 1, pw), conv_state, conv_w, hp, out_norm, s0)
    return og.reshape(nb, z_w), s_new


def _ffn_kernel(h_ref, g_ref, wg_ref, wu_ref, wd_ref, o_ref, u_ref, acc_ref):
    f = pl.program_id(1)

    @pl.when(f == 0)
    def _():
        u_ref[...] = _rms_rows(h_ref[...], g_ref[...]).astype(BF16)
        acc_ref[...] = jnp.zeros_like(acc_ref)

    u = u_ref[...]
    gate = jnp.dot(u, wg_ref[...], preferred_element_type=F32)
    up = jnp.dot(u, wu_ref[...], preferred_element_type=F32)
    act = (_silu(gate) * up).astype(BF16)
    acc_ref[...] += jnp.dot(act, wd_ref[...], preferred_element_type=F32)

    @pl.when(f == pl.num_programs(1) - 1)
    def _():
        o_ref[...] = h_ref[...] + acc_ref[...]


def _ffn(h, gain, w_gu, w_down, *, tm, tf):
    m, dm = h.shape
    ff = w_down.shape[0]
    nf = ff // tf
    return pl.pallas_call(
        _ffn_kernel,
        grid=(m // tm, nf),
        in_specs=[pl.BlockSpec((tm, dm), lambda i, f: (i, 0)),
                  pl.BlockSpec((1, dm), lambda i, f: (0, 0)),
                  pl.BlockSpec((dm, tf), lambda i, f: (0, f)),
                  pl.BlockSpec((dm, tf), lambda i, f: (0, nf + f)),
                  pl.BlockSpec((tf, dm), lambda i, f: (f, 0))],
        out_specs=pl.BlockSpec((tm, dm), lambda i, f: (i, 0)),
        out_shape=jax.ShapeDtypeStruct((m, dm), F32),
        scratch_shapes=[pltpu.VMEM((tm, dm), BF16),
                        pltpu.VMEM((tm, dm), F32)],
        compiler_params=_cparams("parallel", "arbitrary"),
        name="ffn_dense",
    )(h, gain.reshape(1, dm), w_gu, w_gu, w_down)


N_EXPERTS = 8
SEG_ALIGN = LANES


def _top2(logits):
    t = logits.shape[0]
    lane = lax.broadcasted_iota(jnp.int32, (t, LANES), 1)
    valid = lane < N_EXPERTS
    lg = jnp.where(valid, logits, NEG)
    mx = jnp.max(lg, -1, keepdims=True)
    e = jnp.where(valid, jnp.exp(lg - mx), 0.0)
    probs = e / jnp.sum(e, -1, keepdims=True)
    p1 = jnp.max(probs, -1, keepdims=True)
    i1 = jnp.min(jnp.where((probs == p1) & valid, lane, LANES), -1, keepdims=True)
    rest = jnp.where((lane == i1) | ~valid, -1.0, probs)
    p2 = jnp.max(rest, -1, keepdims=True)
    i2 = jnp.min(jnp.where(rest == p2, lane, LANES), -1, keepdims=True)
    tot = p1 + p2
    return i1, i2, p1 / tot, p2 / tot


def _router_kernel(h_ref, g_ref, r_ref, tri_ref, upper_ref, col_ref, row_ref, seg_ref):
    t = h_ref.shape[0]
    u = _rms_rows(h_ref[...], g_ref[...]).astype(BF16)
    logits = jnp.dot(u, r_ref[...], preferred_element_type=F32)
    i1, i2, g1, g2 = _top2(logits)
    lane = lax.broadcasted_iota(jnp.int32, (t, LANES), 1)
    sel = jnp.where((lane == i1) | (lane == i2), 1.0, 0.0)
    pos = jnp.dot(tri_ref[...], sel.astype(BF16), preferred_element_type=F32)
    counts = jnp.sum(sel, 0, keepdims=True)
    nblk = jnp.floor((counts + (SEG_ALIGN - 1)) * (1.0 / SEG_ALIGN))
    nblk8 = jnp.broadcast_to(nblk, (8, LANES))
    start8 = jnp.dot(nblk8.astype(BF16), upper_ref[...], preferred_element_type=F32)
    dest = start8[0:1] * SEG_ALIGN + pos
    d1 = jnp.sum(jnp.where(lane == i1, dest, 0.0), -1, keepdims=True)
    d2 = jnp.sum(jnp.where(lane == i2, dest, 0.0), -1, keepdims=True)
    col = jnp.where(lane == 0, d1, jnp.where(lane == 1, d2,
                    jnp.where(lane == 2, g1, jnp.where(lane == 3, g2, 0.0))))
    col_ref[...] = col
    row_ref[...] = col.T[0:8, :]
    lane8 = lane[0:8]
    seg = jnp.where(lane8 < N_EXPERTS, start8,
                    jnp.where(lane8 < 2 * N_EXPERTS, pltpu.roll(nblk8, N_EXPERTS, 1), 0.0))
    seg_ref[...] = seg[0:1].astype(jnp.int32)


def _route(h, gain, router, *, tm):
    m, dm = h.shape
    nt = m // tm
    tri = jnp.tril(jnp.ones((tm, tm), F32), -1).astype(BF16)
    upper = jnp.triu(jnp.ones((LANES, LANES), F32), 1).astype(BF16)
    return pl.pallas_call(
        _router_kernel,
        grid=(nt,),
        in_specs=[pl.BlockSpec((tm, dm), lambda i: (i, 0)),
                  pl.BlockSpec((1, dm), lambda i: (0, 0)),
                  pl.BlockSpec((dm, LANES), lambda i: (0, 0)),
                  pl.BlockSpec((tm, tm), lambda i: (0, 0)),
                  pl.BlockSpec((LANES, LANES), lambda i: (0, 0))],
        out_specs=[pl.BlockSpec((tm, LANES), lambda i: (i, 0)),
                   pl.BlockSpec((None, 8, tm), lambda i: (i, 0, 0)),
                   pl.BlockSpec((None, 1, LANES), lambda i: (i, 0, 0))],
        out_shape=[jax.ShapeDtypeStruct((m, LANES), F32),
                   jax.ShapeDtypeStruct((nt, 8, tm), F32),
                   jax.ShapeDtypeStruct((nt, 1, LANES), jnp.int32)],
        compiler_params=_cparams("parallel"),
        name="moe_route",
    )(h, gain.reshape(1, dm), router, tri, upper)


def _one_hot_rows(row0, n, d1_row, d2_row):
    ridx = (lax.broadcasted_iota(jnp.int32, (n, d1_row.shape[1]), 0) + row0).astype(F32)
    return jnp.where((ridx == d1_row) | (ridx == d2_row), 1.0, 0.0).astype(BF16)


def _experts_kernel(seg_ref, h_ref, g_ref, col_ref, row_ref, wg_ref, wu_ref, wd_ref, o_ref,
                    xs_scr, acc_scr, y_scr, *, rows_max, gch, kch):
    i, e, f = pl.program_id(0), pl.program_id(1), pl.program_id(2)
    t = h_ref.shape[0]

    @pl.when((e == 0) & (f == 0))
    def _():
        u = _rms_rows(h_ref[...], g_ref[...]).astype(BF16)
        d1_row, d2_row = row_ref[0:1, :], row_ref[1:2, :]
        for c in range(rows_max // gch):
            p = _one_hot_rows(c * gch, gch, d1_row, d2_row)
            xs_scr[c * gch:(c + 1) * gch, :] = jnp.dot(
                p, u, preferred_element_type=F32).astype(BF16)
        acc_scr[...] = jnp.zeros_like(acc_scr)

    start = seg_ref[i * LANES + e]
    nblk = seg_ref[i * LANES + N_EXPERTS + e]

    def run_blocks(b0, nb):
        x = xs_scr[pl.ds(pl.multiple_of(b0 * LANES, LANES), nb * LANES), :]
        gate_t = _dot_nt(wg_ref[...], x)
        up_t = _dot_nt(wu_ref[...], x)
        act_t = (_silu(gate_t) * up_t).astype(BF16)
        down_t = jnp.dot(wd_ref[...], act_t, preferred_element_type=F32)
        for k in range(nb):
            acc_scr[b0 + k] += down_t[:, k * LANES:(k + 1) * LANES]

    n4 = nblk // 4

    def body(j, carry):
        run_blocks(start + 4 * j, 4)
        return carry

    lax.fori_loop(0, n4, body, 0)
    rem = nblk - 4 * n4
    tail = start + 4 * n4

    @pl.when((rem & 2) != 0)
    def _():
        run_blocks(tail, 2)

    @pl.when((rem & 1) != 0)
    def _():
        run_blocks(tail + (rem & 2), 1)

    @pl.when((e == pl.num_programs(1) - 1) & (f == pl.num_programs(2) - 1))
    def _():
        col = col_ref[...]
        d1_col, d2_col, g1_col, g2_col = col[:, 0:1], col[:, 1:2], col[:, 2:3], col[:, 3:4]
        d1_row, d2_row = row_ref[0:1, :], row_ref[1:2, :]
        lane = lax.broadcasted_iota(jnp.int32, (t, LANES), 1)
        for c in range(rows_max // kch):
            parts = []
            for k in range(kch // LANES):
                b = c * (kch // LANES) + k
                li = (lane + b * LANES).astype(F32)
                gs = jnp.sum(jnp.where(li == d1_col, g1_col, 0.0) +
                             jnp.where(li == d2_col, g2_col, 0.0), 0, keepdims=True)
                parts.append((acc_scr[b] * gs).astype(BF16))
            contrib = jnp.dot(jnp.concatenate(parts, axis=1),
                              _one_hot_rows(c * kch, kch, d1_row, d2_row),
                              preferred_element_type=F32)
            if c == 0:
                y_scr[...] = contrib
            else:
                y_scr[...] += contrib
        o_ref[...] = h_ref[...] + y_scr[...].T


def _experts(h, gain, col, row, seg, wgu_t, wd_t, *, tm, tf):
    m, dm = h.shape
    ne, _, ff = wd_t.shape
    nf = ff // tf
    nt = m // tm
    rows_max = 2 * tm + ne * SEG_ALIGN
    gch = 1024 if rows_max % 1024 == 0 else 256
    kch = 512 if rows_max % 512 == 0 else 256
    grid_spec = pltpu.PrefetchScalarGridSpec(
        num_scalar_prefetch=1,
        grid=(nt, ne, nf),
        in_specs=[pl.BlockSpec((tm, dm), lambda i, e, f, s: (i, 0),
                               pipeline_mode=pl.Buffered(1)),
                  pl.BlockSpec((1, dm), lambda i, e, f, s: (0, 0)),
                  pl.BlockSpec((tm, LANES), lambda i, e, f, s: (i, 0),
                               pipeline_mode=pl.Buffered(1)),
                  pl.BlockSpec((None, 8, tm), lambda i, e, f, s: (i, 0, 0)),
                  pl.BlockSpec((None, tf, dm), lambda i, e, f, s: (e, f, 0)),
                  pl.BlockSpec((None, tf, dm), lambda i, e, f, s: (e, nf + f, 0)),
                  pl.BlockSpec((None, dm, tf), lambda i, e, f, s: (e, 0, f))],
        out_specs=pl.BlockSpec((tm, dm), lambda i, e, f, s: (i, 0)),
        scratch_shapes=[pltpu.VMEM((rows_max, dm), BF16),
                        pltpu.VMEM((rows_max // LANES, dm, LANES), F32),
                        pltpu.VMEM((dm, tm), F32)])
    return pl.pallas_call(
        functools.partial(_experts_kernel, rows_max=rows_max, gch=gch, kch=kch),
        grid_spec=grid_spec,
        out_shape=jax.ShapeDtypeStruct((m, dm), F32),
        compiler_params=_cparams("parallel", "arbitrary", "arbitrary"),
        name="moe_experts",
    )(seg.reshape(-1), h, gain.reshape(1, dm), col, row, wgu_t, wgu_t, wd_t)


def _ple_kernel(h_ref, g_ref, gw_ref, p_ref, pw_ref, o_ref):
    h = h_ref[...]
    u = _rms_rows(h, g_ref[...]).astype(BF16)
    gate = _sigmoid(jnp.dot(u, gw_ref[...], preferred_element_type=F32))
    emb = jnp.dot(p_ref[...].astype(BF16), pw_ref[...], preferred_element_type=F32)
    o_ref[...] = h + emb * gate


def _ple(h, gain, gate_w, p, ple_w, *, tm):
    m, dm = h.shape
    pd = p.shape[1]
    return pl.pallas_call(
        _ple_kernel,
        grid=(m // tm,),
        in_specs=[pl.BlockSpec((tm, dm), lambda i: (i, 0)),
                  pl.BlockSpec((1, dm), lambda i: (0, 0)),
                  pl.BlockSpec((dm, dm), lambda i: (0, 0)),
                  pl.BlockSpec((tm, pd), lambda i: (i, 0)),
                  pl.BlockSpec((pd, dm), lambda i: (0, 0))],
        out_specs=pl.BlockSpec((tm, dm), lambda i: (i, 0)),
        out_shape=jax.ShapeDtypeStruct((m, dm), F32),
        compiler_params=_cparams("parallel"),
        name="ple",
    )(h, gain.reshape(1, dm), gate_w, p, ple_w)


def _band_attn_kernel(q_ref, kp_ref, kc_ref, vp_ref, vc_ref, o_ref, l_ref, *, span, tq):
    j = pl.program_id(2)
    q = q_ref[...] * (HD_B ** -0.5)
    kk = jnp.concatenate([kp_ref[...], kc_ref[...]], axis=0)
    vv = jnp.concatenate([vp_ref[...], vc_ref[...]], axis=0)
    qi = lax.broadcasted_iota(jnp.int32, (span, 2 * span), 0)
    ki = lax.broadcasted_iota(jnp.int32, (span, 2 * span), 1)
    dist = qi + span - ki
    band = (dist >= 0) & (dist <= span)
    lane = lax.broadcasted_iota(jnp.int32, (span, LANES), 1)
    for sb in range(tq // span):
        r0 = sb * span
        mask = band & (ki >= jnp.where(j > 0, 0, span)) if sb == 0 else band
        outs = []
        lse_tile = jnp.zeros((span, LANES), F32)
        for h in range(HG):
            hs = slice(h * HD_B, (h + 1) * HD_B)
            s = _dot_nt(q[r0:r0 + span, hs], kk[r0:r0 + 2 * span, hs])
            s = jnp.where(mask, s, NEG)
            m = jnp.max(s, -1, keepdims=True)
            e = jnp.exp(s - m)
            den = jnp.sum(e, -1, keepdims=True)
            outs.append(_dot(e * (1.0 / den), vv[r0:r0 + 2 * span, hs]))
            lse_tile = jnp.where(lane == h, m + jnp.log(den), lse_tile)
        o_ref[r0:r0 + span, :] = jnp.concatenate(outs, axis=1)
        l_ref[r0:r0 + span, :] = lse_tile


def _band_attn(q, k, v, gi, *, batch, seq):
    win, dil = GROUPS[gi]
    span = win // dil
    n = seq // dil
    tq = min(4 * span, n)
    nb = n // tq
    sub = tq // span
    cur = lambda b, r, j: (r, b * nb + j, 0)
    prev = lambda b, r, j: (r, b * nb * sub + jnp.maximum(j * sub - 1, 0), 0)
    return pl.pallas_call(
        functools.partial(_band_attn_kernel, span=span, tq=tq),
        grid=(batch, dil, nb),
        in_specs=[pl.BlockSpec((None, tq, GW), cur),
                  pl.BlockSpec((None, span, GW), prev),
                  pl.BlockSpec((None, tq, GW), cur),
                  pl.BlockSpec((None, span, GW), prev),
                  pl.BlockSpec((None, tq, GW), cur)],
        out_specs=[pl.BlockSpec((None, tq, GW), cur),
                   pl.BlockSpec((None, tq, LANES), cur)],
        out_shape=[jax.ShapeDtypeStruct((dil, batch * n, GW), F32),
                   jax.ShapeDtypeStruct((dil, batch * n, LANES), F32)],
        compiler_params=_cparams("parallel", "parallel", "arbitrary"),
        name=f"band_attn_g{gi}",
    )(q, k, k, v, v)


def _merge_out_kernel(o0_ref, o1_ref, o2_ref, l0_ref, l1_ref, l2_ref, w_ref, r_ref, o_ref,
                      o_scr, l_scr, *, tm):
    for gi, (o_ph, l_ph) in enumerate(((o1_ref, l1_ref), (o2_ref, l2_ref))):
        d = o_ph.shape[0]
        for r in range(d):
            rows = pl.ds(r, tm // d, stride=d)
            l_scr[gi, rows, :] = l_ph[r]
            for c in range(GW // LANES):
                o_scr[gi, c, rows, :] = o_ph[r, :, c * LANES:(c + 1) * LANES]
    ls = [l0_ref[0], l_scr[0], l_scr[1]]

    def o_cols(g, c):
        if g == 0:
            return o0_ref[0, :, c * LANES:(c + 1) * LANES]
        return o_scr[g - 1, c]

    m = jnp.maximum(jnp.maximum(ls[0], ls[1]), ls[2])
    es = [jnp.exp(l - m) for l in ls]
    inv = 1.0 / (es[0] + es[1] + es[2])
    t = ls[0].shape[0]
    lo = lax.broadcasted_iota(jnp.int32, (t, LANES), 1) < HD_B
    cols = []
    for c in range(GW // LANES):
        acc = None
        for g in range(N_GROUPS):
            wt = es[g] * inv
            wexp = jnp.where(lo, wt[:, 2 * c:2 * c + 1], wt[:, 2 * c + 1:2 * c + 2])
            term = wexp * o_cols(g, c)
            acc = term if acc is None else acc + term
        cols.append(acc)
    o = jnp.concatenate(cols, axis=1).astype(BF16)
    o_ref[...] = r_ref[...] + jnp.dot(o, w_ref[...], preferred_element_type=F32)


def _merge_out(outs, lses, w, res, *, tm):
    m, dm = res.shape
    ph_spec = lambda a: pl.BlockSpec((a.shape[0], tm // a.shape[0], a.shape[2]),
                                     lambda i: (0, i, 0))
    return pl.pallas_call(
        functools.partial(_merge_out_kernel, tm=tm),
        grid=(m // tm,),
        in_specs=[ph_spec(a) for a in outs] + [ph_spec(a) for a in lses] +
                 [pl.BlockSpec((GW, dm), lambda i: (0, 0)),
                  pl.BlockSpec((tm, dm), lambda i: (i, 0))],
        out_specs=pl.BlockSpec((tm, dm), lambda i: (i, 0)),
        out_shape=jax.ShapeDtypeStruct((m, dm), F32),
        scratch_shapes=[pltpu.VMEM((N_GROUPS - 1, GW // LANES, tm, LANES), F32),
                        pltpu.VMEM((N_GROUPS - 1, tm, LANES), F32)],
        compiler_params=_cparams("parallel"),
        name="merge_out",
    )(*outs, *lses, w, res)


def _gather_attn_kernel(q_ref, kvn_ref, c0_ref, c1_ref, c2_ref, o_ref):
    caches = [c0_ref, c1_ref, c2_ref]
    outs, lses = [], []
    for g in range(N_GROUPS):
        q = q_ref[g] * (HD_B ** -0.5)
        kn, vn = kvn_ref[0, g], kvn_ref[1, g]
        kc, vc = caches[g][:, 0], caches[g][:, 1]
        s = jnp.sum(kc * q[None], -1, keepdims=True)
        s_new = jnp.sum(kn * q, -1, keepdims=True)
        m = jnp.maximum(jnp.max(s, 0), s_new)
        e = jnp.exp(s - m[None])
        e_new = jnp.exp(s_new - m)
        den = jnp.sum(e, 0) + e_new
        outs.append((jnp.sum(e * vc, 0) + e_new * vn) / den)
        lses.append(m + jnp.log(den))
    m = jnp.maximum(jnp.maximum(lses[0], lses[1]), lses[2])
    es = [jnp.exp(l - m) for l in lses]
    o_ref[...] = (es[0] * outs[0] + es[1] * outs[1] + es[2] * outs[2]) / (es[0] + es[1] + es[2])


def _gather_attn(q, kv_new, caches):
    nb = q.shape[0]
    span = GROUPS[0][0] // GROUPS[0][1]
    views = []
    for (win, dil), c in zip(GROUPS, caches):
        lb = c.shape[1]
        assert lb == win and lb // dil == span, "window buffer must hold the full window"
        views.append(c.reshape(nb, lb // dil, dil, 2, HG, HD_B))
    return pl.pallas_call(
        _gather_attn_kernel,
        grid=(nb,),
        in_specs=[pl.BlockSpec((None, N_GROUPS, HG, HD_B), lambda b: (b, 0, 0, 0)),
                  pl.BlockSpec((None, 2, N_GROUPS, HG, HD_B), lambda b: (b, 0, 0, 0, 0))] +
                 [pl.BlockSpec((None, span, None, 2, HG, HD_B), lambda b: (b, 0, 0, 0, 0, 0))
                  for _ in views],
        out_specs=pl.BlockSpec((None, HG, HD_B), lambda b: (b, 0, 0)),
        out_shape=jax.ShapeDtypeStruct((nb, HG, HD_B), F32),
        compiler_params=_cparams("parallel"),
        name="gather_attn",
    )(q, kv_new, *views)


def _prep_weights(a_w_in, a_A_log, a_dt_bias, a_w_out, w_kv, b_w_q, b_w_out, dense_w_gu,
                  dense_w_down, moe_router, moe_w_gu, moe_w_down, ple_w, ple_gate_w, k_norm,
                  b_q_norm):
    d_model, a_in = a_w_in.shape[1:]
    a_in_pad = -(-a_in // LANES) * LANES
    w = {}
    w['a_w_in'] = jnp.pad(a_w_in[0], ((0, 0), (0, a_in_pad - a_in))).astype(BF16)
    hp = jnp.stack([a_A_log[0], a_dt_bias[0]])
    w['a_hp'] = jnp.pad(hp, ((0, 0), (H_A, LANES - 2 * H_A)))
    w['a_w_out'] = a_w_out[0].astype(BF16)
    w['w_kv'] = w_kv.astype(BF16)
    w['b_w_q'] = b_w_q[0].astype(BF16)
    w['b_w_out'] = b_w_out[0].astype(BF16)
    w['dense_w_gu'] = dense_w_gu.astype(BF16)
    w['dense_w_down'] = dense_w_down.astype(BF16)
    w['router'] = jnp.pad(moe_router[0], ((0, 0), (0, LANES - moe_router.shape[2]))).astype(BF16)
    w['moe_wgu_t'] = jnp.swapaxes(moe_w_gu[0], 1, 2).astype(BF16)
    w['moe_wd_t'] = jnp.swapaxes(moe_w_down[0], 1, 2).astype(BF16)
    w['ple_w'] = ple_w.astype(BF16)
    w['ple_gate_w'] = ple_gate_w.astype(BF16)
    w['k_gain'] = jnp.tile(k_norm, HG).reshape(1, GW)
    w['q_gain'] = jnp.tile(b_q_norm[0], HG).reshape(1, GW)
    return w


def _layer0(x, p0, w, P, *, tm, in_tn, mixer):
    proj = _norm_mm(x, P['a_norm'][0], w['a_w_in'], tm=tm, tn=in_tn)
    og, s_new = mixer(proj)
    h = _mm_res(og, w['a_w_out'], x, tm=tm)
    h = _ffn(h, P['ffn_norm'][0], w['dense_w_gu'][0], w['dense_w_down'][0], tm=tm, tf=512)
    h = _ple(h, P['ple_norm'][0], w['ple_gate_w'][0], p0, w['ple_w'][0], tm=tm)
    return h, proj, s_new


def _layer1_tail(h, p1, w, P, *, tm):
    m = h.shape[0]
    tmr = max(tm, LANES)
    if m % tmr:
        h = jnp.pad(h, ((0, tmr - m % tmr), (0, 0)))
    col, row, seg = _route(h, P['ffn_norm'][1], w['router'], tm=tmr)
    h = _experts(h, P['ffn_norm'][1], col, row, seg, w['moe_wgu_t'], w['moe_wd_t'],
                 tm=tmr, tf=896)[:m]
    return _ple(h, P['ple_norm'][1], w['ple_gate_w'][1], p1, w['ple_w'][1], tm=tm)


def kernel(x_prompt, x_sample, p_prompt, p_sample, state_conv, state_delta, cache_kv_w128, cache_kv_w512, cache_kv_w2048, a_norm, a_w_in, a_conv_w, a_A_log, a_dt_bias, a_out_norm, a_w_out, kv_norm, w_kv, k_norm, b_norm, b_w_q, b_q_norm, b_w_out, ffn_norm, dense_w_gu, dense_w_down, moe_router, moe_w_gu, moe_w_down, ple_w, ple_norm, ple_gate_w):
    assert a_w_in.shape[0] == 1 and b_w_q.shape[0] == 1, "one mixer of each kind"
    bp, sp, dm = x_prompt.shape
    bs, ls, _ = x_sample.shape
    assert ls == 1, "sample group decodes one token per sequence"
    qkv_w = 3 * H_A * DK_A
    P = dict(a_norm=a_norm, ffn_norm=ffn_norm, ple_norm=ple_norm)
    w = _prep_weights(a_w_in, a_A_log, a_dt_bias, a_w_out, w_kv, b_w_q, b_w_out, dense_w_gu,
                      dense_w_down, moe_router, moe_w_gu, moe_w_down, ple_w, ple_gate_w, k_norm,
                      b_q_norm)
    a_in_pad = w['a_w_in'].shape[1]
    in_tn = a_in_pad // 3 if a_in_pad % (3 * LANES) == 0 else LANES
    conv_w = a_conv_w[0]
    out_norm = a_out_norm[0].reshape(1, DK_A)

    mp = bp * sp
    tm = min(1024, sp)
    xp = x_prompt.reshape(mp, dm)
    conv0 = jnp.zeros((bp, CONV_W - 1, qkv_w), F32)
    s0 = jnp.zeros((bp, H_A, DK_A, DK_A), F32)
    h, proj, delta_p = _layer0(
        xp, p_prompt[0].reshape(mp, -1), w, P, tm=tm, in_tn=in_tn,
        mixer=lambda pr: _gdn_prompt(pr, conv_w, w['a_hp'], out_norm, conv0, s0, batch=bp, seq=sp))
    conv_p = proj.reshape(bp, sp, -1)[:, sp - (CONV_W - 1):, :qkv_w][None]

    cos, sin = _rope_tables(jnp.arange(sp, dtype=jnp.int32))
    dils = [d for _, d in GROUPS]
    kv, *kv_ph = _proj_rope(h, kv_norm, w['w_kv'], w['k_gain'], cos, sin, tm=tm, n_rope=N_GROUPS,
                            natural=True, dils=dils + dils)
    q_ph = _proj_rope(h, b_norm[0], w['b_w_q'], w['q_gain'], cos, sin, tm=tm, n_rope=N_GROUPS,
                      natural=False, dils=dils)
    outs, lses = [], []
    for gi in range(N_GROUPS):
        o, lse = _band_attn(q_ph[gi], kv_ph[gi], kv_ph[N_GROUPS + gi], gi, batch=bp, seq=sp)
        outs.append(o)
        lses.append(lse)
    h = _merge_out(outs, lses, w['b_w_out'], h, tm=tm)
    y_prompt = _layer1_tail(h, p_prompt[1].reshape(mp, -1), w, P, tm=tm).reshape(bp, sp, dm)
    kv3 = kv.reshape(bp, sp, 2 * N_GROUPS * GW)
    kv_p = []
    for gi, (win, _) in enumerate(GROUPS):
        rows = kv3[:, sp - min(win, sp):]
        k_g = rows[:, :, gi * GW:(gi + 1) * GW]
        v_g = rows[:, :, (N_GROUPS + gi) * GW:(N_GROUPS + gi + 1) * GW]
        kv_p.append(jnp.stack([k_g, v_g], axis=2).reshape(bp, -1, 2, HG, HD_B))

    xs = x_sample.reshape(bs, dm)
    hs, proj_s, delta_s = _layer0(
        xs, p_sample[0].reshape(bs, -1), w, P, tm=bs, in_tn=in_tn,
        mixer=lambda pr: _gdn_step(pr, state_conv[0], conv_w, w['a_hp'], out_norm, state_delta[0]))
    conv_s = jnp.concatenate([state_conv[0][:, 1:], proj_s[:, None, :qkv_w]], axis=1)[None]
    cos_s, sin_s = _rope_tables(jnp.full((bs,), PAST_LEN, jnp.int32))
    kv_s, = _proj_rope(hs, kv_norm, w['w_kv'], w['k_gain'], cos_s, sin_s, tm=bs, n_rope=N_GROUPS,
                       natural=True, dils=())
    q_s, = _proj_rope(hs, b_norm[0], w['b_w_q'], w['q_gain'], cos_s, sin_s, tm=bs, n_rope=N_GROUPS,
                      natural=True, dils=())
    o_s = _gather_attn(q_s.reshape(bs, N_GROUPS, HG, HD_B),
                       kv_s.reshape(bs, 2, N_GROUPS, HG, HD_B),
                       (cache_kv_w128, cache_kv_w512, cache_kv_w2048))
    hs = _mm_res(o_s.reshape(bs, GW), w['b_w_out'], hs, tm=bs)
    y_sample = _layer1_tail(hs, p_sample[1].reshape(bs, -1), w, P, tm=bs).reshape(bs, 1, dm)
    kvs5 = kv_s.reshape(bs, 1, 2, N_GROUPS, HG, HD_B)
    kv_sn = [kvs5[:, :, :, gi] for gi in range(N_GROUPS)]

    return (y_prompt, y_sample, conv_p, conv_s, delta_p[None], delta_s[None],
            kv_p[0], kv_sn[0], kv_p[1], kv_sn[1], kv_p[2], kv_sn[2])
```

```python
import functools

import jax
import jax.numpy as jnp
from jax import lax
from jax.experimental import pallas as pl
from jax.experimental.pallas import tpu as pltpu

F32 = jnp.float32
BF16 = jnp.bfloat16

EPS = 1e-6
PAST_LEN = 16384
GROUPS = ((128, 1), (512, 4), (2048, 16))
N_GROUPS = len(GROUPS)
HG = 8
HD_B = 64
ROT_DIM = HD_B // 4
ROPE_THETA = 500000.0
GW = HG * HD_B
H_A = 8
DK_A = 128
CONV_W = 4
CHUNK = 64
LANES = 128
VMEM_LIMIT = 52 * 1024 * 1024
NEG = -1e30


def _cparams(*sem):
    return pltpu.CompilerParams(dimension_semantics=sem, vmem_limit_bytes=VMEM_LIMIT)


def _rms_rows(x, gain):
    return x * lax.rsqrt(jnp.mean(x * x, -1, keepdims=True) + EPS) * gain


def _silu(x):
    return x * (1.0 / (1.0 + jnp.exp(-x)))


def _sigmoid(x):
    return 1.0 / (1.0 + jnp.exp(-x))


def _dot(a, b):
    return jnp.dot(a.astype(BF16), b.astype(BF16), preferred_element_type=F32)


def _wdot(a, w):
    dot = functools.partial(jnp.dot, preferred_element_type=F32)
    if w.dtype == BF16:
        return dot(a.astype(BF16), w)
    a = a.astype(F32)
    a_hi = a.astype(BF16)
    a_lo = (a - a_hi.astype(F32)).astype(BF16)
    w_hi = w.astype(BF16)
    w_lo = (w - w_hi.astype(F32)).astype(BF16)
    return dot(a_hi, w_hi) + (dot(a_lo, w_hi) + dot(a_hi, w_lo))


def _act_dtype(w):
    return BF16 if w.dtype == BF16 else F32


def _dot_nt(a, b):
    return lax.dot_general(a.astype(BF16), b.astype(BF16), (((1,), (1,)), ((), ())),
                           preferred_element_type=F32)


def _dot_tn(a, b):
    return lax.dot_general(a.astype(BF16), b.astype(BF16), (((0,), (0,)), ((), ())),
                           preferred_element_type=F32)


def _head_norm_rope(x, hgain, cos, sin):
    t = x.shape[0]
    lane = lax.broadcasted_iota(jnp.int32, (t, LANES), 1)
    lo = lane < HD_B
    d = lane & (HD_B - 1)
    outs = []
    for c in range(GW // LANES):
        sl = slice(c * LANES, (c + 1) * LANES)
        xb = x[:, sl]
        sq = xb * xb
        s_lo = jnp.sum(jnp.where(lo, sq, 0.0), -1, keepdims=True)
        s_hi = jnp.sum(jnp.where(lo, 0.0, sq), -1, keepdims=True)
        scale = jnp.where(lo, lax.rsqrt(s_lo * (1.0 / HD_B) + EPS),
                          lax.rsqrt(s_hi * (1.0 / HD_B) + EPS))
        yb = xb * scale * hgain[:, sl]
        half = ROT_DIM // 2
        rot = jnp.where(d < half, pltpu.roll(yb, LANES - half, 1), pltpu.roll(yb, half, 1))
        outs.append(yb * cos[:, sl] + rot * sin[:, sl])
    return jnp.concatenate(outs, axis=1)


def _norm_mm_kernel(x_ref, g_ref, w_ref, o_ref, u_ref):
    @pl.when(pl.program_id(1) == 0)
    def _():
        u_ref[...] = _rms_rows(x_ref[...], g_ref[...]).astype(u_ref.dtype)

    o_ref[...] = _wdot(u_ref[...], w_ref[...])


def _norm_mm(x, gain, w, *, tm, tn):
    m, k = x.shape
    n = w.shape[1]
    return pl.pallas_call(
        _norm_mm_kernel,
        grid=(m // tm, n // tn),
        in_specs=[pl.BlockSpec((tm, k), lambda i, j: (i, 0)),
                  pl.BlockSpec((1, k), lambda i, j: (0, 0)),
                  pl.BlockSpec((k, tn), lambda i, j: (0, j))],
        out_specs=pl.BlockSpec((tm, tn), lambda i, j: (i, j)),
        out_shape=jax.ShapeDtypeStruct((m, n), F32),
        scratch_shapes=[pltpu.VMEM((tm, k), _act_dtype(w))],
        compiler_params=_cparams("parallel", "arbitrary"),
        name="norm_mm",
    )(x, gain.reshape(1, k), w)


def _proj_rope_kernel(x_ref, g_ref, w_ref, hg_ref, cos_ref, sin_ref, *rest,
                      n_rope, natural, dils, tm):
    n_out = int(natural) + len(dils)
    outs, (slab_ref,) = rest[:n_out], rest[n_out:]
    nat_ref = outs[0] if natural else None
    ph_refs = outs[int(natural):]
    u = _rms_rows(x_ref[...], g_ref[...]).astype(_act_dtype(w_ref))
    slab = 0
    for jj in range(w_ref.shape[1] // GW):
        cols = slice(jj * GW, (jj + 1) * GW)
        y = _wdot(u, w_ref[:, cols])
        if jj < n_rope:
            y = _head_norm_rope(y, hg_ref[...], cos_ref[...], sin_ref[...])
        if natural:
            nat_ref[:, cols] = y
        if jj >= len(dils):
            continue
        d = dils[jj]
        if d == 1:
            ph_refs[jj][0] = y.astype(BF16)
            continue
        for c in range(GW // LANES):
            slab_ref[slab, c] = y[:, c * LANES:(c + 1) * LANES]
        for r in range(d):
            for c in range(GW // LANES):
                ph_refs[jj][r, :, c * LANES:(c + 1) * LANES] = (
                    slab_ref[slab, c, pl.ds(r, tm // d, stride=d), :].astype(BF16))
        slab += 1


def _proj_rope(x, gain, w, hgain, cos, sin, *, tm, n_rope, natural, dils):
    m, k = x.shape
    n = w.shape[1]
    pos_blocks = cos.shape[0] // tm
    out_specs, out_shape = [], []
    if natural:
        out_specs.append(pl.BlockSpec((tm, n), lambda i: (i, 0)))
        out_shape.append(jax.ShapeDtypeStruct((m, n), F32))
    for d in dils:
        out_specs.append(pl.BlockSpec((d, tm // d, GW), lambda i: (0, i, 0)))
        out_shape.append(jax.ShapeDtypeStruct((d, m // d, GW), BF16))
    n_slabs = max(1, sum(d > 1 for d in dils))
    return pl.pallas_call(
        functools.partial(_proj_rope_kernel, n_rope=n_rope, natural=natural, dils=tuple(dils),
                          tm=tm),
        grid=(m // tm,),
        in_specs=[pl.BlockSpec((tm, k), lambda i: (i, 0)),
                  pl.BlockSpec((1, k), lambda i: (0, 0)),
                  pl.BlockSpec((k, n), lambda i: (0, 0)),
                  pl.BlockSpec((1, GW), lambda i: (0, 0)),
                  pl.BlockSpec((tm, GW), lambda i: (i % pos_blocks, 0)),
                  pl.BlockSpec((tm, GW), lambda i: (i % pos_blocks, 0))],
        out_specs=out_specs,
        out_shape=out_shape,
        scratch_shapes=[pltpu.VMEM((n_slabs, GW // LANES, tm, LANES), F32)],
        compiler_params=_cparams("parallel"),
        name="proj_rope",
    )(x, gain.reshape(1, k), w, hgain, cos, sin)


def _rope_tables(pos):
    half = ROT_DIM // 2
    inv = ROPE_THETA ** (-jnp.arange(half, dtype=F32) * 2.0 / ROT_DIM)
    ang = pos.astype(F32)[:, None] * inv[None]
    c, s = jnp.cos(ang), jnp.sin(ang)
    n = pos.shape[0]
    cos_h = jnp.concatenate([c, c, jnp.ones((n, HD_B - ROT_DIM), F32)], 1)
    sin_h = jnp.concatenate([-s, s, jnp.zeros((n, HD_B - ROT_DIM), F32)], 1)
    return jnp.tile(cos_h, (1, HG)), jnp.tile(sin_h, (1, HG))


def _mm_res_kernel(x_ref, w_ref, r_ref, o_ref):
    o_ref[...] = r_ref[...] + _wdot(x_ref[...], w_ref[...])


def _mm_res(x, w, res, *, tm):
    m, k = x.shape
    n = w.shape[1]
    return pl.pallas_call(
        _mm_res_kernel,
        grid=(m // tm,),
        in_specs=[pl.BlockSpec((tm, k), lambda i: (i, 0)),
                  pl.BlockSpec((k, n), lambda i: (0, 0)),
                  pl.BlockSpec((tm, n), lambda i: (i, 0))],
        out_specs=pl.BlockSpec((tm, n), lambda i: (i, 0)),
        out_shape=jax.ShapeDtypeStruct((m, n), F32),
        compiler_params=_cparams("parallel"),
        name="mm_res",
    )(x, w, res)


def _unit_lower_inverses(mats):
    c = mats[0].shape[0]
    row = lax.broadcasted_iota(jnp.int32, (c, c), 0)
    col = lax.broadcasted_iota(jnp.int32, (c, c), 1)
    eye = jnp.where(row == col, 1.0, 0.0).astype(F32)
    ts = None
    b = 1
    while b < c:
        sel = ((row ^ col) < 2 * b) & ((row & b) != 0) & ((col & b) == 0)
        lows = [jnp.where(sel, a, 0.0) for a in mats]
        if ts is None:
            ts = [eye - low for low in lows]
        else:
            tl = [_dot(t, low) for t, low in zip(ts, lows)]
            ts = [t - _dot(x, t) for t, x in zip(ts, tl)]
        b *= 2
    return ts


def _gdn_head_params(ba, hp):
    beta = _sigmoid(ba)
    x = ba + hp[1:2, :]
    softplus = jnp.maximum(x, 0.0) + jnp.log(1.0 + jnp.exp(-jnp.abs(x)))
    g = -jnp.exp(hp[0:1, :]) * softplus
    return beta, g


def _gdn_kernel(qkv_ref, z_ref, ba_ref, cw_ref, hp_ref, on_ref, conv0_ref, s0_ref,
                og_ref, sout_ref, xbuf, s_scr, *, C, nch):
    n = pl.program_id(1)
    R = nch * C
    pad = 8

    @pl.when(n == 0)
    def _():
        xbuf[pad - (CONV_W - 1):pad, :] = conv0_ref[...]
        s_scr[...] = s0_ref[...]

    xbuf[pad:pad + R, :] = qkv_ref[...]

    def conv_cols(c0):
        acc = None
        for j in range(CONV_W):
            r0 = pad - (CONV_W - 1) + j
            term = xbuf[r0:r0 + R, c0:c0 + DK_A] * cw_ref[j:j + 1, c0:c0 + DK_A]
            acc = term if acc is None else acc + term
        return _silu(acc)

    beta, g = _gdn_head_params(ba_ref[...], hp_ref[...])
    rr = lax.broadcasted_iota(jnp.int32, (R, R), 0)
    rc = lax.broadcasted_iota(jnp.int32, (R, R), 1)
    blocktri = ((rr >= rc) & ((rr ^ rc) < C)).astype(F32)
    gcum = jnp.dot(blocktri, g, preferred_element_type=F32, precision=lax.Precision.HIGHEST)
    gcum_t = gcum.T
    row = lax.broadcasted_iota(jnp.int32, (C, C), 0)
    col = lax.broadcasted_iota(jnp.int32, (C, C), 1)
    incl = row >= col
    strict = row > col

    units = [(c, h) for c in range(nch) for h in range(H_A)]
    qs, ks, vs = {}, {}, {}
    for h in range(H_A):
        q = conv_cols(h * DK_A)
        k = conv_cols((H_A + h) * DK_A)
        v = conv_cols((2 * H_A + h) * DK_A)
        q = q * lax.rsqrt(jnp.sum(q * q, -1, keepdims=True) + EPS) * (DK_A ** -0.5)
        k = k * lax.rsqrt(jnp.sum(k * k, -1, keepdims=True) + EPS)
        for c in range(nch):
            rs = slice(c * C, (c + 1) * C)
            qs[c, h], ks[c, h], vs[c, h] = q[rs], k[rs], v[rs]

    bcs, gcs, decays, kbs = {}, {}, {}, {}
    for c, h in units:
        rs = slice(c * C, (c + 1) * C)
        bcs[c, h] = beta[rs, h:h + 1]
        gcs[c, h] = gcum[rs, H_A + h:H_A + h + 1]
        gr = gcum_t[H_A + h:H_A + h + 1, rs]
        decays[c, h] = jnp.exp(jnp.where(incl, gcs[c, h] - gr, NEG))
        kbs[c, h] = ks[c, h] * bcs[c, h]
    grams = {u: _dot_nt(jnp.concatenate([kbs[u], qs[u]], axis=0), ks[u]) for u in units}
    a_mats = [jnp.where(strict, grams[u][:C] * decays[u], 0.0) for u in units]
    aqks = {u: grams[u][C:] * decays[u] for u in units}
    t_mats = dict(zip(units, _unit_lower_inverses(a_mats)))
    egs = {u: jnp.exp(gcs[u]) for u in units}
    sols = {u: _dot(t_mats[u], jnp.concatenate([vs[u] * bcs[u], kbs[u] * egs[u]], axis=1))
            for u in units}

    states = [s_scr[h] for h in range(H_A)]
    for c in range(nch):
        rs = slice(c * C, (c + 1) * C)
        for h in range(H_A):
            u = (c, h)
            g_last = gcs[u][C - 1:C, :]
            ws = _dot(jnp.concatenate([sols[u][:, DK_A:], qs[u] * egs[u]], axis=0), states[h])
            v_new = sols[u][:, :DK_A] - ws[:C]
            o = ws[C:] + _dot(aqks[u], v_new)
            kd = ks[u] * jnp.exp(g_last - gcs[u])
            states[h] = states[h] * jnp.exp(g_last) + _dot_tn(kd, v_new)
            o = _rms_rows(o, on_ref[...]) * _silu(z_ref[rs, h * DK_A:(h + 1) * DK_A])
            og_ref[rs, h * DK_A:(h + 1) * DK_A] = o
    for h in range(H_A):
        s_scr[h] = states[h]

    xbuf[pad - (CONV_W - 1):pad, :] = xbuf[pad + R - (CONV_W - 1):pad + R, :]

    @pl.when(n == pl.num_programs(1) - 1)
    def _():
        sout_ref[...] = s_scr[...]


def _gdn_prompt(proj, conv_w, hp, out_norm, conv0, s0, *, batch, seq):
    C = min(CHUNK, seq)
    nch = 2 if seq % (2 * C) == 0 else 1
    R = nch * C
    nc = seq // R
    qkv_w = 3 * H_A * DK_A
    z_w = H_A * DK_A
    return pl.pallas_call(
        functools.partial(_gdn_kernel, C=C, nch=nch),
        grid=(batch, nc),
        in_specs=[pl.BlockSpec((R, qkv_w), lambda b, n: (b * nc + n, 0)),
                  pl.BlockSpec((R, z_w), lambda b, n: (b * nc + n, qkv_w // z_w)),
                  pl.BlockSpec((R, LANES), lambda b, n: (b * nc + n, (qkv_w + z_w) // LANES)),
                  pl.BlockSpec((CONV_W, qkv_w), lambda b, n: (0, 0)),
                  pl.BlockSpec((2, LANES), lambda b, n: (0, 0)),
                  pl.BlockSpec((1, DK_A), lambda b, n: (0, 0)),
                  pl.BlockSpec((None, CONV_W - 1, qkv_w), lambda b, n: (b, 0, 0)),
                  pl.BlockSpec((None, H_A, DK_A, DK_A), lambda b, n: (b, 0, 0, 0))],
        out_specs=[pl.BlockSpec((R, z_w), lambda b, n: (b * nc + n, 0)),
                   pl.BlockSpec((None, H_A, DK_A, DK_A), lambda b, n: (b, 0, 0, 0))],
        out_shape=[jax.ShapeDtypeStruct((batch * seq, z_w), F32),
                   jax.ShapeDtypeStruct((batch, H_A, DK_A, DK_A), F32)],
        scratch_shapes=[pltpu.VMEM((R + 8, qkv_w), F32),
                        pltpu.VMEM((H_A, DK_A, DK_A), F32)],
        compiler_params=_cparams("parallel", "arbitrary"),
        name="gdn_chunked",
    )(proj, proj, proj, conv_w, hp, out_norm, conv0, s0)


def _gdn_step_kernel(proj_ref, conv_ref, cw_ref, hp_ref, on_ref, s0_ref, og_ref, sout_ref, qk_scr):
    qkv_w = 3 * H_A * DK_A
    z_w = H_A * DK_A

    def conv_cols(c0):
        sl = slice(c0, c0 + DK_A)
        acc = proj_ref[:, sl] * cw_ref[CONV_W - 1:CONV_W, sl]
        for j in range(CONV_W - 1):
            acc = acc + conv_ref[j:j + 1, sl] * cw_ref[j:j + 1, sl]
        return _silu(acc)

    beta, g = _gdn_head_params(proj_ref[:, qkv_w + z_w:qkv_w + z_w + LANES], hp_ref[...])
    qk_scr[...] = jnp.zeros_like(qk_scr)
    vs = []
    for h in range(H_A):
        q = conv_cols(h * DK_A)
        k = conv_cols((H_A + h) * DK_A)
        vs.append(conv_cols((2 * H_A + h) * DK_A))
        qk_scr[H_A + h:H_A + h + 1, :] = (
            q * lax.rsqrt(jnp.sum(q * q, -1, keepdims=True) + EPS) * (DK_A ** -0.5))
        qk_scr[h:h + 1, :] = k * lax.rsqrt(jnp.sum(k * k, -1, keepdims=True) + EPS)
    qk = qk_scr[...]
    qk_t = qk.T
    for h in range(H_A):
        k_row = qk[h:h + 1, :]
        q_row = qk[H_A + h:H_A + h + 1, :]
        k_col = qk_t[:, h:h + 1]
        q_col = qk_t[:, H_A + h:H_A + h + 1]
        bh = beta[:, h:h + 1]
        eg = jnp.exp(g[:, H_A + h:H_A + h + 1])
        s = s0_ref[h]
        k_s = jnp.sum(s * k_col, 0, keepdims=True)
        q_s = jnp.sum(s * q_col, 0, keepdims=True)
        v_new = bh * (vs[h] - eg * k_s)
        o = eg * q_s + jnp.sum(q_row * k_row, -1, keepdims=True) * v_new
        sout_ref[h] = s * eg + k_col * v_new
        o = _rms_rows(o, on_ref[...]) * _silu(proj_ref[:, qkv_w + h * DK_A:qkv_w + (h + 1) * DK_A])
        og_ref[:, h * DK_A:(h + 1) * DK_A] = o


def _gdn_step(proj, conv_state, conv_w, hp, out_norm, s0):
    nb, pw = proj.shape
    qkv_w = 3 * H_A * DK_A
    z_w = H_A * DK_A
    og, s_new = pl.pallas_call(
        _gdn_step_kernel,
        grid=(nb,),
        in_specs=[pl.BlockSpec((None, 1, pw), lambda b: (b, 0, 0)),
                  pl.BlockSpec((None, CONV_W - 1, qkv_w), lambda b: (b, 0, 0)),
                  pl.BlockSpec((CONV_W, qkv_w), lambda b: (0, 0)),
                  pl.BlockSpec((2, LANES), lambda b: (0, 0)),
                  pl.BlockSpec((1, DK_A), lambda b: (0, 0)),
                  pl.BlockSpec((None, H_A, DK_A, DK_A), lambda b: (b, 0, 0, 0))],
        out_specs=[pl.BlockSpec((None, 1, z_w), lambda b: (b, 0, 0)),
                   pl.BlockSpec((None, H_A, DK_A, DK_A), lambda b: (b, 0, 0, 0))],
        out_shape=[jax.ShapeDtypeStruct((nb, 1, z_w), F32),
                   jax.ShapeDtypeStruct((nb, H_A, DK_A, DK_A), F32)],
        scratch_shapes=[pltpu.VMEM((LANES, DK_A), F32)],
        compiler_params=_cparams("parallel"),
        name="gdn_step",
    )(proj.reshape(nb, 1, pw), conv_state, conv_w, hp, out_norm, s0)
    return og.reshape(nb, z_w), s_new


def _ffn_kernel(h_ref, g_ref, wg_ref, wu_ref, wd_ref, o_ref, u_ref, acc_ref):
    f = pl.program_id(1)

    @pl.when(f == 0)
    def _():
        u_ref[...] = _rms_rows(h_ref[...], g_ref[...]).astype(u_ref.dtype)
        acc_ref[...] = jnp.zeros_like(acc_ref)

    u = u_ref[...]
    gate = _wdot(u, wg_ref[...])
    up = _wdot(u, wu_ref[...])
    acc_ref[...] += _wdot(_silu(gate) * up, wd_ref[...])

    @pl.when(f == pl.num_programs(1) - 1)
    def _():
        o_ref[...] = h_ref[...] + acc_ref[...]


def _ffn(h, gain, w_gu, w_down, *, tm, tf):
    m, dm = h.shape
    ff = w_down.shape[0]
    nf = ff // tf
    return pl.pallas_call(
        _ffn_kernel,
        grid=(m // tm, nf),
        in_specs=[pl.BlockSpec((tm, dm), lambda i, f: (i, 0)),
                  pl.BlockSpec((1, dm), lambda i, f: (0, 0)),
                  pl.BlockSpec((dm, tf), lambda i, f: (0, f)),
                  pl.BlockSpec((dm, tf), lambda i, f: (0, nf + f)),
                  pl.BlockSpec((tf, dm), lambda i, f: (f, 0))],
        out_specs=pl.BlockSpec((tm, dm), lambda i, f: (i, 0)),
        out_shape=jax.ShapeDtypeStruct((m, dm), F32),
        scratch_shapes=[pltpu.VMEM((tm, dm), _act_dtype(w_gu)),
                        pltpu.VMEM((tm, dm), F32)],
        compiler_params=_cparams("parallel", "arbitrary"),
        name="ffn_dense",
    )(h, gain.reshape(1, dm), w_gu, w_gu, w_down)


N_EXPERTS = 8
SEG_ALIGN = LANES


def _top2(logits):
    t = logits.shape[0]
    lane = lax.broadcasted_iota(jnp.int32, (t, LANES), 1)
    valid = lane < N_EXPERTS
    lg = jnp.where(valid, logits, NEG)
    mx = jnp.max(lg, -1, keepdims=True)
    e = jnp.where(valid, jnp.exp(lg - mx), 0.0)
    probs = e / jnp.sum(e, -1, keepdims=True)
    p1 = jnp.max(probs, -1, keepdims=True)
    i1 = jnp.min(jnp.where((probs == p1) & valid, lane, LANES), -1, keepdims=True)
    rest = jnp.where((lane == i1) | ~valid, -1.0, probs)
    p2 = jnp.max(rest, -1, keepdims=True)
    i2 = jnp.min(jnp.where(rest == p2, lane, LANES), -1, keepdims=True)
    tot = p1 + p2
    return i1, i2, p1 / tot, p2 / tot


def _router_kernel(h_ref, g_ref, r_ref, tri_ref, upper_ref, col_ref, row_ref, seg_ref):
    t = h_ref.shape[0]
    logits = _wdot(_rms_rows(h_ref[...], g_ref[...]), r_ref[...])
    i1, i2, g1, g2 = _top2(logits)
    lane = lax.broadcasted_iota(jnp.int32, (t, LANES), 1)
    sel = jnp.where((lane == i1) | (lane == i2), 1.0, 0.0)
    pos = jnp.dot(tri_ref[...], sel.astype(BF16), preferred_element_type=F32)
    counts = jnp.sum(sel, 0, keepdims=True)
    nblk = jnp.floor((counts + (SEG_ALIGN - 1)) * (1.0 / SEG_ALIGN))
    nblk8 = jnp.broadcast_to(nblk, (8, LANES))
    start8 = jnp.dot(nblk8.astype(BF16), upper_ref[...], preferred_element_type=F32)
    dest = start8[0:1] * SEG_ALIGN + pos
    d1 = jnp.sum(jnp.where(lane == i1, dest, 0.0), -1, keepdims=True)
    d2 = jnp.sum(jnp.where(lane == i2, dest, 0.0), -1, keepdims=True)
    col = jnp.where(lane == 0, d1, jnp.where(lane == 1, d2,
                    jnp.where(lane == 2, g1, jnp.where(lane == 3, g2, 0.0))))
    col_ref[...] = col
    row_ref[...] = col.T[0:8, :]
    lane8 = lane[0:8]
    seg = jnp.where(lane8 < N_EXPERTS, start8,
                    jnp.where(lane8 < 2 * N_EXPERTS, pltpu.roll(nblk8, N_EXPERTS, 1), 0.0))
    seg_ref[...] = seg[0:1].astype(jnp.int32)


def _route(h, gain, router, *, tm):
    m, dm = h.shape
    nt = m // tm
    tri = jnp.tril(jnp.ones((tm, tm), F32), -1).astype(BF16)
    upper = jnp.triu(jnp.ones((LANES, LANES), F32), 1).astype(BF16)
    return pl.pallas_call(
        _router_kernel,
        grid=(nt,),
        in_specs=[pl.BlockSpec((tm, dm), lambda i: (i, 0)),
                  pl.BlockSpec((1, dm), lambda i: (0, 0)),
                  pl.BlockSpec((dm, LANES), lambda i: (0, 0)),
                  pl.BlockSpec((tm, tm), lambda i: (0, 0)),
                  pl.BlockSpec((LANES, LANES), lambda i: (0, 0))],
        out_specs=[pl.BlockSpec((tm, LANES), lambda i: (i, 0)),
                   pl.BlockSpec((None, 8, tm), lambda i: (i, 0, 0)),
                   pl.BlockSpec((None, 1, LANES), lambda i: (i, 0, 0))],
        out_shape=[jax.ShapeDtypeStruct((m, LANES), F32),
                   jax.ShapeDtypeStruct((nt, 8, tm), F32),
                   jax.ShapeDtypeStruct((nt, 1, LANES), jnp.int32)],
        compiler_params=_cparams("parallel"),
        name="moe_route",
    )(h, gain.reshape(1, dm), router, tri, upper)


def _one_hot_rows(row0, n, d1_row, d2_row):
    ridx = (lax.broadcasted_iota(jnp.int32, (n, d1_row.shape[1]), 0) + row0).astype(F32)
    return jnp.where((ridx == d1_row) | (ridx == d2_row), 1.0, 0.0).astype(BF16)


def _experts_kernel(seg_ref, h_ref, g_ref, col_ref, row_ref, wg_ref, wu_ref, wd_ref, o_ref,
                    xs_scr, acc_scr, y_scr, *, rows_max, gch, kch):
    i, e, f = pl.program_id(0), pl.program_id(1), pl.program_id(2)
    t = h_ref.shape[0]

    @pl.when((e == 0) & (f == 0))
    def _():
        u = _rms_rows(h_ref[...], g_ref[...]).astype(BF16)
        d1_row, d2_row = row_ref[0:1, :], row_ref[1:2, :]
        for c in range(rows_max // gch):
            p = _one_hot_rows(c * gch, gch, d1_row, d2_row)
            xs_scr[c * gch:(c + 1) * gch, :] = jnp.dot(
                p, u, preferred_element_type=F32).astype(BF16)
        acc_scr[...] = jnp.zeros_like(acc_scr)

    start = seg_ref[i * LANES + e]
    nblk = seg_ref[i * LANES + N_EXPERTS + e]

    def run_blocks(b0, nb):
        x = xs_scr[pl.ds(pl.multiple_of(b0 * LANES, LANES), nb * LANES), :]
        gate_t = _dot_nt(wg_ref[...], x)
        up_t = _dot_nt(wu_ref[...], x)
        act_t = (_silu(gate_t) * up_t).astype(BF16)
        down_t = jnp.dot(wd_ref[...], act_t, preferred_element_type=F32)
        for k in range(nb):
            acc_scr[b0 + k] += down_t[:, k * LANES:(k + 1) * LANES]

    n4 = nblk // 4

    def body(j, carry):
        run_blocks(start + 4 * j, 4)
        return carry

    lax.fori_loop(0, n4, body, 0)
    rem = nblk - 4 * n4
    tail = start + 4 * n4

    @pl.when((rem & 2) != 0)
    def _():
        run_blocks(tail, 2)

    @pl.when((rem & 1) != 0)
    def _():
        run_blocks(tail + (rem & 2), 1)

    @pl.when((e == pl.num_programs(1) - 1) & (f == pl.num_programs(2) - 1))
    def _():
        col = col_ref[...]
        d1_col, d2_col, g1_col, g2_col = col[:, 0:1], col[:, 1:2], col[:, 2:3], col[:, 3:4]
        d1_row, d2_row = row_ref[0:1, :], row_ref[1:2, :]
        lane = lax.broadcasted_iota(jnp.int32, (t, LANES), 1)
        for c in range(rows_max // kch):
            parts = []
            for k in range(kch // LANES):
                b = c * (kch // LANES) + k
                li = (lane + b * LANES).astype(F32)
                gs = jnp.sum(jnp.where(li == d1_col, g1_col, 0.0) +
                             jnp.where(li == d2_col, g2_col, 0.0), 0, keepdims=True)
                parts.append((acc_scr[b] * gs).astype(BF16))
            contrib = jnp.dot(jnp.concatenate(parts, axis=1),
                              _one_hot_rows(c * kch, kch, d1_row, d2_row),
                              preferred_element_type=F32)
            if c == 0:
                y_scr[...] = contrib
            else:
                y_scr[...] += contrib
        o_ref[...] = h_ref[...] + y_scr[...].T


def _experts(h, gain, col, row, seg, wgu_t, wd_t, *, tm, tf):
    m, dm = h.shape
    ne, _, ff = wd_t.shape
    nf = ff // tf
    nt = m // tm
    rows_max = 2 * tm + ne * SEG_ALIGN
    gch = 1024 if rows_max % 1024 == 0 else 256
    kch = 512 if rows_max % 512 == 0 else 256
    grid_spec = pltpu.PrefetchScalarGridSpec(
        num_scalar_prefetch=1,
        grid=(nt, ne, nf),
        in_specs=[pl.BlockSpec((tm, dm), lambda i, e, f, s: (i, 0),
                               pipeline_mode=pl.Buffered(1)),
                  pl.BlockSpec((1, dm), lambda i, e, f, s: (0, 0)),
                  pl.BlockSpec((tm, LANES), lambda i, e, f, s: (i, 0),
                               pipeline_mode=pl.Buffered(1)),
                  pl.BlockSpec((None, 8, tm), lambda i, e, f, s: (i, 0, 0)),
                  pl.BlockSpec((None, tf, dm), lambda i, e, f, s: (e, f, 0)),
                  pl.BlockSpec((None, tf, dm), lambda i, e, f, s: (e, nf + f, 0)),
                  pl.BlockSpec((None, dm, tf), lambda i, e, f, s: (e, 0, f))],
        out_specs=pl.BlockSpec((tm, dm), lambda i, e, f, s: (i, 0)),
        scratch_shapes=[pltpu.VMEM((rows_max, dm), BF16),
                        pltpu.VMEM((rows_max // LANES, dm, LANES), F32),
                        pltpu.VMEM((dm, tm), F32)])
    return pl.pallas_call(
        functools.partial(_experts_kernel, rows_max=rows_max, gch=gch, kch=kch),
        grid_spec=grid_spec,
        out_shape=jax.ShapeDtypeStruct((m, dm), F32),
        compiler_params=_cparams("parallel", "arbitrary", "arbitrary"),
        name="moe_experts",
    )(seg.reshape(-1), h, gain.reshape(1, dm), col, row, wgu_t, wgu_t, wd_t)


def _ple_kernel(h_ref, g_ref, gw_ref, p_ref, pw_ref, o_ref):
    h = h_ref[...]
    gate = _sigmoid(_wdot(_rms_rows(h, g_ref[...]), gw_ref[...]))
    o_ref[...] = h + _wdot(p_ref[...], pw_ref[...]) * gate


def _ple(h, gain, gate_w, p, ple_w, *, tm):
    m, dm = h.shape
    pd = p.shape[1]
    return pl.pallas_call(
        _ple_kernel,
        grid=(m // tm,),
        in_specs=[pl.BlockSpec((tm, dm), lambda i: (i, 0)),
                  pl.BlockSpec((1, dm), lambda i: (0, 0)),
                  pl.BlockSpec((dm, dm), lambda i: (0, 0)),
                  pl.BlockSpec((tm, pd), lambda i: (i, 0)),
                  pl.BlockSpec((pd, dm), lambda i: (0, 0))],
        out_specs=pl.BlockSpec((tm, dm), lambda i: (i, 0)),
        out_shape=jax.ShapeDtypeStruct((m, dm), F32),
        compiler_params=_cparams("parallel"),
        name="ple",
    )(h, gain.reshape(1, dm), gate_w, p, ple_w)


def _band_attn_kernel(q_ref, kp_ref, kc_ref, vp_ref, vc_ref, o_ref, l_ref, *, span, tq):
    j = pl.program_id(2)
    q = q_ref[...] * (HD_B ** -0.5)
    kk = jnp.concatenate([kp_ref[...], kc_ref[...]], axis=0)
    vv = jnp.concatenate([vp_ref[...], vc_ref[...]], axis=0)
    qi = lax.broadcasted_iota(jnp.int32, (span, 2 * span), 0)
    ki = lax.broadcasted_iota(jnp.int32, (span, 2 * span), 1)
    dist = qi + span - ki
    band = (dist >= 0) & (dist <= span)
    lane = lax.broadcasted_iota(jnp.int32, (span, LANES), 1)
    for sb in range(tq // span):
        r0 = sb * span
        mask = band & (ki >= jnp.where(j > 0, 0, span)) if sb == 0 else band
        outs = []
        lse_tile = jnp.zeros((span, LANES), F32)
        for h in range(HG):
            hs = slice(h * HD_B, (h + 1) * HD_B)
            s = _dot_nt(q[r0:r0 + span, hs], kk[r0:r0 + 2 * span, hs])
            s = jnp.where(mask, s, NEG)
            m = jnp.max(s, -1, keepdims=True)
            e = jnp.exp(s - m)
            den = jnp.sum(e, -1, keepdims=True)
            outs.append(_dot(e * (1.0 / den), vv[r0:r0 + 2 * span, hs]))
            lse_tile = jnp.where(lane == h, m + jnp.log(den), lse_tile)
        o_ref[r0:r0 + span, :] = jnp.concatenate(outs, axis=1)
        l_ref[r0:r0 + span, :] = lse_tile


def _band_attn(q, k, v, gi, *, batch, seq):
    win, dil = GROUPS[gi]
    span = win // dil
    n = seq // dil
    tq = min(4 * span, n)
    nb = n // tq
    sub = tq // span
    cur = lambda b, r, j: (r, b * nb + j, 0)
    prev = lambda b, r, j: (r, b * nb * sub + jnp.maximum(j * sub - 1, 0), 0)
    return pl.pallas_call(
        functools.partial(_band_attn_kernel, span=span, tq=tq),
        grid=(batch, dil, nb),
        in_specs=[pl.BlockSpec((None, tq, GW), cur),
                  pl.BlockSpec((None, span, GW), prev),
                  pl.BlockSpec((None, tq, GW), cur),
                  pl.BlockSpec((None, span, GW), prev),
                  pl.BlockSpec((None, tq, GW), cur)],
        out_specs=[pl.BlockSpec((None, tq, GW), cur),
                   pl.BlockSpec((None, tq, LANES), cur)],
        out_shape=[jax.ShapeDtypeStruct((dil, batch * n, GW), F32),
                   jax.ShapeDtypeStruct((dil, batch * n, LANES), F32)],
        compiler_params=_cparams("parallel", "parallel", "arbitrary"),
        name=f"band_attn_g{gi}",
    )(q, k, k, v, v)


def _merge_out_kernel(o0_ref, o1_ref, o2_ref, l0_ref, l1_ref, l2_ref, w_ref, r_ref, o_ref,
                      o_scr, l_scr, *, tm):
    for gi, (o_ph, l_ph) in enumerate(((o1_ref, l1_ref), (o2_ref, l2_ref))):
        d = o_ph.shape[0]
        for r in range(d):
            rows = pl.ds(r, tm // d, stride=d)
            l_scr[gi, rows, :] = l_ph[r]
            for c in range(GW // LANES):
                o_scr[gi, c, rows, :] = o_ph[r, :, c * LANES:(c + 1) * LANES]
    ls = [l0_ref[0], l_scr[0], l_scr[1]]

    def o_cols(g, c):
        if g == 0:
            return o0_ref[0, :, c * LANES:(c + 1) * LANES]
        return o_scr[g - 1, c]

    m = jnp.maximum(jnp.maximum(ls[0], ls[1]), ls[2])
    es = [jnp.exp(l - m) for l in ls]
    inv = 1.0 / (es[0] + es[1] + es[2])
    t = ls[0].shape[0]
    lo = lax.broadcasted_iota(jnp.int32, (t, LANES), 1) < HD_B
    cols = []
    for c in range(GW // LANES):
        acc = None
        for g in range(N_GROUPS):
            wt = es[g] * inv
            wexp = jnp.where(lo, wt[:, 2 * c:2 * c + 1], wt[:, 2 * c + 1:2 * c + 2])
            term = wexp * o_cols(g, c)
            acc = term if acc is None else acc + term
        cols.append(acc)
    o = jnp.concatenate(cols, axis=1).astype(BF16)
    o_ref[...] = r_ref[...] + jnp.dot(o, w_ref[...], preferred_element_type=F32)


def _merge_out(outs, lses, w, res, *, tm):
    m, dm = res.shape
    ph_spec = lambda a: pl.BlockSpec((a.shape[0], tm // a.shape[0], a.shape[2]),
                                     lambda i: (0, i, 0))
    return pl.pallas_call(
        functools.partial(_merge_out_kernel, tm=tm),
        grid=(m // tm,),
        in_specs=[ph_spec(a) for a in outs] + [ph_spec(a) for a in lses] +
                 [pl.BlockSpec((GW, dm), lambda i: (0, 0)),
                  pl.BlockSpec((tm, dm), lambda i: (i, 0))],
        out_specs=pl.BlockSpec((tm, dm), lambda i: (i, 0)),
        out_shape=jax.ShapeDtypeStruct((m, dm), F32),
        scratch_shapes=[pltpu.VMEM((N_GROUPS - 1, GW // LANES, tm, LANES), F32),
                        pltpu.VMEM((N_GROUPS - 1, tm, LANES), F32)],
        compiler_params=_cparams("parallel"),
        name="merge_out",
    )(*outs, *lses, w, res)


def _gather_attn_kernel(q_ref, kvn_ref, c0_ref, c1_ref, c2_ref, o_ref):
    caches = [c0_ref, c1_ref, c2_ref]
    outs, lses = [], []
    for g in range(N_GROUPS):
        q = q_ref[g] * (HD_B ** -0.5)
        kn, vn = kvn_ref[0, g], kvn_ref[1, g]
        kc, vc = caches[g][:, 0], caches[g][:, 1]
        s = jnp.sum(kc * q[None], -1, keepdims=True)
        s_new = jnp.sum(kn * q, -1, keepdims=True)
        m = jnp.maximum(jnp.max(s, 0), s_new)
        e = jnp.exp(s - m[None])
        e_new = jnp.exp(s_new - m)
        den = jnp.sum(e, 0) + e_new
        outs.append((jnp.sum(e * vc, 0) + e_new * vn) / den)
        lses.append(m + jnp.log(den))
    m = jnp.maximum(jnp.maximum(lses[0], lses[1]), lses[2])
    es = [jnp.exp(l - m) for l in lses]
    o_ref[...] = (es[0] * outs[0] + es[1] * outs[1] + es[2] * outs[2]) / (es[0] + es[1] + es[2])


def _gather_attn(q, kv_new, caches):
    nb = q.shape[0]
    span = GROUPS[0][0] // GROUPS[0][1]
    views = []
    for (win, dil), c in zip(GROUPS, caches):
        lb = c.shape[1]
        assert lb == win and lb // dil == span, "window buffer must hold the full window"
        views.append(c.reshape(nb, lb // dil, dil, 2, HG, HD_B))
    return pl.pallas_call(
        _gather_attn_kernel,
        grid=(nb,),
        in_specs=[pl.BlockSpec((None, N_GROUPS, HG, HD_B), lambda b: (b, 0, 0, 0)),
                  pl.BlockSpec((None, 2, N_GROUPS, HG, HD_B), lambda b: (b, 0, 0, 0, 0))] +
                 [pl.BlockSpec((None, span, None, 2, HG, HD_B), lambda b: (b, 0, 0, 0, 0, 0))
                  for _ in views],
        out_specs=pl.BlockSpec((None, HG, HD_B), lambda b: (b, 0, 0)),
        out_shape=jax.ShapeDtypeStruct((nb, HG, HD_B), F32),
        compiler_params=_cparams("parallel"),
        name="gather_attn",
    )(q, kv_new, *views)


def _transpose_cast_kernel(x_ref, o_ref):
    o_ref[...] = x_ref[...].T.astype(BF16)


def _transpose_cast(w, *, tk, tn):
    ne, k, n = w.shape
    return pl.pallas_call(
        _transpose_cast_kernel,
        grid=(ne, k // tk, n // tn),
        in_specs=[pl.BlockSpec((None, tk, tn), lambda e, i, j: (e, i, j))],
        out_specs=pl.BlockSpec((None, tn, tk), lambda e, i, j: (e, j, i)),
        out_shape=jax.ShapeDtypeStruct((ne, n, k), BF16),
        compiler_params=_cparams("parallel", "parallel", "parallel"),
        name="transpose_cast",
    )(w)


def _prep_weights(a_w_in, a_A_log, a_dt_bias, a_w_out, w_kv, b_w_q, b_w_out, dense_w_gu,
                  dense_w_down, moe_router, moe_w_gu, moe_w_down, ple_w, ple_gate_w, k_norm,
                  b_q_norm):
    d_model, a_in = a_w_in.shape[1:]
    a_in_pad = -(-a_in // LANES) * LANES
    wf = {
        'a_w_in': jnp.pad(a_w_in[0], ((0, 0), (0, a_in_pad - a_in))),
        'a_w_out': a_w_out[0], 'w_kv': w_kv, 'b_w_q': b_w_q[0], 'b_w_out': b_w_out[0],
        'dense_w_gu': dense_w_gu[0], 'dense_w_down': dense_w_down[0],
        'router': jnp.pad(moe_router[0], ((0, 0), (0, LANES - moe_router.shape[2]))),
        'ple_w': ple_w, 'ple_gate_w': ple_gate_w,
    }
    shared = {}
    hp = jnp.stack([a_A_log[0], a_dt_bias[0]])
    shared['a_hp'] = jnp.pad(hp, ((0, 0), (H_A, LANES - 2 * H_A)))
    shared['moe_wgu_t'] = _transpose_cast(moe_w_gu[0], tk=d_model, tn=1024)
    shared['moe_wd_t'] = _transpose_cast(moe_w_down[0], tk=896, tn=d_model)
    shared['k_gain'] = jnp.tile(k_norm, HG).reshape(1, GW)
    shared['q_gain'] = jnp.tile(b_q_norm[0], HG).reshape(1, GW)
    w_prompt = dict(shared, **{k: v.astype(BF16) for k, v in wf.items()})
    w_sample = dict(shared, **wf)
    return w_prompt, w_sample


def _layer0(x, p0, w, P, *, tm, in_tn, mixer):
    proj = _norm_mm(x, P['a_norm'][0], w['a_w_in'], tm=tm, tn=in_tn)
    og, s_new = mixer(proj)
    h = _mm_res(og, w['a_w_out'], x, tm=tm)
    h = _ffn(h, P['ffn_norm'][0], w['dense_w_gu'], w['dense_w_down'], tm=tm, tf=512)
    h = _ple(h, P['ple_norm'][0], w['ple_gate_w'][0], p0, w['ple_w'][0], tm=tm)
    return h, proj, s_new


def _layer1_tail(h, p1, w, P, *, tm):
    m = h.shape[0]
    tmr = max(tm, LANES)
    if m % tmr:
        h = jnp.pad(h, ((0, tmr - m % tmr), (0, 0)))
    col, row, seg = _route(h, P['ffn_norm'][1], w['router'], tm=tmr)
    h = _experts(h, P['ffn_norm'][1], col, row, seg, w['moe_wgu_t'], w['moe_wd_t'],
                 tm=tmr, tf=896)[:m]
    return _ple(h, P['ple_norm'][1], w['ple_gate_w'][1], p1, w['ple_w'][1], tm=tm)


def kernel(x_prompt, x_sample, p_prompt, p_sample, state_conv, state_delta, cache_kv_w128, cache_kv_w512, cache_kv_w2048, a_norm, a_w_in, a_conv_w, a_A_log, a_dt_bias, a_out_norm, a_w_out, kv_norm, w_kv, k_norm, b_norm, b_w_q, b_q_norm, b_w_out, ffn_norm, dense_w_gu, dense_w_down, moe_router, moe_w_gu, moe_w_down, ple_w, ple_norm, ple_gate_w):
    assert a_w_in.shape[0] == 1 and b_w_q.shape[0] == 1, "one mixer of each kind"
    bp, sp, dm = x_prompt.shape
    bs, ls, _ = x_sample.shape
    assert ls == 1, "sample group decodes one token per sequence"
    qkv_w = 3 * H_A * DK_A
    P = dict(a_norm=a_norm, ffn_norm=ffn_norm, ple_norm=ple_norm)
    w, ws = _prep_weights(a_w_in, a_A_log, a_dt_bias, a_w_out, w_kv, b_w_q, b_w_out, dense_w_gu,
                          dense_w_down, moe_router, moe_w_gu, moe_w_down, ple_w, ple_gate_w,
                          k_norm, b_q_norm)
    a_in_pad = w['a_w_in'].shape[1]
    in_tn = a_in_pad // 3 if a_in_pad % (3 * LANES) == 0 else LANES
    conv_w = a_conv_w[0]
    out_norm = a_out_norm[0].reshape(1, DK_A)

    mp = bp * sp
    tm = min(1024, sp)
    xp = x_prompt.reshape(mp, dm)
    conv0 = jnp.zeros((bp, CONV_W - 1, qkv_w), F32)
    s0 = jnp.zeros((bp, H_A, DK_A, DK_A), F32)
    h, proj, delta_p = _layer0(
        xp, p_prompt[0].reshape(mp, -1), w, P, tm=tm, in_tn=in_tn,
        mixer=lambda pr: _gdn_prompt(pr, conv_w, w['a_hp'], out_norm, conv0, s0, batch=bp, seq=sp))
    conv_p = proj.reshape(bp, sp, -1)[:, sp - (CONV_W - 1):, :qkv_w][None]

    cos, sin = _rope_tables(jnp.arange(sp, dtype=jnp.int32))
    dils = [d for _, d in GROUPS]
    tmp = min(512, sp)
    kv, *kv_ph = _proj_rope(h, kv_norm, w['w_kv'], w['k_gain'], cos, sin, tm=tmp, n_rope=N_GROUPS,
                            natural=True, dils=dils + dils)
    q_ph = _proj_rope(h, b_norm[0], w['b_w_q'], w['q_gain'], cos, sin, tm=tmp, n_rope=N_GROUPS,
                      natural=False, dils=dils)
    outs, lses = [], []
    for gi in range(N_GROUPS):
        o, lse = _band_attn(q_ph[gi], kv_ph[gi], kv_ph[N_GROUPS + gi], gi, batch=bp, seq=sp)
        outs.append(o)
        lses.append(lse)
    h = _merge_out(outs, lses, w['b_w_out'], h, tm=tm)
    y_prompt = _layer1_tail(h, p_prompt[1].reshape(mp, -1), w, P, tm=tm).reshape(bp, sp, dm)
    kv3 = kv.reshape(bp, sp, 2 * N_GROUPS * GW)
    kv_p = []
    for gi, (win, _) in enumerate(GROUPS):
        rows = kv3[:, sp - min(win, sp):]
        k_g = rows[:, :, gi * GW:(gi + 1) * GW]
        v_g = rows[:, :, (N_GROUPS + gi) * GW:(N_GROUPS + gi + 1) * GW]
        kv_p.append(jnp.stack([k_g, v_g], axis=2).reshape(bp, -1, 2, HG, HD_B))

    xs = x_sample.reshape(bs, dm)
    hs, proj_s, delta_s = _layer0(
        xs, p_sample[0].reshape(bs, -1), ws, P, tm=bs, in_tn=in_tn,
        mixer=lambda pr: _gdn_step(pr, state_conv[0], conv_w, ws['a_hp'], out_norm, state_delta[0]))
    conv_s = jnp.concatenate([state_conv[0][:, 1:], proj_s[:, None, :qkv_w]], axis=1)[None]
    cos_s, sin_s = _rope_tables(jnp.full((bs,), PAST_LEN, jnp.int32))
    kv_s, = _proj_rope(hs, kv_norm, ws['w_kv'], ws['k_gain'], cos_s, sin_s, tm=bs,
                       n_rope=N_GROUPS, natural=True, dils=())
    q_s, = _proj_rope(hs, b_norm[0], ws['b_w_q'], ws['q_gain'], cos_s, sin_s, tm=bs,
                      n_rope=N_GROUPS, natural=True, dils=())
    o_s = _gather_attn(q_s.reshape(bs, N_GROUPS, HG, HD_B),
                       kv_s.reshape(bs, 2, N_GROUPS, HG, HD_B),
                       (cache_kv_w128, cache_kv_w512, cache_kv_w2048))
    hs = _mm_res(o_s.reshape(bs, GW), ws['b_w_out'], hs, tm=bs)
    y_sample = _layer1_tail(hs, p_sample[1].reshape(bs, -1), ws, P, tm=bs).reshape(bs, 1, dm)
    kvs5 = kv_s.reshape(bs, 1, 2, N_GROUPS, HG, HD_B)
    kv_sn = [kvs5[:, :, :, gi] for gi in range(N_GROUPS)]

    return (y_prompt, y_sample, conv_p, conv_s, delta_p[None], delta_s[None],
            kv_p[0], kv_sn[0], kv_p[1], kv_sn[1], kv_p[2], kv_sn[2])
```

```python
import functools

import jax
import jax.numpy as jnp
from jax import lax
from jax.experimental import pallas as pl
from jax.experimental.pallas import tpu as pltpu

F32 = jnp.float32
BF16 = jnp.bfloat16

EPS = 1e-6
PAST_LEN = 16384
GROUPS = ((128, 1), (512, 4), (2048, 16))
N_GROUPS = len(GROUPS)
HG = 8
HD_B = 64
ROT_DIM = HD_B // 4
ROPE_THETA = 500000.0
GW = HG * HD_B
H_A = 8
DK_A = 128
CONV_W = 4
CHUNK = 64
LANES = 128
VMEM_LIMIT = 52 * 1024 * 1024
NEG = -1e30


def _cparams(*sem):
    return pltpu.CompilerParams(dimension_semantics=sem, vmem_limit_bytes=VMEM_LIMIT)


def _rms_rows(x, gain):
    return x * lax.rsqrt(jnp.mean(x * x, -1, keepdims=True) + EPS) * gain


def _silu(x):
    return x * (1.0 / (1.0 + jnp.exp(-x)))


def _sigmoid(x):
    return 1.0 / (1.0 + jnp.exp(-x))


def _dot(a, b):
    return jnp.dot(a.astype(BF16), b.astype(BF16), preferred_element_type=F32)


def _wdot(a, w):
    dot = functools.partial(jnp.dot, preferred_element_type=F32)
    if w.dtype == BF16:
        return dot(a.astype(BF16), w)
    a = a.astype(F32)
    a_hi = a.astype(BF16)
    a_lo = (a - a_hi.astype(F32)).astype(BF16)
    w_hi = w.astype(BF16)
    w_lo = (w - w_hi.astype(F32)).astype(BF16)
    return dot(a_hi, w_hi) + (dot(a_lo, w_hi) + dot(a_hi, w_lo))


def _act_dtype(w):
    return BF16 if w.dtype == BF16 else F32


def _dot_nt(a, b):
    return lax.dot_general(a.astype(BF16), b.astype(BF16), (((1,), (1,)), ((), ())),
                           preferred_element_type=F32)


def _dot_tn(a, b):
    return lax.dot_general(a.astype(BF16), b.astype(BF16), (((0,), (0,)), ((), ())),
                           preferred_element_type=F32)


def _head_norm_rope(x, hgain, cos, sin):
    t = x.shape[0]
    lane = lax.broadcasted_iota(jnp.int32, (t, LANES), 1)
    lo = lane < HD_B
    d = lane & (HD_B - 1)
    outs = []
    for c in range(GW // LANES):
        sl = slice(c * LANES, (c + 1) * LANES)
        xb = x[:, sl]
        sq = xb * xb
        s_lo = jnp.sum(jnp.where(lo, sq, 0.0), -1, keepdims=True)
        s_hi = jnp.sum(jnp.where(lo, 0.0, sq), -1, keepdims=True)
        scale = jnp.where(lo, lax.rsqrt(s_lo * (1.0 / HD_B) + EPS),
                          lax.rsqrt(s_hi * (1.0 / HD_B) + EPS))
        yb = xb * scale * hgain[:, sl]
        half = ROT_DIM // 2
        rot = jnp.where(d < half, pltpu.roll(yb, LANES - half, 1), pltpu.roll(yb, half, 1))
        outs.append(yb * cos[:, sl] + rot * sin[:, sl])
    return jnp.concatenate(outs, axis=1)


def _norm_mm_kernel(x_ref, g_ref, w_ref, o_ref, u_ref):
    @pl.when(pl.program_id(1) == 0)
    def _():
        u_ref[...] = _rms_rows(x_ref[...], g_ref[...]).astype(u_ref.dtype)

    o_ref[...] = _wdot(u_ref[...], w_ref[...])


def _norm_mm(x, gain, w, *, tm, tn):
    m, k = x.shape
    n = w.shape[1]
    return pl.pallas_call(
        _norm_mm_kernel,
        grid=(m // tm, n // tn),
        in_specs=[pl.BlockSpec((tm, k), lambda i, j: (i, 0)),
                  pl.BlockSpec((1, k), lambda i, j: (0, 0)),
                  pl.BlockSpec((k, tn), lambda i, j: (0, j))],
        out_specs=pl.BlockSpec((tm, tn), lambda i, j: (i, j)),
        out_shape=jax.ShapeDtypeStruct((m, n), F32),
        scratch_shapes=[pltpu.VMEM((tm, k), _act_dtype(w))],
        compiler_params=_cparams("parallel", "arbitrary"),
        name="norm_mm",
    )(x, gain.reshape(1, k), w)


def _proj_rope_kernel(x_ref, g_ref, w_ref, hg_ref, cos_ref, sin_ref, *rest,
                      n_rope, natural, dils, tm):
    n_out = int(natural) + len(dils)
    outs, (slab_ref,) = rest[:n_out], rest[n_out:]
    nat_ref = outs[0] if natural else None
    ph_refs = outs[int(natural):]
    u = _rms_rows(x_ref[...], g_ref[...]).astype(_act_dtype(w_ref))
    slab = 0
    for jj in range(w_ref.shape[1] // GW):
        cols = slice(jj * GW, (jj + 1) * GW)
        y = _wdot(u, w_ref[:, cols])
        if jj < n_rope:
            y = _head_norm_rope(y, hg_ref[...], cos_ref[...], sin_ref[...])
        if natural:
            nat_ref[:, cols] = y
        if jj >= len(dils):
            continue
        d = dils[jj]
        if d == 1:
            ph_refs[jj][0] = y.astype(BF16)
            continue
        for c in range(GW // LANES):
            slab_ref[slab, c] = y[:, c * LANES:(c + 1) * LANES]
        for r in range(d):
            for c in range(GW // LANES):
                ph_refs[jj][r, :, c * LANES:(c + 1) * LANES] = (
                    slab_ref[slab, c, pl.ds(r, tm // d, stride=d), :].astype(BF16))
        slab += 1


def _proj_rope(x, gain, w, hgain, cos, sin, *, tm, n_rope, natural, dils):
    m, k = x.shape
    n = w.shape[1]
    pos_blocks = cos.shape[0] // tm
    out_specs, out_shape = [], []
    if natural:
        out_specs.append(pl.BlockSpec((tm, n), lambda i: (i, 0)))
        out_shape.append(jax.ShapeDtypeStruct((m, n), F32))
    for d in dils:
        out_specs.append(pl.BlockSpec((d, tm // d, GW), lambda i: (0, i, 0)))
        out_shape.append(jax.ShapeDtypeStruct((d, m // d, GW), BF16))
    n_slabs = max(1, sum(d > 1 for d in dils))
    return pl.pallas_call(
        functools.partial(_proj_rope_kernel, n_rope=n_rope, natural=natural, dils=tuple(dils),
                          tm=tm),
        grid=(m // tm,),
        in_specs=[pl.BlockSpec((tm, k), lambda i: (i, 0)),
                  pl.BlockSpec((1, k), lambda i: (0, 0)),
                  pl.BlockSpec((k, n), lambda i: (0, 0)),
                  pl.BlockSpec((1, GW), lambda i: (0, 0)),
                  pl.BlockSpec((tm, GW), lambda i: (i % pos_blocks, 0)),
                  pl.BlockSpec((tm, GW), lambda i: (i % pos_blocks, 0))],
        out_specs=out_specs,
        out_shape=out_shape,
        scratch_shapes=[pltpu.VMEM((n_slabs, GW // LANES, tm, LANES), F32)],
        compiler_params=_cparams("parallel"),
        name="proj_rope",
    )(x, gain.reshape(1, k), w, hgain, cos, sin)


def _rope_tables(pos):
    half = ROT_DIM // 2
    inv = ROPE_THETA ** (-jnp.arange(half, dtype=F32) * 2.0 / ROT_DIM)
    ang = pos.astype(F32)[:, None] * inv[None]
    c, s = jnp.cos(ang), jnp.sin(ang)
    n = pos.shape[0]
    cos_h = jnp.concatenate([c, c, jnp.ones((n, HD_B - ROT_DIM), F32)], 1)
    sin_h = jnp.concatenate([-s, s, jnp.zeros((n, HD_B - ROT_DIM), F32)], 1)
    return jnp.tile(cos_h, (1, HG)), jnp.tile(sin_h, (1, HG))


def _mm_res_kernel(x_ref, w_ref, r_ref, o_ref):
    o_ref[...] = r_ref[...] + _wdot(x_ref[...], w_ref[...])


def _mm_res(x, w, res, *, tm):
    m, k = x.shape
    n = w.shape[1]
    return pl.pallas_call(
        _mm_res_kernel,
        grid=(m // tm,),
        in_specs=[pl.BlockSpec((tm, k), lambda i: (i, 0)),
                  pl.BlockSpec((k, n), lambda i: (0, 0)),
                  pl.BlockSpec((tm, n), lambda i: (i, 0))],
        out_specs=pl.BlockSpec((tm, n), lambda i: (i, 0)),
        out_shape=jax.ShapeDtypeStruct((m, n), F32),
        compiler_params=_cparams("parallel"),
        name="mm_res",
    )(x, w, res)


def _unit_lower_inverses(mats):
    c = mats[0].shape[0]
    row = lax.broadcasted_iota(jnp.int32, (c, c), 0)
    col = lax.broadcasted_iota(jnp.int32, (c, c), 1)
    eye = jnp.where(row == col, 1.0, 0.0).astype(F32)
    ts = None
    b = 1
    while b < c:
        sel = ((row ^ col) < 2 * b) & ((row & b) != 0) & ((col & b) == 0)
        lows = [jnp.where(sel, a, 0.0) for a in mats]
        if ts is None:
            ts = [eye - low for low in lows]
        else:
            tl = [_dot(t, low) for t, low in zip(ts, lows)]
            ts = [t - _dot(x, t) for t, x in zip(ts, tl)]
        b *= 2
    return ts


def _gdn_head_params(ba, hp):
    beta = _sigmoid(ba)
    x = ba + hp[1:2, :]
    softplus = jnp.maximum(x, 0.0) + jnp.log(1.0 + jnp.exp(-jnp.abs(x)))
    g = -jnp.exp(hp[0:1, :]) * softplus
    return beta, g


def _gdn_kernel(qkv_ref, z_ref, ba_ref, cw_ref, hp_ref, on_ref, conv0_ref, s0_ref,
                og_ref, sout_ref, xbuf, s_scr, *, C, nch):
    n = pl.program_id(1)
    R = nch * C
    pad = 8

    @pl.when(n == 0)
    def _():
        xbuf[pad - (CONV_W - 1):pad, :] = conv0_ref[...]
        s_scr[...] = s0_ref[...]

    xbuf[pad:pad + R, :] = qkv_ref[...]

    def conv_cols(c0):
        acc = None
        for j in range(CONV_W):
            r0 = pad - (CONV_W - 1) + j
            term = xbuf[r0:r0 + R, c0:c0 + DK_A] * cw_ref[j:j + 1, c0:c0 + DK_A]
            acc = term if acc is None else acc + term
        return _silu(acc)

    beta, g = _gdn_head_params(ba_ref[...], hp_ref[...])
    rr = lax.broadcasted_iota(jnp.int32, (R, R), 0)
    rc = lax.broadcasted_iota(jnp.int32, (R, R), 1)
    blocktri = ((rr >= rc) & ((rr ^ rc) < C)).astype(F32)
    gcum = jnp.dot(blocktri, g, preferred_element_type=F32, precision=lax.Precision.HIGHEST)
    gcum_t = gcum.T
    row = lax.broadcasted_iota(jnp.int32, (C, C), 0)
    col = lax.broadcasted_iota(jnp.int32, (C, C), 1)
    incl = row >= col
    strict = row > col

    units = [(c, h) for c in range(nch) for h in range(H_A)]
    qs, ks, vs = {}, {}, {}
    for h in range(H_A):
        q = conv_cols(h * DK_A)
        k = conv_cols((H_A + h) * DK_A)
        v = conv_cols((2 * H_A + h) * DK_A)
        q = q * lax.rsqrt(jnp.sum(q * q, -1, keepdims=True) + EPS) * (DK_A ** -0.5)
        k = k * lax.rsqrt(jnp.sum(k * k, -1, keepdims=True) + EPS)
        for c in range(nch):
            rs = slice(c * C, (c + 1) * C)
            qs[c, h], ks[c, h], vs[c, h] = q[rs], k[rs], v[rs]

    bcs, gcs, decays, kbs = {}, {}, {}, {}
    for c, h in units:
        rs = slice(c * C, (c + 1) * C)
        bcs[c, h] = beta[rs, h:h + 1]
        gcs[c, h] = gcum[rs, H_A + h:H_A + h + 1]
        gr = gcum_t[H_A + h:H_A + h + 1, rs]
        decays[c, h] = jnp.exp(jnp.where(incl, gcs[c, h] - gr, NEG))
        kbs[c, h] = ks[c, h] * bcs[c, h]
    grams = {u: _dot_nt(jnp.concatenate([kbs[u], qs[u]], axis=0), ks[u]) for u in units}
    a_mats = [jnp.where(strict, grams[u][:C] * decays[u], 0.0) for u in units]
    aqks = {u: grams[u][C:] * decays[u] for u in units}
    t_mats = dict(zip(units, _unit_lower_inverses(a_mats)))
    egs = {u: jnp.exp(gcs[u]) for u in units}
    sols = {u: _dot(t_mats[u], jnp.concatenate([vs[u] * bcs[u], kbs[u] * egs[u]], axis=1))
            for u in units}

    states = [s_scr[h] for h in range(H_A)]
    for c in range(nch):
        rs = slice(c * C, (c + 1) * C)
        for h in range(H_A):
            u = (c, h)
            g_last = gcs[u][C - 1:C, :]
            ws = _dot(jnp.concatenate([sols[u][:, DK_A:], qs[u] * egs[u]], axis=0), states[h])
            v_new = sols[u][:, :DK_A] - ws[:C]
            o = ws[C:] + _dot(aqks[u], v_new)
            kd = ks[u] * jnp.exp(g_last - gcs[u])
            states[h] = states[h] * jnp.exp(g_last) + _dot_tn(kd, v_new)
            o = _rms_rows(o, on_ref[...]) * _silu(z_ref[rs, h * DK_A:(h + 1) * DK_A])
            og_ref[rs, h * DK_A:(h + 1) * DK_A] = o
    for h in range(H_A):
        s_scr[h] = states[h]

    xbuf[pad - (CONV_W - 1):pad, :] = xbuf[pad + R - (CONV_W - 1):pad + R, :]

    @pl.when(n == pl.num_programs(1) - 1)
    def _():
        sout_ref[...] = s_scr[...]


def _gdn_prompt(proj, conv_w, hp, out_norm, conv0, s0, *, batch, seq):
    C = min(CHUNK, seq)
    nch = next(n for n in (4, 2, 1) if seq % (n * C) == 0)
    R = nch * C
    nc = seq // R
    qkv_w = 3 * H_A * DK_A
    z_w = H_A * DK_A
    return pl.pallas_call(
        functools.partial(_gdn_kernel, C=C, nch=nch),
        grid=(batch, nc),
        in_specs=[pl.BlockSpec((R, qkv_w), lambda b, n: (b * nc + n, 0)),
                  pl.BlockSpec((R, z_w), lambda b, n: (b * nc + n, qkv_w // z_w)),
                  pl.BlockSpec((R, LANES), lambda b, n: (b * nc + n, (qkv_w + z_w) // LANES)),
                  pl.BlockSpec((CONV_W, qkv_w), lambda b, n: (0, 0)),
                  pl.BlockSpec((2, LANES), lambda b, n: (0, 0)),
                  pl.BlockSpec((1, DK_A), lambda b, n: (0, 0)),
                  pl.BlockSpec((None, CONV_W - 1, qkv_w), lambda b, n: (b, 0, 0)),
                  pl.BlockSpec((None, H_A, DK_A, DK_A), lambda b, n: (b, 0, 0, 0))],
        out_specs=[pl.BlockSpec((R, z_w), lambda b, n: (b * nc + n, 0)),
                   pl.BlockSpec((None, H_A, DK_A, DK_A), lambda b, n: (b, 0, 0, 0))],
        out_shape=[jax.ShapeDtypeStruct((batch * seq, z_w), F32),
                   jax.ShapeDtypeStruct((batch, H_A, DK_A, DK_A), F32)],
        scratch_shapes=[pltpu.VMEM((R + 8, qkv_w), F32),
                        pltpu.VMEM((H_A, DK_A, DK_A), F32)],
        compiler_params=_cparams("parallel", "arbitrary"),
        name="gdn_chunked",
    )(proj, proj, proj, conv_w, hp, out_norm, conv0, s0)


def _gdn_step_kernel(proj_ref, conv_ref, cw_ref, hp_ref, on_ref, s0_ref, og_ref, sout_ref, qk_scr):
    qkv_w = 3 * H_A * DK_A
    z_w = H_A * DK_A

    def conv_cols(c0):
        sl = slice(c0, c0 + DK_A)
        acc = proj_ref[:, sl] * cw_ref[CONV_W - 1:CONV_W, sl]
        for j in range(CONV_W - 1):
            acc = acc + conv_ref[j:j + 1, sl] * cw_ref[j:j + 1, sl]
        return _silu(acc)

    beta, g = _gdn_head_params(proj_ref[:, qkv_w + z_w:qkv_w + z_w + LANES], hp_ref[...])
    qk_scr[...] = jnp.zeros_like(qk_scr)
    vs = []
    for h in range(H_A):
        q = conv_cols(h * DK_A)
        k = conv_cols((H_A + h) * DK_A)
        vs.append(conv_cols((2 * H_A + h) * DK_A))
        qk_scr[H_A + h:H_A + h + 1, :] = (
            q * lax.rsqrt(jnp.sum(q * q, -1, keepdims=True) + EPS) * (DK_A ** -0.5))
        qk_scr[h:h + 1, :] = k * lax.rsqrt(jnp.sum(k * k, -1, keepdims=True) + EPS)
    qk = qk_scr[...]
    qk_t = qk.T
    for h in range(H_A):
        k_row = qk[h:h + 1, :]
        q_row = qk[H_A + h:H_A + h + 1, :]
        k_col = qk_t[:, h:h + 1]
        q_col = qk_t[:, H_A + h:H_A + h + 1]
        bh = beta[:, h:h + 1]
        eg = jnp.exp(g[:, H_A + h:H_A + h + 1])
        s = s0_ref[h]
        k_s = jnp.sum(s * k_col, 0, keepdims=True)
        q_s = jnp.sum(s * q_col, 0, keepdims=True)
        v_new = bh * (vs[h] - eg * k_s)
        o = eg * q_s + jnp.sum(q_row * k_row, -1, keepdims=True) * v_new
        sout_ref[h] = s * eg + k_col * v_new
        o = _rms_rows(o, on_ref[...]) * _silu(proj_ref[:, qkv_w + h * DK_A:qkv_w + (h + 1) * DK_A])
        og_ref[:, h * DK_A:(h + 1) * DK_A] = o


def _gdn_step(proj, conv_state, conv_w, hp, out_norm, s0):
    nb, pw = proj.shape
    qkv_w = 3 * H_A * DK_A
    z_w = H_A * DK_A
    og, s_new = pl.pallas_call(
        _gdn_step_kernel,
        grid=(nb,),
        in_specs=[pl.BlockSpec((None, 1, pw), lambda b: (b, 0, 0)),
                  pl.BlockSpec((None, CONV_W - 1, qkv_w), lambda b: (b, 0, 0)),
                  pl.BlockSpec((CONV_W, qkv_w), lambda b: (0, 0)),
                  pl.BlockSpec((2, LANES), lambda b: (0, 0)),
                  pl.BlockSpec((1, DK_A), lambda b: (0, 0)),
                  pl.BlockSpec((None, H_A, DK_A, DK_A), lambda b: (b, 0, 0, 0))],
        out_specs=[pl.BlockSpec((None, 1, z_w), lambda b: (b, 0, 0)),
                   pl.BlockSpec((None, H_A, DK_A, DK_A), lambda b: (b, 0, 0, 0))],
        out_shape=[jax.ShapeDtypeStruct((nb, 1, z_w), F32),
                   jax.ShapeDtypeStruct((nb, H_A, DK_A, DK_A), F32)],
        scratch_shapes=[pltpu.VMEM((LANES, DK_A), F32)],
        compiler_params=_cparams("parallel"),
        name="gdn_step",
    )(proj.reshape(nb, 1, pw), conv_state, conv_w, hp, out_norm, s0)
    return og.reshape(nb, z_w), s_new


def _ffn_kernel(h_ref, g_ref, wg_ref, wu_ref, wd_ref, o_ref, u_ref, acc_ref):
    f = pl.program_id(1)

    @pl.when(f == 0)
    def _():
        u_ref[...] = _rms_rows(h_ref[...], g_ref[...]).astype(u_ref.dtype)
        acc_ref[...] = jnp.zeros_like(acc_ref)

    u = u_ref[...]
    gate = _wdot(u, wg_ref[...])
    up = _wdot(u, wu_ref[...])
    acc_ref[...] += _wdot(_silu(gate) * up, wd_ref[...])

    @pl.when(f == pl.num_programs(1) - 1)
    def _():
        o_ref[...] = h_ref[...] + acc_ref[...]


def _ffn(h, gain, w_gu, w_down, *, tm, tf):
    m, dm = h.shape
    ff = w_down.shape[0]
    nf = ff // tf
    return pl.pallas_call(
        _ffn_kernel,
        grid=(m // tm, nf),
        in_specs=[pl.BlockSpec((tm, dm), lambda i, f: (i, 0)),
                  pl.BlockSpec((1, dm), lambda i, f: (0, 0)),
                  pl.BlockSpec((dm, tf), lambda i, f: (0, f)),
                  pl.BlockSpec((dm, tf), lambda i, f: (0, nf + f)),
                  pl.BlockSpec((tf, dm), lambda i, f: (f, 0))],
        out_specs=pl.BlockSpec((tm, dm), lambda i, f: (i, 0)),
        out_shape=jax.ShapeDtypeStruct((m, dm), F32),
        scratch_shapes=[pltpu.VMEM((tm, dm), _act_dtype(w_gu)),
                        pltpu.VMEM((tm, dm), F32)],
        compiler_params=_cparams("parallel", "arbitrary"),
        name="ffn_dense",
    )(h, gain.reshape(1, dm), w_gu, w_gu, w_down)


N_EXPERTS = 8
SEG_ALIGN = LANES


def _top2(logits):
    t = logits.shape[0]
    lane = lax.broadcasted_iota(jnp.int32, (t, LANES), 1)
    valid = lane < N_EXPERTS
    lg = jnp.where(valid, logits, NEG)
    mx = jnp.max(lg, -1, keepdims=True)
    e = jnp.where(valid, jnp.exp(lg - mx), 0.0)
    probs = e / jnp.sum(e, -1, keepdims=True)
    p1 = jnp.max(probs, -1, keepdims=True)
    i1 = jnp.min(jnp.where((probs == p1) & valid, lane, LANES), -1, keepdims=True)
    rest = jnp.where((lane == i1) | ~valid, -1.0, probs)
    p2 = jnp.max(rest, -1, keepdims=True)
    i2 = jnp.min(jnp.where(rest == p2, lane, LANES), -1, keepdims=True)
    tot = p1 + p2
    return i1, i2, p1 / tot, p2 / tot


def _router_kernel(h_ref, g_ref, r_ref, tri_ref, upper_ref, col_ref, row_ref, seg_ref):
    t = h_ref.shape[0]
    logits = _wdot(_rms_rows(h_ref[...], g_ref[...]), r_ref[...])
    i1, i2, g1, g2 = _top2(logits)
    lane = lax.broadcasted_iota(jnp.int32, (t, LANES), 1)
    sel = jnp.where((lane == i1) | (lane == i2), 1.0, 0.0)
    pos = jnp.dot(tri_ref[...], sel.astype(BF16), preferred_element_type=F32)
    counts = jnp.sum(sel, 0, keepdims=True)
    nblk = jnp.floor((counts + (SEG_ALIGN - 1)) * (1.0 / SEG_ALIGN))
    nblk8 = jnp.broadcast_to(nblk, (8, LANES))
    start8 = jnp.dot(nblk8.astype(BF16), upper_ref[...], preferred_element_type=F32)
    dest = start8[0:1] * SEG_ALIGN + pos
    d1 = jnp.sum(jnp.where(lane == i1, dest, 0.0), -1, keepdims=True)
    d2 = jnp.sum(jnp.where(lane == i2, dest, 0.0), -1, keepdims=True)
    col = jnp.where(lane == 0, d1, jnp.where(lane == 1, d2,
                    jnp.where(lane == 2, g1, jnp.where(lane == 3, g2, 0.0))))
    col_ref[...] = col
    row_ref[...] = col.T[0:8, :]
    lane8 = lane[0:8]
    seg = jnp.where(lane8 < N_EXPERTS, start8,
                    jnp.where(lane8 < 2 * N_EXPERTS, pltpu.roll(nblk8, N_EXPERTS, 1), 0.0))
    seg_ref[...] = seg[0:1].astype(jnp.int32)


def _route(h, gain, router, *, tm):
    m, dm = h.shape
    nt = m // tm
    tri = jnp.tril(jnp.ones((tm, tm), F32), -1).astype(BF16)
    upper = jnp.triu(jnp.ones((LANES, LANES), F32), 1).astype(BF16)
    return pl.pallas_call(
        _router_kernel,
        grid=(nt,),
        in_specs=[pl.BlockSpec((tm, dm), lambda i: (i, 0)),
                  pl.BlockSpec((1, dm), lambda i: (0, 0)),
                  pl.BlockSpec((dm, LANES), lambda i: (0, 0)),
                  pl.BlockSpec((tm, tm), lambda i: (0, 0)),
                  pl.BlockSpec((LANES, LANES), lambda i: (0, 0))],
        out_specs=[pl.BlockSpec((tm, LANES), lambda i: (i, 0)),
                   pl.BlockSpec((None, 8, tm), lambda i: (i, 0, 0)),
                   pl.BlockSpec((None, 1, LANES), lambda i: (i, 0, 0))],
        out_shape=[jax.ShapeDtypeStruct((m, LANES), F32),
                   jax.ShapeDtypeStruct((nt, 8, tm), F32),
                   jax.ShapeDtypeStruct((nt, 1, LANES), jnp.int32)],
        compiler_params=_cparams("parallel"),
        name="moe_route",
    )(h, gain.reshape(1, dm), router, tri, upper)


def _one_hot_rows(row0, n, d1_row, d2_row):
    ridx = (lax.broadcasted_iota(jnp.int32, (n, d1_row.shape[1]), 0) + row0).astype(F32)
    return jnp.where((ridx == d1_row) | (ridx == d2_row), 1.0, 0.0).astype(BF16)


def _experts_kernel(seg_ref, h_ref, g_ref, col_ref, row_ref, wg_ref, wu_ref, wd_ref, o_ref,
                    xs_scr, acc_scr, y_scr, *, rows_max, gch, kch):
    i, e, f = pl.program_id(0), pl.program_id(1), pl.program_id(2)
    t = h_ref.shape[0]

    @pl.when((e == 0) & (f == 0))
    def _():
        u = _rms_rows(h_ref[...], g_ref[...]).astype(BF16)
        d1_row, d2_row = row_ref[0:1, :], row_ref[1:2, :]
        for c in range(rows_max // gch):
            p = _one_hot_rows(c * gch, gch, d1_row, d2_row)
            xs_scr[c * gch:(c + 1) * gch, :] = jnp.dot(
                p, u, preferred_element_type=F32).astype(BF16)
        acc_scr[...] = jnp.zeros_like(acc_scr)

    start = seg_ref[i * LANES + e]
    nblk = seg_ref[i * LANES + N_EXPERTS + e]

    def run_blocks(b0, nb):
        x = xs_scr[pl.ds(pl.multiple_of(b0 * LANES, LANES), nb * LANES), :]
        gate_t = _dot_nt(wg_ref[...], x)
        up_t = _dot_nt(wu_ref[...], x)
        act_t = (_silu(gate_t) * up_t).astype(BF16)
        down_t = jnp.dot(wd_ref[...], act_t, preferred_element_type=F32)
        for k in range(nb):
            acc_scr[b0 + k] += down_t[:, k * LANES:(k + 1) * LANES]

    n4 = nblk // 4

    def body(j, carry):
        run_blocks(start + 4 * j, 4)
        return carry

    lax.fori_loop(0, n4, body, 0)
    rem = nblk - 4 * n4
    tail = start + 4 * n4

    @pl.when((rem & 2) != 0)
    def _():
        run_blocks(tail, 2)

    @pl.when((rem & 1) != 0)
    def _():
        run_blocks(tail + (rem & 2), 1)

    @pl.when((e == pl.num_programs(1) - 1) & (f == pl.num_programs(2) - 1))
    def _():
        col = col_ref[...]
        d1_col, d2_col, g1_col, g2_col = col[:, 0:1], col[:, 1:2], col[:, 2:3], col[:, 3:4]
        d1_row, d2_row = row_ref[0:1, :], row_ref[1:2, :]
        lane = lax.broadcasted_iota(jnp.int32, (t, LANES), 1)
        for c in range(rows_max // kch):
            parts = []
            for k in range(kch // LANES):
                b = c * (kch // LANES) + k
                li = (lane + b * LANES).astype(F32)
                gs = jnp.sum(jnp.where(li == d1_col, g1_col, 0.0) +
                             jnp.where(li == d2_col, g2_col, 0.0), 0, keepdims=True)
                parts.append((acc_scr[b] * gs).astype(BF16))
            contrib = jnp.dot(jnp.concatenate(parts, axis=1),
                              _one_hot_rows(c * kch, kch, d1_row, d2_row),
                              preferred_element_type=F32)
            if c == 0:
                y_scr[...] = contrib
            else:
                y_scr[...] += contrib
        o_ref[...] = h_ref[...] + y_scr[...].T


def _experts(h, gain, col, row, seg, wgu_t, wd_t, *, tm, tf):
    m, dm = h.shape
    ne, _, ff = wd_t.shape
    nf = ff // tf
    nt = m // tm
    rows_max = 2 * tm + ne * SEG_ALIGN
    gch = 1024 if rows_max % 1024 == 0 else 256
    kch = 512 if rows_max % 512 == 0 else 256
    grid_spec = pltpu.PrefetchScalarGridSpec(
        num_scalar_prefetch=1,
        grid=(nt, ne, nf),
        in_specs=[pl.BlockSpec((tm, dm), lambda i, e, f, s: (i, 0),
                               pipeline_mode=pl.Buffered(1)),
                  pl.BlockSpec((1, dm), lambda i, e, f, s: (0, 0)),
                  pl.BlockSpec((tm, LANES), lambda i, e, f, s: (i, 0),
                               pipeline_mode=pl.Buffered(1)),
                  pl.BlockSpec((None, 8, tm), lambda i, e, f, s: (i, 0, 0)),
                  pl.BlockSpec((None, tf, dm), lambda i, e, f, s: (e, f, 0)),
                  pl.BlockSpec((None, tf, dm), lambda i, e, f, s: (e, nf + f, 0)),
                  pl.BlockSpec((None, dm, tf), lambda i, e, f, s: (e, 0, f))],
        out_specs=pl.BlockSpec((tm, dm), lambda i, e, f, s: (i, 0)),
        scratch_shapes=[pltpu.VMEM((rows_max, dm), BF16),
                        pltpu.VMEM((rows_max // LANES, dm, LANES), F32),
                        pltpu.VMEM((dm, tm), F32)])
    return pl.pallas_call(
        functools.partial(_experts_kernel, rows_max=rows_max, gch=gch, kch=kch),
        grid_spec=grid_spec,
        out_shape=jax.ShapeDtypeStruct((m, dm), F32),
        compiler_params=_cparams("parallel", "arbitrary", "arbitrary"),
        name="moe_experts",
    )(seg.reshape(-1), h, gain.reshape(1, dm), col, row, wgu_t, wgu_t, wd_t)


def _ple_kernel(h_ref, g_ref, gw_ref, p_ref, pw_ref, o_ref):
    h = h_ref[...]
    gate = _sigmoid(_wdot(_rms_rows(h, g_ref[...]), gw_ref[...]))
    o_ref[...] = h + _wdot(p_ref[...], pw_ref[...]) * gate


def _ple(h, gain, gate_w, p, ple_w, *, tm):
    m, dm = h.shape
    pd = p.shape[1]
    return pl.pallas_call(
        _ple_kernel,
        grid=(m // tm,),
        in_specs=[pl.BlockSpec((tm, dm), lambda i: (i, 0)),
                  pl.BlockSpec((1, dm), lambda i: (0, 0)),
                  pl.BlockSpec((dm, dm), lambda i: (0, 0)),
                  pl.BlockSpec((tm, pd), lambda i: (i, 0)),
                  pl.BlockSpec((pd, dm), lambda i: (0, 0))],
        out_specs=pl.BlockSpec((tm, dm), lambda i: (i, 0)),
        out_shape=jax.ShapeDtypeStruct((m, dm), F32),
        compiler_params=_cparams("parallel"),
        name="ple",
    )(h, gain.reshape(1, dm), gate_w, p, ple_w)


def _band_attn_kernel(q_ref, kp_ref, kc_ref, vp_ref, vc_ref, o_ref, l_ref, *, span, tq):
    j = pl.program_id(2)
    q = q_ref[...] * (HD_B ** -0.5)
    kk = jnp.concatenate([kp_ref[...], kc_ref[...]], axis=0)
    vv = jnp.concatenate([vp_ref[...], vc_ref[...]], axis=0)
    qi = lax.broadcasted_iota(jnp.int32, (span, 2 * span), 0)
    ki = lax.broadcasted_iota(jnp.int32, (span, 2 * span), 1)
    dist = qi + span - ki
    band = (dist >= 0) & (dist <= span)
    lane = lax.broadcasted_iota(jnp.int32, (span, LANES), 1)
    for sb in range(tq // span):
        r0 = sb * span
        mask = band & (ki >= jnp.where(j > 0, 0, span)) if sb == 0 else band
        heads = [slice(h * HD_B, (h + 1) * HD_B) for h in range(HG)]
        ss = [jnp.where(mask, _dot_nt(q[r0:r0 + span, hs], kk[r0:r0 + 2 * span, hs]), NEG)
              for hs in heads]
        ms = [jnp.max(s, -1, keepdims=True) for s in ss]
        es = [jnp.exp(s - m) for s, m in zip(ss, ms)]
        dens = [jnp.sum(e, -1, keepdims=True) for e in es]
        outs = [_dot(e, vv[r0:r0 + 2 * span, hs]) * (1.0 / den)
                for e, den, hs in zip(es, dens, heads)]
        lse_tile = jnp.zeros((span, LANES), F32)
        for h in range(HG):
            lse_tile = jnp.where(lane == h, ms[h] + jnp.log(dens[h]), lse_tile)
        o_ref[r0:r0 + span, :] = jnp.concatenate(outs, axis=1)
        l_ref[r0:r0 + span, :] = lse_tile


def _band_attn(q, k, v, gi, *, batch, seq):
    win, dil = GROUPS[gi]
    span = win // dil
    n = seq // dil
    tq = min(4 * span, n)
    nb = n // tq
    sub = tq // span
    cur = lambda b, r, j: (r, b * nb + j, 0)
    prev = lambda b, r, j: (r, b * nb * sub + jnp.maximum(j * sub - 1, 0), 0)
    return pl.pallas_call(
        functools.partial(_band_attn_kernel, span=span, tq=tq),
        grid=(batch, dil, nb),
        in_specs=[pl.BlockSpec((None, tq, GW), cur),
                  pl.BlockSpec((None, span, GW), prev),
                  pl.BlockSpec((None, tq, GW), cur),
                  pl.BlockSpec((None, span, GW), prev),
                  pl.BlockSpec((None, tq, GW), cur)],
        out_specs=[pl.BlockSpec((None, tq, GW), cur),
                   pl.BlockSpec((None, tq, LANES), cur)],
        out_shape=[jax.ShapeDtypeStruct((dil, batch * n, GW), F32),
                   jax.ShapeDtypeStruct((dil, batch * n, LANES), F32)],
        compiler_params=_cparams("parallel", "parallel", "arbitrary"),
        name=f"band_attn_g{gi}",
    )(q, k, k, v, v)


def _merge_out_kernel(o0_ref, o1_ref, o2_ref, l0_ref, l1_ref, l2_ref, w_ref, r_ref, o_ref,
                      o_scr, l_scr, *, tm):
    for gi, (o_ph, l_ph) in enumerate(((o1_ref, l1_ref), (o2_ref, l2_ref))):
        d = o_ph.shape[0]
        for r in range(d):
            rows = pl.ds(r, tm // d, stride=d)
            l_scr[gi, rows, :] = l_ph[r]
            for c in range(GW // LANES):
                o_scr[gi, c, rows, :] = o_ph[r, :, c * LANES:(c + 1) * LANES]
    ls = [l0_ref[0], l_scr[0], l_scr[1]]

    def o_cols(g, c):
        if g == 0:
            return o0_ref[0, :, c * LANES:(c + 1) * LANES]
        return o_scr[g - 1, c]

    m = jnp.maximum(jnp.maximum(ls[0], ls[1]), ls[2])
    es = [jnp.exp(l - m) for l in ls]
    inv = 1.0 / (es[0] + es[1] + es[2])
    t = ls[0].shape[0]
    lo = lax.broadcasted_iota(jnp.int32, (t, LANES), 1) < HD_B
    cols = []
    for c in range(GW // LANES):
        acc = None
        for g in range(N_GROUPS):
            wt = es[g] * inv
            wexp = jnp.where(lo, wt[:, 2 * c:2 * c + 1], wt[:, 2 * c + 1:2 * c + 2])
            term = wexp * o_cols(g, c)
            acc = term if acc is None else acc + term
        cols.append(acc)
    o = jnp.concatenate(cols, axis=1).astype(BF16)
    o_ref[...] = r_ref[...] + jnp.dot(o, w_ref[...], preferred_element_type=F32)


def _merge_out(outs, lses, w, res, *, tm):
    m, dm = res.shape
    ph_spec = lambda a: pl.BlockSpec((a.shape[0], tm // a.shape[0], a.shape[2]),
                                     lambda i: (0, i, 0))
    return pl.pallas_call(
        functools.partial(_merge_out_kernel, tm=tm),
        grid=(m // tm,),
        in_specs=[ph_spec(a) for a in outs] + [ph_spec(a) for a in lses] +
                 [pl.BlockSpec((GW, dm), lambda i: (0, 0)),
                  pl.BlockSpec((tm, dm), lambda i: (i, 0))],
        out_specs=pl.BlockSpec((tm, dm), lambda i: (i, 0)),
        out_shape=jax.ShapeDtypeStruct((m, dm), F32),
        scratch_shapes=[pltpu.VMEM((N_GROUPS - 1, GW // LANES, tm, LANES), F32),
                        pltpu.VMEM((N_GROUPS - 1, tm, LANES), F32)],
        compiler_params=_cparams("parallel"),
        name="merge_out",
    )(*outs, *lses, w, res)


def _gather_attn_kernel(q_ref, kvn_ref, c0_ref, c1_ref, c2_ref, o_ref):
    caches = [c0_ref, c1_ref, c2_ref]
    outs, lses = [], []
    for g, (_, dil) in enumerate(GROUPS):
        q = q_ref[g] * (HD_B ** -0.5)
        kn, vn = kvn_ref[0, g], kvn_ref[1, g]
        kc, vc = caches[g][0], caches[g][1]
        rows = kc.shape[-1]
        s = jnp.sum(kc * q, 1, keepdims=True)
        row = lax.broadcasted_iota(jnp.int32, (1, 1, rows), 2)
        s = jnp.where((row & (dil - 1)) == 0, s, NEG)
        s_new = jnp.sum(kn * q, 1, keepdims=True)
        m = jnp.maximum(jnp.max(s, 2, keepdims=True), s_new)
        e = jnp.exp(s - m)
        e_new = jnp.exp(s_new - m)
        den = jnp.sum(e, 2, keepdims=True) + e_new
        outs.append((jnp.sum(e * vc, 2, keepdims=True) + e_new * vn) / den)
        lses.append(m + jnp.log(den))
    m = jnp.maximum(jnp.maximum(lses[0], lses[1]), lses[2])
    es = [jnp.exp(l - m) for l in lses]
    o_ref[...] = (es[0] * outs[0] + es[1] * outs[1] + es[2] * outs[2]) / (es[0] + es[1] + es[2])


def _gather_attn(q, kv_new, caches):
    nb = q.shape[0]
    span = GROUPS[0][0] // GROUPS[0][1]
    views = []
    for (win, dil), c in zip(GROUPS, caches):
        lb = c.shape[1]
        assert lb == win and lb // dil == span, "window buffer must hold the full window"
        assert dil & (dil - 1) == 0, "dilations are powers of two"
        views.append(jnp.transpose(c, (0, 2, 3, 4, 1)))
    return pl.pallas_call(
        _gather_attn_kernel,
        grid=(nb,),
        in_specs=[pl.BlockSpec((None, N_GROUPS, HG, HD_B, 1), lambda b: (b, 0, 0, 0, 0)),
                  pl.BlockSpec((None, 2, N_GROUPS, HG, HD_B, 1), lambda b: (b, 0, 0, 0, 0, 0))] +
                 [pl.BlockSpec((None, 2, HG, HD_B, v.shape[-1]), lambda b: (b, 0, 0, 0, 0))
                  for v in views],
        out_specs=pl.BlockSpec((None, HG, HD_B, 1), lambda b: (b, 0, 0, 0)),
        out_shape=jax.ShapeDtypeStruct((nb, HG, HD_B, 1), F32),
        compiler_params=_cparams("parallel"),
        name="gather_attn",
    )(q, kv_new, *views)


def _transpose_cast_kernel(x_ref, o_ref):
    o_ref[...] = x_ref[...].T.astype(BF16)


def _transpose_cast(w, *, tk, tn):
    ne, k, n = w.shape
    return pl.pallas_call(
        _transpose_cast_kernel,
        grid=(ne, k // tk, n // tn),
        in_specs=[pl.BlockSpec((None, tk, tn), lambda e, i, j: (e, i, j))],
        out_specs=pl.BlockSpec((None, tn, tk), lambda e, i, j: (e, j, i)),
        out_shape=jax.ShapeDtypeStruct((ne, n, k), BF16),
        compiler_params=_cparams("parallel", "parallel", "parallel"),
        name="transpose_cast",
    )(w)


def _prep_weights(a_w_in, a_A_log, a_dt_bias, a_w_out, w_kv, b_w_q, b_w_out, dense_w_gu,
                  dense_w_down, moe_router, moe_w_gu, moe_w_down, ple_w, ple_gate_w, k_norm,
                  b_q_norm):
    d_model, a_in = a_w_in.shape[1:]
    a_in_pad = -(-a_in // LANES) * LANES
    wf = {
        'a_w_in': jnp.pad(a_w_in[0], ((0, 0), (0, a_in_pad - a_in))),
        'a_w_out': a_w_out[0], 'w_kv': w_kv, 'b_w_q': b_w_q[0], 'b_w_out': b_w_out[0],
        'dense_w_gu': dense_w_gu[0], 'dense_w_down': dense_w_down[0],
        'router': jnp.pad(moe_router[0], ((0, 0), (0, LANES - moe_router.shape[2]))),
        'ple_w': ple_w, 'ple_gate_w': ple_gate_w,
    }
    shared = {}
    hp = jnp.stack([a_A_log[0], a_dt_bias[0]])
    shared['a_hp'] = jnp.pad(hp, ((0, 0), (H_A, LANES - 2 * H_A)))
    shared['moe_wgu_t'] = _transpose_cast(moe_w_gu[0], tk=d_model, tn=1024)
    shared['moe_wd_t'] = _transpose_cast(moe_w_down[0], tk=896, tn=d_model)
    shared['k_gain'] = jnp.tile(k_norm, HG).reshape(1, GW)
    shared['q_gain'] = jnp.tile(b_q_norm[0], HG).reshape(1, GW)
    w_prompt = dict(shared, **{k: v.astype(BF16) for k, v in wf.items()})
    w_sample = dict(shared, **wf)
    return w_prompt, w_sample


def _layer0(x, p0, w, P, *, tm, in_tn, mixer):
    proj = _norm_mm(x, P['a_norm'][0], w['a_w_in'], tm=tm, tn=in_tn)
    og, s_new = mixer(proj)
    h = _mm_res(og, w['a_w_out'], x, tm=tm)
    h = _ffn(h, P['ffn_norm'][0], w['dense_w_gu'], w['dense_w_down'], tm=tm, tf=512)
    h = _ple(h, P['ple_norm'][0], w['ple_gate_w'][0], p0, w['ple_w'][0], tm=tm)
    return h, proj, s_new


def _layer1_tail(h, p1, w, P, *, tm):
    m = h.shape[0]
    tmr = max(tm, LANES)
    if m % tmr:
        h = jnp.pad(h, ((0, tmr - m % tmr), (0, 0)))
    col, row, seg = _route(h, P['ffn_norm'][1], w['router'], tm=tmr)
    h = _experts(h, P['ffn_norm'][1], col, row, seg, w['moe_wgu_t'], w['moe_wd_t'],
                 tm=tmr, tf=896)[:m]
    return _ple(h, P['ple_norm'][1], w['ple_gate_w'][1], p1, w['ple_w'][1], tm=tm)


def kernel(x_prompt, x_sample, p_prompt, p_sample, state_conv, state_delta, cache_kv_w128, cache_kv_w512, cache_kv_w2048, a_norm, a_w_in, a_conv_w, a_A_log, a_dt_bias, a_out_norm, a_w_out, kv_norm, w_kv, k_norm, b_norm, b_w_q, b_q_norm, b_w_out, ffn_norm, dense_w_gu, dense_w_down, moe_router, moe_w_gu, moe_w_down, ple_w, ple_norm, ple_gate_w):
    assert a_w_in.shape[0] == 1 and b_w_q.shape[0] == 1, "one mixer of each kind"
    bp, sp, dm = x_prompt.shape
    bs, ls, _ = x_sample.shape
    assert ls == 1, "sample group decodes one token per sequence"
    qkv_w = 3 * H_A * DK_A
    P = dict(a_norm=a_norm, ffn_norm=ffn_norm, ple_norm=ple_norm)
    w, ws = _prep_weights(a_w_in, a_A_log, a_dt_bias, a_w_out, w_kv, b_w_q, b_w_out, dense_w_gu,
                          dense_w_down, moe_router, moe_w_gu, moe_w_down, ple_w, ple_gate_w,
                          k_norm, b_q_norm)
    a_in_pad = w['a_w_in'].shape[1]
    in_tn = a_in_pad // 3 if a_in_pad % (3 * LANES) == 0 else LANES
    conv_w = a_conv_w[0]
    out_norm = a_out_norm[0].reshape(1, DK_A)

    mp = bp * sp
    tm = min(1024, sp)
    xp = x_prompt.reshape(mp, dm)
    conv0 = jnp.zeros((bp, CONV_W - 1, qkv_w), F32)
    s0 = jnp.zeros((bp, H_A, DK_A, DK_A), F32)
    h, proj, delta_p = _layer0(
        xp, p_prompt[0].reshape(mp, -1), w, P, tm=tm, in_tn=in_tn,
        mixer=lambda pr: _gdn_prompt(pr, conv_w, w['a_hp'], out_norm, conv0, s0, batch=bp, seq=sp))
    conv_p = proj.reshape(bp, sp, -1)[:, sp - (CONV_W - 1):, :qkv_w][None]

    cos, sin = _rope_tables(jnp.arange(sp, dtype=jnp.int32))
    dils = [d for _, d in GROUPS]
    tmp = min(512, sp)
    kv, *kv_ph = _proj_rope(h, kv_norm, w['w_kv'], w['k_gain'], cos, sin, tm=tmp, n_rope=N_GROUPS,
                            natural=True, dils=dils + dils)
    q_ph = _proj_rope(h, b_norm[0], w['b_w_q'], w['q_gain'], cos, sin, tm=tmp, n_rope=N_GROUPS,
                      natural=False, dils=dils)
    outs, lses = [], []
    for gi in range(N_GROUPS):
        o, lse = _band_attn(q_ph[gi], kv_ph[gi], kv_ph[N_GROUPS + gi], gi, batch=bp, seq=sp)
        outs.append(o)
        lses.append(lse)
    h = _merge_out(outs, lses, w['b_w_out'], h, tm=tm)
    y_prompt = _layer1_tail(h, p_prompt[1].reshape(mp, -1), w, P, tm=tm).reshape(bp, sp, dm)
    kv3 = kv.reshape(bp, sp, 2 * N_GROUPS * GW)
    kv_p = []
    for gi, (win, _) in enumerate(GROUPS):
        rows = kv3[:, sp - min(win, sp):]
        k_g = rows[:, :, gi * GW:(gi + 1) * GW]
        v_g = rows[:, :, (N_GROUPS + gi) * GW:(N_GROUPS + gi + 1) * GW]
        kv_p.append(jnp.stack([k_g, v_g], axis=2).reshape(bp, -1, 2, HG, HD_B))

    xs = x_sample.reshape(bs, dm)
    hs, proj_s, delta_s = _layer0(
        xs, p_sample[0].reshape(bs, -1), ws, P, tm=bs, in_tn=in_tn,
        mixer=lambda pr: _gdn_step(pr, state_conv[0], conv_w, ws['a_hp'], out_norm, state_delta[0]))
    conv_s = jnp.concatenate([state_conv[0][:, 1:], proj_s[:, None, :qkv_w]], axis=1)[None]
    cos_s, sin_s = _rope_tables(jnp.full((bs,), PAST_LEN, jnp.int32))
    kv_s, = _proj_rope(hs, kv_norm, ws['w_kv'], ws['k_gain'], cos_s, sin_s, tm=bs,
                       n_rope=N_GROUPS, natural=True, dils=())
    q_s, = _proj_rope(hs, b_norm[0], ws['b_w_q'], ws['q_gain'], cos_s, sin_s, tm=bs,
                      n_rope=N_GROUPS, natural=True, dils=())
    o_s = _gather_attn(q_s.reshape(bs, N_GROUPS, HG, HD_B, 1),
                       kv_s.reshape(bs, 2, N_GROUPS, HG, HD_B, 1),
                       (cache_kv_w128, cache_kv_w512, cache_kv_w2048))
    hs = _mm_res(o_s.reshape(bs, GW), ws['b_w_out'], hs, tm=bs)
    y_sample = _layer1_tail(hs, p_sample[1].reshape(bs, -1), ws, P, tm=bs).reshape(bs, 1, dm)
    kvs5 = kv_s.reshape(bs, 1, 2, N_GROUPS, HG, HD_B)
    kv_sn = [kvs5[:, :, :, gi] for gi in range(N_GROUPS)]

    return (y_prompt, y_sample, conv_p, conv_s, delta_p[None], delta_s[None],
            kv_p[0], kv_sn[0], kv_p[1], kv_sn[1], kv_p[2], kv_sn[2])
```

```python
import functools

import jax
import jax.numpy as jnp
from jax import lax
from jax.experimental import pallas as pl
from jax.experimental.pallas import tpu as pltpu

F32 = jnp.float32
BF16 = jnp.bfloat16

EPS = 1e-6
PAST_LEN = 16384
GROUPS = ((128, 1), (512, 4), (2048, 16))
N_GROUPS = len(GROUPS)
HG = 8
HD_B = 64
ROT_DIM = HD_B // 4
ROPE_THETA = 500000.0
GW = HG * HD_B
H_A = 8
DK_A = 128
CONV_W = 4
CHUNK = 64
LANES = 128
VMEM_LIMIT = 57 * 1024 * 1024
NEG = -1e30


def _cparams(*sem):
    return pltpu.CompilerParams(dimension_semantics=sem, vmem_limit_bytes=VMEM_LIMIT)


def _rms_rows(x, gain):
    return x * lax.rsqrt(jnp.mean(x * x, -1, keepdims=True) + EPS) * gain


def _silu(x):
    return x * (1.0 / (1.0 + jnp.exp(-x)))


def _sigmoid(x):
    return 1.0 / (1.0 + jnp.exp(-x))


def _dot(a, b):
    return jnp.dot(a.astype(BF16), b.astype(BF16), preferred_element_type=F32)


def _wdot(a, w):
    dot = functools.partial(jnp.dot, preferred_element_type=F32)
    if w.dtype == BF16:
        return dot(a.astype(BF16), w)
    a = a.astype(F32)
    a_hi = a.astype(BF16)
    a_lo = (a - a_hi.astype(F32)).astype(BF16)
    w_hi = w.astype(BF16)
    w_lo = (w - w_hi.astype(F32)).astype(BF16)
    return dot(a_hi, w_hi) + (dot(a_lo, w_hi) + dot(a_hi, w_lo))


def _act_dtype(w):
    return BF16 if w.dtype == BF16 else F32


def _dot_nt(a, b):
    return lax.dot_general(a.astype(BF16), b.astype(BF16), (((1,), (1,)), ((), ())),
                           preferred_element_type=F32)


def _dot_tn(a, b):
    return lax.dot_general(a.astype(BF16), b.astype(BF16), (((0,), (0,)), ((), ())),
                           preferred_element_type=F32)


def _head_norm_rope(x, hgain, cos, sin):
    t = x.shape[0]
    lane = lax.broadcasted_iota(jnp.int32, (t, LANES), 1)
    lo = lane < HD_B
    d = lane & (HD_B - 1)
    outs = []
    for c in range(GW // LANES):
        sl = slice(c * LANES, (c + 1) * LANES)
        xb = x[:, sl]
        sq = xb * xb
        s_lo = jnp.sum(jnp.where(lo, sq, 0.0), -1, keepdims=True)
        s_hi = jnp.sum(jnp.where(lo, 0.0, sq), -1, keepdims=True)
        scale = jnp.where(lo, lax.rsqrt(s_lo * (1.0 / HD_B) + EPS),
                          lax.rsqrt(s_hi * (1.0 / HD_B) + EPS))
        yb = xb * scale * hgain[:, sl]
        half = ROT_DIM // 2
        rot = jnp.where(d < half, pltpu.roll(yb, LANES - half, 1), pltpu.roll(yb, half, 1))
        outs.append(yb * cos[:, sl] + rot * sin[:, sl])
    return jnp.concatenate(outs, axis=1)


def _norm_mm_kernel(x_ref, g_ref, w_ref, o_ref, u_ref):
    @pl.when(pl.program_id(1) == 0)
    def _():
        u_ref[...] = _rms_rows(x_ref[...], g_ref[...]).astype(u_ref.dtype)

    o_ref[...] = _wdot(u_ref[...], w_ref[...])


def _norm_mm(x, gain, w, *, tm, tn):
    m, k = x.shape
    n = w.shape[1]
    return pl.pallas_call(
        _norm_mm_kernel,
        grid=(m // tm, n // tn),
        in_specs=[pl.BlockSpec((tm, k), lambda i, j: (i, 0)),
                  pl.BlockSpec((1, k), lambda i, j: (0, 0)),
                  pl.BlockSpec((k, tn), lambda i, j: (0, j))],
        out_specs=pl.BlockSpec((tm, tn), lambda i, j: (i, j)),
        out_shape=jax.ShapeDtypeStruct((m, n), F32),
        scratch_shapes=[pltpu.VMEM((tm, k), _act_dtype(w))],
        compiler_params=_cparams("parallel", "arbitrary"),
        name="norm_mm",
    )(x, gain.reshape(1, k), w)


def _proj_rope_kernel(x_ref, g_ref, w_ref, hg_ref, cos_ref, sin_ref, *rest,
                      n_rope, natural, dils, tm):
    n_out = int(natural) + len(dils)
    outs, (slab_ref,) = rest[:n_out], rest[n_out:]
    nat_ref = outs[0] if natural else None
    ph_refs = outs[int(natural):]
    u = _rms_rows(x_ref[...], g_ref[...]).astype(_act_dtype(w_ref))
    slab = 0
    for jj in range(w_ref.shape[1] // GW):
        cols = slice(jj * GW, (jj + 1) * GW)
        y = _wdot(u, w_ref[:, cols])
        if jj < n_rope:
            y = _head_norm_rope(y, hg_ref[...], cos_ref[...], sin_ref[...])
        if natural:
            nat_ref[:, cols] = y
        if jj >= len(dils):
            continue
        d = dils[jj]
        if d == 1:
            ph_refs[jj][0] = y.astype(BF16)
            continue
        for c in range(GW // LANES):
            slab_ref[slab, c] = y[:, c * LANES:(c + 1) * LANES]
        for r in range(d):
            for c in range(GW // LANES):
                ph_refs[jj][r, :, c * LANES:(c + 1) * LANES] = (
                    slab_ref[slab, c, pl.ds(r, tm // d, stride=d), :].astype(BF16))
        slab += 1


def _proj_rope(x, gain, w, hgain, cos, sin, *, tm, n_rope, natural, dils):
    m, k = x.shape
    n = w.shape[1]
    pos_blocks = cos.shape[0] // tm
    out_specs, out_shape = [], []
    if natural:
        out_specs.append(pl.BlockSpec((tm, n), lambda i: (i, 0)))
        out_shape.append(jax.ShapeDtypeStruct((m, n), F32))
    for d in dils:
        out_specs.append(pl.BlockSpec((d, tm // d, GW), lambda i: (0, i, 0)))
        out_shape.append(jax.ShapeDtypeStruct((d, m // d, GW), BF16))
    n_slabs = max(1, sum(d > 1 for d in dils))
    return pl.pallas_call(
        functools.partial(_proj_rope_kernel, n_rope=n_rope, natural=natural, dils=tuple(dils),
                          tm=tm),
        grid=(m // tm,),
        in_specs=[pl.BlockSpec((tm, k), lambda i: (i, 0)),
                  pl.BlockSpec((1, k), lambda i: (0, 0)),
                  pl.BlockSpec((k, n), lambda i: (0, 0)),
                  pl.BlockSpec((1, GW), lambda i: (0, 0)),
                  pl.BlockSpec((tm, GW), lambda i: (i % pos_blocks, 0)),
                  pl.BlockSpec((tm, GW), lambda i: (i % pos_blocks, 0))],
        out_specs=out_specs,
        out_shape=out_shape,
        scratch_shapes=[pltpu.VMEM((n_slabs, GW // LANES, tm, LANES), F32)],
        compiler_params=_cparams("parallel"),
        name="proj_rope",
    )(x, gain.reshape(1, k), w, hgain, cos, sin)


def _rope_tables(pos):
    half = ROT_DIM // 2
    inv = ROPE_THETA ** (-jnp.arange(half, dtype=F32) * 2.0 / ROT_DIM)
    ang = pos.astype(F32)[:, None] * inv[None]
    c, s = jnp.cos(ang), jnp.sin(ang)
    n = pos.shape[0]
    cos_h = jnp.concatenate([c, c, jnp.ones((n, HD_B - ROT_DIM), F32)], 1)
    sin_h = jnp.concatenate([-s, s, jnp.zeros((n, HD_B - ROT_DIM), F32)], 1)
    return jnp.tile(cos_h, (1, HG)), jnp.tile(sin_h, (1, HG))


def _mm_res_kernel(x_ref, w_ref, r_ref, o_ref):
    o_ref[...] = r_ref[...] + _wdot(x_ref[...], w_ref[...])


def _mm_res(x, w, res, *, tm):
    m, k = x.shape
    n = w.shape[1]
    return pl.pallas_call(
        _mm_res_kernel,
        grid=(m // tm,),
        in_specs=[pl.BlockSpec((tm, k), lambda i: (i, 0)),
                  pl.BlockSpec((k, n), lambda i: (0, 0)),
                  pl.BlockSpec((tm, n), lambda i: (i, 0))],
        out_specs=pl.BlockSpec((tm, n), lambda i: (i, 0)),
        out_shape=jax.ShapeDtypeStruct((m, n), F32),
        compiler_params=_cparams("parallel"),
        name="mm_res",
    )(x, w, res)


def _unit_lower_inverses(mats):
    c = mats[0].shape[0]
    row = lax.broadcasted_iota(jnp.int32, (c, c), 0)
    col = lax.broadcasted_iota(jnp.int32, (c, c), 1)
    eye = jnp.where(row == col, 1.0, 0.0).astype(F32)
    ts = None
    b = 1
    while b < c:
        sel = ((row ^ col) < 2 * b) & ((row & b) != 0) & ((col & b) == 0)
        lows = [jnp.where(sel, a, 0.0) for a in mats]
        if ts is None:
            ts = [eye - low for low in lows]
        else:
            tl = [_dot(t, low) for t, low in zip(ts, lows)]
            ts = [t - _dot(x, t) for t, x in zip(ts, tl)]
        b *= 2
    return ts


def _gdn_head_params(ba, hp):
    beta = _sigmoid(ba)
    x = ba + hp[1:2, :]
    softplus = jnp.maximum(x, 0.0) + jnp.log(1.0 + jnp.exp(-jnp.abs(x)))
    g = -jnp.exp(hp[0:1, :]) * softplus
    return beta, g


def _gdn_kernel(qkv_ref, z_ref, ba_ref, cw_ref, hp_ref, on_ref, conv0_ref, s0_ref,
                og_ref, sout_ref, xbuf, s_scr, *, C, nch):
    n = pl.program_id(1)
    R = nch * C
    pad = 8

    @pl.when(n == 0)
    def _():
        xbuf[pad - (CONV_W - 1):pad, :] = conv0_ref[...]
        s_scr[...] = s0_ref[...]

    xbuf[pad:pad + R, :] = qkv_ref[...]

    def conv_cols(c0):
        acc = None
        for j in range(CONV_W):
            r0 = pad - (CONV_W - 1) + j
            term = xbuf[r0:r0 + R, c0:c0 + DK_A] * cw_ref[j:j + 1, c0:c0 + DK_A]
            acc = term if acc is None else acc + term
        return _silu(acc)

    beta, g = _gdn_head_params(ba_ref[...], hp_ref[...])
    rr = lax.broadcasted_iota(jnp.int32, (R, R), 0)
    rc = lax.broadcasted_iota(jnp.int32, (R, R), 1)
    blocktri = ((rr >= rc) & ((rr ^ rc) < C)).astype(F32)
    gcum = jnp.dot(blocktri, g, preferred_element_type=F32, precision=lax.Precision.HIGHEST)
    gcum_t = gcum.T
    row = lax.broadcasted_iota(jnp.int32, (C, C), 0)
    col = lax.broadcasted_iota(jnp.int32, (C, C), 1)
    incl = row >= col
    strict = row > col

    units = [(c, h) for c in range(nch) for h in range(H_A)]
    qs, ks, vs = {}, {}, {}
    for h in range(H_A):
        q = conv_cols(h * DK_A)
        k = conv_cols((H_A + h) * DK_A)
        v = conv_cols((2 * H_A + h) * DK_A)
        q = q * lax.rsqrt(jnp.sum(q * q, -1, keepdims=True) + EPS) * (DK_A ** -0.5)
        k = k * lax.rsqrt(jnp.sum(k * k, -1, keepdims=True) + EPS)
        for c in range(nch):
            rs = slice(c * C, (c + 1) * C)
            qs[c, h], ks[c, h], vs[c, h] = q[rs], k[rs], v[rs]

    bcs, gcs, decays, kbs = {}, {}, {}, {}
    for c, h in units:
        rs = slice(c * C, (c + 1) * C)
        bcs[c, h] = beta[rs, h:h + 1]
        gcs[c, h] = gcum[rs, H_A + h:H_A + h + 1]
        gr = gcum_t[H_A + h:H_A + h + 1, rs]
        decays[c, h] = jnp.exp(jnp.where(incl, gcs[c, h] - gr, NEG))
        kbs[c, h] = ks[c, h] * bcs[c, h]
    grams = {u: _dot_nt(jnp.concatenate([kbs[u], qs[u]], axis=0), ks[u]) for u in units}
    a_mats = [jnp.where(strict, grams[u][:C] * decays[u], 0.0) for u in units]
    aqks = {u: grams[u][C:] * decays[u] for u in units}
    t_mats = dict(zip(units, _unit_lower_inverses(a_mats)))
    egs = {u: jnp.exp(gcs[u]) for u in units}
    sols = {u: _dot(t_mats[u], jnp.concatenate([vs[u] * bcs[u], kbs[u] * egs[u]], axis=1))
            for u in units}

    states = [s_scr[h] for h in range(H_A)]
    for c in range(nch):
        rs = slice(c * C, (c + 1) * C)
        for h in range(H_A):
            u = (c, h)
            g_last = gcs[u][C - 1:C, :]
            ws = _dot(jnp.concatenate([sols[u][:, DK_A:], qs[u] * egs[u]], axis=0), states[h])
            v_new = sols[u][:, :DK_A] - ws[:C]
            o = ws[C:] + _dot(aqks[u], v_new)
            kd = ks[u] * jnp.exp(g_last - gcs[u])
            states[h] = states[h] * jnp.exp(g_last) + _dot_tn(kd, v_new)
            o = _rms_rows(o, on_ref[...]) * _silu(z_ref[rs, h * DK_A:(h + 1) * DK_A])
            og_ref[rs, h * DK_A:(h + 1) * DK_A] = o
    for h in range(H_A):
        s_scr[h] = states[h]

    xbuf[pad - (CONV_W - 1):pad, :] = xbuf[pad + R - (CONV_W - 1):pad + R, :]

    @pl.when(n == pl.num_programs(1) - 1)
    def _():
        sout_ref[...] = s_scr[...]


def _gdn_prompt(proj, conv_w, hp, out_norm, conv0, s0, *, batch, seq):
    C = min(CHUNK, seq)
    nch = next(n for n in (4, 2, 1) if seq % (n * C) == 0)
    R = nch * C
    nc = seq // R
    qkv_w = 3 * H_A * DK_A
    z_w = H_A * DK_A
    return pl.pallas_call(
        functools.partial(_gdn_kernel, C=C, nch=nch),
        grid=(batch, nc),
        in_specs=[pl.BlockSpec((R, qkv_w), lambda b, n: (b * nc + n, 0)),
                  pl.BlockSpec((R, z_w), lambda b, n: (b * nc + n, qkv_w // z_w)),
                  pl.BlockSpec((R, LANES), lambda b, n: (b * nc + n, (qkv_w + z_w) // LANES)),
                  pl.BlockSpec((CONV_W, qkv_w), lambda b, n: (0, 0)),
                  pl.BlockSpec((2, LANES), lambda b, n: (0, 0)),
                  pl.BlockSpec((1, DK_A), lambda b, n: (0, 0)),
                  pl.BlockSpec((None, CONV_W - 1, qkv_w), lambda b, n: (b, 0, 0)),
                  pl.BlockSpec((None, H_A, DK_A, DK_A), lambda b, n: (b, 0, 0, 0))],
        out_specs=[pl.BlockSpec((R, z_w), lambda b, n: (b * nc + n, 0)),
                   pl.BlockSpec((None, H_A, DK_A, DK_A), lambda b, n: (b, 0, 0, 0))],
        out_shape=[jax.ShapeDtypeStruct((batch * seq, z_w), F32),
                   jax.ShapeDtypeStruct((batch, H_A, DK_A, DK_A), F32)],
        scratch_shapes=[pltpu.VMEM((R + 8, qkv_w), F32),
                        pltpu.VMEM((H_A, DK_A, DK_A), F32)],
        compiler_params=_cparams("parallel", "arbitrary"),
        name="gdn_chunked",
    )(proj, proj, proj, conv_w, hp, out_norm, conv0, s0)


def _gdn_step_kernel(proj_ref, conv_ref, cw_ref, hp_ref, on_ref, s0_ref, og_ref, sout_ref, qk_scr):
    qkv_w = 3 * H_A * DK_A
    z_w = H_A * DK_A

    def conv_cols(c0):
        sl = slice(c0, c0 + DK_A)
        acc = proj_ref[:, sl] * cw_ref[CONV_W - 1:CONV_W, sl]
        for j in range(CONV_W - 1):
            acc = acc + conv_ref[j:j + 1, sl] * cw_ref[j:j + 1, sl]
        return _silu(acc)

    beta, g = _gdn_head_params(proj_ref[:, qkv_w + z_w:qkv_w + z_w + LANES], hp_ref[...])
    qk_scr[...] = jnp.zeros_like(qk_scr)
    vs = []
    for h in range(H_A):
        q = conv_cols(h * DK_A)
        k = conv_cols((H_A + h) * DK_A)
        vs.append(conv_cols((2 * H_A + h) * DK_A))
        qk_scr[H_A + h:H_A + h + 1, :] = (
            q * lax.rsqrt(jnp.sum(q * q, -1, keepdims=True) + EPS) * (DK_A ** -0.5))
        qk_scr[h:h + 1, :] = k * lax.rsqrt(jnp.sum(k * k, -1, keepdims=True) + EPS)
    qk = qk_scr[...]
    qk_t = qk.T
    for h in range(H_A):
        k_row = qk[h:h + 1, :]
        q_row = qk[H_A + h:H_A + h + 1, :]
        k_col = qk_t[:, h:h + 1]
        q_col = qk_t[:, H_A + h:H_A + h + 1]
        bh = beta[:, h:h + 1]
        eg = jnp.exp(g[:, H_A + h:H_A + h + 1])
        s = s0_ref[h]
        k_s = jnp.sum(s * k_col, 0, keepdims=True)
        q_s = jnp.sum(s * q_col, 0, keepdims=True)
        v_new = bh * (vs[h] - eg * k_s)
        o = eg * q_s + jnp.sum(q_row * k_row, -1, keepdims=True) * v_new
        sout_ref[h] = s * eg + k_col * v_new
        o = _rms_rows(o, on_ref[...]) * _silu(proj_ref[:, qkv_w + h * DK_A:qkv_w + (h + 1) * DK_A])
        og_ref[:, h * DK_A:(h + 1) * DK_A] = o


def _gdn_step(proj, conv_state, conv_w, hp, out_norm, s0):
    nb, pw = proj.shape
    qkv_w = 3 * H_A * DK_A
    z_w = H_A * DK_A
    og, s_new = pl.pallas_call(
        _gdn_step_kernel,
        grid=(nb,),
        in_specs=[pl.BlockSpec((None, 1, pw), lambda b: (b, 0, 0)),
                  pl.BlockSpec((None, CONV_W - 1, qkv_w), lambda b: (b, 0, 0)),
                  pl.BlockSpec((CONV_W, qkv_w), lambda b: (0, 0)),
                  pl.BlockSpec((2, LANES), lambda b: (0, 0)),
                  pl.BlockSpec((1, DK_A), lambda b: (0, 0)),
                  pl.BlockSpec((None, H_A, DK_A, DK_A), lambda b: (b, 0, 0, 0))],
        out_specs=[pl.BlockSpec((None, 1, z_w), lambda b: (b, 0, 0)),
                   pl.BlockSpec((None, H_A, DK_A, DK_A), lambda b: (b, 0, 0, 0))],
        out_shape=[jax.ShapeDtypeStruct((nb, 1, z_w), F32),
                   jax.ShapeDtypeStruct((nb, H_A, DK_A, DK_A), F32)],
        scratch_shapes=[pltpu.VMEM((LANES, DK_A), F32)],
        compiler_params=_cparams("parallel"),
        name="gdn_step",
    )(proj.reshape(nb, 1, pw), conv_state, conv_w, hp, out_norm, s0)
    return og.reshape(nb, z_w), s_new


def _ffn_kernel(h_ref, g_ref, wg_ref, wu_ref, wd_ref, o_ref, u_ref, acc_ref):
    f = pl.program_id(1)

    @pl.when(f == 0)
    def _():
        u_ref[...] = _rms_rows(h_ref[...], g_ref[...]).astype(u_ref.dtype)
        acc_ref[...] = jnp.zeros_like(acc_ref)

    u = u_ref[...]
    gate = _wdot(u, wg_ref[...])
    up = _wdot(u, wu_ref[...])
    acc_ref[...] += _wdot(_silu(gate) * up, wd_ref[...])

    @pl.when(f == pl.num_programs(1) - 1)
    def _():
        o_ref[...] = h_ref[...] + acc_ref[...]


def _ffn(h, gain, w_gu, w_down, *, tm, tf):
    m, dm = h.shape
    ff = w_down.shape[0]
    nf = ff // tf
    return pl.pallas_call(
        _ffn_kernel,
        grid=(m // tm, nf),
        in_specs=[pl.BlockSpec((tm, dm), lambda i, f: (i, 0)),
                  pl.BlockSpec((1, dm), lambda i, f: (0, 0)),
                  pl.BlockSpec((dm, tf), lambda i, f: (0, f)),
                  pl.BlockSpec((dm, tf), lambda i, f: (0, nf + f)),
                  pl.BlockSpec((tf, dm), lambda i, f: (f, 0))],
        out_specs=pl.BlockSpec((tm, dm), lambda i, f: (i, 0)),
        out_shape=jax.ShapeDtypeStruct((m, dm), F32),
        scratch_shapes=[pltpu.VMEM((tm, dm), _act_dtype(w_gu)),
                        pltpu.VMEM((tm, dm), F32)],
        compiler_params=_cparams("parallel", "arbitrary"),
        name="ffn_dense",
    )(h, gain.reshape(1, dm), w_gu, w_gu, w_down)


N_EXPERTS = 8
SEG_ALIGN = LANES


def _top2(logits):
    t = logits.shape[0]
    lane = lax.broadcasted_iota(jnp.int32, (t, LANES), 1)
    valid = lane < N_EXPERTS
    lg = jnp.where(valid, logits, NEG)
    mx = jnp.max(lg, -1, keepdims=True)
    e = jnp.where(valid, jnp.exp(lg - mx), 0.0)
    probs = e / jnp.sum(e, -1, keepdims=True)
    p1 = jnp.max(probs, -1, keepdims=True)
    i1 = jnp.min(jnp.where((probs == p1) & valid, lane, LANES), -1, keepdims=True)
    rest = jnp.where((lane == i1) | ~valid, -1.0, probs)
    p2 = jnp.max(rest, -1, keepdims=True)
    i2 = jnp.min(jnp.where(rest == p2, lane, LANES), -1, keepdims=True)
    tot = p1 + p2
    return i1, i2, p1 / tot, p2 / tot


def _router_kernel(h_ref, g_ref, r_ref, tri_ref, upper_ref, ut_ref, col_ref, row_ref, seg_ref):
    t = h_ref.shape[0]
    sub = tri_ref.shape[0]
    u = _rms_rows(h_ref[...], g_ref[...])
    ut_ref[...] = u.T.astype(BF16)
    i1, i2, g1, g2 = _top2(_wdot(u, r_ref[...]))
    lane = lax.broadcasted_iota(jnp.int32, (t, LANES), 1)
    sel = jnp.where((lane == i1) | (lane == i2), 1.0, 0.0)
    counts = jnp.zeros((1, LANES), F32)
    pos_parts = []
    for k in range(t // sub):
        sel_k = sel[k * sub:(k + 1) * sub]
        pos_parts.append(counts + jnp.dot(tri_ref[...], sel_k.astype(BF16),
                                          preferred_element_type=F32))
        counts = counts + jnp.sum(sel_k, 0, keepdims=True)
    pos = jnp.concatenate(pos_parts, axis=0)
    nblk = jnp.floor((counts + (SEG_ALIGN - 1)) * (1.0 / SEG_ALIGN))
    nblk8 = jnp.broadcast_to(nblk, (8, LANES))
    start8 = jnp.dot(nblk8.astype(BF16), upper_ref[...], preferred_element_type=F32)
    dest = start8[0:1] * SEG_ALIGN + pos
    d1 = jnp.sum(jnp.where(lane == i1, dest, 0.0), -1, keepdims=True)
    d2 = jnp.sum(jnp.where(lane == i2, dest, 0.0), -1, keepdims=True)
    col = jnp.where(lane == 0, d1, jnp.where(lane == 1, d2,
                    jnp.where(lane == 2, g1, jnp.where(lane == 3, g2, 0.0))))
    col_ref[...] = col
    row_ref[...] = col.T[0:8, :]
    lane8 = lane[0:8]
    seg = jnp.where(lane8 < N_EXPERTS, start8,
                    jnp.where(lane8 < 2 * N_EXPERTS, pltpu.roll(nblk8, N_EXPERTS, 1), 0.0))
    seg_ref[...] = seg[0:1].astype(jnp.int32)


def _route(h, gain, router, *, tm):
    m, dm = h.shape
    nt = m // tm
    sub = min(tm, 1024)
    tri = jnp.tril(jnp.ones((sub, sub), F32), -1).astype(BF16)
    upper = jnp.triu(jnp.ones((LANES, LANES), F32), 1).astype(BF16)
    return pl.pallas_call(
        _router_kernel,
        grid=(nt,),
        in_specs=[pl.BlockSpec((tm, dm), lambda i: (i, 0)),
                  pl.BlockSpec((1, dm), lambda i: (0, 0)),
                  pl.BlockSpec((dm, LANES), lambda i: (0, 0)),
                  pl.BlockSpec((sub, sub), lambda i: (0, 0)),
                  pl.BlockSpec((LANES, LANES), lambda i: (0, 0))],
        out_specs=[pl.BlockSpec((None, dm, tm), lambda i: (i, 0, 0)),
                   pl.BlockSpec((tm, LANES), lambda i: (i, 0)),
                   pl.BlockSpec((None, 8, tm), lambda i: (i, 0, 0)),
                   pl.BlockSpec((None, 1, LANES), lambda i: (i, 0, 0))],
        out_shape=[jax.ShapeDtypeStruct((nt, dm, tm), BF16),
                   jax.ShapeDtypeStruct((m, LANES), F32),
                   jax.ShapeDtypeStruct((nt, 8, tm), F32),
                   jax.ShapeDtypeStruct((nt, 1, LANES), jnp.int32)],
        compiler_params=_cparams("parallel"),
        name="moe_route",
    )(h, gain.reshape(1, dm), router, tri, upper)


def _one_hot_rows(row0, n, d1_row, d2_row):
    ridx = (lax.broadcasted_iota(jnp.int32, (n, d1_row.shape[1]), 0) + row0).astype(F32)
    return jnp.where((ridx == d1_row) | (ridx == d2_row), 1.0, 0.0).astype(BF16)


def _experts_kernel(seg_ref, ut_ref, col_ref, row_ref, wg_ref, wu_ref, wd_ref, o_ref,
                    xs_scr, acc_scr):
    i, e, f = pl.program_id(0), pl.program_id(1), pl.program_id(2)
    last_f = pl.num_programs(2) - 1
    t = ut_ref.shape[1]
    start = seg_ref[i * LANES + e]
    nblk = seg_ref[i * LANES + N_EXPERTS + e]

    @pl.when((e == 0) & (f == 0))
    def _():
        o_ref[...] = jnp.zeros_like(o_ref)

    def gather(lb, nb):
        w = nb * LANES
        col = col_ref[...]
        ridx = (lax.broadcasted_iota(jnp.int32, (t, w), 1) + (start + lb) * LANES).astype(F32)
        p_t = jnp.where((ridx == col[:, 0:1]) | (ridx == col[:, 1:2]), 1.0, 0.0).astype(BF16)
        x_t = jnp.dot(ut_ref[...], p_t, preferred_element_type=F32).astype(BF16)
        for k in range(nb):
            xs_scr[lb + k] = x_t[:, k * LANES:(k + 1) * LANES]
            acc_scr[lb + k] = jnp.zeros(acc_scr.shape[1:], F32)

    def swiglu(lb, nb):
        x_t = jnp.concatenate([xs_scr[lb + k] for k in range(nb)], axis=1)
        gate_t = jnp.dot(wg_ref[...], x_t, preferred_element_type=F32)
        up_t = jnp.dot(wu_ref[...], x_t, preferred_element_type=F32)
        act_t = (_silu(gate_t) * up_t).astype(BF16)
        down_t = jnp.dot(wd_ref[...], act_t, preferred_element_type=F32)
        for k in range(nb):
            acc_scr[lb + k] += down_t[:, k * LANES:(k + 1) * LANES]

    def combine(lb, nb):
        w = nb * LANES
        col = col_ref[...]
        row0 = (start + lb) * LANES
        ridx = (lax.broadcasted_iota(jnp.int32, (t, w), 1) + row0).astype(F32)
        gs = jnp.sum(jnp.where(ridx == col[:, 0:1], col[:, 2:3], 0.0) +
                     jnp.where(ridx == col[:, 1:2], col[:, 3:4], 0.0), 0, keepdims=True)
        acc_t = jnp.concatenate([acc_scr[lb + k] for k in range(nb)], axis=1)
        p = _one_hot_rows(row0, w, row_ref[0:1, :], row_ref[1:2, :])
        o_ref[...] += jnp.dot((acc_t * gs).astype(BF16), p, preferred_element_type=F32)

    def for_blocks(*stages):
        def run(lb, nb):
            for stage in stages:
                stage(lb, nb)

        n4 = nblk // 4

        def body(j, carry):
            run(4 * j, 4)
            return carry

        lax.fori_loop(0, n4, body, 0)
        rem = nblk - 4 * n4

        @pl.when((rem & 2) != 0)
        def _():
            run(4 * n4, 2)

        @pl.when((rem & 1) != 0)
        def _():
            run(4 * n4 + (rem & 2), 1)

    @pl.when(f == 0)
    def _():
        for_blocks(gather, swiglu)

    @pl.when((f > 0) & (f < last_f))
    def _():
        for_blocks(swiglu)

    @pl.when(f == last_f)
    def _():
        for_blocks(swiglu, combine)


def _experts(ut, col, row, seg, wgu_t, wd_t, *, tf):
    nt, dm, tm = ut.shape
    ne, _, ff = wd_t.shape
    nf = ff // tf
    assert nf >= 2, "first and last F block are distinct steps"
    grid_spec = pltpu.PrefetchScalarGridSpec(
        num_scalar_prefetch=1,
        grid=(nt, ne, nf),
        in_specs=[pl.BlockSpec((None, dm, tm), lambda i, e, f, s: (i, 0, 0),
                               pipeline_mode=pl.Buffered(1)),
                  pl.BlockSpec((tm, LANES), lambda i, e, f, s: (i, 0),
                               pipeline_mode=pl.Buffered(1)),
                  pl.BlockSpec((None, 8, tm), lambda i, e, f, s: (i, 0, 0)),
                  pl.BlockSpec((None, tf, dm), lambda i, e, f, s: (e, f, 0)),
                  pl.BlockSpec((None, tf, dm), lambda i, e, f, s: (e, nf + f, 0)),
                  pl.BlockSpec((None, dm, tf), lambda i, e, f, s: (e, 0, f))],
        out_specs=pl.BlockSpec((dm, tm), lambda i, e, f, s: (0, i)),
        scratch_shapes=[pltpu.VMEM((tm // LANES, dm, LANES), BF16),
                        pltpu.VMEM((tm // LANES, dm, LANES), F32)])
    return pl.pallas_call(
        _experts_kernel,
        grid_spec=grid_spec,
        out_shape=jax.ShapeDtypeStruct((dm, nt * tm), F32),
        compiler_params=_cparams("parallel", "arbitrary", "arbitrary"),
        name="moe_experts",
    )(seg.reshape(-1), ut, col, row, wgu_t, wgu_t, wd_t)


def _ple_kernel(h_ref, g_ref, gw_ref, p_ref, pw_ref, *rest):
    o_ref = rest[-1]
    h = h_ref[...]
    if len(rest) == 2:
        h = h + rest[0][...].T
    gate = _sigmoid(_wdot(_rms_rows(h, g_ref[...]), gw_ref[...]))
    o_ref[...] = h + _wdot(p_ref[...], pw_ref[...]) * gate


def _ple(h, gain, gate_w, p, ple_w, *, tm, y_t=None):
    m, dm = h.shape
    pd = p.shape[1]
    in_specs = [pl.BlockSpec((tm, dm), lambda i: (i, 0)),
                pl.BlockSpec((1, dm), lambda i: (0, 0)),
                pl.BlockSpec((dm, dm), lambda i: (0, 0)),
                pl.BlockSpec((tm, pd), lambda i: (i, 0)),
                pl.BlockSpec((pd, dm), lambda i: (0, 0))]
    args = [h, gain.reshape(1, dm), gate_w, p, ple_w]
    if y_t is not None:
        in_specs.append(pl.BlockSpec((dm, tm), lambda i: (0, i)))
        args.append(y_t)
    return pl.pallas_call(
        _ple_kernel,
        grid=(m // tm,),
        in_specs=in_specs,
        out_specs=pl.BlockSpec((tm, dm), lambda i: (i, 0)),
        out_shape=jax.ShapeDtypeStruct((m, dm), F32),
        compiler_params=_cparams("parallel"),
        name="ple",
    )(*args)


def _band_attn_kernel(q_ref, kp_ref, kc_ref, vp_ref, vc_ref, o_ref, l_ref, *, span, tq):
    j = pl.program_id(2)
    q = q_ref[...] * (HD_B ** -0.5)
    kk = jnp.concatenate([kp_ref[...], kc_ref[...]], axis=0)
    vv = jnp.concatenate([vp_ref[...], vc_ref[...]], axis=0)
    qi = lax.broadcasted_iota(jnp.int32, (span, 2 * span), 0)
    ki = lax.broadcasted_iota(jnp.int32, (span, 2 * span), 1)
    dist = qi + span - ki
    band = (dist >= 0) & (dist <= span)
    lane = lax.broadcasted_iota(jnp.int32, (span, LANES), 1)
    for sb in range(tq // span):
        r0 = sb * span
        mask = band & (ki >= jnp.where(j > 0, 0, span)) if sb == 0 else band
        heads = [slice(h * HD_B, (h + 1) * HD_B) for h in range(HG)]
        ss = [jnp.where(mask, _dot_nt(q[r0:r0 + span, hs], kk[r0:r0 + 2 * span, hs]), NEG)
              for hs in heads]
        ms = [jnp.max(s, -1, keepdims=True) for s in ss]
        es = [jnp.exp(s - m) for s, m in zip(ss, ms)]
        dens = [jnp.sum(e, -1, keepdims=True) for e in es]
        outs = [_dot(e, vv[r0:r0 + 2 * span, hs]) * (1.0 / den)
                for e, den, hs in zip(es, dens, heads)]
        lse_tile = jnp.zeros((span, LANES), F32)
        for h in range(HG):
            lse_tile = jnp.where(lane == h, ms[h] + jnp.log(dens[h]), lse_tile)
        o_ref[r0:r0 + span, :] = jnp.concatenate(outs, axis=1)
        l_ref[r0:r0 + span, :] = lse_tile


def _band_attn(q, k, v, gi, *, batch, seq):
    win, dil = GROUPS[gi]
    span = win // dil
    n = seq // dil
    tq = min(4 * span, n)
    nb = n // tq
    sub = tq // span
    cur = lambda b, r, j: (r, b * nb + j, 0)
    prev = lambda b, r, j: (r, b * nb * sub + jnp.maximum(j * sub - 1, 0), 0)
    return pl.pallas_call(
        functools.partial(_band_attn_kernel, span=span, tq=tq),
        grid=(batch, dil, nb),
        in_specs=[pl.BlockSpec((None, tq, GW), cur),
                  pl.BlockSpec((None, span, GW), prev),
                  pl.BlockSpec((None, tq, GW), cur),
                  pl.BlockSpec((None, span, GW), prev),
                  pl.BlockSpec((None, tq, GW), cur)],
        out_specs=[pl.BlockSpec((None, tq, GW), cur),
                   pl.BlockSpec((None, tq, LANES), cur)],
        out_shape=[jax.ShapeDtypeStruct((dil, batch * n, GW), F32),
                   jax.ShapeDtypeStruct((dil, batch * n, LANES), F32)],
        compiler_params=_cparams("parallel", "parallel", "arbitrary"),
        name=f"band_attn_g{gi}",
    )(q, k, k, v, v)


def _merge_out_kernel(o0_ref, o1_ref, o2_ref, l0_ref, l1_ref, l2_ref, w_ref, r_ref, o_ref,
                      o_scr, l_scr, *, tm):
    for gi, (o_ph, l_ph) in enumerate(((o1_ref, l1_ref), (o2_ref, l2_ref))):
        d = o_ph.shape[0]
        for r in range(d):
            rows = pl.ds(r, tm // d, stride=d)
            l_scr[gi, rows, :] = l_ph[r]
            for c in range(GW // LANES):
                o_scr[gi, c, rows, :] = o_ph[r, :, c * LANES:(c + 1) * LANES]
    ls = [l0_ref[0], l_scr[0], l_scr[1]]

    def o_cols(g, c):
        if g == 0:
            return o0_ref[0, :, c * LANES:(c + 1) * LANES]
        return o_scr[g - 1, c]

    m = jnp.maximum(jnp.maximum(ls[0], ls[1]), ls[2])
    es = [jnp.exp(l - m) for l in ls]
    inv = 1.0 / (es[0] + es[1] + es[2])
    t = ls[0].shape[0]
    lo = lax.broadcasted_iota(jnp.int32, (t, LANES), 1) < HD_B
    cols = []
    for c in range(GW // LANES):
        acc = None
        for g in range(N_GROUPS):
            wt = es[g] * inv
            wexp = jnp.where(lo, wt[:, 2 * c:2 * c + 1], wt[:, 2 * c + 1:2 * c + 2])
            term = wexp * o_cols(g, c)
            acc = term if acc is None else acc + term
        cols.append(acc)
    o = jnp.concatenate(cols, axis=1).astype(BF16)
    o_ref[...] = r_ref[...] + jnp.dot(o, w_ref[...], preferred_element_type=F32)


def _merge_out(outs, lses, w, res, *, tm):
    m, dm = res.shape
    ph_spec = lambda a: pl.BlockSpec((a.shape[0], tm // a.shape[0], a.shape[2]),
                                     lambda i: (0, i, 0))
    return pl.pallas_call(
        functools.partial(_merge_out_kernel, tm=tm),
        grid=(m // tm,),
        in_specs=[ph_spec(a) for a in outs] + [ph_spec(a) for a in lses] +
                 [pl.BlockSpec((GW, dm), lambda i: (0, 0)),
                  pl.BlockSpec((tm, dm), lambda i: (i, 0))],
        out_specs=pl.BlockSpec((tm, dm), lambda i: (i, 0)),
        out_shape=jax.ShapeDtypeStruct((m, dm), F32),
        scratch_shapes=[pltpu.VMEM((N_GROUPS - 1, GW // LANES, tm, LANES), F32),
                        pltpu.VMEM((N_GROUPS - 1, tm, LANES), F32)],
        compiler_params=_cparams("parallel"),
        name="merge_out",
    )(*outs, *lses, w, res)


def _gather_attn_kernel(q_ref, kvn_ref, c0_ref, c1_ref, c2_ref, o_ref):
    caches = [c0_ref, c1_ref, c2_ref]
    outs, lses = [], []
    for g, (_, dil) in enumerate(GROUPS):
        q = q_ref[g] * (HD_B ** -0.5)
        kn, vn = kvn_ref[0, g], kvn_ref[1, g]
        kc, vc = caches[g][0], caches[g][1]
        rows = kc.shape[-1]
        s = jnp.sum(kc * q, 1, keepdims=True)
        row = lax.broadcasted_iota(jnp.int32, (1, 1, rows), 2)
        s = jnp.where((row & (dil - 1)) == 0, s, NEG)
        s_new = jnp.sum(kn * q, 1, keepdims=True)
        m = jnp.maximum(jnp.max(s, 2, keepdims=True), s_new)
        e = jnp.exp(s - m)
        e_new = jnp.exp(s_new - m)
        den = jnp.sum(e, 2, keepdims=True) + e_new
        outs.append((jnp.sum(e * vc, 2, keepdims=True) + e_new * vn) / den)
        lses.append(m + jnp.log(den))
    m = jnp.maximum(jnp.maximum(lses[0], lses[1]), lses[2])
    es = [jnp.exp(l - m) for l in lses]
    o_ref[...] = (es[0] * outs[0] + es[1] * outs[1] + es[2] * outs[2]) / (es[0] + es[1] + es[2])


def _gather_attn(q, kv_new, caches):
    nb = q.shape[0]
    span = GROUPS[0][0] // GROUPS[0][1]
    views = []
    for (win, dil), c in zip(GROUPS, caches):
        lb = c.shape[1]
        assert lb == win and lb // dil == span, "window buffer must hold the full window"
        assert dil & (dil - 1) == 0, "dilations are powers of two"
        views.append(jnp.transpose(c, (0, 2, 3, 4, 1)))
    return pl.pallas_call(
        _gather_attn_kernel,
        grid=(nb,),
        in_specs=[pl.BlockSpec((None, N_GROUPS, HG, HD_B, 1), lambda b: (b, 0, 0, 0, 0)),
                  pl.BlockSpec((None, 2, N_GROUPS, HG, HD_B, 1), lambda b: (b, 0, 0, 0, 0, 0))] +
                 [pl.BlockSpec((None, 2, HG, HD_B, v.shape[-1]), lambda b: (b, 0, 0, 0, 0))
                  for v in views],
        out_specs=pl.BlockSpec((None, HG, HD_B, 1), lambda b: (b, 0, 0, 0)),
        out_shape=jax.ShapeDtypeStruct((nb, HG, HD_B, 1), F32),
        compiler_params=_cparams("parallel"),
        name="gather_attn",
    )(q, kv_new, *views)


def _transpose_cast_kernel(x_ref, o_ref):
    o_ref[...] = x_ref[...].T.astype(BF16)


def _transpose_cast(w, *, tk, tn):
    ne, k, n = w.shape
    return pl.pallas_call(
        _transpose_cast_kernel,
        grid=(ne, k // tk, n // tn),
        in_specs=[pl.BlockSpec((None, tk, tn), lambda e, i, j: (e, i, j))],
        out_specs=pl.BlockSpec((None, tn, tk), lambda e, i, j: (e, j, i)),
        out_shape=jax.ShapeDtypeStruct((ne, n, k), BF16),
        compiler_params=_cparams("parallel", "parallel", "parallel"),
        name="transpose_cast",
    )(w)


def _prep_weights(a_w_in, a_A_log, a_dt_bias, a_w_out, w_kv, b_w_q, b_w_out, dense_w_gu,
                  dense_w_down, moe_router, moe_w_gu, moe_w_down, ple_w, ple_gate_w, k_norm,
                  b_q_norm):
    d_model, a_in = a_w_in.shape[1:]
    a_in_pad = -(-a_in // LANES) * LANES
    wf = {
        'a_w_in': jnp.pad(a_w_in[0], ((0, 0), (0, a_in_pad - a_in))),
        'a_w_out': a_w_out[0], 'w_kv': w_kv, 'b_w_q': b_w_q[0], 'b_w_out': b_w_out[0],
        'dense_w_gu': dense_w_gu[0], 'dense_w_down': dense_w_down[0],
        'router': jnp.pad(moe_router[0], ((0, 0), (0, LANES - moe_router.shape[2]))),
        'ple_w': ple_w, 'ple_gate_w': ple_gate_w,
    }
    shared = {}
    hp = jnp.stack([a_A_log[0], a_dt_bias[0]])
    shared['a_hp'] = jnp.pad(hp, ((0, 0), (H_A, LANES - 2 * H_A)))
    shared['moe_wgu_t'] = _transpose_cast(moe_w_gu[0], tk=d_model, tn=1024)
    shared['moe_wd_t'] = _transpose_cast(moe_w_down[0], tk=896, tn=d_model)
    shared['k_gain'] = jnp.tile(k_norm, HG).reshape(1, GW)
    shared['q_gain'] = jnp.tile(b_q_norm[0], HG).reshape(1, GW)
    w_prompt = dict(shared, **{k: v.astype(BF16) for k, v in wf.items()})
    w_sample = dict(shared, **wf)
    return w_prompt, w_sample


def _layer0(x, p0, w, P, *, tm, in_tn, mixer):
    proj = _norm_mm(x, P['a_norm'][0], w['a_w_in'], tm=tm, tn=in_tn)
    og, s_new = mixer(proj)
    h = _mm_res(og, w['a_w_out'], x, tm=tm)
    h = _ffn(h, P['ffn_norm'][0], w['dense_w_gu'], w['dense_w_down'], tm=tm, tf=512)
    h = _ple(h, P['ple_norm'][0], w['ple_gate_w'][0], p0, w['ple_w'][0], tm=tm)
    return h, proj, s_new


def _layer1_tail(h, p1, w, P, *, tm):
    m = h.shape[0]
    tmr = next((t for t in (2048, 1024, 512, 256, LANES) if m % t == 0), LANES)
    hp = jnp.pad(h, ((0, -m % tmr), (0, 0)))
    ut, col, row, seg = _route(hp, P['ffn_norm'][1], w['router'], tm=tmr)
    y_t = _experts(ut, col, row, seg, w['moe_wgu_t'], w['moe_wd_t'], tf=896)
    if m % tmr:
        p1, tm = jnp.pad(p1, ((0, -m % tmr), (0, 0))), tmr
    return _ple(hp, P['ple_norm'][1], w['ple_gate_w'][1], p1, w['ple_w'][1], tm=tm, y_t=y_t)[:m]


def kernel(x_prompt, x_sample, p_prompt, p_sample, state_conv, state_delta, cache_kv_w128, cache_kv_w512, cache_kv_w2048, a_norm, a_w_in, a_conv_w, a_A_log, a_dt_bias, a_out_norm, a_w_out, kv_norm, w_kv, k_norm, b_norm, b_w_q, b_q_norm, b_w_out, ffn_norm, dense_w_gu, dense_w_down, moe_router, moe_w_gu, moe_w_down, ple_w, ple_norm, ple_gate_w):
    assert a_w_in.shape[0] == 1 and b_w_q.shape[0] == 1, "one mixer of each kind"
    bp, sp, dm = x_prompt.shape
    bs, ls, _ = x_sample.shape
    assert ls == 1, "sample group decodes one token per sequence"
    qkv_w = 3 * H_A * DK_A
    P = dict(a_norm=a_norm, ffn_norm=ffn_norm, ple_norm=ple_norm)
    w, ws = _prep_weights(a_w_in, a_A_log, a_dt_bias, a_w_out, w_kv, b_w_q, b_w_out, dense_w_gu,
                          dense_w_down, moe_router, moe_w_gu, moe_w_down, ple_w, ple_gate_w,
                          k_norm, b_q_norm)
    a_in_pad = w['a_w_in'].shape[1]
    in_tn = a_in_pad // 3 if a_in_pad % (3 * LANES) == 0 else LANES
    conv_w = a_conv_w[0]
    out_norm = a_out_norm[0].reshape(1, DK_A)

    mp = bp * sp
    tm = min(1024, sp)
    xp = x_prompt.reshape(mp, dm)
    conv0 = jnp.zeros((bp, CONV_W - 1, qkv_w), F32)
    s0 = jnp.zeros((bp, H_A, DK_A, DK_A), F32)
    h, proj, delta_p = _layer0(
        xp, p_prompt[0].reshape(mp, -1), w, P, tm=tm, in_tn=in_tn,
        mixer=lambda pr: _gdn_prompt(pr, conv_w, w['a_hp'], out_norm, conv0, s0, batch=bp, seq=sp))
    conv_p = proj.reshape(bp, sp, -1)[:, sp - (CONV_W - 1):, :qkv_w][None]

    cos, sin = _rope_tables(jnp.arange(sp, dtype=jnp.int32))
    dils = [d for _, d in GROUPS]
    tmp = min(512, sp)
    kv, *kv_ph = _proj_rope(h, kv_norm, w['w_kv'], w['k_gain'], cos, sin, tm=tmp, n_rope=N_GROUPS,
                            natural=True, dils=dils + dils)
    q_ph = _proj_rope(h, b_norm[0], w['b_w_q'], w['q_gain'], cos, sin, tm=tmp, n_rope=N_GROUPS,
                      natural=False, dils=dils)
    outs, lses = [], []
    for gi in range(N_GROUPS):
        o, lse = _band_attn(q_ph[gi], kv_ph[gi], kv_ph[N_GROUPS + gi], gi, batch=bp, seq=sp)
        outs.append(o)
        lses.append(lse)
    h = _merge_out(outs, lses, w['b_w_out'], h, tm=tm)
    y_prompt = _layer1_tail(h, p_prompt[1].reshape(mp, -1), w, P, tm=tm).reshape(bp, sp, dm)
    kv3 = kv.reshape(bp, sp, 2 * N_GROUPS * GW)
    kv_p = []
    for gi, (win, _) in enumerate(GROUPS):
        rows = kv3[:, sp - min(win, sp):]
        k_g = rows[:, :, gi * GW:(gi + 1) * GW]
        v_g = rows[:, :, (N_GROUPS + gi) * GW:(N_GROUPS + gi + 1) * GW]
        kv_p.append(jnp.stack([k_g, v_g], axis=2).reshape(bp, -1, 2, HG, HD_B))

    xs = x_sample.reshape(bs, dm)
    hs, proj_s, delta_s = _layer0(
        xs, p_sample[0].reshape(bs, -1), ws, P, tm=bs, in_tn=in_tn,
        mixer=lambda pr: _gdn_step(pr, state_conv[0], conv_w, ws['a_hp'], out_norm, state_delta[0]))
    conv_s = jnp.concatenate([state_conv[0][:, 1:], proj_s[:, None, :qkv_w]], axis=1)[None]
    cos_s, sin_s = _rope_tables(jnp.full((bs,), PAST_LEN, jnp.int32))
    kv_s, = _proj_rope(hs, kv_norm, ws['w_kv'], ws['k_gain'], cos_s, sin_s, tm=bs,
                       n_rope=N_GROUPS, natural=True, dils=())
    q_s, = _proj_rope(hs, b_norm[0], ws['b_w_q'], ws['q_gain'], cos_s, sin_s, tm=bs,
                      n_rope=N_GROUPS, natural=True, dils=())
    o_s = _gather_attn(q_s.reshape(bs, N_GROUPS, HG, HD_B, 1),
                       kv_s.reshape(bs, 2, N_GROUPS, HG, HD_B, 1),
                       (cache_kv_w128, cache_kv_w512, cache_kv_w2048))
    hs = _mm_res(o_s.reshape(bs, GW), ws['b_w_out'], hs, tm=bs)
    y_sample = _layer1_tail(hs, p_sample[1].reshape(bs, -1), ws, P, tm=bs).reshape(bs, 1, dm)
    kvs5 = kv_s.reshape(bs, 1, 2, N_GROUPS, HG, HD_B)
    kv_sn = [kvs5[:, :, :, gi] for gi in range(N_GROUPS)]

    return (y_prompt, y_sample, conv_p, conv_s, delta_p[None], delta_s[None],
            kv_p[0], kv_sn[0], kv_p[1], kv_sn[1], kv_p[2], kv_sn[2])
```

```python
import functools

import jax
import jax.numpy as jnp
from jax import lax
from jax.experimental import pallas as pl
from jax.experimental.pallas import tpu as pltpu

F32 = jnp.float32
BF16 = jnp.bfloat16

EPS = 1e-6
PAST_LEN = 16384
GROUPS = ((128, 1), (512, 4), (2048, 16))
N_GROUPS = len(GROUPS)
HG = 8
HD_B = 64
ROT_DIM = HD_B // 4
ROPE_THETA = 500000.0
GW = HG * HD_B
H_A = 8
DK_A = 128
CONV_W = 4
CHUNK = 64
LANES = 128
VMEM_LIMIT = 57 * 1024 * 1024
NEG = -1e30


def _cparams(*sem):
    return pltpu.CompilerParams(dimension_semantics=sem, vmem_limit_bytes=VMEM_LIMIT)


def _rms_rows(x, gain):
    return x * lax.rsqrt(jnp.mean(x * x, -1, keepdims=True) + EPS) * gain


def _silu(x):
    return x * (1.0 / (1.0 + jnp.exp(-x)))


def _sigmoid(x):
    return 1.0 / (1.0 + jnp.exp(-x))


def _dot(a, b):
    return jnp.dot(a.astype(BF16), b.astype(BF16), preferred_element_type=F32)


def _wdot(a, w):
    dot = functools.partial(jnp.dot, preferred_element_type=F32)
    if w.dtype == BF16:
        return dot(a.astype(BF16), w)
    a = a.astype(F32)
    a_hi = a.astype(BF16)
    a_lo = (a - a_hi.astype(F32)).astype(BF16)
    w_hi = w.astype(BF16)
    w_lo = (w - w_hi.astype(F32)).astype(BF16)
    return dot(a_hi, w_hi) + (dot(a_lo, w_hi) + dot(a_hi, w_lo))


def _act_dtype(w):
    return BF16 if w.dtype == BF16 else F32


def _dot_nt(a, b):
    return lax.dot_general(a.astype(BF16), b.astype(BF16), (((1,), (1,)), ((), ())),
                           preferred_element_type=F32)


def _dot_tn(a, b):
    return lax.dot_general(a.astype(BF16), b.astype(BF16), (((0,), (0,)), ((), ())),
                           preferred_element_type=F32)


def _head_norm_rope(x, hgain, cos, sin):
    t = x.shape[0]
    lane = lax.broadcasted_iota(jnp.int32, (t, LANES), 1)
    lo = lane < HD_B
    d = lane & (HD_B - 1)
    outs = []
    for c in range(GW // LANES):
        sl = slice(c * LANES, (c + 1) * LANES)
        xb = x[:, sl]
        sq = xb * xb
        s_lo = jnp.sum(jnp.where(lo, sq, 0.0), -1, keepdims=True)
        s_hi = jnp.sum(jnp.where(lo, 0.0, sq), -1, keepdims=True)
        scale = jnp.where(lo, lax.rsqrt(s_lo * (1.0 / HD_B) + EPS),
                          lax.rsqrt(s_hi * (1.0 / HD_B) + EPS))
        yb = xb * scale * hgain[:, sl]
        half = ROT_DIM // 2
        rot = jnp.where(d < half, pltpu.roll(yb, LANES - half, 1), pltpu.roll(yb, half, 1))
        outs.append(yb * cos[:, sl] + rot * sin[:, sl])
    return jnp.concatenate(outs, axis=1)


def _norm_mm_kernel(x_ref, g_ref, w_ref, o_ref, *, tn):
    u = _rms_rows(x_ref[...], g_ref[...]).astype(_act_dtype(w_ref))
    for j in range(w_ref.shape[1] // tn):
        cols = slice(j * tn, (j + 1) * tn)
        o_ref[:, cols] = _wdot(u, w_ref[:, cols])


def _norm_mm(x, gain, w, *, tm, tn):
    m, k = x.shape
    n = w.shape[1]
    return pl.pallas_call(
        functools.partial(_norm_mm_kernel, tn=tn),
        grid=(m // tm,),
        in_specs=[pl.BlockSpec((tm, k), lambda i: (i, 0)),
                  pl.BlockSpec((1, k), lambda i: (0, 0)),
                  pl.BlockSpec((k, n), lambda i: (0, 0))],
        out_specs=pl.BlockSpec((tm, n), lambda i: (i, 0)),
        out_shape=jax.ShapeDtypeStruct((m, n), F32),
        compiler_params=_cparams("parallel"),
        name="norm_mm",
    )(x, gain.reshape(1, k), w)


def _proj_rope_kernel(x_ref, g_ref, w_ref, hg_ref, cos_ref, sin_ref, *rest,
                      n_rope, natural, dils, tm):
    n_out = int(natural) + len(dils)
    outs, (slab_ref,) = rest[:n_out], rest[n_out:]
    nat_ref = outs[0] if natural else None
    ph_refs = outs[int(natural):]
    u = _rms_rows(x_ref[...], g_ref[...]).astype(_act_dtype(w_ref))
    slab = 0
    for jj in range(w_ref.shape[1] // GW):
        cols = slice(jj * GW, (jj + 1) * GW)
        y = _wdot(u, w_ref[:, cols])
        if jj < n_rope:
            y = _head_norm_rope(y, hg_ref[...], cos_ref[...], sin_ref[...])
        if natural:
            nat_ref[:, cols] = y
        if jj >= len(dils):
            continue
        d = dils[jj]
        if d == 1:
            ph_refs[jj][0] = y.astype(BF16)
            continue
        for c in range(GW // LANES):
            slab_ref[slab, c] = y[:, c * LANES:(c + 1) * LANES]
        for r in range(d):
            for c in range(GW // LANES):
                ph_refs[jj][r, :, c * LANES:(c + 1) * LANES] = (
                    slab_ref[slab, c, pl.ds(r, tm // d, stride=d), :].astype(BF16))
        slab += 1


def _proj_rope(x, gain, w, hgain, cos, sin, *, tm, n_rope, natural, dils):
    m, k = x.shape
    n = w.shape[1]
    pos_blocks = cos.shape[0] // tm
    out_specs, out_shape = [], []
    if natural:
        out_specs.append(pl.BlockSpec((tm, n), lambda i: (i, 0)))
        out_shape.append(jax.ShapeDtypeStruct((m, n), F32))
    for d in dils:
        out_specs.append(pl.BlockSpec((d, tm // d, GW), lambda i: (0, i, 0)))
        out_shape.append(jax.ShapeDtypeStruct((d, m // d, GW), BF16))
    n_slabs = max(1, sum(d > 1 for d in dils))
    return pl.pallas_call(
        functools.partial(_proj_rope_kernel, n_rope=n_rope, natural=natural, dils=tuple(dils),
                          tm=tm),
        grid=(m // tm,),
        in_specs=[pl.BlockSpec((tm, k), lambda i: (i, 0)),
                  pl.BlockSpec((1, k), lambda i: (0, 0)),
                  pl.BlockSpec((k, n), lambda i: (0, 0)),
                  pl.BlockSpec((1, GW), lambda i: (0, 0)),
                  pl.BlockSpec((tm, GW), lambda i: (i % pos_blocks, 0)),
                  pl.BlockSpec((tm, GW), lambda i: (i % pos_blocks, 0))],
        out_specs=out_specs,
        out_shape=out_shape,
        scratch_shapes=[pltpu.VMEM((n_slabs, GW // LANES, tm, LANES), F32)],
        compiler_params=_cparams("parallel"),
        name="proj_rope",
    )(x, gain.reshape(1, k), w, hgain, cos, sin)


def _rope_tables(pos):
    half = ROT_DIM // 2
    inv = ROPE_THETA ** (-jnp.arange(half, dtype=F32) * 2.0 / ROT_DIM)
    ang = pos.astype(F32)[:, None] * inv[None]
    c, s = jnp.cos(ang), jnp.sin(ang)
    n = pos.shape[0]
    cos_h = jnp.concatenate([c, c, jnp.ones((n, HD_B - ROT_DIM), F32)], 1)
    sin_h = jnp.concatenate([-s, s, jnp.zeros((n, HD_B - ROT_DIM), F32)], 1)
    return jnp.tile(cos_h, (1, HG)), jnp.tile(sin_h, (1, HG))


def _mm_res_kernel(x_ref, w_ref, r_ref, o_ref):
    o_ref[...] = r_ref[...] + _wdot(x_ref[...], w_ref[...])


def _mm_res(x, w, res, *, tm):
    m, k = x.shape
    n = w.shape[1]
    return pl.pallas_call(
        _mm_res_kernel,
        grid=(m // tm,),
        in_specs=[pl.BlockSpec((tm, k), lambda i: (i, 0)),
                  pl.BlockSpec((k, n), lambda i: (0, 0)),
                  pl.BlockSpec((tm, n), lambda i: (i, 0))],
        out_specs=pl.BlockSpec((tm, n), lambda i: (i, 0)),
        out_shape=jax.ShapeDtypeStruct((m, n), F32),
        compiler_params=_cparams("parallel"),
        name="mm_res",
    )(x, w, res)


def _unit_lower_inverses(mats):
    c = mats[0].shape[0]
    row = lax.broadcasted_iota(jnp.int32, (c, c), 0)
    col = lax.broadcasted_iota(jnp.int32, (c, c), 1)
    eye = jnp.where(row == col, 1.0, 0.0).astype(F32)
    ts = None
    b = 1
    while b < c:
        sel = ((row ^ col) < 2 * b) & ((row & b) != 0) & ((col & b) == 0)
        lows = [jnp.where(sel, a, 0.0) for a in mats]
        if ts is None:
            ts = [eye - low for low in lows]
        else:
            tl = [_dot(t, low) for t, low in zip(ts, lows)]
            ts = [t - _dot(x, t) for t, x in zip(ts, tl)]
        b *= 2
    return ts


def _gdn_head_params(ba, hp):
    beta = _sigmoid(ba)
    x = ba + hp[1:2, :]
    softplus = jnp.maximum(x, 0.0) + jnp.log(1.0 + jnp.exp(-jnp.abs(x)))
    g = -jnp.exp(hp[0:1, :]) * softplus
    return beta, g


def _gdn_kernel(qkv_ref, z_ref, ba_ref, cw_ref, hp_ref, on_ref, conv0_ref, s0_ref,
                og_ref, sout_ref, xbuf, s_scr, *, C, nch):
    n = pl.program_id(1)
    R = nch * C
    pad = 8

    @pl.when(n == 0)
    def _():
        xbuf[pad - (CONV_W - 1):pad, :] = conv0_ref[...]
        s_scr[...] = s0_ref[...]

    xbuf[pad:pad + R, :] = qkv_ref[...]

    def conv_cols(c0):
        acc = None
        for j in range(CONV_W):
            r0 = pad - (CONV_W - 1) + j
            term = xbuf[r0:r0 + R, c0:c0 + DK_A] * cw_ref[j:j + 1, c0:c0 + DK_A]
            acc = term if acc is None else acc + term
        return _silu(acc)

    beta, g = _gdn_head_params(ba_ref[...], hp_ref[...])
    rr = lax.broadcasted_iota(jnp.int32, (R, R), 0)
    rc = lax.broadcasted_iota(jnp.int32, (R, R), 1)
    blocktri = ((rr >= rc) & ((rr ^ rc) < C)).astype(F32)
    gcum = jnp.dot(blocktri, g, preferred_element_type=F32, precision=lax.Precision.HIGHEST)
    gcum_t = gcum.T
    row = lax.broadcasted_iota(jnp.int32, (C, C), 0)
    col = lax.broadcasted_iota(jnp.int32, (C, C), 1)
    incl = row >= col
    strict = row > col

    units = [(c, h) for c in range(nch) for h in range(H_A)]
    qs, ks, vs = {}, {}, {}
    for h in range(H_A):
        q = conv_cols(h * DK_A)
        k = conv_cols((H_A + h) * DK_A)
        v = conv_cols((2 * H_A + h) * DK_A)
        q = q * lax.rsqrt(jnp.sum(q * q, -1, keepdims=True) + EPS) * (DK_A ** -0.5)
        k = k * lax.rsqrt(jnp.sum(k * k, -1, keepdims=True) + EPS)
        for c in range(nch):
            rs = slice(c * C, (c + 1) * C)
            qs[c, h], ks[c, h], vs[c, h] = q[rs], k[rs], v[rs]

    bcs, gcs, decays, kbs = {}, {}, {}, {}
    for c, h in units:
        rs = slice(c * C, (c + 1) * C)
        bcs[c, h] = beta[rs, h:h + 1]
        gcs[c, h] = gcum[rs, H_A + h:H_A + h + 1]
        gr = gcum_t[H_A + h:H_A + h + 1, rs]
        decays[c, h] = jnp.exp(jnp.where(incl, gcs[c, h] - gr, NEG))
        kbs[c, h] = ks[c, h] * bcs[c, h]
    grams = {u: _dot_nt(jnp.concatenate([kbs[u], qs[u]], axis=0), ks[u]) for u in units}
    a_mats = [jnp.where(strict, grams[u][:C] * decays[u], 0.0) for u in units]
    aqks = {u: grams[u][C:] * decays[u] for u in units}
    t_mats = dict(zip(units, _unit_lower_inverses(a_mats)))
    egs = {u: jnp.exp(gcs[u]) for u in units}
    sols = {u: _dot(t_mats[u], jnp.concatenate([vs[u] * bcs[u], kbs[u] * egs[u]], axis=1))
            for u in units}

    states = [s_scr[h] for h in range(H_A)]
    for c in range(nch):
        rs = slice(c * C, (c + 1) * C)
        for h in range(H_A):
            u = (c, h)
            g_last = gcs[u][C - 1:C, :]
            ws = _dot(jnp.concatenate([sols[u][:, DK_A:], qs[u] * egs[u]], axis=0), states[h])
            v_new = sols[u][:, :DK_A] - ws[:C]
            o = ws[C:] + _dot(aqks[u], v_new)
            kd = ks[u] * jnp.exp(g_last - gcs[u])
            states[h] = states[h] * jnp.exp(g_last) + _dot_tn(kd, v_new)
            o = _rms_rows(o, on_ref[...]) * _silu(z_ref[rs, h * DK_A:(h + 1) * DK_A])
            og_ref[rs, h * DK_A:(h + 1) * DK_A] = o
    for h in range(H_A):
        s_scr[h] = states[h]

    xbuf[pad - (CONV_W - 1):pad, :] = xbuf[pad + R - (CONV_W - 1):pad + R, :]

    @pl.when(n == pl.num_programs(1) - 1)
    def _():
        sout_ref[...] = s_scr[...]


def _gdn_prompt(proj, conv_w, hp, out_norm, conv0, s0, *, batch, seq):
    C = min(CHUNK, seq)
    nch = next(n for n in (4, 2, 1) if seq % (n * C) == 0)
    R = nch * C
    nc = seq // R
    qkv_w = 3 * H_A * DK_A
    z_w = H_A * DK_A
    return pl.pallas_call(
        functools.partial(_gdn_kernel, C=C, nch=nch),
        grid=(batch, nc),
        in_specs=[pl.BlockSpec((R, qkv_w), lambda b, n: (b * nc + n, 0)),
                  pl.BlockSpec((R, z_w), lambda b, n: (b * nc + n, qkv_w // z_w)),
                  pl.BlockSpec((R, LANES), lambda b, n: (b * nc + n, (qkv_w + z_w) // LANES)),
                  pl.BlockSpec((CONV_W, qkv_w), lambda b, n: (0, 0)),
                  pl.BlockSpec((2, LANES), lambda b, n: (0, 0)),
                  pl.BlockSpec((1, DK_A), lambda b, n: (0, 0)),
                  pl.BlockSpec((None, CONV_W - 1, qkv_w), lambda b, n: (b, 0, 0)),
                  pl.BlockSpec((None, H_A, DK_A, DK_A), lambda b, n: (b, 0, 0, 0))],
        out_specs=[pl.BlockSpec((R, z_w), lambda b, n: (b * nc + n, 0)),
                   pl.BlockSpec((None, H_A, DK_A, DK_A), lambda b, n: (b, 0, 0, 0))],
        out_shape=[jax.ShapeDtypeStruct((batch * seq, z_w), F32),
                   jax.ShapeDtypeStruct((batch, H_A, DK_A, DK_A), F32)],
        scratch_shapes=[pltpu.VMEM((R + 8, qkv_w), F32),
                        pltpu.VMEM((H_A, DK_A, DK_A), F32)],
        compiler_params=_cparams("parallel", "arbitrary"),
        name="gdn_chunked",
    )(proj, proj, proj, conv_w, hp, out_norm, conv0, s0)


def _gdn_step_kernel(proj_ref, conv_ref, cw_ref, hp_ref, on_ref, s0_ref, og_ref, sout_ref, qk_scr):
    qkv_w = 3 * H_A * DK_A
    z_w = H_A * DK_A

    def conv_cols(c0):
        sl = slice(c0, c0 + DK_A)
        acc = proj_ref[:, sl] * cw_ref[CONV_W - 1:CONV_W, sl]
        for j in range(CONV_W - 1):
            acc = acc + conv_ref[j:j + 1, sl] * cw_ref[j:j + 1, sl]
        return _silu(acc)

    beta, g = _gdn_head_params(proj_ref[:, qkv_w + z_w:qkv_w + z_w + LANES], hp_ref[...])
    qk_scr[...] = jnp.zeros_like(qk_scr)
    vs = []
    for h in range(H_A):
        q = conv_cols(h * DK_A)
        k = conv_cols((H_A + h) * DK_A)
        vs.append(conv_cols((2 * H_A + h) * DK_A))
        qk_scr[H_A + h:H_A + h + 1, :] = (
            q * lax.rsqrt(jnp.sum(q * q, -1, keepdims=True) + EPS) * (DK_A ** -0.5))
        qk_scr[h:h + 1, :] = k * lax.rsqrt(jnp.sum(k * k, -1, keepdims=True) + EPS)
    qk = qk_scr[...]
    qk_t = qk.T
    for h in range(H_A):
        k_row = qk[h:h + 1, :]
        q_row = qk[H_A + h:H_A + h + 1, :]
        k_col = qk_t[:, h:h + 1]
        q_col = qk_t[:, H_A + h:H_A + h + 1]
        bh = beta[:, h:h + 1]
        eg = jnp.exp(g[:, H_A + h:H_A + h + 1])
        s = s0_ref[h]
        k_s = jnp.sum(s * k_col, 0, keepdims=True)
        q_s = jnp.sum(s * q_col, 0, keepdims=True)
        v_new = bh * (vs[h] - eg * k_s)
        o = eg * q_s + jnp.sum(q_row * k_row, -1, keepdims=True) * v_new
        sout_ref[h] = s * eg + k_col * v_new
        o = _rms_rows(o, on_ref[...]) * _silu(proj_ref[:, qkv_w + h * DK_A:qkv_w + (h + 1) * DK_A])
        og_ref[:, h * DK_A:(h + 1) * DK_A] = o


def _gdn_step(proj, conv_state, conv_w, hp, out_norm, s0):
    nb, pw = proj.shape
    qkv_w = 3 * H_A * DK_A
    z_w = H_A * DK_A
    og, s_new = pl.pallas_call(
        _gdn_step_kernel,
        grid=(nb,),
        in_specs=[pl.BlockSpec((None, 1, pw), lambda b: (b, 0, 0)),
                  pl.BlockSpec((None, CONV_W - 1, qkv_w), lambda b: (b, 0, 0)),
                  pl.BlockSpec((CONV_W, qkv_w), lambda b: (0, 0)),
                  pl.BlockSpec((2, LANES), lambda b: (0, 0)),
                  pl.BlockSpec((1, DK_A), lambda b: (0, 0)),
                  pl.BlockSpec((None, H_A, DK_A, DK_A), lambda b: (b, 0, 0, 0))],
        out_specs=[pl.BlockSpec((None, 1, z_w), lambda b: (b, 0, 0)),
                   pl.BlockSpec((None, H_A, DK_A, DK_A), lambda b: (b, 0, 0, 0))],
        out_shape=[jax.ShapeDtypeStruct((nb, 1, z_w), F32),
                   jax.ShapeDtypeStruct((nb, H_A, DK_A, DK_A), F32)],
        scratch_shapes=[pltpu.VMEM((LANES, DK_A), F32)],
        compiler_params=_cparams("parallel"),
        name="gdn_step",
    )(proj.reshape(nb, 1, pw), conv_state, conv_w, hp, out_norm, s0)
    return og.reshape(nb, z_w), s_new


def _ffn_kernel(h_ref, g_ref, wg_ref, wu_ref, wd_ref, o_ref, u_ref, acc_ref):
    f = pl.program_id(1)

    @pl.when(f == 0)
    def _():
        u_ref[...] = _rms_rows(h_ref[...], g_ref[...]).astype(u_ref.dtype)
        acc_ref[...] = jnp.zeros_like(acc_ref)

    u = u_ref[...]
    gate = _wdot(u, wg_ref[...])
    up = _wdot(u, wu_ref[...])
    acc_ref[...] += _wdot(_silu(gate) * up, wd_ref[...])

    @pl.when(f == pl.num_programs(1) - 1)
    def _():
        o_ref[...] = h_ref[...] + acc_ref[...]


def _ffn(h, gain, w_gu, w_down, *, tm, tf):
    m, dm = h.shape
    ff = w_down.shape[0]
    nf = ff // tf
    return pl.pallas_call(
        _ffn_kernel,
        grid=(m // tm, nf),
        in_specs=[pl.BlockSpec((tm, dm), lambda i, f: (i, 0)),
                  pl.BlockSpec((1, dm), lambda i, f: (0, 0)),
                  pl.BlockSpec((dm, tf), lambda i, f: (0, f)),
                  pl.BlockSpec((dm, tf), lambda i, f: (0, nf + f)),
                  pl.BlockSpec((tf, dm), lambda i, f: (f, 0))],
        out_specs=pl.BlockSpec((tm, dm), lambda i, f: (i, 0)),
        out_shape=jax.ShapeDtypeStruct((m, dm), F32),
        scratch_shapes=[pltpu.VMEM((tm, dm), _act_dtype(w_gu)),
                        pltpu.VMEM((tm, dm), F32)],
        compiler_params=_cparams("parallel", "arbitrary"),
        name="ffn_dense",
    )(h, gain.reshape(1, dm), w_gu, w_gu, w_down)


N_EXPERTS = 8
SEG_ALIGN = LANES


def _top2(logits):
    t = logits.shape[0]
    lane = lax.broadcasted_iota(jnp.int32, (t, LANES), 1)
    valid = lane < N_EXPERTS
    lg = jnp.where(valid, logits, NEG)
    mx = jnp.max(lg, -1, keepdims=True)
    e = jnp.where(valid, jnp.exp(lg - mx), 0.0)
    probs = e / jnp.sum(e, -1, keepdims=True)
    p1 = jnp.max(probs, -1, keepdims=True)
    i1 = jnp.min(jnp.where((probs == p1) & valid, lane, LANES), -1, keepdims=True)
    rest = jnp.where((lane == i1) | ~valid, -1.0, probs)
    p2 = jnp.max(rest, -1, keepdims=True)
    i2 = jnp.min(jnp.where(rest == p2, lane, LANES), -1, keepdims=True)
    tot = p1 + p2
    return i1, i2, p1 / tot, p2 / tot


def _router_kernel(h_ref, g_ref, r_ref, tri_ref, upper_ref, ut_ref, col_ref, row_ref, seg_ref):
    t = h_ref.shape[0]
    sub = tri_ref.shape[0]
    u = _rms_rows(h_ref[...], g_ref[...])
    ut_ref[...] = u.T.astype(BF16)
    i1, i2, g1, g2 = _top2(_wdot(u, r_ref[...]))
    lane = lax.broadcasted_iota(jnp.int32, (t, LANES), 1)
    sel = jnp.where((lane == i1) | (lane == i2), 1.0, 0.0)
    counts = jnp.zeros((1, LANES), F32)
    pos_parts = []
    for k in range(t // sub):
        sel_k = sel[k * sub:(k + 1) * sub]
        pos_parts.append(counts + jnp.dot(tri_ref[...], sel_k.astype(BF16),
                                          preferred_element_type=F32))
        counts = counts + jnp.sum(sel_k, 0, keepdims=True)
    pos = jnp.concatenate(pos_parts, axis=0)
    nblk = jnp.floor((counts + (SEG_ALIGN - 1)) * (1.0 / SEG_ALIGN))
    nblk8 = jnp.broadcast_to(nblk, (8, LANES))
    start8 = jnp.dot(nblk8.astype(BF16), upper_ref[...], preferred_element_type=F32)
    dest = start8[0:1] * SEG_ALIGN + pos
    d1 = jnp.sum(jnp.where(lane == i1, dest, 0.0), -1, keepdims=True)
    d2 = jnp.sum(jnp.where(lane == i2, dest, 0.0), -1, keepdims=True)
    col = jnp.where(lane == 0, d1, jnp.where(lane == 1, d2,
                    jnp.where(lane == 2, g1, jnp.where(lane == 3, g2, 0.0))))
    col_ref[...] = col
    row_ref[...] = col.T[0:8, :]
    lane8 = lane[0:8]
    seg = jnp.where(lane8 < N_EXPERTS, start8,
                    jnp.where(lane8 < 2 * N_EXPERTS, pltpu.roll(nblk8, N_EXPERTS, 1), 0.0))
    seg_ref[...] = seg[0:1].astype(jnp.int32)


def _route(h, gain, router, *, tm):
    m, dm = h.shape
    nt = m // tm
    sub = min(tm, 1024)
    tri = jnp.tril(jnp.ones((sub, sub), F32), -1).astype(BF16)
    upper = jnp.triu(jnp.ones((LANES, LANES), F32), 1).astype(BF16)
    return pl.pallas_call(
        _router_kernel,
        grid=(nt,),
        in_specs=[pl.BlockSpec((tm, dm), lambda i: (i, 0)),
                  pl.BlockSpec((1, dm), lambda i: (0, 0)),
                  pl.BlockSpec((dm, LANES), lambda i: (0, 0)),
                  pl.BlockSpec((sub, sub), lambda i: (0, 0)),
                  pl.BlockSpec((LANES, LANES), lambda i: (0, 0))],
        out_specs=[pl.BlockSpec((None, dm, tm), lambda i: (i, 0, 0)),
                   pl.BlockSpec((tm, LANES), lambda i: (i, 0)),
                   pl.BlockSpec((None, 8, tm), lambda i: (i, 0, 0)),
                   pl.BlockSpec((None, 1, LANES), lambda i: (i, 0, 0))],
        out_shape=[jax.ShapeDtypeStruct((nt, dm, tm), BF16),
                   jax.ShapeDtypeStruct((m, LANES), F32),
                   jax.ShapeDtypeStruct((nt, 8, tm), F32),
                   jax.ShapeDtypeStruct((nt, 1, LANES), jnp.int32)],
        compiler_params=_cparams("parallel"),
        name="moe_route",
    )(h, gain.reshape(1, dm), router, tri, upper)


def _one_hot_rows(row0, n, d1_row, d2_row):
    ridx = (lax.broadcasted_iota(jnp.int32, (n, d1_row.shape[1]), 0) + row0).astype(F32)
    return jnp.where((ridx == d1_row) | (ridx == d2_row), 1.0, 0.0).astype(BF16)


def _experts_kernel(seg_ref, ut_ref, col_ref, row_ref, wg_ref, wu_ref, wd_ref, o_ref,
                    xs_scr, acc_scr):
    i, e, f = pl.program_id(0), pl.program_id(1), pl.program_id(2)
    last_f = pl.num_programs(2) - 1
    t = ut_ref.shape[1]
    start = seg_ref[i * LANES + e]
    nblk = seg_ref[i * LANES + N_EXPERTS + e]

    @pl.when((e == 0) & (f == 0))
    def _():
        o_ref[...] = jnp.zeros_like(o_ref)

    def gather(lb, nb):
        w = nb * LANES
        col = col_ref[...]
        ridx = (lax.broadcasted_iota(jnp.int32, (t, w), 1) + (start + lb) * LANES).astype(F32)
        p_t = jnp.where((ridx == col[:, 0:1]) | (ridx == col[:, 1:2]), 1.0, 0.0).astype(BF16)
        x_t = jnp.dot(ut_ref[...], p_t, preferred_element_type=F32).astype(BF16)
        for k in range(nb):
            xs_scr[lb + k] = x_t[:, k * LANES:(k + 1) * LANES]
            acc_scr[lb + k] = jnp.zeros(acc_scr.shape[1:], F32)

    def swiglu(lb, nb):
        x_t = jnp.concatenate([xs_scr[lb + k] for k in range(nb)], axis=1)
        gate_t = jnp.dot(wg_ref[...], x_t, preferred_element_type=F32)
        up_t = jnp.dot(wu_ref[...], x_t, preferred_element_type=F32)
        act_t = (_silu(gate_t) * up_t).astype(BF16)
        down_t = jnp.dot(wd_ref[...], act_t, preferred_element_type=F32)
        for k in range(nb):
            acc_scr[lb + k] += down_t[:, k * LANES:(k + 1) * LANES]

    def combine(lb, nb):
        w = nb * LANES
        col = col_ref[...]
        row0 = (start + lb) * LANES
        ridx = (lax.broadcasted_iota(jnp.int32, (t, w), 1) + row0).astype(F32)
        gs = jnp.sum(jnp.where(ridx == col[:, 0:1], col[:, 2:3], 0.0) +
                     jnp.where(ridx == col[:, 1:2], col[:, 3:4], 0.0), 0, keepdims=True)
        acc_t = jnp.concatenate([acc_scr[lb + k] for k in range(nb)], axis=1)
        p = _one_hot_rows(row0, w, row_ref[0:1, :], row_ref[1:2, :])
        o_ref[...] += jnp.dot((acc_t * gs).astype(BF16), p, preferred_element_type=F32)

    def for_blocks(*stages):
        def run(lb, nb):
            for stage in stages:
                stage(lb, nb)

        n4 = nblk // 4

        def body(j, carry):
            run(4 * j, 4)
            return carry

        lax.fori_loop(0, n4, body, 0)
        rem = nblk - 4 * n4

        @pl.when((rem & 2) != 0)
        def _():
            run(4 * n4, 2)

        @pl.when((rem & 1) != 0)
        def _():
            run(4 * n4 + (rem & 2), 1)

    @pl.when(f == 0)
    def _():
        for_blocks(gather, swiglu)

    @pl.when((f > 0) & (f < last_f))
    def _():
        for_blocks(swiglu)

    @pl.when(f == last_f)
    def _():
        for_blocks(swiglu, combine)


def _experts(ut, col, row, seg, wgu_t, wd_t, *, tf):
    nt, dm, tm = ut.shape
    ne, _, ff = wd_t.shape
    nf = ff // tf
    assert nf >= 2, "first and last F block are distinct steps"
    grid_spec = pltpu.PrefetchScalarGridSpec(
        num_scalar_prefetch=1,
        grid=(nt, ne, nf),
        in_specs=[pl.BlockSpec((None, dm, tm), lambda i, e, f, s: (i, 0, 0),
                               pipeline_mode=pl.Buffered(1)),
                  pl.BlockSpec((tm, LANES), lambda i, e, f, s: (i, 0),
                               pipeline_mode=pl.Buffered(1)),
                  pl.BlockSpec((None, 8, tm), lambda i, e, f, s: (i, 0, 0)),
                  pl.BlockSpec((None, tf, dm), lambda i, e, f, s: (e, f, 0)),
                  pl.BlockSpec((None, tf, dm), lambda i, e, f, s: (e, nf + f, 0)),
                  pl.BlockSpec((None, dm, tf), lambda i, e, f, s: (e, 0, f))],
        out_specs=pl.BlockSpec((dm, tm), lambda i, e, f, s: (0, i)),
        scratch_shapes=[pltpu.VMEM((tm // LANES, dm, LANES), BF16),
                        pltpu.VMEM((tm // LANES, dm, LANES), F32)])
    return pl.pallas_call(
        _experts_kernel,
        grid_spec=grid_spec,
        out_shape=jax.ShapeDtypeStruct((dm, nt * tm), F32),
        compiler_params=_cparams("parallel", "arbitrary", "arbitrary"),
        name="moe_experts",
    )(seg.reshape(-1), ut, col, row, wgu_t, wgu_t, wd_t)


def _ple_kernel(h_ref, g_ref, gw_ref, p_ref, pw_ref, *rest):
    o_ref = rest[-1]
    h = h_ref[...]
    if len(rest) == 2:
        h = h + rest[0][...].T
    gate = _sigmoid(_wdot(_rms_rows(h, g_ref[...]), gw_ref[...]))
    o_ref[...] = h + _wdot(p_ref[...], pw_ref[...]) * gate


def _ple(h, gain, gate_w, p, ple_w, *, tm, y_t=None):
    m, dm = h.shape
    pd = p.shape[1]
    in_specs = [pl.BlockSpec((tm, dm), lambda i: (i, 0)),
                pl.BlockSpec((1, dm), lambda i: (0, 0)),
                pl.BlockSpec((dm, dm), lambda i: (0, 0)),
                pl.BlockSpec((tm, pd), lambda i: (i, 0)),
                pl.BlockSpec((pd, dm), lambda i: (0, 0))]
    args = [h, gain.reshape(1, dm), gate_w, p, ple_w]
    if y_t is not None:
        in_specs.append(pl.BlockSpec((dm, tm), lambda i: (0, i)))
        args.append(y_t)
    return pl.pallas_call(
        _ple_kernel,
        grid=(m // tm,),
        in_specs=in_specs,
        out_specs=pl.BlockSpec((tm, dm), lambda i: (i, 0)),
        out_shape=jax.ShapeDtypeStruct((m, dm), F32),
        compiler_params=_cparams("parallel"),
        name="ple",
    )(*args)


def _band_attn_kernel(q_ref, kp_ref, kc_ref, vp_ref, vc_ref, o_ref, l_ref, *, span, tq):
    j = pl.program_id(2)
    q = q_ref[...] * (HD_B ** -0.5)
    kk = jnp.concatenate([kp_ref[...], kc_ref[...]], axis=0)
    vv = jnp.concatenate([vp_ref[...], vc_ref[...]], axis=0)
    qi = lax.broadcasted_iota(jnp.int32, (span, 2 * span), 0)
    ki = lax.broadcasted_iota(jnp.int32, (span, 2 * span), 1)
    dist = qi + span - ki
    band = (dist >= 0) & (dist <= span)
    lane = lax.broadcasted_iota(jnp.int32, (span, LANES), 1)
    for sb in range(tq // span):
        r0 = sb * span
        mask = band & (ki >= jnp.where(j > 0, 0, span)) if sb == 0 else band
        heads = [slice(h * HD_B, (h + 1) * HD_B) for h in range(HG)]
        ss = [jnp.where(mask, _dot_nt(q[r0:r0 + span, hs], kk[r0:r0 + 2 * span, hs]), NEG)
              for hs in heads]
        ms = [jnp.max(s, -1, keepdims=True) for s in ss]
        es = [jnp.exp(s - m).astype(BF16) for s, m in zip(ss, ms)]
        ones = jnp.ones((2 * span, LANES), BF16)
        dens = [_dot(e, ones) for e in es]
        outs = [_dot(e, vv[r0:r0 + 2 * span, hs]) * (1.0 / den[:, :HD_B])
                for e, den, hs in zip(es, dens, heads)]
        lse_tile = jnp.zeros((span, LANES), F32)
        for h in range(HG):
            lse_tile = jnp.where(lane == h, ms[h] + jnp.log(dens[h]), lse_tile)
        o_ref[r0:r0 + span, :] = jnp.concatenate(outs, axis=1)
        l_ref[r0:r0 + span, :] = lse_tile


def _band_attn(q, k, v, gi, *, batch, seq):
    win, dil = GROUPS[gi]
    span = win // dil
    n = seq // dil
    tq = min(4 * span, n)
    nb = n // tq
    sub = tq // span
    cur = lambda b, r, j: (r, b * nb + j, 0)
    prev = lambda b, r, j: (r, b * nb * sub + jnp.maximum(j * sub - 1, 0), 0)
    return pl.pallas_call(
        functools.partial(_band_attn_kernel, span=span, tq=tq),
        grid=(batch, dil, nb),
        in_specs=[pl.BlockSpec((None, tq, GW), cur),
                  pl.BlockSpec((None, span, GW), prev),
                  pl.BlockSpec((None, tq, GW), cur),
                  pl.BlockSpec((None, span, GW), prev),
                  pl.BlockSpec((None, tq, GW), cur)],
        out_specs=[pl.BlockSpec((None, tq, GW), cur),
                   pl.BlockSpec((None, tq, LANES), cur)],
        out_shape=[jax.ShapeDtypeStruct((dil, batch * n, GW), F32),
                   jax.ShapeDtypeStruct((dil, batch * n, LANES), F32)],
        compiler_params=_cparams("parallel", "parallel", "arbitrary"),
        name=f"band_attn_g{gi}",
    )(q, k, k, v, v)


def _merge_out_kernel(o0_ref, o1_ref, o2_ref, l0_ref, l1_ref, l2_ref, w_ref, r_ref, o_ref,
                      o_scr, l_scr, *, tm):
    for gi, (o_ph, l_ph) in enumerate(((o1_ref, l1_ref), (o2_ref, l2_ref))):
        d = o_ph.shape[0]
        for r in range(d):
            rows = pl.ds(r, tm // d, stride=d)
            l_scr[gi, rows, :] = l_ph[r]
            for c in range(GW // LANES):
                o_scr[gi, c, rows, :] = o_ph[r, :, c * LANES:(c + 1) * LANES]
    ls = [l0_ref[0], l_scr[0], l_scr[1]]

    def o_cols(g, c):
        if g == 0:
            return o0_ref[0, :, c * LANES:(c + 1) * LANES]
        return o_scr[g - 1, c]

    m = jnp.maximum(jnp.maximum(ls[0], ls[1]), ls[2])
    es = [jnp.exp(l - m) for l in ls]
    inv = 1.0 / (es[0] + es[1] + es[2])
    t = ls[0].shape[0]
    lo = lax.broadcasted_iota(jnp.int32, (t, LANES), 1) < HD_B
    cols = []
    for c in range(GW // LANES):
        acc = None
        for g in range(N_GROUPS):
            wt = es[g] * inv
            wexp = jnp.where(lo, wt[:, 2 * c:2 * c + 1], wt[:, 2 * c + 1:2 * c + 2])
            term = wexp * o_cols(g, c)
            acc = term if acc is None else acc + term
        cols.append(acc)
    o = jnp.concatenate(cols, axis=1).astype(BF16)
    o_ref[...] = r_ref[...] + jnp.dot(o, w_ref[...], preferred_element_type=F32)


def _merge_out(outs, lses, w, res, *, tm):
    m, dm = res.shape
    ph_spec = lambda a: pl.BlockSpec((a.shape[0], tm // a.shape[0], a.shape[2]),
                                     lambda i: (0, i, 0))
    return pl.pallas_call(
        functools.partial(_merge_out_kernel, tm=tm),
        grid=(m // tm,),
        in_specs=[ph_spec(a) for a in outs] + [ph_spec(a) for a in lses] +
                 [pl.BlockSpec((GW, dm), lambda i: (0, 0)),
                  pl.BlockSpec((tm, dm), lambda i: (i, 0))],
        out_specs=pl.BlockSpec((tm, dm), lambda i: (i, 0)),
        out_shape=jax.ShapeDtypeStruct((m, dm), F32),
        scratch_shapes=[pltpu.VMEM((N_GROUPS - 1, GW // LANES, tm, LANES), F32),
                        pltpu.VMEM((N_GROUPS - 1, tm, LANES), F32)],
        compiler_params=_cparams("parallel"),
        name="merge_out",
    )(*outs, *lses, w, res)


def _gather_attn_kernel(q_ref, kvn_ref, c0_ref, c1_ref, c2_ref, o_ref):
    caches = [c0_ref, c1_ref, c2_ref]
    outs, lses = [], []
    for g, (_, dil) in enumerate(GROUPS):
        q = q_ref[g] * (HD_B ** -0.5)
        kn, vn = kvn_ref[0, g], kvn_ref[1, g]
        kc, vc = caches[g][0], caches[g][1]
        rows = kc.shape[-1]
        s = jnp.sum(kc * q, 1, keepdims=True)
        row = lax.broadcasted_iota(jnp.int32, (1, 1, rows), 2)
        s = jnp.where((row & (dil - 1)) == 0, s, NEG)
        s_new = jnp.sum(kn * q, 1, keepdims=True)
        m = jnp.maximum(jnp.max(s, 2, keepdims=True), s_new)
        e = jnp.exp(s - m)
        e_new = jnp.exp(s_new - m)
        den = jnp.sum(e, 2, keepdims=True) + e_new
        outs.append((jnp.sum(e * vc, 2, keepdims=True) + e_new * vn) / den)
        lses.append(m + jnp.log(den))
    m = jnp.maximum(jnp.maximum(lses[0], lses[1]), lses[2])
    es = [jnp.exp(l - m) for l in lses]
    o_ref[...] = (es[0] * outs[0] + es[1] * outs[1] + es[2] * outs[2]) / (es[0] + es[1] + es[2])


def _gather_attn(q, kv_new, caches):
    nb = q.shape[0]
    span = GROUPS[0][0] // GROUPS[0][1]
    views = []
    for (win, dil), c in zip(GROUPS, caches):
        lb = c.shape[1]
        assert lb == win and lb // dil == span, "window buffer must hold the full window"
        assert dil & (dil - 1) == 0, "dilations are powers of two"
        views.append(jnp.transpose(c, (0, 2, 3, 4, 1)))
    return pl.pallas_call(
        _gather_attn_kernel,
        grid=(nb,),
        in_specs=[pl.BlockSpec((None, N_GROUPS, HG, HD_B, 1), lambda b: (b, 0, 0, 0, 0)),
                  pl.BlockSpec((None, 2, N_GROUPS, HG, HD_B, 1), lambda b: (b, 0, 0, 0, 0, 0))] +
                 [pl.BlockSpec((None, 2, HG, HD_B, v.shape[-1]), lambda b: (b, 0, 0, 0, 0))
                  for v in views],
        out_specs=pl.BlockSpec((None, HG, HD_B, 1), lambda b: (b, 0, 0, 0)),
        out_shape=jax.ShapeDtypeStruct((nb, HG, HD_B, 1), F32),
        compiler_params=_cparams("parallel"),
        name="gather_attn",
    )(q, kv_new, *views)


def _transpose_cast_kernel(x_ref, o_ref):
    o_ref[...] = x_ref[...].T.astype(BF16)


def _transpose_cast(w, *, tk, tn):
    ne, k, n = w.shape
    return pl.pallas_call(
        _transpose_cast_kernel,
        grid=(ne, k // tk, n // tn),
        in_specs=[pl.BlockSpec((None, tk, tn), lambda e, i, j: (e, i, j))],
        out_specs=pl.BlockSpec((None, tn, tk), lambda e, i, j: (e, j, i)),
        out_shape=jax.ShapeDtypeStruct((ne, n, k), BF16),
        compiler_params=_cparams("parallel", "parallel", "parallel"),
        name="transpose_cast",
    )(w)


def _prep_weights(a_w_in, a_A_log, a_dt_bias, a_w_out, w_kv, b_w_q, b_w_out, dense_w_gu,
                  dense_w_down, moe_router, moe_w_gu, moe_w_down, ple_w, ple_gate_w, k_norm,
                  b_q_norm):
    d_model, a_in = a_w_in.shape[1:]
    a_in_pad = -(-a_in // LANES) * LANES
    wf = {
        'a_w_in': jnp.pad(a_w_in[0], ((0, 0), (0, a_in_pad - a_in))),
        'a_w_out': a_w_out[0], 'w_kv': w_kv, 'b_w_q': b_w_q[0], 'b_w_out': b_w_out[0],
        'dense_w_gu': dense_w_gu[0], 'dense_w_down': dense_w_down[0],
        'router': jnp.pad(moe_router[0], ((0, 0), (0, LANES - moe_router.shape[2]))),
        'ple_w': ple_w, 'ple_gate_w': ple_gate_w,
    }
    shared = {}
    hp = jnp.stack([a_A_log[0], a_dt_bias[0]])
    shared['a_hp'] = jnp.pad(hp, ((0, 0), (H_A, LANES - 2 * H_A)))
    shared['moe_wgu_t'] = _transpose_cast(moe_w_gu[0], tk=d_model, tn=1024)
    shared['moe_wd_t'] = _transpose_cast(moe_w_down[0], tk=896, tn=d_model)
    shared['k_gain'] = jnp.tile(k_norm, HG).reshape(1, GW)
    shared['q_gain'] = jnp.tile(b_q_norm[0], HG).reshape(1, GW)
    w_prompt = dict(shared, **{k: v.astype(BF16) for k, v in wf.items()})
    w_sample = dict(shared, **wf)
    return w_prompt, w_sample


def _layer0(x, p0, w, P, *, tm, in_tn, mixer):
    proj = _norm_mm(x, P['a_norm'][0], w['a_w_in'], tm=min(tm, 512), tn=in_tn)
    og, s_new = mixer(proj)
    h = _mm_res(og, w['a_w_out'], x, tm=tm)
    h = _ffn(h, P['ffn_norm'][0], w['dense_w_gu'], w['dense_w_down'], tm=tm, tf=512)
    h = _ple(h, P['ple_norm'][0], w['ple_gate_w'][0], p0, w['ple_w'][0], tm=tm)
    return h, proj, s_new


def _layer1_tail(h, p1, w, P, *, tm):
    m = h.shape[0]
    tmr = next((t for t in (2048, 1024, 512, 256, LANES) if m % t == 0), LANES)
    hp = jnp.pad(h, ((0, -m % tmr), (0, 0)))
    ut, col, row, seg = _route(hp, P['ffn_norm'][1], w['router'], tm=tmr)
    y_t = _experts(ut, col, row, seg, w['moe_wgu_t'], w['moe_wd_t'], tf=896)
    if m % tmr:
        p1, tm = jnp.pad(p1, ((0, -m % tmr), (0, 0))), tmr
    return _ple(hp, P['ple_norm'][1], w['ple_gate_w'][1], p1, w['ple_w'][1], tm=tm, y_t=y_t)[:m]


def kernel(x_prompt, x_sample, p_prompt, p_sample, state_conv, state_delta, cache_kv_w128, cache_kv_w512, cache_kv_w2048, a_norm, a_w_in, a_conv_w, a_A_log, a_dt_bias, a_out_norm, a_w_out, kv_norm, w_kv, k_norm, b_norm, b_w_q, b_q_norm, b_w_out, ffn_norm, dense_w_gu, dense_w_down, moe_router, moe_w_gu, moe_w_down, ple_w, ple_norm, ple_gate_w):
    assert a_w_in.shape[0] == 1 and b_w_q.shape[0] == 1, "one mixer of each kind"
    bp, sp, dm = x_prompt.shape
    bs, ls, _ = x_sample.shape
    assert ls == 1, "sample group decodes one token per sequence"
    qkv_w = 3 * H_A * DK_A
    P = dict(a_norm=a_norm, ffn_norm=ffn_norm, ple_norm=ple_norm)
    w, ws = _prep_weights(a_w_in, a_A_log, a_dt_bias, a_w_out, w_kv, b_w_q, b_w_out, dense_w_gu,
                          dense_w_down, moe_router, moe_w_gu, moe_w_down, ple_w, ple_gate_w,
                          k_norm, b_q_norm)
    a_in_pad = w['a_w_in'].shape[1]
    in_tn = a_in_pad // 3 if a_in_pad % (3 * LANES) == 0 else LANES
    conv_w = a_conv_w[0]
    out_norm = a_out_norm[0].reshape(1, DK_A)

    mp = bp * sp
    tm = min(1024, sp)
    xp = x_prompt.reshape(mp, dm)
    conv0 = jnp.zeros((bp, CONV_W - 1, qkv_w), F32)
    s0 = jnp.zeros((bp, H_A, DK_A, DK_A), F32)
    h, proj, delta_p = _layer0(
        xp, p_prompt[0].reshape(mp, -1), w, P, tm=tm, in_tn=in_tn,
        mixer=lambda pr: _gdn_prompt(pr, conv_w, w['a_hp'], out_norm, conv0, s0, batch=bp, seq=sp))
    conv_p = proj.reshape(bp, sp, -1)[:, sp - (CONV_W - 1):, :qkv_w][None]

    cos, sin = _rope_tables(jnp.arange(sp, dtype=jnp.int32))
    dils = [d for _, d in GROUPS]
    tmp = min(512, sp)
    kv, *kv_ph = _proj_rope(h, kv_norm, w['w_kv'], w['k_gain'], cos, sin, tm=tmp, n_rope=N_GROUPS,
                            natural=True, dils=dils + dils)
    q_ph = _proj_rope(h, b_norm[0], w['b_w_q'], w['q_gain'], cos, sin, tm=tmp, n_rope=N_GROUPS,
                      natural=False, dils=dils)
    outs, lses = [], []
    for gi in range(N_GROUPS):
        o, lse = _band_attn(q_ph[gi], kv_ph[gi], kv_ph[N_GROUPS + gi], gi, batch=bp, seq=sp)
        outs.append(o)
        lses.append(lse)
    h = _merge_out(outs, lses, w['b_w_out'], h, tm=tm)
    y_prompt = _layer1_tail(h, p_prompt[1].reshape(mp, -1), w, P, tm=tm).reshape(bp, sp, dm)
    kv3 = kv.reshape(bp, sp, 2 * N_GROUPS * GW)
    kv_p = []
    for gi, (win, _) in enumerate(GROUPS):
        rows = kv3[:, sp - min(win, sp):]
        k_g = rows[:, :, gi * GW:(gi + 1) * GW]
        v_g = rows[:, :, (N_GROUPS + gi) * GW:(N_GROUPS + gi + 1) * GW]
        kv_p.append(jnp.stack([k_g, v_g], axis=2).reshape(bp, -1, 2, HG, HD_B))

    xs = x_sample.reshape(bs, dm)
    hs, proj_s, delta_s = _layer0(
        xs, p_sample[0].reshape(bs, -1), ws, P, tm=bs, in_tn=in_tn,
        mixer=lambda pr: _gdn_step(pr, state_conv[0], conv_w, ws['a_hp'], out_norm, state_delta[0]))
    conv_s = jnp.concatenate([state_conv[0][:, 1:], proj_s[:, None, :qkv_w]], axis=1)[None]
    cos_s, sin_s = _rope_tables(jnp.full((bs,), PAST_LEN, jnp.int32))
    kv_s, = _proj_rope(hs, kv_norm, ws['w_kv'], ws['k_gain'], cos_s, sin_s, tm=bs,
                       n_rope=N_GROUPS, natural=True, dils=())
    q_s, = _proj_rope(hs, b_norm[0], ws['b_w_q'], ws['q_gain'], cos_s, sin_s, tm=bs,
                      n_rope=N_GROUPS, natural=True, dils=())
    o_s = _gather_attn(q_s.reshape(bs, N_GROUPS, HG, HD_B, 1),
                       kv_s.reshape(bs, 2, N_GROUPS, HG, HD_B, 1),
                       (cache_kv_w128, cache_kv_w512, cache_kv_w2048))
    hs = _mm_res(o_s.reshape(bs, GW), ws['b_w_out'], hs, tm=bs)
    y_sample = _layer1_tail(hs, p_sample[1].reshape(bs, -1), ws, P, tm=bs).reshape(bs, 1, dm)
    kvs5 = kv_s.reshape(bs, 1, 2, N_GROUPS, HG, HD_B)
    kv_sn = [kvs5[:, :, :, gi] for gi in range(N_GROUPS)]

    return (y_prompt, y_sample, conv_p, conv_s, delta_p[None], delta_s[None],
            kv_p[0], kv_sn[0], kv_p[1], kv_sn[1], kv_p[2], kv_sn[2])
```

```python
import functools

import jax
import jax.numpy as jnp
from jax import lax
from jax.experimental import pallas as pl
from jax.experimental.pallas import tpu as pltpu

F32 = jnp.float32
BF16 = jnp.bfloat16

EPS = 1e-6
PAST_LEN = 16384
GROUPS = ((128, 1), (512, 4), (2048, 16))
N_GROUPS = len(GROUPS)
HG = 8
HD_B = 64
ROT_DIM = HD_B // 4
ROPE_THETA = 500000.0
GW = HG * HD_B
H_A = 8
DK_A = 128
CONV_W = 4
CHUNK = 64
LANES = 128
VMEM_LIMIT = 57 * 1024 * 1024
NEG = -1e30


def _cparams(*sem):
    return pltpu.CompilerParams(dimension_semantics=sem, vmem_limit_bytes=VMEM_LIMIT)


def _rms_rows(x, gain):
    return x * lax.rsqrt(jnp.mean(x * x, -1, keepdims=True) + EPS) * gain


def _silu(x):
    return x * (1.0 / (1.0 + jnp.exp(-x)))


def _sigmoid(x):
    return 1.0 / (1.0 + jnp.exp(-x))


def _dot(a, b):
    return jnp.dot(a.astype(BF16), b.astype(BF16), preferred_element_type=F32)


def _wdot(a, w):
    dot = functools.partial(jnp.dot, preferred_element_type=F32)
    if w.dtype == BF16:
        return dot(a.astype(BF16), w)
    a = a.astype(F32)
    a_hi = a.astype(BF16)
    a_lo = (a - a_hi.astype(F32)).astype(BF16)
    w_hi = w.astype(BF16)
    w_lo = (w - w_hi.astype(F32)).astype(BF16)
    return dot(a_hi, w_hi) + (dot(a_lo, w_hi) + dot(a_hi, w_lo))


def _act_dtype(w):
    return BF16 if w.dtype == BF16 else F32


def _dot_nt(a, b):
    return lax.dot_general(a.astype(BF16), b.astype(BF16), (((1,), (1,)), ((), ())),
                           preferred_element_type=F32)


def _dot_tn(a, b):
    return lax.dot_general(a.astype(BF16), b.astype(BF16), (((0,), (0,)), ((), ())),
                           preferred_element_type=F32)


def _head_norm_rope(x, hgain, cos, sin):
    t = x.shape[0]
    lane = lax.broadcasted_iota(jnp.int32, (t, LANES), 1)
    lo = lane < HD_B
    d = lane & (HD_B - 1)
    outs = []
    for c in range(GW // LANES):
        sl = slice(c * LANES, (c + 1) * LANES)
        xb = x[:, sl]
        sq = xb * xb
        s_lo = jnp.sum(jnp.where(lo, sq, 0.0), -1, keepdims=True)
        s_hi = jnp.sum(jnp.where(lo, 0.0, sq), -1, keepdims=True)
        scale = jnp.where(lo, lax.rsqrt(s_lo * (1.0 / HD_B) + EPS),
                          lax.rsqrt(s_hi * (1.0 / HD_B) + EPS))
        yb = xb * scale * hgain[:, sl]
        half = ROT_DIM // 2
        rot = jnp.where(d < half, pltpu.roll(yb, LANES - half, 1), pltpu.roll(yb, half, 1))
        outs.append(yb * cos[:, sl] + rot * sin[:, sl])
    return jnp.concatenate(outs, axis=1)


def _norm_mm_kernel(x_ref, g_ref, w_ref, o_ref, *, tn):
    u = _rms_rows(x_ref[...], g_ref[...]).astype(_act_dtype(w_ref))
    for j in range(w_ref.shape[1] // tn):
        cols = slice(j * tn, (j + 1) * tn)
        o_ref[:, cols] = _wdot(u, w_ref[:, cols])


def _norm_mm(x, gain, w, *, tm, tn):
    m, k = x.shape
    n = w.shape[1]
    return pl.pallas_call(
        functools.partial(_norm_mm_kernel, tn=tn),
        grid=(m // tm,),
        in_specs=[pl.BlockSpec((tm, k), lambda i: (i, 0)),
                  pl.BlockSpec((1, k), lambda i: (0, 0)),
                  pl.BlockSpec((k, n), lambda i: (0, 0))],
        out_specs=pl.BlockSpec((tm, n), lambda i: (i, 0)),
        out_shape=jax.ShapeDtypeStruct((m, n), F32),
        compiler_params=_cparams("parallel"),
        name="norm_mm",
    )(x, gain.reshape(1, k), w)


def _proj_rope_kernel(x_ref, g_ref, w_ref, hg_ref, cos_ref, sin_ref, *rest,
                      n_rope, natural, dils, tm):
    n_out = int(natural) + len(dils)
    outs, (slab_ref,) = rest[:n_out], rest[n_out:]
    nat_ref = outs[0] if natural else None
    ph_refs = outs[int(natural):]
    u = _rms_rows(x_ref[...], g_ref[...]).astype(_act_dtype(w_ref))
    slab = 0
    for jj in range(w_ref.shape[1] // GW):
        cols = slice(jj * GW, (jj + 1) * GW)
        y = _wdot(u, w_ref[:, cols])
        if jj < n_rope:
            y = _head_norm_rope(y, hg_ref[...], cos_ref[...], sin_ref[...])
        if natural:
            nat_ref[:, cols] = y
        if jj >= len(dils):
            continue
        d = dils[jj]
        if d == 1:
            ph_refs[jj][0] = y.astype(BF16)
            continue
        for c in range(GW // LANES):
            slab_ref[slab, c] = y[:, c * LANES:(c + 1) * LANES]
        for r in range(d):
            for c in range(GW // LANES):
                ph_refs[jj][r, :, c * LANES:(c + 1) * LANES] = (
                    slab_ref[slab, c, pl.ds(r, tm // d, stride=d), :].astype(BF16))
        slab += 1


def _proj_rope(x, gain, w, hgain, cos, sin, *, tm, n_rope, natural, dils):
    m, k = x.shape
    n = w.shape[1]
    pos_blocks = cos.shape[0] // tm
    out_specs, out_shape = [], []
    if natural:
        out_specs.append(pl.BlockSpec((tm, n), lambda i: (i, 0)))
        out_shape.append(jax.ShapeDtypeStruct((m, n), F32))
    for d in dils:
        out_specs.append(pl.BlockSpec((d, tm // d, GW), lambda i: (0, i, 0)))
        out_shape.append(jax.ShapeDtypeStruct((d, m // d, GW), BF16))
    n_slabs = max(1, sum(d > 1 for d in dils))
    return pl.pallas_call(
        functools.partial(_proj_rope_kernel, n_rope=n_rope, natural=natural, dils=tuple(dils),
                          tm=tm),
        grid=(m // tm,),
        in_specs=[pl.BlockSpec((tm, k), lambda i: (i, 0)),
                  pl.BlockSpec((1, k), lambda i: (0, 0)),
                  pl.BlockSpec((k, n), lambda i: (0, 0)),
                  pl.BlockSpec((1, GW), lambda i: (0, 0)),
                  pl.BlockSpec((tm, GW), lambda i: (i % pos_blocks, 0)),
                  pl.BlockSpec((tm, GW), lambda i: (i % pos_blocks, 0))],
        out_specs=out_specs,
        out_shape=out_shape,
        scratch_shapes=[pltpu.VMEM((n_slabs, GW // LANES, tm, LANES), F32)],
        compiler_params=_cparams("parallel"),
        name="proj_rope",
    )(x, gain.reshape(1, k), w, hgain, cos, sin)


def _rope_tables(pos):
    half = ROT_DIM // 2
    inv = ROPE_THETA ** (-jnp.arange(half, dtype=F32) * 2.0 / ROT_DIM)
    ang = pos.astype(F32)[:, None] * inv[None]
    c, s = jnp.cos(ang), jnp.sin(ang)
    n = pos.shape[0]
    cos_h = jnp.concatenate([c, c, jnp.ones((n, HD_B - ROT_DIM), F32)], 1)
    sin_h = jnp.concatenate([-s, s, jnp.zeros((n, HD_B - ROT_DIM), F32)], 1)
    return jnp.tile(cos_h, (1, HG)), jnp.tile(sin_h, (1, HG))


def _mm_res_kernel(x_ref, w_ref, r_ref, o_ref):
    o_ref[...] = r_ref[...] + _wdot(x_ref[...], w_ref[...])


def _mm_res(x, w, res, *, tm):
    m, k = x.shape
    n = w.shape[1]
    return pl.pallas_call(
        _mm_res_kernel,
        grid=(m // tm,),
        in_specs=[pl.BlockSpec((tm, k), lambda i: (i, 0)),
                  pl.BlockSpec((k, n), lambda i: (0, 0)),
                  pl.BlockSpec((tm, n), lambda i: (i, 0))],
        out_specs=pl.BlockSpec((tm, n), lambda i: (i, 0)),
        out_shape=jax.ShapeDtypeStruct((m, n), F32),
        compiler_params=_cparams("parallel"),
        name="mm_res",
    )(x, w, res)


def _unit_lower_inverses(mats):
    c = mats[0].shape[0]
    row = lax.broadcasted_iota(jnp.int32, (c, c), 0)
    col = lax.broadcasted_iota(jnp.int32, (c, c), 1)
    eye = jnp.where(row == col, 1.0, 0.0).astype(F32)
    ts = None
    b = 1
    while b < c:
        sel = ((row ^ col) < 2 * b) & ((row & b) != 0) & ((col & b) == 0)
        lows = [jnp.where(sel, a, 0.0) for a in mats]
        if ts is None:
            ts = [eye - low for low in lows]
        else:
            tl = [_dot(t, low) for t, low in zip(ts, lows)]
            ts = [t - _dot(x, t) for t, x in zip(ts, tl)]
        b *= 2
    return ts


def _gdn_head_params(ba, hp):
    beta = _sigmoid(ba)
    x = ba + hp[1:2, :]
    softplus = jnp.maximum(x, 0.0) + jnp.log(1.0 + jnp.exp(-jnp.abs(x)))
    g = -jnp.exp(hp[0:1, :]) * softplus
    return beta, g


def _gdn_kernel(qkv_ref, z_ref, ba_ref, cw_ref, hp_ref, on_ref, conv0_ref, s0_ref,
                og_ref, sout_ref, xbuf, s_scr, *, C, nch):
    n = pl.program_id(1)
    R = nch * C
    pad = 8

    @pl.when(n == 0)
    def _():
        xbuf[pad - (CONV_W - 1):pad, :] = conv0_ref[...]
        s_scr[...] = s0_ref[...]

    xbuf[pad:pad + R, :] = qkv_ref[...]

    def conv_cols(c0):
        acc = None
        for j in range(CONV_W):
            r0 = pad - (CONV_W - 1) + j
            term = xbuf[r0:r0 + R, c0:c0 + DK_A] * cw_ref[j:j + 1, c0:c0 + DK_A]
            acc = term if acc is None else acc + term
        return _silu(acc)

    beta, g = _gdn_head_params(ba_ref[...], hp_ref[...])
    rr = lax.broadcasted_iota(jnp.int32, (R, R), 0)
    rc = lax.broadcasted_iota(jnp.int32, (R, R), 1)
    blocktri = ((rr >= rc) & ((rr ^ rc) < C)).astype(F32)
    gcum = jnp.dot(blocktri, g, preferred_element_type=F32, precision=lax.Precision.HIGHEST)
    gcum_t = gcum.T
    row = lax.broadcasted_iota(jnp.int32, (C, C), 0)
    col = lax.broadcasted_iota(jnp.int32, (C, C), 1)
    incl = row >= col
    strict = row > col

    units = [(c, h) for c in range(nch) for h in range(H_A)]
    qs, ks, vs = {}, {}, {}
    for h in range(H_A):
        q = conv_cols(h * DK_A)
        k = conv_cols((H_A + h) * DK_A)
        v = conv_cols((2 * H_A + h) * DK_A)
        q = q * lax.rsqrt(jnp.sum(q * q, -1, keepdims=True) + EPS) * (DK_A ** -0.5)
        k = k * lax.rsqrt(jnp.sum(k * k, -1, keepdims=True) + EPS)
        for c in range(nch):
            rs = slice(c * C, (c + 1) * C)
            qs[c, h], ks[c, h], vs[c, h] = q[rs], k[rs], v[rs]

    bcs, gcs, decays, kbs = {}, {}, {}, {}
    for c, h in units:
        rs = slice(c * C, (c + 1) * C)
        bcs[c, h] = beta[rs, h:h + 1]
        gcs[c, h] = gcum[rs, H_A + h:H_A + h + 1]
        gr = gcum_t[H_A + h:H_A + h + 1, rs]
        decays[c, h] = jnp.exp(jnp.where(incl, gcs[c, h] - gr, NEG))
        kbs[c, h] = ks[c, h] * bcs[c, h]
    grams = {u: _dot_nt(jnp.concatenate([kbs[u], qs[u]], axis=0), ks[u]) for u in units}
    a_mats = [jnp.where(strict, grams[u][:C] * decays[u], 0.0) for u in units]
    aqks = {u: grams[u][C:] * decays[u] for u in units}
    t_mats = dict(zip(units, _unit_lower_inverses(a_mats)))
    egs = {u: jnp.exp(gcs[u]) for u in units}
    sols = {u: _dot(t_mats[u], jnp.concatenate([vs[u] * bcs[u], kbs[u] * egs[u]], axis=1))
            for u in units}

    states = [s_scr[h] for h in range(H_A)]
    for c in range(nch):
        rs = slice(c * C, (c + 1) * C)
        for h in range(H_A):
            u = (c, h)
            g_last = gcs[u][C - 1:C, :]
            ws = _dot(jnp.concatenate([sols[u][:, DK_A:], qs[u] * egs[u]], axis=0), states[h])
            v_new = sols[u][:, :DK_A] - ws[:C]
            o = ws[C:] + _dot(aqks[u], v_new)
            kd = ks[u] * jnp.exp(g_last - gcs[u])
            states[h] = states[h] * jnp.exp(g_last) + _dot_tn(kd, v_new)
            o = _rms_rows(o, on_ref[...]) * _silu(z_ref[rs, h * DK_A:(h + 1) * DK_A])
            og_ref[rs, h * DK_A:(h + 1) * DK_A] = o
    for h in range(H_A):
        s_scr[h] = states[h]

    xbuf[pad - (CONV_W - 1):pad, :] = xbuf[pad + R - (CONV_W - 1):pad + R, :]

    @pl.when(n == pl.num_programs(1) - 1)
    def _():
        sout_ref[...] = s_scr[...]


def _gdn_prompt(proj, conv_w, hp, out_norm, conv0, s0, *, batch, seq):
    C = min(CHUNK, seq)
    nch = next(n for n in (4, 2, 1) if seq % (n * C) == 0)
    R = nch * C
    nc = seq // R
    qkv_w = 3 * H_A * DK_A
    z_w = H_A * DK_A
    return pl.pallas_call(
        functools.partial(_gdn_kernel, C=C, nch=nch),
        grid=(batch, nc),
        in_specs=[pl.BlockSpec((R, qkv_w), lambda b, n: (b * nc + n, 0)),
                  pl.BlockSpec((R, z_w), lambda b, n: (b * nc + n, qkv_w // z_w)),
                  pl.BlockSpec((R, LANES), lambda b, n: (b * nc + n, (qkv_w + z_w) // LANES)),
                  pl.BlockSpec((CONV_W, qkv_w), lambda b, n: (0, 0)),
                  pl.BlockSpec((2, LANES), lambda b, n: (0, 0)),
                  pl.BlockSpec((1, DK_A), lambda b, n: (0, 0)),
                  pl.BlockSpec((None, CONV_W - 1, qkv_w), lambda b, n: (b, 0, 0)),
                  pl.BlockSpec((None, H_A, DK_A, DK_A), lambda b, n: (b, 0, 0, 0))],
        out_specs=[pl.BlockSpec((R, z_w), lambda b, n: (b * nc + n, 0)),
                   pl.BlockSpec((None, H_A, DK_A, DK_A), lambda b, n: (b, 0, 0, 0))],
        out_shape=[jax.ShapeDtypeStruct((batch * seq, z_w), F32),
                   jax.ShapeDtypeStruct((batch, H_A, DK_A, DK_A), F32)],
        scratch_shapes=[pltpu.VMEM((R + 8, qkv_w), F32),
                        pltpu.VMEM((H_A, DK_A, DK_A), F32)],
        compiler_params=_cparams("parallel", "arbitrary"),
        name="gdn_chunked",
    )(proj, proj, proj, conv_w, hp, out_norm, conv0, s0)


def _gdn_step_kernel(proj_ref, conv_ref, cw_ref, hp_ref, on_ref, s0_ref, og_ref, sout_ref, qk_scr):
    qkv_w = 3 * H_A * DK_A
    z_w = H_A * DK_A

    def conv_cols(c0):
        sl = slice(c0, c0 + DK_A)
        acc = proj_ref[:, sl] * cw_ref[CONV_W - 1:CONV_W, sl]
        for j in range(CONV_W - 1):
            acc = acc + conv_ref[j:j + 1, sl] * cw_ref[j:j + 1, sl]
        return _silu(acc)

    beta, g = _gdn_head_params(proj_ref[:, qkv_w + z_w:qkv_w + z_w + LANES], hp_ref[...])
    qk_scr[...] = jnp.zeros_like(qk_scr)
    vs = []
    for h in range(H_A):
        q = conv_cols(h * DK_A)
        k = conv_cols((H_A + h) * DK_A)
        vs.append(conv_cols((2 * H_A + h) * DK_A))
        qk_scr[H_A + h:H_A + h + 1, :] = (
            q * lax.rsqrt(jnp.sum(q * q, -1, keepdims=True) + EPS) * (DK_A ** -0.5))
        qk_scr[h:h + 1, :] = k * lax.rsqrt(jnp.sum(k * k, -1, keepdims=True) + EPS)
    qk = qk_scr[...]
    qk_t = qk.T
    for h in range(H_A):
        k_row = qk[h:h + 1, :]
        q_row = qk[H_A + h:H_A + h + 1, :]
        k_col = qk_t[:, h:h + 1]
        q_col = qk_t[:, H_A + h:H_A + h + 1]
        bh = beta[:, h:h + 1]
        eg = jnp.exp(g[:, H_A + h:H_A + h + 1])
        s = s0_ref[h]
        k_s = jnp.sum(s * k_col, 0, keepdims=True)
        q_s = jnp.sum(s * q_col, 0, keepdims=True)
        v_new = bh * (vs[h] - eg * k_s)
        o = eg * q_s + jnp.sum(q_row * k_row, -1, keepdims=True) * v_new
        sout_ref[h] = s * eg + k_col * v_new
        o = _rms_rows(o, on_ref[...]) * _silu(proj_ref[:, qkv_w + h * DK_A:qkv_w + (h + 1) * DK_A])
        og_ref[:, h * DK_A:(h + 1) * DK_A] = o


def _gdn_step(proj, conv_state, conv_w, hp, out_norm, s0):
    nb, pw = proj.shape
    qkv_w = 3 * H_A * DK_A
    z_w = H_A * DK_A
    og, s_new = pl.pallas_call(
        _gdn_step_kernel,
        grid=(nb,),
        in_specs=[pl.BlockSpec((None, 1, pw), lambda b: (b, 0, 0)),
                  pl.BlockSpec((None, CONV_W - 1, qkv_w), lambda b: (b, 0, 0)),
                  pl.BlockSpec((CONV_W, qkv_w), lambda b: (0, 0)),
                  pl.BlockSpec((2, LANES), lambda b: (0, 0)),
                  pl.BlockSpec((1, DK_A), lambda b: (0, 0)),
                  pl.BlockSpec((None, H_A, DK_A, DK_A), lambda b: (b, 0, 0, 0))],
        out_specs=[pl.BlockSpec((None, 1, z_w), lambda b: (b, 0, 0)),
                   pl.BlockSpec((None, H_A, DK_A, DK_A), lambda b: (b, 0, 0, 0))],
        out_shape=[jax.ShapeDtypeStruct((nb, 1, z_w), F32),
                   jax.ShapeDtypeStruct((nb, H_A, DK_A, DK_A), F32)],
        scratch_shapes=[pltpu.VMEM((LANES, DK_A), F32)],
        compiler_params=_cparams("parallel"),
        name="gdn_step",
    )(proj.reshape(nb, 1, pw), conv_state, conv_w, hp, out_norm, s0)
    return og.reshape(nb, z_w), s_new


def _ffn_kernel(h_ref, g_ref, wg_ref, wu_ref, wd_ref, o_ref, u_ref, acc_ref):
    f = pl.program_id(1)

    @pl.when(f == 0)
    def _():
        u_ref[...] = _rms_rows(h_ref[...], g_ref[...]).astype(u_ref.dtype)
        acc_ref[...] = jnp.zeros_like(acc_ref)

    u = u_ref[...]
    gate = _wdot(u, wg_ref[...])
    up = _wdot(u, wu_ref[...])
    acc_ref[...] += _wdot(_silu(gate) * up, wd_ref[...])

    @pl.when(f == pl.num_programs(1) - 1)
    def _():
        o_ref[...] = h_ref[...] + acc_ref[...]


def _ffn(h, gain, w_gu, w_down, *, tm, tf):
    m, dm = h.shape
    ff = w_down.shape[0]
    nf = ff // tf
    return pl.pallas_call(
        _ffn_kernel,
        grid=(m // tm, nf),
        in_specs=[pl.BlockSpec((tm, dm), lambda i, f: (i, 0)),
                  pl.BlockSpec((1, dm), lambda i, f: (0, 0)),
                  pl.BlockSpec((dm, tf), lambda i, f: (0, f)),
                  pl.BlockSpec((dm, tf), lambda i, f: (0, nf + f)),
                  pl.BlockSpec((tf, dm), lambda i, f: (f, 0))],
        out_specs=pl.BlockSpec((tm, dm), lambda i, f: (i, 0)),
        out_shape=jax.ShapeDtypeStruct((m, dm), F32),
        scratch_shapes=[pltpu.VMEM((tm, dm), _act_dtype(w_gu)),
                        pltpu.VMEM((tm, dm), F32)],
        compiler_params=_cparams("parallel", "arbitrary"),
        name="ffn_dense",
    )(h, gain.reshape(1, dm), w_gu, w_gu, w_down)


N_EXPERTS = 8
SEG_ALIGN = LANES


def _top2(logits):
    t = logits.shape[0]
    lane = lax.broadcasted_iota(jnp.int32, (t, LANES), 1)
    valid = lane < N_EXPERTS
    lg = jnp.where(valid, logits, NEG)
    mx = jnp.max(lg, -1, keepdims=True)
    e = jnp.where(valid, jnp.exp(lg - mx), 0.0)
    probs = e / jnp.sum(e, -1, keepdims=True)
    p1 = jnp.max(probs, -1, keepdims=True)
    i1 = jnp.min(jnp.where((probs == p1) & valid, lane, LANES), -1, keepdims=True)
    rest = jnp.where((lane == i1) | ~valid, -1.0, probs)
    p2 = jnp.max(rest, -1, keepdims=True)
    i2 = jnp.min(jnp.where(rest == p2, lane, LANES), -1, keepdims=True)
    tot = p1 + p2
    return i1, i2, p1 / tot, p2 / tot


def _router_kernel(h_ref, g_ref, r_ref, tri_ref, upper_ref, ut_ref, col_ref, row_ref, seg_ref,
                   *, n_valid):
    t = h_ref.shape[0]
    sub = tri_ref.shape[0]
    u = _rms_rows(h_ref[...], g_ref[...])
    ut_ref[...] = u.T.astype(BF16)
    i1, i2, g1, g2 = _top2(_wdot(u, r_ref[...]))
    lane = lax.broadcasted_iota(jnp.int32, (t, LANES), 1)
    tok = pl.program_id(0) * t + lax.broadcasted_iota(jnp.int32, (t, 1), 0)
    valid = tok < n_valid
    sel = jnp.where(valid & ((lane == i1) | (lane == i2)), 1.0, 0.0)
    counts = jnp.zeros((1, LANES), F32)
    pos_parts = []
    for k in range(t // sub):
        sel_k = sel[k * sub:(k + 1) * sub]
        pos_parts.append(counts + jnp.dot(tri_ref[...], sel_k.astype(BF16),
                                          preferred_element_type=F32))
        counts = counts + jnp.sum(sel_k, 0, keepdims=True)
    pos = jnp.concatenate(pos_parts, axis=0)
    nblk = jnp.floor((counts + (SEG_ALIGN - 1)) * (1.0 / SEG_ALIGN))
    nblk8 = jnp.broadcast_to(nblk, (8, LANES))
    start8 = jnp.dot(nblk8.astype(BF16), upper_ref[...], preferred_element_type=F32)
    dest = start8[0:1] * SEG_ALIGN + pos
    d1 = jnp.where(valid, jnp.sum(jnp.where(lane == i1, dest, 0.0), -1, keepdims=True), -1.0)
    d2 = jnp.where(valid, jnp.sum(jnp.where(lane == i2, dest, 0.0), -1, keepdims=True), -1.0)
    col = jnp.where(lane == 0, d1, jnp.where(lane == 1, d2,
                    jnp.where(lane == 2, g1, jnp.where(lane == 3, g2, 0.0))))
    col_ref[...] = col
    row_ref[...] = col.T[0:8, :]
    lane8 = lane[0:8]
    seg = jnp.where(lane8 < N_EXPERTS, start8,
                    jnp.where(lane8 < 2 * N_EXPERTS, pltpu.roll(nblk8, N_EXPERTS, 1), 0.0))
    seg_ref[...] = seg[0:1].astype(jnp.int32)


def _route(h, gain, router, *, tm, n_valid):
    m, dm = h.shape
    nt = m // tm
    sub = next(s for s in (1024, 640, 512, 256, LANES) if tm % s == 0)
    tri = jnp.tril(jnp.ones((sub, sub), F32), -1).astype(BF16)
    upper = jnp.triu(jnp.ones((LANES, LANES), F32), 1).astype(BF16)
    return pl.pallas_call(
        functools.partial(_router_kernel, n_valid=n_valid),
        grid=(nt,),
        in_specs=[pl.BlockSpec((tm, dm), lambda i: (i, 0)),
                  pl.BlockSpec((1, dm), lambda i: (0, 0)),
                  pl.BlockSpec((dm, LANES), lambda i: (0, 0)),
                  pl.BlockSpec((sub, sub), lambda i: (0, 0)),
                  pl.BlockSpec((LANES, LANES), lambda i: (0, 0))],
        out_specs=[pl.BlockSpec((None, dm, tm), lambda i: (i, 0, 0)),
                   pl.BlockSpec((tm, LANES), lambda i: (i, 0)),
                   pl.BlockSpec((None, 8, tm), lambda i: (i, 0, 0)),
                   pl.BlockSpec((None, 1, LANES), lambda i: (i, 0, 0))],
        out_shape=[jax.ShapeDtypeStruct((nt, dm, tm), BF16),
                   jax.ShapeDtypeStruct((m, LANES), F32),
                   jax.ShapeDtypeStruct((nt, 8, tm), F32),
                   jax.ShapeDtypeStruct((nt, 1, LANES), jnp.int32)],
        compiler_params=_cparams("parallel"),
        name="moe_route",
    )(h, gain.reshape(1, dm), router, tri, upper)


def _one_hot_rows(row0, n, d1_row, d2_row):
    ridx = (lax.broadcasted_iota(jnp.int32, (n, d1_row.shape[1]), 0) + row0).astype(F32)
    return jnp.where((ridx == d1_row) | (ridx == d2_row), 1.0, 0.0).astype(BF16)


def _experts_kernel(seg_ref, ut_ref, col_ref, row_ref, wg_ref, wu_ref, wd_ref, o_ref,
                    xs_scr, acc_scr):
    i, e, f = pl.program_id(0), pl.program_id(1), pl.program_id(2)
    last_f = pl.num_programs(2) - 1
    t = ut_ref.shape[1]
    start = seg_ref[i * LANES + e]
    nblk = seg_ref[i * LANES + N_EXPERTS + e]

    @pl.when((e == 0) & (f == 0))
    def _():
        o_ref[...] = jnp.zeros_like(o_ref)

    def gather(lb, nb):
        w = nb * LANES
        col = col_ref[...]
        ridx = (lax.broadcasted_iota(jnp.int32, (t, w), 1) + (start + lb) * LANES).astype(F32)
        p_t = jnp.where((ridx == col[:, 0:1]) | (ridx == col[:, 1:2]), 1.0, 0.0).astype(BF16)
        x_t = jnp.dot(ut_ref[...], p_t, preferred_element_type=F32).astype(BF16)
        for k in range(nb):
            xs_scr[lb + k] = x_t[:, k * LANES:(k + 1) * LANES]
            acc_scr[lb + k] = jnp.zeros(acc_scr.shape[1:], F32)

    def swiglu(lb, nb):
        x_t = jnp.concatenate([xs_scr[lb + k] for k in range(nb)], axis=1)
        gate_t = jnp.dot(wg_ref[...], x_t, preferred_element_type=F32)
        up_t = jnp.dot(wu_ref[...], x_t, preferred_element_type=F32)
        act_t = (_silu(gate_t) * up_t).astype(BF16)
        down_t = jnp.dot(wd_ref[...], act_t, preferred_element_type=F32)
        for k in range(nb):
            acc_scr[lb + k] += down_t[:, k * LANES:(k + 1) * LANES]

    def combine(lb, nb):
        w = nb * LANES
        col = col_ref[...]
        row0 = (start + lb) * LANES
        ridx = (lax.broadcasted_iota(jnp.int32, (t, w), 1) + row0).astype(F32)
        gs = jnp.sum(jnp.where(ridx == col[:, 0:1], col[:, 2:3], 0.0) +
                     jnp.where(ridx == col[:, 1:2], col[:, 3:4], 0.0), 0, keepdims=True)
        acc_t = jnp.concatenate([acc_scr[lb + k] for k in range(nb)], axis=1)
        p = _one_hot_rows(row0, w, row_ref[0:1, :], row_ref[1:2, :])
        o_ref[...] += jnp.dot((acc_t * gs).astype(BF16), p, preferred_element_type=F32)

    def for_blocks(*stages):
        def run(lb, nb):
            for stage in stages:
                stage(lb, nb)

        n4 = nblk // 4

        def body(j, carry):
            run(4 * j, 4)
            return carry

        lax.fori_loop(0, n4, body, 0)
        rem = nblk - 4 * n4

        @pl.when((rem & 2) != 0)
        def _():
            run(4 * n4, 2)

        @pl.when((rem & 1) != 0)
        def _():
            run(4 * n4 + (rem & 2), 1)

    @pl.when(f == 0)
    def _():
        for_blocks(gather, swiglu)

    @pl.when((f > 0) & (f < last_f))
    def _():
        for_blocks(swiglu)

    @pl.when(f == last_f)
    def _():
        for_blocks(swiglu, combine)


def _experts(ut, col, row, seg, wgu_t, wd_t, *, tf):
    nt, dm, tm = ut.shape
    ne, _, ff = wd_t.shape
    nf = ff // tf
    assert nf >= 2, "first and last F block are distinct steps"
    grid_spec = pltpu.PrefetchScalarGridSpec(
        num_scalar_prefetch=1,
        grid=(nt, ne, nf),
        in_specs=[pl.BlockSpec((None, dm, tm), lambda i, e, f, s: (i, 0, 0),
                               pipeline_mode=pl.Buffered(1)),
                  pl.BlockSpec((tm, LANES), lambda i, e, f, s: (i, 0),
                               pipeline_mode=pl.Buffered(1)),
                  pl.BlockSpec((None, 8, tm), lambda i, e, f, s: (i, 0, 0)),
                  pl.BlockSpec((None, tf, dm), lambda i, e, f, s: (e, f, 0)),
                  pl.BlockSpec((None, tf, dm), lambda i, e, f, s: (e, nf + f, 0)),
                  pl.BlockSpec((None, dm, tf), lambda i, e, f, s: (e, 0, f))],
        out_specs=pl.BlockSpec((dm, tm), lambda i, e, f, s: (0, i)),
        scratch_shapes=[pltpu.VMEM((tm // LANES, dm, LANES), BF16),
                        pltpu.VMEM((tm // LANES, dm, LANES), F32)])
    return pl.pallas_call(
        _experts_kernel,
        grid_spec=grid_spec,
        out_shape=jax.ShapeDtypeStruct((dm, nt * tm), F32),
        compiler_params=_cparams("parallel", "arbitrary", "arbitrary"),
        name="moe_experts",
    )(seg.reshape(-1), ut, col, row, wgu_t, wgu_t, wd_t)


def _ple_kernel(h_ref, g_ref, gw_ref, p_ref, pw_ref, *rest):
    o_ref = rest[-1]
    h = h_ref[...]
    if len(rest) == 2:
        h = h + rest[0][...].T
    gate = _sigmoid(_wdot(_rms_rows(h, g_ref[...]), gw_ref[...]))
    o_ref[...] = h + _wdot(p_ref[...], pw_ref[...]) * gate


def _ple(h, gain, gate_w, p, ple_w, *, tm, y_t=None):
    m, dm = h.shape
    pd = p.shape[1]
    in_specs = [pl.BlockSpec((tm, dm), lambda i: (i, 0)),
                pl.BlockSpec((1, dm), lambda i: (0, 0)),
                pl.BlockSpec((dm, dm), lambda i: (0, 0)),
                pl.BlockSpec((tm, pd), lambda i: (i, 0)),
                pl.BlockSpec((pd, dm), lambda i: (0, 0))]
    args = [h, gain.reshape(1, dm), gate_w, p, ple_w]
    if y_t is not None:
        in_specs.append(pl.BlockSpec((dm, tm), lambda i: (0, i)))
        args.append(y_t)
    return pl.pallas_call(
        _ple_kernel,
        grid=(m // tm,),
        in_specs=in_specs,
        out_specs=pl.BlockSpec((tm, dm), lambda i: (i, 0)),
        out_shape=jax.ShapeDtypeStruct((m, dm), F32),
        compiler_params=_cparams("parallel"),
        name="ple",
    )(*args)


def _band_attn_kernel(q_ref, kp_ref, kc_ref, vp_ref, vc_ref, o_ref, l_ref, *, span, tq):
    j = pl.program_id(2)
    q = q_ref[...] * (HD_B ** -0.5)
    kk = jnp.concatenate([kp_ref[...], kc_ref[...]], axis=0)
    vv = jnp.concatenate([vp_ref[...], vc_ref[...]], axis=0)
    qi = lax.broadcasted_iota(jnp.int32, (span, 2 * span), 0)
    ki = lax.broadcasted_iota(jnp.int32, (span, 2 * span), 1)
    dist = qi + span - ki
    band = (dist >= 0) & (dist <= span)
    lane = lax.broadcasted_iota(jnp.int32, (span, LANES), 1)
    for sb in range(tq // span):
        r0 = sb * span
        mask = band & (ki >= jnp.where(j > 0, 0, span)) if sb == 0 else band
        heads = [slice(h * HD_B, (h + 1) * HD_B) for h in range(HG)]
        ss = [jnp.where(mask, _dot_nt(q[r0:r0 + span, hs], kk[r0:r0 + 2 * span, hs]), NEG)
              for hs in heads]
        ms = [jnp.max(s, -1, keepdims=True) for s in ss]
        es = [jnp.exp(s - m).astype(BF16) for s, m in zip(ss, ms)]
        ones = jnp.ones((2 * span, LANES), BF16)
        dens = [_dot(e, ones) for e in es]
        outs = [_dot(e, vv[r0:r0 + 2 * span, hs]) * (1.0 / den[:, :HD_B])
                for e, den, hs in zip(es, dens, heads)]
        lse_tile = jnp.zeros((span, LANES), F32)
        for h in range(HG):
            lse_tile = jnp.where(lane == h, ms[h] + jnp.log(dens[h]), lse_tile)
        o_ref[r0:r0 + span, :] = jnp.concatenate(outs, axis=1)
        l_ref[r0:r0 + span, :] = lse_tile


def _band_attn(q, k, v, gi, *, batch, seq):
    win, dil = GROUPS[gi]
    span = win // dil
    n = seq // dil
    tq = min(4 * span, n)
    nb = n // tq
    sub = tq // span
    cur = lambda b, r, j: (r, b * nb + j, 0)
    prev = lambda b, r, j: (r, b * nb * sub + jnp.maximum(j * sub - 1, 0), 0)
    return pl.pallas_call(
        functools.partial(_band_attn_kernel, span=span, tq=tq),
        grid=(batch, dil, nb),
        in_specs=[pl.BlockSpec((None, tq, GW), cur),
                  pl.BlockSpec((None, span, GW), prev),
                  pl.BlockSpec((None, tq, GW), cur),
                  pl.BlockSpec((None, span, GW), prev),
                  pl.BlockSpec((None, tq, GW), cur)],
        out_specs=[pl.BlockSpec((None, tq, GW), cur),
                   pl.BlockSpec((None, tq, LANES), cur)],
        out_shape=[jax.ShapeDtypeStruct((dil, batch * n, GW), F32),
                   jax.ShapeDtypeStruct((dil, batch * n, LANES), F32)],
        compiler_params=_cparams("parallel", "parallel", "arbitrary"),
        name=f"band_attn_g{gi}",
    )(q, k, k, v, v)


def _merge_out_kernel(o0_ref, o1_ref, o2_ref, l0_ref, l1_ref, l2_ref, w_ref, r_ref, o_ref,
                      o_scr, l_scr, *, tm):
    for gi, (o_ph, l_ph) in enumerate(((o1_ref, l1_ref), (o2_ref, l2_ref))):
        d = o_ph.shape[0]
        for r in range(d):
            rows = pl.ds(r, tm // d, stride=d)
            l_scr[gi, rows, :] = l_ph[r]
            for c in range(GW // LANES):
                o_scr[gi, c, rows, :] = o_ph[r, :, c * LANES:(c + 1) * LANES]
    ls = [l0_ref[0], l_scr[0], l_scr[1]]

    def o_cols(g, c):
        if g == 0:
            return o0_ref[0, :, c * LANES:(c + 1) * LANES]
        return o_scr[g - 1, c]

    m = jnp.maximum(jnp.maximum(ls[0], ls[1]), ls[2])
    es = [jnp.exp(l - m) for l in ls]
    inv = 1.0 / (es[0] + es[1] + es[2])
    t = ls[0].shape[0]
    lo = lax.broadcasted_iota(jnp.int32, (t, LANES), 1) < HD_B
    cols = []
    for c in range(GW // LANES):
        acc = None
        for g in range(N_GROUPS):
            wt = es[g] * inv
            wexp = jnp.where(lo, wt[:, 2 * c:2 * c + 1], wt[:, 2 * c + 1:2 * c + 2])
            term = wexp * o_cols(g, c)
            acc = term if acc is None else acc + term
        cols.append(acc)
    o = jnp.concatenate(cols, axis=1).astype(BF16)
    o_ref[...] = r_ref[...] + jnp.dot(o, w_ref[...], preferred_element_type=F32)


def _merge_out(outs, lses, w, res, *, tm):
    m, dm = res.shape
    ph_spec = lambda a: pl.BlockSpec((a.shape[0], tm // a.shape[0], a.shape[2]),
                                     lambda i: (0, i, 0))
    return pl.pallas_call(
        functools.partial(_merge_out_kernel, tm=tm),
        grid=(m // tm,),
        in_specs=[ph_spec(a) for a in outs] + [ph_spec(a) for a in lses] +
                 [pl.BlockSpec((GW, dm), lambda i: (0, 0)),
                  pl.BlockSpec((tm, dm), lambda i: (i, 0))],
        out_specs=pl.BlockSpec((tm, dm), lambda i: (i, 0)),
        out_shape=jax.ShapeDtypeStruct((m, dm), F32),
        scratch_shapes=[pltpu.VMEM((N_GROUPS - 1, GW // LANES, tm, LANES), F32),
                        pltpu.VMEM((N_GROUPS - 1, tm, LANES), F32)],
        compiler_params=_cparams("parallel"),
        name="merge_out",
    )(*outs, *lses, w, res)


def _gather_attn_kernel(q_ref, kvn_ref, c0_ref, c1_ref, c2_ref, o_ref):
    caches = [c0_ref, c1_ref, c2_ref]
    outs, lses = [], []
    for g, (_, dil) in enumerate(GROUPS):
        q = q_ref[g] * (HD_B ** -0.5)
        kn, vn = kvn_ref[0, g], kvn_ref[1, g]
        kc, vc = caches[g][0], caches[g][1]
        rows = kc.shape[-1]
        s = jnp.sum(kc * q, 1, keepdims=True)
        row = lax.broadcasted_iota(jnp.int32, (1, 1, rows), 2)
        s = jnp.where((row & (dil - 1)) == 0, s, NEG)
        s_new = jnp.sum(kn * q, 1, keepdims=True)
        m = jnp.maximum(jnp.max(s, 2, keepdims=True), s_new)
        e = jnp.exp(s - m)
        e_new = jnp.exp(s_new - m)
        den = jnp.sum(e, 2, keepdims=True) + e_new
        outs.append((jnp.sum(e * vc, 2, keepdims=True) + e_new * vn) / den)
        lses.append(m + jnp.log(den))
    m = jnp.maximum(jnp.maximum(lses[0], lses[1]), lses[2])
    es = [jnp.exp(l - m) for l in lses]
    o_ref[...] = (es[0] * outs[0] + es[1] * outs[1] + es[2] * outs[2]) / (es[0] + es[1] + es[2])


def _gather_attn(q, kv_new, caches):
    nb = q.shape[0]
    span = GROUPS[0][0] // GROUPS[0][1]
    views = []
    for (win, dil), c in zip(GROUPS, caches):
        lb = c.shape[1]
        assert lb == win and lb // dil == span, "window buffer must hold the full window"
        assert dil & (dil - 1) == 0, "dilations are powers of two"
        views.append(jnp.transpose(c, (0, 2, 3, 4, 1)))
    return pl.pallas_call(
        _gather_attn_kernel,
        grid=(nb,),
        in_specs=[pl.BlockSpec((None, N_GROUPS, HG, HD_B, 1), lambda b: (b, 0, 0, 0, 0)),
                  pl.BlockSpec((None, 2, N_GROUPS, HG, HD_B, 1), lambda b: (b, 0, 0, 0, 0, 0))] +
                 [pl.BlockSpec((None, 2, HG, HD_B, v.shape[-1]), lambda b: (b, 0, 0, 0, 0))
                  for v in views],
        out_specs=pl.BlockSpec((None, HG, HD_B, 1), lambda b: (b, 0, 0, 0)),
        out_shape=jax.ShapeDtypeStruct((nb, HG, HD_B, 1), F32),
        compiler_params=_cparams("parallel"),
        name="gather_attn",
    )(q, kv_new, *views)


def _transpose_cast_kernel(x_ref, o_ref):
    o_ref[...] = x_ref[...].T.astype(BF16)


def _transpose_cast(w, *, tk, tn):
    ne, k, n = w.shape
    return pl.pallas_call(
        _transpose_cast_kernel,
        grid=(ne, k // tk, n // tn),
        in_specs=[pl.BlockSpec((None, tk, tn), lambda e, i, j: (e, i, j))],
        out_specs=pl.BlockSpec((None, tn, tk), lambda e, i, j: (e, j, i)),
        out_shape=jax.ShapeDtypeStruct((ne, n, k), BF16),
        compiler_params=_cparams("parallel", "parallel", "parallel"),
        name="transpose_cast",
    )(w)


def _prep_weights(a_w_in, a_A_log, a_dt_bias, a_w_out, w_kv, b_w_q, b_w_out, dense_w_gu,
                  dense_w_down, moe_router, moe_w_gu, moe_w_down, ple_w, ple_gate_w, k_norm,
                  b_q_norm):
    d_model, a_in = a_w_in.shape[1:]
    a_in_pad = -(-a_in // LANES) * LANES
    wf = {
        'a_w_in': jnp.pad(a_w_in[0], ((0, 0), (0, a_in_pad - a_in))),
        'a_w_out': a_w_out[0], 'w_kv': w_kv, 'b_w_q': b_w_q[0], 'b_w_out': b_w_out[0],
        'dense_w_gu': dense_w_gu[0], 'dense_w_down': dense_w_down[0],
        'router': jnp.pad(moe_router[0], ((0, 0), (0, LANES - moe_router.shape[2]))),
        'ple_w': ple_w, 'ple_gate_w': ple_gate_w,
    }
    shared = {}
    hp = jnp.stack([a_A_log[0], a_dt_bias[0]])
    shared['a_hp'] = jnp.pad(hp, ((0, 0), (H_A, LANES - 2 * H_A)))
    shared['moe_wgu_t'] = _transpose_cast(moe_w_gu[0], tk=d_model, tn=1024)
    shared['moe_wd_t'] = _transpose_cast(moe_w_down[0], tk=896, tn=d_model)
    shared['k_gain'] = jnp.tile(k_norm, HG).reshape(1, GW)
    shared['q_gain'] = jnp.tile(b_q_norm[0], HG).reshape(1, GW)
    w_prompt = dict(shared, **{k: v.astype(BF16) for k, v in wf.items()})
    w_sample = dict(shared, **wf)
    return w_prompt, w_sample


def _layer0(x, p0, w, P, *, tm, in_tn, mixer):
    proj = _norm_mm(x, P['a_norm'][0], w['a_w_in'], tm=min(tm, 512), tn=in_tn)
    og, s_new = mixer(proj)
    h = _mm_res(og, w['a_w_out'], x, tm=tm)
    h = _ffn(h, P['ffn_norm'][0], w['dense_w_gu'], w['dense_w_down'], tm=tm, tf=512)
    h = _ple(h, P['ple_norm'][0], w['ple_gate_w'][0], p0, w['ple_w'][0], tm=tm)
    return h, proj, s_new


def _layer1_tail(h, p1, w, P, *, tm):
    m = h.shape[0]
    tmr = 15 * LANES if m >= 2048 else LANES * max(1, min(8, m // LANES))
    hp = jnp.pad(h, ((0, -m % tmr), (0, 0)))
    ut, col, row, seg = _route(hp, P['ffn_norm'][1], w['router'], tm=tmr, n_valid=m)
    y_t = _experts(ut, col, row, seg, w['moe_wgu_t'], w['moe_wd_t'], tf=896)
    if tm % LANES:
        h, p1, tm = hp, jnp.pad(p1, ((0, -m % tmr), (0, 0))), tmr
    return _ple(h, P['ple_norm'][1], w['ple_gate_w'][1], p1, w['ple_w'][1], tm=tm, y_t=y_t)[:m]


def kernel(x_prompt, x_sample, p_prompt, p_sample, state_conv, state_delta, cache_kv_w128, cache_kv_w512, cache_kv_w2048, a_norm, a_w_in, a_conv_w, a_A_log, a_dt_bias, a_out_norm, a_w_out, kv_norm, w_kv, k_norm, b_norm, b_w_q, b_q_norm, b_w_out, ffn_norm, dense_w_gu, dense_w_down, moe_router, moe_w_gu, moe_w_down, ple_w, ple_norm, ple_gate_w):
    assert a_w_in.shape[0] == 1 and b_w_q.shape[0] == 1, "one mixer of each kind"
    bp, sp, dm = x_prompt.shape
    bs, ls, _ = x_sample.shape
    assert ls == 1, "sample group decodes one token per sequence"
    qkv_w = 3 * H_A * DK_A
    P = dict(a_norm=a_norm, ffn_norm=ffn_norm, ple_norm=ple_norm)
    w, ws = _prep_weights(a_w_in, a_A_log, a_dt_bias, a_w_out, w_kv, b_w_q, b_w_out, dense_w_gu,
                          dense_w_down, moe_router, moe_w_gu, moe_w_down, ple_w, ple_gate_w,
                          k_norm, b_q_norm)
    a_in_pad = w['a_w_in'].shape[1]
    in_tn = a_in_pad // 3 if a_in_pad % (3 * LANES) == 0 else LANES
    conv_w = a_conv_w[0]
    out_norm = a_out_norm[0].reshape(1, DK_A)

    mp = bp * sp
    tm = min(1024, sp)
    xp = x_prompt.reshape(mp, dm)
    conv0 = jnp.zeros((bp, CONV_W - 1, qkv_w), F32)
    s0 = jnp.zeros((bp, H_A, DK_A, DK_A), F32)
    h, proj, delta_p = _layer0(
        xp, p_prompt[0].reshape(mp, -1), w, P, tm=tm, in_tn=in_tn,
        mixer=lambda pr: _gdn_prompt(pr, conv_w, w['a_hp'], out_norm, conv0, s0, batch=bp, seq=sp))
    conv_p = proj.reshape(bp, sp, -1)[:, sp - (CONV_W - 1):, :qkv_w][None]

    cos, sin = _rope_tables(jnp.arange(sp, dtype=jnp.int32))
    dils = [d for _, d in GROUPS]
    tmp = min(512, sp)
    kv, *kv_ph = _proj_rope(h, kv_norm, w['w_kv'], w['k_gain'], cos, sin, tm=tmp, n_rope=N_GROUPS,
                            natural=True, dils=dils + dils)
    q_ph = _proj_rope(h, b_norm[0], w['b_w_q'], w['q_gain'], cos, sin, tm=tmp, n_rope=N_GROUPS,
                      natural=False, dils=dils)
    outs, lses = [], []
    for gi in range(N_GROUPS):
        o, lse = _band_attn(q_ph[gi], kv_ph[gi], kv_ph[N_GROUPS + gi], gi, batch=bp, seq=sp)
        outs.append(o)
        lses.append(lse)
    h = _merge_out(outs, lses, w['b_w_out'], h, tm=tm)
    y_prompt = _layer1_tail(h, p_prompt[1].reshape(mp, -1), w, P, tm=tm).reshape(bp, sp, dm)
    kv3 = kv.reshape(bp, sp, 2 * N_GROUPS * GW)
    kv_p = []
    for gi, (win, _) in enumerate(GROUPS):
        rows = kv3[:, sp - min(win, sp):]
        k_g = rows[:, :, gi * GW:(gi + 1) * GW]
        v_g = rows[:, :, (N_GROUPS + gi) * GW:(N_GROUPS + gi + 1) * GW]
        kv_p.append(jnp.stack([k_g, v_g], axis=2).reshape(bp, -1, 2, HG, HD_B))

    xs = x_sample.reshape(bs, dm)
    hs, proj_s, delta_s = _layer0(
        xs, p_sample[0].reshape(bs, -1), ws, P, tm=bs, in_tn=in_tn,
        mixer=lambda pr: _gdn_step(pr, state_conv[0], conv_w, ws['a_hp'], out_norm, state_delta[0]))
    conv_s = jnp.concatenate([state_conv[0][:, 1:], proj_s[:, None, :qkv_w]], axis=1)[None]
    cos_s, sin_s = _rope_tables(jnp.full((bs,), PAST_LEN, jnp.int32))
    kv_s, = _proj_rope(hs, kv_norm, ws['w_kv'], ws['k_gain'], cos_s, sin_s, tm=bs,
                       n_rope=N_GROUPS, natural=True, dils=())
    q_s, = _proj_rope(hs, b_norm[0], ws['b_w_q'], ws['q_gain'], cos_s, sin_s, tm=bs,
                      n_rope=N_GROUPS, natural=True, dils=())
    o_s = _gather_attn(q_s.reshape(bs, N_GROUPS, HG, HD_B, 1),
                       kv_s.reshape(bs, 2, N_GROUPS, HG, HD_B, 1),
                       (cache_kv_w128, cache_kv_w512, cache_kv_w2048))
    hs = _mm_res(o_s.reshape(bs, GW), ws['b_w_out'], hs, tm=bs)
    y_sample = _layer1_tail(hs, p_sample[1].reshape(bs, -1), ws, P, tm=bs).reshape(bs, 1, dm)
    kvs5 = kv_s.reshape(bs, 1, 2, N_GROUPS, HG, HD_B)
    kv_sn = [kvs5[:, :, :, gi] for gi in range(N_GROUPS)]

    return (y_prompt, y_sample, conv_p, conv_s, delta_p[None], delta_s[None],
            kv_p[0], kv_sn[0], kv_p[1], kv_sn[1], kv_p[2], kv_sn[2])
```

```python
import functools

import jax
import jax.numpy as jnp
from jax import lax
from jax.experimental import pallas as pl
from jax.experimental.pallas import tpu as pltpu

F32 = jnp.float32
BF16 = jnp.bfloat16

EPS = 1e-6
PAST_LEN = 16384
GROUPS = ((128, 1), (512, 4), (2048, 16))
N_GROUPS = len(GROUPS)
HG = 8
HD_B = 64
ROT_DIM = HD_B // 4
ROPE_THETA = 500000.0
GW = HG * HD_B
H_A = 8
DK_A = 128
CONV_W = 4
CHUNK = 64
LANES = 128
VMEM_LIMIT = 57 * 1024 * 1024
NEG = -1e30


def _cparams(*sem):
    return pltpu.CompilerParams(dimension_semantics=sem, vmem_limit_bytes=VMEM_LIMIT)


def _rms_rows(x, gain):
    return x * lax.rsqrt(jnp.mean(x * x, -1, keepdims=True) + EPS) * gain


def _silu(x):
    return x * (1.0 / (1.0 + jnp.exp(-x)))


def _sigmoid(x):
    return 1.0 / (1.0 + jnp.exp(-x))


def _dot(a, b):
    return jnp.dot(a.astype(BF16), b.astype(BF16), preferred_element_type=F32)


def _wdot(a, w):
    dot = functools.partial(jnp.dot, preferred_element_type=F32)
    if w.dtype == BF16:
        return dot(a.astype(BF16), w)
    a = a.astype(F32)
    a_hi = a.astype(BF16)
    a_lo = (a - a_hi.astype(F32)).astype(BF16)
    w_hi = w.astype(BF16)
    w_lo = (w - w_hi.astype(F32)).astype(BF16)
    return dot(a_hi, w_hi) + (dot(a_lo, w_hi) + dot(a_hi, w_lo))


def _act_dtype(w):
    return BF16 if w.dtype == BF16 else F32


def _dot_nt(a, b):
    return lax.dot_general(a.astype(BF16), b.astype(BF16), (((1,), (1,)), ((), ())),
                           preferred_element_type=F32)


def _dot_tn(a, b):
    return lax.dot_general(a.astype(BF16), b.astype(BF16), (((0,), (0,)), ((), ())),
                           preferred_element_type=F32)


def _head_norm_rope(x, hgain, cos, sin):
    t = x.shape[0]
    lane = lax.broadcasted_iota(jnp.int32, (t, LANES), 1)
    d = lane & (HD_B - 1)
    r = lax.broadcasted_iota(jnp.int32, (LANES, LANES), 0)
    c_ = lax.broadcasted_iota(jnp.int32, (LANES, LANES), 1)
    same_head = jnp.where((r < HD_B) == (c_ < HD_B), 1.0, 0.0).astype(BF16)
    dot = functools.partial(jnp.dot, preferred_element_type=F32)
    outs = []
    for c in range(GW // LANES):
        sl = slice(c * LANES, (c + 1) * LANES)
        xb = x[:, sl]
        sq = xb * xb
        sq_hi = sq.astype(BF16)
        sq_lo = (sq - sq_hi.astype(F32)).astype(BF16)
        ssq = dot(sq_hi, same_head) + dot(sq_lo, same_head)
        yb = xb * lax.rsqrt(ssq * (1.0 / HD_B) + EPS) * hgain[:, sl]
        half = ROT_DIM // 2
        rot = jnp.where(d < half, pltpu.roll(yb, LANES - half, 1), pltpu.roll(yb, half, 1))
        outs.append(yb * cos[:, sl] + rot * sin[:, sl])
    return jnp.concatenate(outs, axis=1)


def _norm_mm_kernel(x_ref, g_ref, w_ref, o_ref, *, tn):
    u = _rms_rows(x_ref[...], g_ref[...]).astype(_act_dtype(w_ref))
    for j in range(w_ref.shape[1] // tn):
        cols = slice(j * tn, (j + 1) * tn)
        o_ref[:, cols] = _wdot(u, w_ref[:, cols])


def _norm_mm(x, gain, w, *, tm, tn):
    m, k = x.shape
    n = w.shape[1]
    return pl.pallas_call(
        functools.partial(_norm_mm_kernel, tn=tn),
        grid=(m // tm,),
        in_specs=[pl.BlockSpec((tm, k), lambda i: (i, 0)),
                  pl.BlockSpec((1, k), lambda i: (0, 0)),
                  pl.BlockSpec((k, n), lambda i: (0, 0))],
        out_specs=pl.BlockSpec((tm, n), lambda i: (i, 0)),
        out_shape=jax.ShapeDtypeStruct((m, n), F32),
        compiler_params=_cparams("parallel"),
        name="norm_mm",
    )(x, gain.reshape(1, k), w)


def _proj_rope_kernel(x_ref, g_ref, w_ref, hg_ref, cos_ref, sin_ref, *rest,
                      n_rope, natural, columns, dils, tm):
    n_out = int(natural) + int(columns) + len(dils)
    outs, (slab_ref,) = rest[:n_out], rest[n_out:]
    nat_ref = outs[0] if natural else None
    col_ref = outs[int(natural)] if columns else None
    ph_refs = outs[int(natural) + int(columns):]
    u = _rms_rows(x_ref[...], g_ref[...]).astype(_act_dtype(w_ref))
    slab = 0
    for jj in range(w_ref.shape[1] // GW):
        cols = slice(jj * GW, (jj + 1) * GW)
        y = _wdot(u, w_ref[:, cols])
        if jj < n_rope:
            y = _head_norm_rope(y, hg_ref[...], cos_ref[...], sin_ref[...])
        if natural:
            nat_ref[:, cols] = y
        if columns:
            col_ref[cols, :] = y.T
        if jj >= len(dils):
            continue
        d = dils[jj]
        if d == 1:
            ph_refs[jj][0] = y.astype(BF16)
            continue
        for c in range(GW // LANES):
            slab_ref[slab, c] = y[:, c * LANES:(c + 1) * LANES]
        for r in range(d):
            for c in range(GW // LANES):
                ph_refs[jj][r, :, c * LANES:(c + 1) * LANES] = (
                    slab_ref[slab, c, pl.ds(r, tm // d, stride=d), :].astype(BF16))
        slab += 1


def _proj_rope(x, gain, w, hgain, cos, sin, *, tm, n_rope, natural, dils, columns=False):
    m, k = x.shape
    n = w.shape[1]
    pos_blocks = cos.shape[0] // tm
    out_specs, out_shape = [], []
    if natural:
        out_specs.append(pl.BlockSpec((tm, n), lambda i: (i, 0)))
        out_shape.append(jax.ShapeDtypeStruct((m, n), F32))
    if columns:
        out_specs.append(pl.BlockSpec((n, tm), lambda i: (0, i)))
        out_shape.append(jax.ShapeDtypeStruct((n, m), F32))
    for d in dils:
        out_specs.append(pl.BlockSpec((d, tm // d, GW), lambda i: (0, i, 0)))
        out_shape.append(jax.ShapeDtypeStruct((d, m // d, GW), BF16))
    n_slabs = max(1, sum(d > 1 for d in dils))
    return pl.pallas_call(
        functools.partial(_proj_rope_kernel, n_rope=n_rope, natural=natural, columns=columns,
                          dils=tuple(dils), tm=tm),
        grid=(m // tm,),
        in_specs=[pl.BlockSpec((tm, k), lambda i: (i, 0)),
                  pl.BlockSpec((1, k), lambda i: (0, 0)),
                  pl.BlockSpec((k, n), lambda i: (0, 0)),
                  pl.BlockSpec((1, GW), lambda i: (0, 0)),
                  pl.BlockSpec((tm, GW), lambda i: (i % pos_blocks, 0)),
                  pl.BlockSpec((tm, GW), lambda i: (i % pos_blocks, 0))],
        out_specs=out_specs,
        out_shape=out_shape,
        scratch_shapes=[pltpu.VMEM((n_slabs, GW // LANES, tm, LANES), F32)],
        compiler_params=_cparams("parallel"),
        name="proj_rope",
    )(x, gain.reshape(1, k), w, hgain, cos, sin)


def _rope_tables(pos):
    half = ROT_DIM // 2
    inv = ROPE_THETA ** (-jnp.arange(half, dtype=F32) * 2.0 / ROT_DIM)
    ang = pos.astype(F32)[:, None] * inv[None]
    c, s = jnp.cos(ang), jnp.sin(ang)
    n = pos.shape[0]
    cos_h = jnp.concatenate([c, c, jnp.ones((n, HD_B - ROT_DIM), F32)], 1)
    sin_h = jnp.concatenate([-s, s, jnp.zeros((n, HD_B - ROT_DIM), F32)], 1)
    return jnp.tile(cos_h, (1, HG)), jnp.tile(sin_h, (1, HG))


def _mm_res_kernel(x_ref, w_ref, r_ref, o_ref):
    o_ref[...] = r_ref[...] + _wdot(x_ref[...], w_ref[...])


def _mm_res(x, w, res, *, tm):
    m, k = x.shape
    n = w.shape[1]
    return pl.pallas_call(
        _mm_res_kernel,
        grid=(m // tm,),
        in_specs=[pl.BlockSpec((tm, k), lambda i: (i, 0)),
                  pl.BlockSpec((k, n), lambda i: (0, 0)),
                  pl.BlockSpec((tm, n), lambda i: (i, 0))],
        out_specs=pl.BlockSpec((tm, n), lambda i: (i, 0)),
        out_shape=jax.ShapeDtypeStruct((m, n), F32),
        compiler_params=_cparams("parallel"),
        name="mm_res",
    )(x, w, res)


def _unit_lower_inverses(mats):
    c = mats[0].shape[0]
    row = lax.broadcasted_iota(jnp.int32, (c, c), 0)
    col = lax.broadcasted_iota(jnp.int32, (c, c), 1)
    eye = jnp.where(row == col, 1.0, 0.0).astype(F32)
    ts = None
    b = 1
    while b < c:
        sel = ((row ^ col) < 2 * b) & ((row & b) != 0) & ((col & b) == 0)
        lows = [jnp.where(sel, a, 0.0) for a in mats]
        if ts is None:
            ts = [eye - low for low in lows]
        else:
            tl = [_dot(t, low) for t, low in zip(ts, lows)]
            ts = [t - _dot(x, t) for t, x in zip(ts, tl)]
        b *= 2
    return ts


def _gdn_head_params(ba, hp):
    beta = _sigmoid(ba)
    x = ba + hp[1:2, :]
    softplus = jnp.maximum(x, 0.0) + jnp.log(1.0 + jnp.exp(-jnp.abs(x)))
    g = -jnp.exp(hp[0:1, :]) * softplus
    return beta, g


def _gdn_kernel(qkv_ref, z_ref, ba_ref, cw_ref, hp_ref, on_ref, conv0_ref, s0_ref,
                og_ref, sout_ref, xbuf, s_scr, *, C, nch):
    n = pl.program_id(1)
    R = nch * C
    pad = 8

    @pl.when(n == 0)
    def _():
        xbuf[pad - (CONV_W - 1):pad, :] = conv0_ref[...]
        s_scr[...] = s0_ref[...]

    xbuf[pad:pad + R, :] = qkv_ref[...]

    def conv_cols(c0):
        acc = None
        for j in range(CONV_W):
            r0 = pad - (CONV_W - 1) + j
            term = xbuf[r0:r0 + R, c0:c0 + DK_A] * cw_ref[j:j + 1, c0:c0 + DK_A]
            acc = term if acc is None else acc + term
        return _silu(acc)

    beta, g = _gdn_head_params(ba_ref[...], hp_ref[...])
    rr = lax.broadcasted_iota(jnp.int32, (R, R), 0)
    rc = lax.broadcasted_iota(jnp.int32, (R, R), 1)
    blocktri = ((rr >= rc) & ((rr ^ rc) < C)).astype(F32)
    gcum = jnp.dot(blocktri, g, preferred_element_type=F32, precision=lax.Precision.HIGHEST)
    gcum_t = gcum.T
    row = lax.broadcasted_iota(jnp.int32, (C, C), 0)
    col = lax.broadcasted_iota(jnp.int32, (C, C), 1)
    incl = row >= col
    strict = row > col

    units = [(c, h) for c in range(nch) for h in range(H_A)]
    qs, ks, vs = {}, {}, {}
    for h in range(H_A):
        q = conv_cols(h * DK_A)
        k = conv_cols((H_A + h) * DK_A)
        v = conv_cols((2 * H_A + h) * DK_A)
        q = q * lax.rsqrt(jnp.sum(q * q, -1, keepdims=True) + EPS) * (DK_A ** -0.5)
        k = k * lax.rsqrt(jnp.sum(k * k, -1, keepdims=True) + EPS)
        for c in range(nch):
            rs = slice(c * C, (c + 1) * C)
            qs[c, h], ks[c, h], vs[c, h] = q[rs], k[rs], v[rs]

    bcs, gcs, decays, kbs = {}, {}, {}, {}
    for c, h in units:
        rs = slice(c * C, (c + 1) * C)
        bcs[c, h] = beta[rs, h:h + 1]
        gcs[c, h] = gcum[rs, H_A + h:H_A + h + 1]
        gr = gcum_t[H_A + h:H_A + h + 1, rs]
        decays[c, h] = jnp.exp(jnp.where(incl, gcs[c, h] - gr, NEG))
        kbs[c, h] = ks[c, h] * bcs[c, h]
    grams = {u: _dot_nt(jnp.concatenate([kbs[u], qs[u]], axis=0), ks[u]) for u in units}
    a_mats = [jnp.where(strict, grams[u][:C] * decays[u], 0.0) for u in units]
    aqks = {u: grams[u][C:] * decays[u] for u in units}
    t_mats = dict(zip(units, _unit_lower_inverses(a_mats)))
    egs = {u: jnp.exp(gcs[u]) for u in units}
    sols = {u: _dot(t_mats[u], jnp.concatenate([vs[u] * bcs[u], kbs[u] * egs[u]], axis=1))
            for u in units}

    states = [s_scr[h] for h in range(H_A)]
    for c in range(nch):
        rs = slice(c * C, (c + 1) * C)
        for h in range(H_A):
            u = (c, h)
            g_last = gcs[u][C - 1:C, :]
            ws = _dot(jnp.concatenate([sols[u][:, DK_A:], qs[u] * egs[u]], axis=0), states[h])
            v_new = sols[u][:, :DK_A] - ws[:C]
            o = ws[C:] + _dot(aqks[u], v_new)
            kd = ks[u] * jnp.exp(g_last - gcs[u])
            states[h] = states[h] * jnp.exp(g_last) + _dot_tn(kd, v_new)
            o = _rms_rows(o, on_ref[...]) * _silu(z_ref[rs, h * DK_A:(h + 1) * DK_A])
            og_ref[rs, h * DK_A:(h + 1) * DK_A] = o
    for h in range(H_A):
        s_scr[h] = states[h]

    xbuf[pad - (CONV_W - 1):pad, :] = xbuf[pad + R - (CONV_W - 1):pad + R, :]

    @pl.when(n == pl.num_programs(1) - 1)
    def _():
        sout_ref[...] = s_scr[...]


def _gdn_prompt(proj, conv_w, hp, out_norm, conv0, s0, *, batch, seq):
    C = min(CHUNK, seq)
    nch = next(n for n in (4, 2, 1) if seq % (n * C) == 0)
    R = nch * C
    nc = seq // R
    qkv_w = 3 * H_A * DK_A
    z_w = H_A * DK_A
    return pl.pallas_call(
        functools.partial(_gdn_kernel, C=C, nch=nch),
        grid=(batch, nc),
        in_specs=[pl.BlockSpec((R, qkv_w), lambda b, n: (b * nc + n, 0)),
                  pl.BlockSpec((R, z_w), lambda b, n: (b * nc + n, qkv_w // z_w)),
                  pl.BlockSpec((R, LANES), lambda b, n: (b * nc + n, (qkv_w + z_w) // LANES)),
                  pl.BlockSpec((CONV_W, qkv_w), lambda b, n: (0, 0)),
                  pl.BlockSpec((2, LANES), lambda b, n: (0, 0)),
                  pl.BlockSpec((1, DK_A), lambda b, n: (0, 0)),
                  pl.BlockSpec((None, CONV_W - 1, qkv_w), lambda b, n: (b, 0, 0)),
                  pl.BlockSpec((None, H_A, DK_A, DK_A), lambda b, n: (b, 0, 0, 0))],
        out_specs=[pl.BlockSpec((R, z_w), lambda b, n: (b * nc + n, 0)),
                   pl.BlockSpec((None, H_A, DK_A, DK_A), lambda b, n: (b, 0, 0, 0))],
        out_shape=[jax.ShapeDtypeStruct((batch * seq, z_w), F32),
                   jax.ShapeDtypeStruct((batch, H_A, DK_A, DK_A), F32)],
        scratch_shapes=[pltpu.VMEM((R + 8, qkv_w), F32),
                        pltpu.VMEM((H_A, DK_A, DK_A), F32)],
        compiler_params=_cparams("parallel", "arbitrary"),
        name="gdn_chunked",
    )(proj, proj, proj, conv_w, hp, out_norm, conv0, s0)


def _gdn_step_kernel(proj_ref, conv_ref, cw_ref, hp_ref, on_ref, s0_ref, og_ref, sout_ref, qk_scr):
    qkv_w = 3 * H_A * DK_A
    z_w = H_A * DK_A

    def conv_cols(c0):
        sl = slice(c0, c0 + DK_A)
        acc = proj_ref[:, sl] * cw_ref[CONV_W - 1:CONV_W, sl]
        for j in range(CONV_W - 1):
            acc = acc + conv_ref[j:j + 1, sl] * cw_ref[j:j + 1, sl]
        return _silu(acc)

    beta, g = _gdn_head_params(proj_ref[:, qkv_w + z_w:qkv_w + z_w + LANES], hp_ref[...])
    qk_scr[...] = jnp.zeros_like(qk_scr)
    vs = []
    for h in range(H_A):
        q = conv_cols(h * DK_A)
        k = conv_cols((H_A + h) * DK_A)
        vs.append(conv_cols((2 * H_A + h) * DK_A))
        qk_scr[H_A + h:H_A + h + 1, :] = (
            q * lax.rsqrt(jnp.sum(q * q, -1, keepdims=True) + EPS) * (DK_A ** -0.5))
        qk_scr[h:h + 1, :] = k * lax.rsqrt(jnp.sum(k * k, -1, keepdims=True) + EPS)
    qk = qk_scr[...]
    qk_t = qk.T
    for h in range(H_A):
        k_row = qk[h:h + 1, :]
        q_row = qk[H_A + h:H_A + h + 1, :]
        k_col = qk_t[:, h:h + 1]
        q_col = qk_t[:, H_A + h:H_A + h + 1]
        bh = beta[:, h:h + 1]
        eg = jnp.exp(g[:, H_A + h:H_A + h + 1])
        s = s0_ref[h]
        k_s = jnp.sum(s * k_col, 0, keepdims=True)
        q_s = jnp.sum(s * q_col, 0, keepdims=True)
        v_new = bh * (vs[h] - eg * k_s)
        o = eg * q_s + jnp.sum(q_row * k_row, -1, keepdims=True) * v_new
        sout_ref[h] = s * eg + k_col * v_new
        o = _rms_rows(o, on_ref[...]) * _silu(proj_ref[:, qkv_w + h * DK_A:qkv_w + (h + 1) * DK_A])
        og_ref[:, h * DK_A:(h + 1) * DK_A] = o


def _gdn_step(proj, conv_state, conv_w, hp, out_norm, s0):
    nb, pw = proj.shape
    qkv_w = 3 * H_A * DK_A
    z_w = H_A * DK_A
    og, s_new = pl.pallas_call(
        _gdn_step_kernel,
        grid=(nb,),
        in_specs=[pl.BlockSpec((None, 1, pw), lambda b: (b, 0, 0)),
                  pl.BlockSpec((None, CONV_W - 1, qkv_w), lambda b: (b, 0, 0)),
                  pl.BlockSpec((CONV_W, qkv_w), lambda b: (0, 0)),
                  pl.BlockSpec((2, LANES), lambda b: (0, 0)),
                  pl.BlockSpec((1, DK_A), lambda b: (0, 0)),
                  pl.BlockSpec((None, H_A, DK_A, DK_A), lambda b: (b, 0, 0, 0))],
        out_specs=[pl.BlockSpec((None, 1, z_w), lambda b: (b, 0, 0)),
                   pl.BlockSpec((None, H_A, DK_A, DK_A), lambda b: (b, 0, 0, 0))],
        out_shape=[jax.ShapeDtypeStruct((nb, 1, z_w), F32),
                   jax.ShapeDtypeStruct((nb, H_A, DK_A, DK_A), F32)],
        scratch_shapes=[pltpu.VMEM((LANES, DK_A), F32)],
        compiler_params=_cparams("parallel"),
        name="gdn_step",
    )(proj.reshape(nb, 1, pw), conv_state, conv_w, hp, out_norm, s0)
    return og.reshape(nb, z_w), s_new


def _ffn_kernel(h_ref, g_ref, wg_ref, wu_ref, wd_ref, o_ref, u_ref, acc_ref):
    f = pl.program_id(1)

    @pl.when(f == 0)
    def _():
        u_ref[...] = _rms_rows(h_ref[...], g_ref[...]).astype(u_ref.dtype)
        acc_ref[...] = jnp.zeros_like(acc_ref)

    u = u_ref[...]
    gate = _wdot(u, wg_ref[...])
    up = _wdot(u, wu_ref[...])
    acc_ref[...] += _wdot(_silu(gate) * up, wd_ref[...])

    @pl.when(f == pl.num_programs(1) - 1)
    def _():
        o_ref[...] = h_ref[...] + acc_ref[...]


def _ffn(h, gain, w_gu, w_down, *, tm, tf):
    m, dm = h.shape
    ff = w_down.shape[0]
    nf = ff // tf
    return pl.pallas_call(
        _ffn_kernel,
        grid=(m // tm, nf),
        in_specs=[pl.BlockSpec((tm, dm), lambda i, f: (i, 0)),
                  pl.BlockSpec((1, dm), lambda i, f: (0, 0)),
                  pl.BlockSpec((dm, tf), lambda i, f: (0, f)),
                  pl.BlockSpec((dm, tf), lambda i, f: (0, nf + f)),
                  pl.BlockSpec((tf, dm), lambda i, f: (f, 0))],
        out_specs=pl.BlockSpec((tm, dm), lambda i, f: (i, 0)),
        out_shape=jax.ShapeDtypeStruct((m, dm), F32),
        scratch_shapes=[pltpu.VMEM((tm, dm), _act_dtype(w_gu)),
                        pltpu.VMEM((tm, dm), F32)],
        compiler_params=_cparams("parallel", "arbitrary"),
        name="ffn_dense",
    )(h, gain.reshape(1, dm), w_gu, w_gu, w_down)


N_EXPERTS = 8
SEG_ALIGN = LANES


def _top2(logits):
    t = logits.shape[0]
    lane = lax.broadcasted_iota(jnp.int32, (t, LANES), 1)
    valid = lane < N_EXPERTS
    lg = jnp.where(valid, logits, NEG)
    mx = jnp.max(lg, -1, keepdims=True)
    e = jnp.where(valid, jnp.exp(lg - mx), 0.0)
    probs = e / jnp.sum(e, -1, keepdims=True)
    p1 = jnp.max(probs, -1, keepdims=True)
    i1 = jnp.min(jnp.where((probs == p1) & valid, lane, LANES), -1, keepdims=True)
    rest = jnp.where((lane == i1) | ~valid, -1.0, probs)
    p2 = jnp.max(rest, -1, keepdims=True)
    i2 = jnp.min(jnp.where(rest == p2, lane, LANES), -1, keepdims=True)
    tot = p1 + p2
    return i1, i2, p1 / tot, p2 / tot


def _router_kernel(h_ref, g_ref, r_ref, tri_ref, upper_ref, ut_ref, col_ref, row_ref, seg_ref,
                   *, n_valid):
    t = h_ref.shape[0]
    sub = tri_ref.shape[0]
    u = _rms_rows(h_ref[...], g_ref[...])
    ut_ref[...] = u.T.astype(BF16)
    i1, i2, g1, g2 = _top2(_wdot(u, r_ref[...]))
    lane = lax.broadcasted_iota(jnp.int32, (t, LANES), 1)
    tok = pl.program_id(0) * t + lax.broadcasted_iota(jnp.int32, (t, 1), 0)
    valid = tok < n_valid
    sel = jnp.where(valid & ((lane == i1) | (lane == i2)), 1.0, 0.0)
    counts = jnp.zeros((1, LANES), F32)
    pos_parts = []
    for k in range(t // sub):
        sel_k = sel[k * sub:(k + 1) * sub]
        pos_parts.append(counts + jnp.dot(tri_ref[...], sel_k.astype(BF16),
                                          preferred_element_type=F32))
        counts = counts + jnp.sum(sel_k, 0, keepdims=True)
    pos = jnp.concatenate(pos_parts, axis=0)
    nblk = jnp.floor((counts + (SEG_ALIGN - 1)) * (1.0 / SEG_ALIGN))
    nblk8 = jnp.broadcast_to(nblk, (8, LANES))
    start8 = jnp.dot(nblk8.astype(BF16), upper_ref[...], preferred_element_type=F32)
    dest = start8[0:1] * SEG_ALIGN + pos
    d1 = jnp.where(valid, jnp.sum(jnp.where(lane == i1, dest, 0.0), -1, keepdims=True), -1.0)
    d2 = jnp.where(valid, jnp.sum(jnp.where(lane == i2, dest, 0.0), -1, keepdims=True), -1.0)
    col = jnp.where(lane == 0, d1, jnp.where(lane == 1, d2,
                    jnp.where(lane == 2, g1, jnp.where(lane == 3, g2, 0.0))))
    col_ref[...] = col
    row_ref[...] = col.T[0:8, :]
    lane8 = lane[0:8]
    seg = jnp.where(lane8 < N_EXPERTS, start8,
                    jnp.where(lane8 < 2 * N_EXPERTS, pltpu.roll(nblk8, N_EXPERTS, 1), 0.0))
    seg_ref[...] = seg[0:1].astype(jnp.int32)


def _route(h, gain, router, *, tm, n_valid):
    m, dm = h.shape
    nt = m // tm
    sub = next(s for s in (1024, 640, 512, 256, LANES) if tm % s == 0)
    tri = jnp.tril(jnp.ones((sub, sub), F32), -1).astype(BF16)
    upper = jnp.triu(jnp.ones((LANES, LANES), F32), 1).astype(BF16)
    return pl.pallas_call(
        functools.partial(_router_kernel, n_valid=n_valid),
        grid=(nt,),
        in_specs=[pl.BlockSpec((tm, dm), lambda i: (i, 0)),
                  pl.BlockSpec((1, dm), lambda i: (0, 0)),
                  pl.BlockSpec((dm, LANES), lambda i: (0, 0)),
                  pl.BlockSpec((sub, sub), lambda i: (0, 0)),
                  pl.BlockSpec((LANES, LANES), lambda i: (0, 0))],
        out_specs=[pl.BlockSpec((None, dm, tm), lambda i: (i, 0, 0)),
                   pl.BlockSpec((tm, LANES), lambda i: (i, 0)),
                   pl.BlockSpec((None, 8, tm), lambda i: (i, 0, 0)),
                   pl.BlockSpec((None, 1, LANES), lambda i: (i, 0, 0))],
        out_shape=[jax.ShapeDtypeStruct((nt, dm, tm), BF16),
                   jax.ShapeDtypeStruct((m, LANES), F32),
                   jax.ShapeDtypeStruct((nt, 8, tm), F32),
                   jax.ShapeDtypeStruct((nt, 1, LANES), jnp.int32)],
        compiler_params=_cparams("parallel"),
        name="moe_route",
    )(h, gain.reshape(1, dm), router, tri, upper)


def _one_hot_rows(row0, n, d1_row, d2_row):
    ridx = (lax.broadcasted_iota(jnp.int32, (n, d1_row.shape[1]), 0) + row0).astype(F32)
    return jnp.where((ridx == d1_row) | (ridx == d2_row), 1.0, 0.0).astype(BF16)


def _experts_kernel(seg_ref, ut_ref, col_ref, row_ref, wg_ref, wu_ref, wd_ref, o_ref,
                    xs_scr, acc_scr):
    i, e, f = pl.program_id(0), pl.program_id(1), pl.program_id(2)
    last_f = pl.num_programs(2) - 1
    t = ut_ref.shape[1]
    start = seg_ref[i * LANES + e]
    nblk = seg_ref[i * LANES + N_EXPERTS + e]

    @pl.when((e == 0) & (f == 0))
    def _():
        o_ref[...] = jnp.zeros_like(o_ref)

    def gather(lb, nb):
        w = nb * LANES
        col = col_ref[...]
        ridx = (lax.broadcasted_iota(jnp.int32, (t, w), 1) + (start + lb) * LANES).astype(F32)
        p_t = jnp.where((ridx == col[:, 0:1]) | (ridx == col[:, 1:2]), 1.0, 0.0).astype(BF16)
        x_t = jnp.dot(ut_ref[...], p_t, preferred_element_type=F32).astype(BF16)
        for k in range(nb):
            xs_scr[lb + k] = x_t[:, k * LANES:(k + 1) * LANES]
            acc_scr[lb + k] = jnp.zeros(acc_scr.shape[1:], F32)

    def swiglu(lb, nb):
        x_t = jnp.concatenate([xs_scr[lb + k] for k in range(nb)], axis=1)
        gate_t = jnp.dot(wg_ref[...], x_t, preferred_element_type=F32)
        up_t = jnp.dot(wu_ref[...], x_t, preferred_element_type=F32)
        act_t = (_silu(gate_t) * up_t).astype(BF16)
        down_t = jnp.dot(wd_ref[...], act_t, preferred_element_type=F32)
        for k in range(nb):
            acc_scr[lb + k] += down_t[:, k * LANES:(k + 1) * LANES]

    def combine(lb, nb):
        w = nb * LANES
        col = col_ref[...]
        row0 = (start + lb) * LANES
        ridx = (lax.broadcasted_iota(jnp.int32, (t, w), 1) + row0).astype(F32)
        gs = jnp.sum(jnp.where(ridx == col[:, 0:1], col[:, 2:3], 0.0) +
                     jnp.where(ridx == col[:, 1:2], col[:, 3:4], 0.0), 0, keepdims=True)
        acc_t = jnp.concatenate([acc_scr[lb + k] for k in range(nb)], axis=1)
        p = _one_hot_rows(row0, w, row_ref[0:1, :], row_ref[1:2, :])
        o_ref[...] += jnp.dot((acc_t * gs).astype(BF16), p, preferred_element_type=F32)

    def for_blocks(*stages):
        def run(lb, nb):
            for stage in stages:
                stage(lb, nb)

        n4 = nblk // 4

        def body(j, carry):
            run(4 * j, 4)
            return carry

        lax.fori_loop(0, n4, body, 0)
        rem = nblk - 4 * n4

        @pl.when((rem & 2) != 0)
        def _():
            run(4 * n4, 2)

        @pl.when((rem & 1) != 0)
        def _():
            run(4 * n4 + (rem & 2), 1)

    @pl.when(f == 0)
    def _():
        for_blocks(gather, swiglu)

    @pl.when((f > 0) & (f < last_f))
    def _():
        for_blocks(swiglu)

    @pl.when(f == last_f)
    def _():
        for_blocks(swiglu, combine)


def _experts(ut, col, row, seg, wgu_t, wd_t, *, tf):
    nt, dm, tm = ut.shape
    ne, _, ff = wd_t.shape
    nf = ff // tf
    assert nf >= 2, "first and last F block are distinct steps"
    grid_spec = pltpu.PrefetchScalarGridSpec(
        num_scalar_prefetch=1,
        grid=(nt, ne, nf),
        in_specs=[pl.BlockSpec((None, dm, tm), lambda i, e, f, s: (i, 0, 0),
                               pipeline_mode=pl.Buffered(1)),
                  pl.BlockSpec((tm, LANES), lambda i, e, f, s: (i, 0),
                               pipeline_mode=pl.Buffered(1)),
                  pl.BlockSpec((None, 8, tm), lambda i, e, f, s: (i, 0, 0)),
                  pl.BlockSpec((None, tf, dm), lambda i, e, f, s: (e, f, 0)),
                  pl.BlockSpec((None, tf, dm), lambda i, e, f, s: (e, nf + f, 0)),
                  pl.BlockSpec((None, dm, tf), lambda i, e, f, s: (e, 0, f))],
        out_specs=pl.BlockSpec((dm, tm), lambda i, e, f, s: (0, i)),
        scratch_shapes=[pltpu.VMEM((tm // LANES, dm, LANES), BF16),
                        pltpu.VMEM((tm // LANES, dm, LANES), F32)])
    return pl.pallas_call(
        _experts_kernel,
        grid_spec=grid_spec,
        out_shape=jax.ShapeDtypeStruct((dm, nt * tm), F32),
        compiler_params=_cparams("parallel", "arbitrary", "arbitrary"),
        name="moe_experts",
    )(seg.reshape(-1), ut, col, row, wgu_t, wgu_t, wd_t)


def _ple_kernel(h_ref, g_ref, gw_ref, p_ref, pw_ref, *rest):
    o_ref = rest[-1]
    h = h_ref[...]
    if len(rest) == 2:
        h = h + rest[0][...].T
    gate = _sigmoid(_wdot(_rms_rows(h, g_ref[...]), gw_ref[...]))
    o_ref[...] = h + _wdot(p_ref[...], pw_ref[...]) * gate


def _ple(h, gain, gate_w, p, ple_w, *, tm, y_t=None):
    m, dm = h.shape
    pd = p.shape[1]
    in_specs = [pl.BlockSpec((tm, dm), lambda i: (i, 0)),
                pl.BlockSpec((1, dm), lambda i: (0, 0)),
                pl.BlockSpec((dm, dm), lambda i: (0, 0)),
                pl.BlockSpec((tm, pd), lambda i: (i, 0)),
                pl.BlockSpec((pd, dm), lambda i: (0, 0))]
    args = [h, gain.reshape(1, dm), gate_w, p, ple_w]
    if y_t is not None:
        in_specs.append(pl.BlockSpec((dm, tm), lambda i: (0, i)))
        args.append(y_t)
    return pl.pallas_call(
        _ple_kernel,
        grid=(m // tm,),
        in_specs=in_specs,
        out_specs=pl.BlockSpec((tm, dm), lambda i: (i, 0)),
        out_shape=jax.ShapeDtypeStruct((m, dm), F32),
        compiler_params=_cparams("parallel"),
        name="ple",
    )(*args)


def _band_attn_kernel(q_ref, kp_ref, kc_ref, vp_ref, vc_ref, o_ref, l_ref, *, span, tq):
    j = pl.program_id(2)
    q = q_ref[...] * (HD_B ** -0.5)
    kk = jnp.concatenate([kp_ref[...], kc_ref[...]], axis=0)
    vv = jnp.concatenate([vp_ref[...], vc_ref[...]], axis=0)
    qi = lax.broadcasted_iota(jnp.int32, (span, 2 * span), 0)
    ki = lax.broadcasted_iota(jnp.int32, (span, 2 * span), 1)
    dist = qi + span - ki
    band = (dist >= 0) & (dist <= span)
    lane = lax.broadcasted_iota(jnp.int32, (span, LANES), 1)
    for sb in range(tq // span):
        r0 = sb * span
        mask = band & (ki >= jnp.where(j > 0, 0, span)) if sb == 0 else band
        heads = [slice(h * HD_B, (h + 1) * HD_B) for h in range(HG)]
        ss = [jnp.where(mask, _dot_nt(q[r0:r0 + span, hs], kk[r0:r0 + 2 * span, hs]), NEG)
              for hs in heads]
        ms = [jnp.max(s, -1, keepdims=True) for s in ss]
        es = [jnp.exp(s - m).astype(BF16) for s, m in zip(ss, ms)]
        ones = jnp.ones((2 * span, LANES), BF16)
        dens = [_dot(e, ones) for e in es]
        outs = [_dot(e, vv[r0:r0 + 2 * span, hs]) * (1.0 / den[:, :HD_B])
                for e, den, hs in zip(es, dens, heads)]
        lse_tile = jnp.zeros((span, LANES), F32)
        for h in range(HG):
            lse_tile = jnp.where(lane == h, ms[h] + jnp.log(dens[h]), lse_tile)
        o_ref[r0:r0 + span, :] = jnp.concatenate(outs, axis=1)
        l_ref[r0:r0 + span, :] = lse_tile


def _band_attn(q, k, v, gi, *, batch, seq):
    win, dil = GROUPS[gi]
    span = win // dil
    n = seq // dil
    tq = min(4 * span, n)
    nb = n // tq
    sub = tq // span
    cur = lambda b, r, j: (r, b * nb + j, 0)
    prev = lambda b, r, j: (r, b * nb * sub + jnp.maximum(j * sub - 1, 0), 0)
    return pl.pallas_call(
        functools.partial(_band_attn_kernel, span=span, tq=tq),
        grid=(batch, dil, nb),
        in_specs=[pl.BlockSpec((None, tq, GW), cur),
                  pl.BlockSpec((None, span, GW), prev),
                  pl.BlockSpec((None, tq, GW), cur),
                  pl.BlockSpec((None, span, GW), prev),
                  pl.BlockSpec((None, tq, GW), cur)],
        out_specs=[pl.BlockSpec((None, tq, GW), cur),
                   pl.BlockSpec((None, tq, LANES), cur)],
        out_shape=[jax.ShapeDtypeStruct((dil, batch * n, GW), F32),
                   jax.ShapeDtypeStruct((dil, batch * n, LANES), F32)],
        compiler_params=_cparams("parallel", "parallel", "arbitrary"),
        name=f"band_attn_g{gi}",
    )(q, k, k, v, v)


def _merge_out_kernel(o0_ref, o1_ref, o2_ref, l0_ref, l1_ref, l2_ref, w_ref, r_ref, o_ref,
                      o_scr, l_scr, *, tm):
    for gi, (o_ph, l_ph) in enumerate(((o1_ref, l1_ref), (o2_ref, l2_ref))):
        d = o_ph.shape[0]
        for r in range(d):
            rows = pl.ds(r, tm // d, stride=d)
            l_scr[gi, rows, :] = l_ph[r]
            for c in range(GW // LANES):
                o_scr[gi, c, rows, :] = o_ph[r, :, c * LANES:(c + 1) * LANES]
    ls = [l0_ref[0], l_scr[0], l_scr[1]]

    def o_cols(g, c):
        if g == 0:
            return o0_ref[0, :, c * LANES:(c + 1) * LANES]
        return o_scr[g - 1, c]

    m = jnp.maximum(jnp.maximum(ls[0], ls[1]), ls[2])
    es = [jnp.exp(l - m) for l in ls]
    inv = 1.0 / (es[0] + es[1] + es[2])
    t = ls[0].shape[0]
    lo = lax.broadcasted_iota(jnp.int32, (t, LANES), 1) < HD_B
    cols = []
    for c in range(GW // LANES):
        acc = None
        for g in range(N_GROUPS):
            wt = es[g] * inv
            wexp = jnp.where(lo, wt[:, 2 * c:2 * c + 1], wt[:, 2 * c + 1:2 * c + 2])
            term = wexp * o_cols(g, c)
            acc = term if acc is None else acc + term
        cols.append(acc)
    o = jnp.concatenate(cols, axis=1).astype(BF16)
    o_ref[...] = r_ref[...] + jnp.dot(o, w_ref[...], preferred_element_type=F32)


def _merge_out(outs, lses, w, res, *, tm):
    m, dm = res.shape
    ph_spec = lambda a: pl.BlockSpec((a.shape[0], tm // a.shape[0], a.shape[2]),
                                     lambda i: (0, i, 0))
    return pl.pallas_call(
        functools.partial(_merge_out_kernel, tm=tm),
        grid=(m // tm,),
        in_specs=[ph_spec(a) for a in outs] + [ph_spec(a) for a in lses] +
                 [pl.BlockSpec((GW, dm), lambda i: (0, 0)),
                  pl.BlockSpec((tm, dm), lambda i: (i, 0))],
        out_specs=pl.BlockSpec((tm, dm), lambda i: (i, 0)),
        out_shape=jax.ShapeDtypeStruct((m, dm), F32),
        scratch_shapes=[pltpu.VMEM((N_GROUPS - 1, GW // LANES, tm, LANES), F32),
                        pltpu.VMEM((N_GROUPS - 1, tm, LANES), F32)],
        compiler_params=_cparams("parallel"),
        name="merge_out",
    )(*outs, *lses, w, res)


def _gather_attn_kernel(q_ref, kvn_ref, c0_ref, c1_ref, c2_ref, o_ref):
    caches = [c0_ref, c1_ref, c2_ref]
    b = pl.program_id(0)

    def column(ref):
        lane = lax.broadcasted_iota(jnp.int32, ref.shape, 1)
        col = jnp.sum(jnp.where(lane == b, ref[...], 0.0), -1, keepdims=True)
        return col.reshape(ref.shape[0] // HD_B, HD_B, 1)

    q_all, kv_all = column(q_ref), column(kvn_ref)
    nh = N_GROUPS * HG
    outs, lses = [], []
    for g, (_, dil) in enumerate(GROUPS):
        q = q_all[g * HG:(g + 1) * HG] * (HD_B ** -0.5)
        kn = kv_all[g * HG:(g + 1) * HG]
        vn = kv_all[nh + g * HG:nh + (g + 1) * HG]
        kc, vc = caches[g][0], caches[g][1]
        rows = kc.shape[-1]
        s = jnp.sum(kc * q, 1, keepdims=True)
        row = lax.broadcasted_iota(jnp.int32, (1, 1, rows), 2)
        s = jnp.where((row & (dil - 1)) == 0, s, NEG)
        s_new = jnp.sum(kn * q, 1, keepdims=True)
        m = jnp.maximum(jnp.max(s, 2, keepdims=True), s_new)
        e = jnp.exp(s - m)
        e_new = jnp.exp(s_new - m)
        den = jnp.sum(e, 2, keepdims=True) + e_new
        outs.append((jnp.sum(e * vc, 2, keepdims=True) + e_new * vn) / den)
        lses.append(m + jnp.log(den))
    m = jnp.maximum(jnp.maximum(lses[0], lses[1]), lses[2])
    es = [jnp.exp(l - m) for l in lses]
    o_ref[...] = (es[0] * outs[0] + es[1] * outs[1] + es[2] * outs[2]) / (es[0] + es[1] + es[2])


def _gather_attn(q, kv_new, caches):
    nb = caches[0].shape[0]
    span = GROUPS[0][0] // GROUPS[0][1]
    views = []
    for (win, dil), c in zip(GROUPS, caches):
        lb = c.shape[1]
        assert lb == win and lb // dil == span, "window buffer must hold the full window"
        assert dil & (dil - 1) == 0, "dilations are powers of two"
        views.append(jnp.transpose(c, (0, 2, 3, 4, 1)))
    return pl.pallas_call(
        _gather_attn_kernel,
        grid=(nb,),
        in_specs=[pl.BlockSpec(q.shape, lambda b: (0, 0)),
                  pl.BlockSpec(kv_new.shape, lambda b: (0, 0))] +
                 [pl.BlockSpec((None, 2, HG, HD_B, v.shape[-1]), lambda b: (b, 0, 0, 0, 0))
                  for v in views],
        out_specs=pl.BlockSpec((None, HG, HD_B, 1), lambda b: (b, 0, 0, 0)),
        out_shape=jax.ShapeDtypeStruct((nb, HG, HD_B, 1), F32),
        compiler_params=_cparams("parallel"),
        name="gather_attn",
    )(q, kv_new, *views)


def _transpose_cast_kernel(x_ref, o_ref):
    o_ref[...] = x_ref[...].T.astype(BF16)


def _transpose_cast(w, *, tk, tn):
    ne, k, n = w.shape
    return pl.pallas_call(
        _transpose_cast_kernel,
        grid=(ne, k // tk, n // tn),
        in_specs=[pl.BlockSpec((None, tk, tn), lambda e, i, j: (e, i, j))],
        out_specs=pl.BlockSpec((None, tn, tk), lambda e, i, j: (e, j, i)),
        out_shape=jax.ShapeDtypeStruct((ne, n, k), BF16),
        compiler_params=_cparams("parallel", "parallel", "parallel"),
        name="transpose_cast",
    )(w)


def _prep_weights(a_w_in, a_A_log, a_dt_bias, a_w_out, w_kv, b_w_q, b_w_out, dense_w_gu,
                  dense_w_down, moe_router, moe_w_gu, moe_w_down, ple_w, ple_gate_w, k_norm,
                  b_q_norm):
    d_model, a_in = a_w_in.shape[1:]
    a_in_pad = -(-a_in // LANES) * LANES
    wf = {
        'a_w_in': jnp.pad(a_w_in[0], ((0, 0), (0, a_in_pad - a_in))),
        'a_w_out': a_w_out[0], 'w_kv': w_kv, 'b_w_q': b_w_q[0], 'b_w_out': b_w_out[0],
        'dense_w_gu': dense_w_gu[0], 'dense_w_down': dense_w_down[0],
        'router': jnp.pad(moe_router[0], ((0, 0), (0, LANES - moe_router.shape[2]))),
        'ple_w': ple_w, 'ple_gate_w': ple_gate_w,
    }
    shared = {}
    hp = jnp.stack([a_A_log[0], a_dt_bias[0]])
    shared['a_hp'] = jnp.pad(hp, ((0, 0), (H_A, LANES - 2 * H_A)))
    shared['moe_wgu_t'] = _transpose_cast(moe_w_gu[0], tk=d_model, tn=1024)
    shared['moe_wd_t'] = _transpose_cast(moe_w_down[0], tk=896, tn=d_model)
    shared['k_gain'] = jnp.tile(k_norm, HG).reshape(1, GW)
    shared['q_gain'] = jnp.tile(b_q_norm[0], HG).reshape(1, GW)
    w_prompt = dict(shared, **{k: v.astype(BF16) for k, v in wf.items()})
    w_sample = dict(shared, **wf)
    return w_prompt, w_sample


def _layer0(x, p0, w, P, *, tm, in_tn, mixer):
    proj = _norm_mm(x, P['a_norm'][0], w['a_w_in'], tm=min(tm, 512), tn=in_tn)
    og, s_new = mixer(proj)
    h = _mm_res(og, w['a_w_out'], x, tm=tm)
    h = _ffn(h, P['ffn_norm'][0], w['dense_w_gu'], w['dense_w_down'], tm=tm, tf=512)
    h = _ple(h, P['ple_norm'][0], w['ple_gate_w'][0], p0, w['ple_w'][0], tm=tm)
    return h, proj, s_new


def _layer1_tail(h, p1, w, P, *, tm):
    m = h.shape[0]
    tmr = next((t for t in (2048, 1024, 512, 256, LANES) if m % t == 0), LANES)
    hp = jnp.pad(h, ((0, -m % tmr), (0, 0)))
    ut, col, row, seg = _route(hp, P['ffn_norm'][1], w['router'], tm=tmr, n_valid=m)
    y_t = _experts(ut, col, row, seg, w['moe_wgu_t'], w['moe_wd_t'], tf=896)
    if tm % LANES:
        h, p1, tm = hp, jnp.pad(p1, ((0, -m % tmr), (0, 0))), tmr
    return _ple(h, P['ple_norm'][1], w['ple_gate_w'][1], p1, w['ple_w'][1], tm=tm, y_t=y_t)[:m]


def kernel(x_prompt, x_sample, p_prompt, p_sample, state_conv, state_delta, cache_kv_w128, cache_kv_w512, cache_kv_w2048, a_norm, a_w_in, a_conv_w, a_A_log, a_dt_bias, a_out_norm, a_w_out, kv_norm, w_kv, k_norm, b_norm, b_w_q, b_q_norm, b_w_out, ffn_norm, dense_w_gu, dense_w_down, moe_router, moe_w_gu, moe_w_down, ple_w, ple_norm, ple_gate_w):
    assert a_w_in.shape[0] == 1 and b_w_q.shape[0] == 1, "one mixer of each kind"
    bp, sp, dm = x_prompt.shape
    bs, ls, _ = x_sample.shape
    assert ls == 1, "sample group decodes one token per sequence"
    qkv_w = 3 * H_A * DK_A
    P = dict(a_norm=a_norm, ffn_norm=ffn_norm, ple_norm=ple_norm)
    w, ws = _prep_weights(a_w_in, a_A_log, a_dt_bias, a_w_out, w_kv, b_w_q, b_w_out, dense_w_gu,
                          dense_w_down, moe_router, moe_w_gu, moe_w_down, ple_w, ple_gate_w,
                          k_norm, b_q_norm)
    a_in_pad = w['a_w_in'].shape[1]
    in_tn = a_in_pad // 3 if a_in_pad % (3 * LANES) == 0 else LANES
    conv_w = a_conv_w[0]
    out_norm = a_out_norm[0].reshape(1, DK_A)

    mp = bp * sp
    tm = min(1024, sp)
    xp = x_prompt.reshape(mp, dm)
    conv0 = jnp.zeros((bp, CONV_W - 1, qkv_w), F32)
    s0 = jnp.zeros((bp, H_A, DK_A, DK_A), F32)
    h, proj, delta_p = _layer0(
        xp, p_prompt[0].reshape(mp, -1), w, P, tm=tm, in_tn=in_tn,
        mixer=lambda pr: _gdn_prompt(pr, conv_w, w['a_hp'], out_norm, conv0, s0, batch=bp, seq=sp))
    conv_p = proj.reshape(bp, sp, -1)[:, sp - (CONV_W - 1):, :qkv_w][None]

    cos, sin = _rope_tables(jnp.arange(sp, dtype=jnp.int32))
    dils = [d for _, d in GROUPS]
    tmp = min(512, sp)
    kv, *kv_ph = _proj_rope(h, kv_norm, w['w_kv'], w['k_gain'], cos, sin, tm=tmp, n_rope=N_GROUPS,
                            natural=True, dils=dils + dils)
    q_ph = _proj_rope(h, b_norm[0], w['b_w_q'], w['q_gain'], cos, sin, tm=tmp, n_rope=N_GROUPS,
                      natural=False, dils=dils)
    outs, lses = [], []
    for gi in range(N_GROUPS):
        o, lse = _band_attn(q_ph[gi], kv_ph[gi], kv_ph[N_GROUPS + gi], gi, batch=bp, seq=sp)
        outs.append(o)
        lses.append(lse)
    h = _merge_out(outs, lses, w['b_w_out'], h, tm=tm)
    y_prompt = _layer1_tail(h, p_prompt[1].reshape(mp, -1), w, P, tm=tm).reshape(bp, sp, dm)
    kv3 = kv.reshape(bp, sp, 2 * N_GROUPS * GW)
    kv_p = []
    for gi, (win, _) in enumerate(GROUPS):
        rows = kv3[:, sp - min(win, sp):]
        k_g = rows[:, :, gi * GW:(gi + 1) * GW]
        v_g = rows[:, :, (N_GROUPS + gi) * GW:(N_GROUPS + gi + 1) * GW]
        kv_p.append(jnp.stack([k_g, v_g], axis=2).reshape(bp, -1, 2, HG, HD_B))

    xs = x_sample.reshape(bs, dm)
    hs, proj_s, delta_s = _layer0(
        xs, p_sample[0].reshape(bs, -1), ws, P, tm=bs, in_tn=in_tn,
        mixer=lambda pr: _gdn_step(pr, state_conv[0], conv_w, ws['a_hp'], out_norm, state_delta[0]))
    conv_s = jnp.concatenate([state_conv[0][:, 1:], proj_s[:, None, :qkv_w]], axis=1)[None]
    bs_pad = -(-bs // LANES) * LANES
    hs_pad = jnp.pad(hs, ((0, bs_pad - bs), (0, 0)))
    cos_s, sin_s = _rope_tables(jnp.full((bs_pad,), PAST_LEN, jnp.int32))
    kv_s, kv_cols = _proj_rope(hs_pad, kv_norm, ws['w_kv'], ws['k_gain'], cos_s, sin_s, tm=bs_pad,
                               n_rope=N_GROUPS, natural=True, columns=True, dils=())
    q_cols, = _proj_rope(hs_pad, b_norm[0], ws['b_w_q'], ws['q_gain'], cos_s, sin_s, tm=bs_pad,
                         n_rope=N_GROUPS, natural=False, columns=True, dils=())
    o_s = _gather_attn(q_cols, kv_cols, (cache_kv_w128, cache_kv_w512, cache_kv_w2048))
    hs = _mm_res(o_s.reshape(bs, GW), ws['b_w_out'], hs, tm=bs)
    y_sample = _layer1_tail(hs, p_sample[1].reshape(bs, -1), ws, P, tm=bs).reshape(bs, 1, dm)
    kvs5 = kv_s[:bs].reshape(bs, 1, 2, N_GROUPS, HG, HD_B)
    kv_sn = [kvs5[:, :, :, gi] for gi in range(N_GROUPS)]

    return (y_prompt, y_sample, conv_p, conv_s, delta_p[None], delta_s[None],
            kv_p[0], kv_sn[0], kv_p[1], kv_sn[1], kv_p[2], kv_sn[2])
```

```python
import functools

import jax
import jax.numpy as jnp
from jax import lax
from jax.experimental import pallas as pl
from jax.experimental.pallas import tpu as pltpu

F32 = jnp.float32
BF16 = jnp.bfloat16

EPS = 1e-6
PAST_LEN = 16384
GROUPS = ((128, 1), (512, 4), (2048, 16))
N_GROUPS = len(GROUPS)
HG = 8
HD_B = 64
ROT_DIM = HD_B // 4
ROPE_THETA = 500000.0
GW = HG * HD_B
H_A = 8
DK_A = 128
CONV_W = 4
CHUNK = 64
LANES = 128
VMEM_LIMIT = 57 * 1024 * 1024
NEG = -1e30


def _cparams(*sem):
    return pltpu.CompilerParams(dimension_semantics=sem, vmem_limit_bytes=VMEM_LIMIT)


def _rms_rows(x, gain):
    return x * lax.rsqrt(jnp.mean(x * x, -1, keepdims=True) + EPS) * gain


def _silu(x):
    return x * (1.0 / (1.0 + jnp.exp(-x)))


def _sigmoid(x):
    return 1.0 / (1.0 + jnp.exp(-x))


def _dot(a, b):
    return jnp.dot(a.astype(BF16), b.astype(BF16), preferred_element_type=F32)


def _wdot(a, w):
    dot = functools.partial(jnp.dot, preferred_element_type=F32)
    if w.dtype == BF16:
        return dot(a.astype(BF16), w)
    a = a.astype(F32)
    a_hi = a.astype(BF16)
    a_lo = (a - a_hi.astype(F32)).astype(BF16)
    w_hi = w.astype(BF16)
    w_lo = (w - w_hi.astype(F32)).astype(BF16)
    return dot(a_hi, w_hi) + (dot(a_lo, w_hi) + dot(a_hi, w_lo))


def _act_dtype(w):
    return BF16 if w.dtype == BF16 else F32


def _dot_nt(a, b):
    return lax.dot_general(a.astype(BF16), b.astype(BF16), (((1,), (1,)), ((), ())),
                           preferred_element_type=F32)


def _dot_tn(a, b):
    return lax.dot_general(a.astype(BF16), b.astype(BF16), (((0,), (0,)), ((), ())),
                           preferred_element_type=F32)


def _head_norm_rope(x, hgain, cos, sin):
    t = x.shape[0]
    lane = lax.broadcasted_iota(jnp.int32, (t, LANES), 1)
    d = lane & (HD_B - 1)
    r = lax.broadcasted_iota(jnp.int32, (LANES, LANES), 0)
    c_ = lax.broadcasted_iota(jnp.int32, (LANES, LANES), 1)
    same_head = jnp.where((r < HD_B) == (c_ < HD_B), 1.0, 0.0).astype(BF16)
    dot = functools.partial(jnp.dot, preferred_element_type=F32)
    outs = []
    for c in range(GW // LANES):
        sl = slice(c * LANES, (c + 1) * LANES)
        xb = x[:, sl]
        sq = xb * xb
        sq_hi = sq.astype(BF16)
        sq_lo = (sq - sq_hi.astype(F32)).astype(BF16)
        ssq = dot(sq_hi, same_head) + dot(sq_lo, same_head)
        yb = xb * lax.rsqrt(ssq * (1.0 / HD_B) + EPS) * hgain[:, sl]
        half = ROT_DIM // 2
        rot = jnp.where(d < half, pltpu.roll(yb, LANES - half, 1), pltpu.roll(yb, half, 1))
        outs.append(yb * cos[:, sl] + rot * sin[:, sl])
    return jnp.concatenate(outs, axis=1)


def _norm_mm_kernel(x_ref, g_ref, w_ref, o_ref, *, tn):
    u = _rms_rows(x_ref[...], g_ref[...]).astype(_act_dtype(w_ref))
    for j in range(w_ref.shape[1] // tn):
        cols = slice(j * tn, (j + 1) * tn)
        o_ref[:, cols] = _wdot(u, w_ref[:, cols])


def _norm_mm(x, gain, w, *, tm, tn):
    m, k = x.shape
    n = w.shape[1]
    return pl.pallas_call(
        functools.partial(_norm_mm_kernel, tn=tn),
        grid=(m // tm,),
        in_specs=[pl.BlockSpec((tm, k), lambda i: (i, 0)),
                  pl.BlockSpec((1, k), lambda i: (0, 0)),
                  pl.BlockSpec((k, n), lambda i: (0, 0))],
        out_specs=pl.BlockSpec((tm, n), lambda i: (i, 0)),
        out_shape=jax.ShapeDtypeStruct((m, n), F32),
        compiler_params=_cparams("parallel"),
        name="norm_mm",
    )(x, gain.reshape(1, k), w)


def _proj_rope_kernel(x_ref, g_ref, w_ref, hg_ref, cos_ref, sin_ref, *rest,
                      n_rope, natural, columns, dils, tm):
    n_out = int(natural) + int(columns) + len(dils)
    outs, (slab_ref,) = rest[:n_out], rest[n_out:]
    nat_ref = outs[0] if natural else None
    col_ref = outs[int(natural)] if columns else None
    ph_refs = outs[int(natural) + int(columns):]
    u = _rms_rows(x_ref[...], g_ref[...]).astype(_act_dtype(w_ref))
    slab = 0
    for jj in range(w_ref.shape[1] // GW):
        cols = slice(jj * GW, (jj + 1) * GW)
        y = _wdot(u, w_ref[:, cols])
        if jj < n_rope:
            y = _head_norm_rope(y, hg_ref[...], cos_ref[...], sin_ref[...])
        if natural:
            nat_ref[:, cols] = y
        if columns:
            col_ref[cols, :] = y.T
        if jj >= len(dils):
            continue
        d = dils[jj]
        if d == 1:
            ph_refs[jj][0] = y.astype(BF16)
            continue
        for c in range(GW // LANES):
            slab_ref[slab, c] = y[:, c * LANES:(c + 1) * LANES]
        for r in range(d):
            for c in range(GW // LANES):
                ph_refs[jj][r, :, c * LANES:(c + 1) * LANES] = (
                    slab_ref[slab, c, pl.ds(r, tm // d, stride=d), :].astype(BF16))
        slab += 1


def _proj_rope(x, gain, w, hgain, cos, sin, *, tm, n_rope, natural, dils, columns=False):
    m, k = x.shape
    n = w.shape[1]
    pos_blocks = cos.shape[0] // tm
    out_specs, out_shape = [], []
    if natural:
        out_specs.append(pl.BlockSpec((tm, n), lambda i: (i, 0)))
        out_shape.append(jax.ShapeDtypeStruct((m, n), F32))
    if columns:
        out_specs.append(pl.BlockSpec((n, tm), lambda i: (0, i)))
        out_shape.append(jax.ShapeDtypeStruct((n, m), F32))
    for d in dils:
        out_specs.append(pl.BlockSpec((d, tm // d, GW), lambda i: (0, i, 0)))
        out_shape.append(jax.ShapeDtypeStruct((d, m // d, GW), BF16))
    n_slabs = max(1, sum(d > 1 for d in dils))
    return pl.pallas_call(
        functools.partial(_proj_rope_kernel, n_rope=n_rope, natural=natural, columns=columns,
                          dils=tuple(dils), tm=tm),
        grid=(m // tm,),
        in_specs=[pl.BlockSpec((tm, k), lambda i: (i, 0)),
                  pl.BlockSpec((1, k), lambda i: (0, 0)),
                  pl.BlockSpec((k, n), lambda i: (0, 0)),
                  pl.BlockSpec((1, GW), lambda i: (0, 0)),
                  pl.BlockSpec((tm, GW), lambda i: (i % pos_blocks, 0)),
                  pl.BlockSpec((tm, GW), lambda i: (i % pos_blocks, 0))],
        out_specs=out_specs,
        out_shape=out_shape,
        scratch_shapes=[pltpu.VMEM((n_slabs, GW // LANES, tm, LANES), F32)],
        compiler_params=_cparams("parallel"),
        name="proj_rope",
    )(x, gain.reshape(1, k), w, hgain, cos, sin)


def _rope_tables(pos):
    half = ROT_DIM // 2
    inv = ROPE_THETA ** (-jnp.arange(half, dtype=F32) * 2.0 / ROT_DIM)
    ang = pos.astype(F32)[:, None] * inv[None]
    c, s = jnp.cos(ang), jnp.sin(ang)
    n = pos.shape[0]
    cos_h = jnp.concatenate([c, c, jnp.ones((n, HD_B - ROT_DIM), F32)], 1)
    sin_h = jnp.concatenate([-s, s, jnp.zeros((n, HD_B - ROT_DIM), F32)], 1)
    return jnp.tile(cos_h, (1, HG)), jnp.tile(sin_h, (1, HG))


def _mm_res_kernel(x_ref, w_ref, r_ref, o_ref):
    o_ref[...] = r_ref[...] + _wdot(x_ref[...], w_ref[...])


def _mm_res(x, w, res, *, tm):
    m, k = x.shape
    n = w.shape[1]
    return pl.pallas_call(
        _mm_res_kernel,
        grid=(m // tm,),
        in_specs=[pl.BlockSpec((tm, k), lambda i: (i, 0)),
                  pl.BlockSpec((k, n), lambda i: (0, 0)),
                  pl.BlockSpec((tm, n), lambda i: (i, 0))],
        out_specs=pl.BlockSpec((tm, n), lambda i: (i, 0)),
        out_shape=jax.ShapeDtypeStruct((m, n), F32),
        compiler_params=_cparams("parallel"),
        name="mm_res",
    )(x, w, res)


def _unit_lower_inverses(mats):
    c = mats[0].shape[0]
    row = lax.broadcasted_iota(jnp.int32, (c, c), 0)
    col = lax.broadcasted_iota(jnp.int32, (c, c), 1)
    eye = jnp.where(row == col, 1.0, 0.0).astype(F32)
    ts = None
    b = 1
    while b < c:
        sel = ((row ^ col) < 2 * b) & ((row & b) != 0) & ((col & b) == 0)
        lows = [jnp.where(sel, a, 0.0) for a in mats]
        if ts is None:
            ts = [eye - low for low in lows]
        else:
            tl = [_dot(t, low) for t, low in zip(ts, lows)]
            ts = [t - _dot(x, t) for t, x in zip(ts, tl)]
        b *= 2
    return ts


def _gdn_head_params(ba, hp):
    beta = _sigmoid(ba)
    x = ba + hp[1:2, :]
    softplus = jnp.maximum(x, 0.0) + jnp.log(1.0 + jnp.exp(-jnp.abs(x)))
    g = -jnp.exp(hp[0:1, :]) * softplus
    return beta, g


def _gdn_kernel(qkv_ref, z_ref, ba_ref, cw_ref, hp_ref, on_ref, conv0_ref, s0_ref,
                og_ref, sout_ref, xbuf, s_scr, *, C, nch):
    n = pl.program_id(1)
    R = nch * C
    pad = 8

    @pl.when(n == 0)
    def _():
        xbuf[pad - (CONV_W - 1):pad, :] = conv0_ref[...]
        s_scr[...] = s0_ref[...]

    xbuf[pad:pad + R, :] = qkv_ref[...]

    def conv_cols(c0):
        acc = None
        for j in range(CONV_W):
            r0 = pad - (CONV_W - 1) + j
            term = xbuf[r0:r0 + R, c0:c0 + DK_A] * cw_ref[j:j + 1, c0:c0 + DK_A]
            acc = term if acc is None else acc + term
        return _silu(acc)

    beta, g = _gdn_head_params(ba_ref[...], hp_ref[...])
    rr = lax.broadcasted_iota(jnp.int32, (R, R), 0)
    rc = lax.broadcasted_iota(jnp.int32, (R, R), 1)
    blocktri = ((rr >= rc) & ((rr ^ rc) < C)).astype(F32)
    gcum = jnp.dot(blocktri, g, preferred_element_type=F32, precision=lax.Precision.HIGHEST)
    gcum_t = gcum.T
    row = lax.broadcasted_iota(jnp.int32, (C, C), 0)
    col = lax.broadcasted_iota(jnp.int32, (C, C), 1)
    incl = row >= col
    strict = row > col

    units = [(c, h) for c in range(nch) for h in range(H_A)]
    qs, ks, vs = {}, {}, {}
    for h in range(H_A):
        q = conv_cols(h * DK_A)
        k = conv_cols((H_A + h) * DK_A)
        v = conv_cols((2 * H_A + h) * DK_A)
        q = q * lax.rsqrt(jnp.sum(q * q, -1, keepdims=True) + EPS) * (DK_A ** -0.5)
        k = k * lax.rsqrt(jnp.sum(k * k, -1, keepdims=True) + EPS)
        for c in range(nch):
            rs = slice(c * C, (c + 1) * C)
            qs[c, h], ks[c, h], vs[c, h] = q[rs], k[rs], v[rs]

    bcs, gcs, decays, kbs = {}, {}, {}, {}
    for c, h in units:
        rs = slice(c * C, (c + 1) * C)
        bcs[c, h] = beta[rs, h:h + 1]
        gcs[c, h] = gcum[rs, H_A + h:H_A + h + 1]
        gr = gcum_t[H_A + h:H_A + h + 1, rs]
        decays[c, h] = jnp.exp(jnp.where(incl, gcs[c, h] - gr, NEG))
        kbs[c, h] = ks[c, h] * bcs[c, h]
    grams = {u: _dot_nt(jnp.concatenate([kbs[u], qs[u]], axis=0), ks[u]) for u in units}
    a_mats = [jnp.where(strict, grams[u][:C] * decays[u], 0.0) for u in units]
    aqks = {u: grams[u][C:] * decays[u] for u in units}
    t_mats = dict(zip(units, _unit_lower_inverses(a_mats)))
    egs = {u: jnp.exp(gcs[u]) for u in units}
    sols = {u: _dot(t_mats[u], jnp.concatenate([vs[u] * bcs[u], kbs[u] * egs[u]], axis=1))
            for u in units}

    states = [s_scr[h] for h in range(H_A)]
    for c in range(nch):
        rs = slice(c * C, (c + 1) * C)
        for h in range(H_A):
            u = (c, h)
            g_last = gcs[u][C - 1:C, :]
            ws = _dot(jnp.concatenate([sols[u][:, DK_A:], qs[u] * egs[u]], axis=0), states[h])
            v_new = sols[u][:, :DK_A] - ws[:C]
            o = ws[C:] + _dot(aqks[u], v_new)
            kd = ks[u] * jnp.exp(g_last - gcs[u])
            states[h] = states[h] * jnp.exp(g_last) + _dot_tn(kd, v_new)
            o = _rms_rows(o, on_ref[...]) * _silu(z_ref[rs, h * DK_A:(h + 1) * DK_A])
            og_ref[rs, h * DK_A:(h + 1) * DK_A] = o
    for h in range(H_A):
        s_scr[h] = states[h]

    xbuf[pad - (CONV_W - 1):pad, :] = xbuf[pad + R - (CONV_W - 1):pad + R, :]

    @pl.when(n == pl.num_programs(1) - 1)
    def _():
        sout_ref[...] = s_scr[...]


def _gdn_prompt(proj, conv_w, hp, out_norm, conv0, s0, *, batch, seq):
    C = min(CHUNK, seq)
    nch = next(n for n in (4, 2, 1) if seq % (n * C) == 0)
    R = nch * C
    nc = seq // R
    qkv_w = 3 * H_A * DK_A
    z_w = H_A * DK_A
    return pl.pallas_call(
        functools.partial(_gdn_kernel, C=C, nch=nch),
        grid=(batch, nc),
        in_specs=[pl.BlockSpec((R, qkv_w), lambda b, n: (b * nc + n, 0)),
                  pl.BlockSpec((R, z_w), lambda b, n: (b * nc + n, qkv_w // z_w)),
                  pl.BlockSpec((R, LANES), lambda b, n: (b * nc + n, (qkv_w + z_w) // LANES)),
                  pl.BlockSpec((CONV_W, qkv_w), lambda b, n: (0, 0)),
                  pl.BlockSpec((2, LANES), lambda b, n: (0, 0)),
                  pl.BlockSpec((1, DK_A), lambda b, n: (0, 0)),
                  pl.BlockSpec((None, CONV_W - 1, qkv_w), lambda b, n: (b, 0, 0)),
                  pl.BlockSpec((None, H_A, DK_A, DK_A), lambda b, n: (b, 0, 0, 0))],
        out_specs=[pl.BlockSpec((R, z_w), lambda b, n: (b * nc + n, 0)),
                   pl.BlockSpec((None, H_A, DK_A, DK_A), lambda b, n: (b, 0, 0, 0))],
        out_shape=[jax.ShapeDtypeStruct((batch * seq, z_w), F32),
                   jax.ShapeDtypeStruct((batch, H_A, DK_A, DK_A), F32)],
        scratch_shapes=[pltpu.VMEM((R + 8, qkv_w), F32),
                        pltpu.VMEM((H_A, DK_A, DK_A), F32)],
        compiler_params=_cparams("parallel", "arbitrary"),
        name="gdn_chunked",
    )(proj, proj, proj, conv_w, hp, out_norm, conv0, s0)


def _gdn_step_kernel(proj_ref, conv_ref, cw_ref, hp_ref, on_ref, s0_ref, og_ref, sout_ref, qk_scr):
    qkv_w = 3 * H_A * DK_A
    z_w = H_A * DK_A

    def conv_cols(c0):
        sl = slice(c0, c0 + DK_A)
        acc = proj_ref[:, sl] * cw_ref[CONV_W - 1:CONV_W, sl]
        for j in range(CONV_W - 1):
            acc = acc + conv_ref[j:j + 1, sl] * cw_ref[j:j + 1, sl]
        return _silu(acc)

    beta, g = _gdn_head_params(proj_ref[:, qkv_w + z_w:qkv_w + z_w + LANES], hp_ref[...])
    qk_scr[...] = jnp.zeros_like(qk_scr)
    vs = []
    for h in range(H_A):
        q = conv_cols(h * DK_A)
        k = conv_cols((H_A + h) * DK_A)
        vs.append(conv_cols((2 * H_A + h) * DK_A))
        qk_scr[H_A + h:H_A + h + 1, :] = (
            q * lax.rsqrt(jnp.sum(q * q, -1, keepdims=True) + EPS) * (DK_A ** -0.5))
        qk_scr[h:h + 1, :] = k * lax.rsqrt(jnp.sum(k * k, -1, keepdims=True) + EPS)
    qk = qk_scr[...]
    qk_t = qk.T
    for h in range(H_A):
        k_row = qk[h:h + 1, :]
        q_row = qk[H_A + h:H_A + h + 1, :]
        k_col = qk_t[:, h:h + 1]
        q_col = qk_t[:, H_A + h:H_A + h + 1]
        bh = beta[:, h:h + 1]
        eg = jnp.exp(g[:, H_A + h:H_A + h + 1])
        s = s0_ref[h]
        k_s = jnp.sum(s * k_col, 0, keepdims=True)
        q_s = jnp.sum(s * q_col, 0, keepdims=True)
        v_new = bh * (vs[h] - eg * k_s)
        o = eg * q_s + jnp.sum(q_row * k_row, -1, keepdims=True) * v_new
        sout_ref[h] = s * eg + k_col * v_new
        o = _rms_rows(o, on_ref[...]) * _silu(proj_ref[:, qkv_w + h * DK_A:qkv_w + (h + 1) * DK_A])
        og_ref[:, h * DK_A:(h + 1) * DK_A] = o


def _gdn_step(proj, conv_state, conv_w, hp, out_norm, s0):
    nb, pw = proj.shape
    qkv_w = 3 * H_A * DK_A
    z_w = H_A * DK_A
    og, s_new = pl.pallas_call(
        _gdn_step_kernel,
        grid=(nb,),
        in_specs=[pl.BlockSpec((None, 1, pw), lambda b: (b, 0, 0)),
                  pl.BlockSpec((None, CONV_W - 1, qkv_w), lambda b: (b, 0, 0)),
                  pl.BlockSpec((CONV_W, qkv_w), lambda b: (0, 0)),
                  pl.BlockSpec((2, LANES), lambda b: (0, 0)),
                  pl.BlockSpec((1, DK_A), lambda b: (0, 0)),
                  pl.BlockSpec((None, H_A, DK_A, DK_A), lambda b: (b, 0, 0, 0))],
        out_specs=[pl.BlockSpec((None, 1, z_w), lambda b: (b, 0, 0)),
                   pl.BlockSpec((None, H_A, DK_A, DK_A), lambda b: (b, 0, 0, 0))],
        out_shape=[jax.ShapeDtypeStruct((nb, 1, z_w), F32),
                   jax.ShapeDtypeStruct((nb, H_A, DK_A, DK_A), F32)],
        scratch_shapes=[pltpu.VMEM((LANES, DK_A), F32)],
        compiler_params=_cparams("parallel"),
        name="gdn_step",
    )(proj.reshape(nb, 1, pw), conv_state, conv_w, hp, out_norm, s0)
    return og.reshape(nb, z_w), s_new


def _ffn_kernel(h_ref, g_ref, wg_ref, wu_ref, wd_ref, o_ref, u_ref, acc_ref):
    f = pl.program_id(1)

    @pl.when(f == 0)
    def _():
        u_ref[...] = _rms_rows(h_ref[...], g_ref[...]).astype(u_ref.dtype)
        acc_ref[...] = jnp.zeros_like(acc_ref)

    u = u_ref[...]
    gate = _wdot(u, wg_ref[...])
    up = _wdot(u, wu_ref[...])
    acc_ref[...] += _wdot(_silu(gate) * up, wd_ref[...])

    @pl.when(f == pl.num_programs(1) - 1)
    def _():
        o_ref[...] = h_ref[...] + acc_ref[...]


def _ffn(h, gain, w_gu, w_down, *, tm, tf):
    m, dm = h.shape
    ff = w_down.shape[0]
    nf = ff // tf
    return pl.pallas_call(
        _ffn_kernel,
        grid=(m // tm, nf),
        in_specs=[pl.BlockSpec((tm, dm), lambda i, f: (i, 0)),
                  pl.BlockSpec((1, dm), lambda i, f: (0, 0)),
                  pl.BlockSpec((dm, tf), lambda i, f: (0, f)),
                  pl.BlockSpec((dm, tf), lambda i, f: (0, nf + f)),
                  pl.BlockSpec((tf, dm), lambda i, f: (f, 0))],
        out_specs=pl.BlockSpec((tm, dm), lambda i, f: (i, 0)),
        out_shape=jax.ShapeDtypeStruct((m, dm), F32),
        scratch_shapes=[pltpu.VMEM((tm, dm), _act_dtype(w_gu)),
                        pltpu.VMEM((tm, dm), F32)],
        compiler_params=_cparams("parallel", "arbitrary"),
        name="ffn_dense",
    )(h, gain.reshape(1, dm), w_gu, w_gu, w_down)


N_EXPERTS = 8
SEG_ALIGN = LANES


def _top2(logits):
    t = logits.shape[0]
    lane = lax.broadcasted_iota(jnp.int32, (t, LANES), 1)
    valid = lane < N_EXPERTS
    lg = jnp.where(valid, logits, NEG)
    mx = jnp.max(lg, -1, keepdims=True)
    e = jnp.where(valid, jnp.exp(lg - mx), 0.0)
    probs = e / jnp.sum(e, -1, keepdims=True)
    p1 = jnp.max(probs, -1, keepdims=True)
    i1 = jnp.min(jnp.where((probs == p1) & valid, lane, LANES), -1, keepdims=True)
    rest = jnp.where((lane == i1) | ~valid, -1.0, probs)
    p2 = jnp.max(rest, -1, keepdims=True)
    i2 = jnp.min(jnp.where(rest == p2, lane, LANES), -1, keepdims=True)
    tot = p1 + p2
    return i1, i2, p1 / tot, p2 / tot


def _router_kernel(h_ref, g_ref, r_ref, tri_ref, upper_ref, ut_ref, col_ref, row_ref, seg_ref,
                   *, n_valid):
    t = h_ref.shape[0]
    sub = tri_ref.shape[0]
    u = _rms_rows(h_ref[...], g_ref[...])
    ut_ref[...] = u.T.astype(BF16)
    i1, i2, g1, g2 = _top2(_wdot(u, r_ref[...]))
    lane = lax.broadcasted_iota(jnp.int32, (t, LANES), 1)
    tok = pl.program_id(0) * t + lax.broadcasted_iota(jnp.int32, (t, 1), 0)
    valid = tok < n_valid
    sel = jnp.where(valid & ((lane == i1) | (lane == i2)), 1.0, 0.0)
    counts = jnp.zeros((1, LANES), F32)
    pos_parts = []
    for k in range(t // sub):
        sel_k = sel[k * sub:(k + 1) * sub]
        pos_parts.append(counts + jnp.dot(tri_ref[...], sel_k.astype(BF16),
                                          preferred_element_type=F32))
        counts = counts + jnp.sum(sel_k, 0, keepdims=True)
    pos = jnp.concatenate(pos_parts, axis=0)
    nblk = jnp.floor((counts + (SEG_ALIGN - 1)) * (1.0 / SEG_ALIGN))
    nblk8 = jnp.broadcast_to(nblk, (8, LANES))
    start8 = jnp.dot(nblk8.astype(BF16), upper_ref[...], preferred_element_type=F32)
    dest = start8[0:1] * SEG_ALIGN + pos
    d1 = jnp.where(valid, jnp.sum(jnp.where(lane == i1, dest, 0.0), -1, keepdims=True), -1.0)
    d2 = jnp.where(valid, jnp.sum(jnp.where(lane == i2, dest, 0.0), -1, keepdims=True), -1.0)
    col = jnp.where(lane == 0, d1, jnp.where(lane == 1, d2,
                    jnp.where(lane == 2, g1, jnp.where(lane == 3, g2, 0.0))))
    col_ref[...] = col
    row_ref[...] = col.T[0:8, :]
    lane8 = lane[0:8]
    seg = jnp.where(lane8 < N_EXPERTS, start8,
                    jnp.where(lane8 < 2 * N_EXPERTS, pltpu.roll(nblk8, N_EXPERTS, 1), 0.0))
    seg_ref[...] = seg[0:1].astype(jnp.int32)


def _route(h, gain, router, *, tm, n_valid):
    m, dm = h.shape
    nt = m // tm
    sub = next(s for s in (1024, 640, 512, 256, LANES) if tm % s == 0)
    tri = jnp.tril(jnp.ones((sub, sub), F32), -1).astype(BF16)
    upper = jnp.triu(jnp.ones((LANES, LANES), F32), 1).astype(BF16)
    return pl.pallas_call(
        functools.partial(_router_kernel, n_valid=n_valid),
        grid=(nt,),
        in_specs=[pl.BlockSpec((tm, dm), lambda i: (i, 0)),
                  pl.BlockSpec((1, dm), lambda i: (0, 0)),
                  pl.BlockSpec((dm, LANES), lambda i: (0, 0)),
                  pl.BlockSpec((sub, sub), lambda i: (0, 0)),
                  pl.BlockSpec((LANES, LANES), lambda i: (0, 0))],
        out_specs=[pl.BlockSpec((None, dm, tm), lambda i: (i, 0, 0)),
                   pl.BlockSpec((tm, LANES), lambda i: (i, 0)),
                   pl.BlockSpec((None, 8, tm), lambda i: (i, 0, 0)),
                   pl.BlockSpec((None, 1, LANES), lambda i: (i, 0, 0))],
        out_shape=[jax.ShapeDtypeStruct((nt, dm, tm), BF16),
                   jax.ShapeDtypeStruct((m, LANES), F32),
                   jax.ShapeDtypeStruct((nt, 8, tm), F32),
                   jax.ShapeDtypeStruct((nt, 1, LANES), jnp.int32)],
        compiler_params=_cparams("parallel"),
        name="moe_route",
    )(h, gain.reshape(1, dm), router, tri, upper)


def _one_hot_rows(row0, n, d1_row, d2_row):
    ridx = (lax.broadcasted_iota(jnp.int32, (n, d1_row.shape[1]), 0) + row0).astype(F32)
    return jnp.where((ridx == d1_row) | (ridx == d2_row), 1.0, 0.0).astype(BF16)


def _experts_kernel(seg_ref, ut_ref, col_ref, row_ref, wg_ref, wu_ref, wd_ref, o_ref,
                    xs_scr, acc_scr):
    i, e, f = pl.program_id(0), pl.program_id(1), pl.program_id(2)
    last_f = pl.num_programs(2) - 1
    t = ut_ref.shape[1]
    start = seg_ref[i * LANES + e]
    nblk = seg_ref[i * LANES + N_EXPERTS + e]

    @pl.when((e == 0) & (f == 0))
    def _():
        o_ref[...] = jnp.zeros_like(o_ref)

    def gather(lb, nb):
        w = nb * LANES
        col = col_ref[...]
        ridx = (lax.broadcasted_iota(jnp.int32, (t, w), 1) + (start + lb) * LANES).astype(F32)
        p_t = jnp.where((ridx == col[:, 0:1]) | (ridx == col[:, 1:2]), 1.0, 0.0).astype(BF16)
        x_t = jnp.dot(ut_ref[...], p_t, preferred_element_type=F32).astype(BF16)
        for k in range(nb):
            xs_scr[lb + k] = x_t[:, k * LANES:(k + 1) * LANES]
            acc_scr[lb + k] = jnp.zeros(acc_scr.shape[1:], F32)

    def swiglu(lb, nb):
        x_t = jnp.concatenate([xs_scr[lb + k] for k in range(nb)], axis=1)
        gate_t = jnp.dot(wg_ref[...], x_t, preferred_element_type=F32)
        up_t = jnp.dot(wu_ref[...], x_t, preferred_element_type=F32)
        act_t = (_silu(gate_t) * up_t).astype(BF16)
        down_t = jnp.dot(wd_ref[...], act_t, preferred_element_type=F32)
        for k in range(nb):
            acc_scr[lb + k] += down_t[:, k * LANES:(k + 1) * LANES]

    def combine(lb, nb):
        w = nb * LANES
        col = col_ref[...]
        row0 = (start + lb) * LANES
        ridx = (lax.broadcasted_iota(jnp.int32, (t, w), 1) + row0).astype(F32)
        gs = jnp.sum(jnp.where(ridx == col[:, 0:1], col[:, 2:3], 0.0) +
                     jnp.where(ridx == col[:, 1:2], col[:, 3:4], 0.0), 0, keepdims=True)
        acc_t = jnp.concatenate([acc_scr[lb + k] for k in range(nb)], axis=1)
        p = _one_hot_rows(row0, w, row_ref[0:1, :], row_ref[1:2, :])
        o_ref[...] += jnp.dot((acc_t * gs).astype(BF16), p, preferred_element_type=F32)

    def for_blocks(*stages):
        def run(lb, nb):
            for stage in stages:
                stage(lb, nb)

        n4 = nblk // 4

        def body(j, carry):
            run(4 * j, 4)
            return carry

        lax.fori_loop(0, n4, body, 0)
        rem = nblk - 4 * n4

        @pl.when((rem & 2) != 0)
        def _():
            run(4 * n4, 2)

        @pl.when((rem & 1) != 0)
        def _():
            run(4 * n4 + (rem & 2), 1)

    @pl.when(f == 0)
    def _():
        for_blocks(gather, swiglu)

    @pl.when((f > 0) & (f < last_f))
    def _():
        for_blocks(swiglu)

    @pl.when(f == last_f)
    def _():
        for_blocks(swiglu, combine)


def _experts(ut, col, row, seg, wgu_t, wd_t, *, tf):
    nt, dm, tm = ut.shape
    ne, _, ff = wd_t.shape
    nf = ff // tf
    assert nf >= 2, "first and last F block are distinct steps"
    grid_spec = pltpu.PrefetchScalarGridSpec(
        num_scalar_prefetch=1,
        grid=(nt, ne, nf),
        in_specs=[pl.BlockSpec((None, dm, tm), lambda i, e, f, s: (i, 0, 0),
                               pipeline_mode=pl.Buffered(1)),
                  pl.BlockSpec((tm, LANES), lambda i, e, f, s: (i, 0),
                               pipeline_mode=pl.Buffered(1)),
                  pl.BlockSpec((None, 8, tm), lambda i, e, f, s: (i, 0, 0)),
                  pl.BlockSpec((None, tf, dm), lambda i, e, f, s: (e, f, 0)),
                  pl.BlockSpec((None, tf, dm), lambda i, e, f, s: (e, nf + f, 0)),
                  pl.BlockSpec((None, dm, tf), lambda i, e, f, s: (e, 0, f))],
        out_specs=pl.BlockSpec((dm, tm), lambda i, e, f, s: (0, i)),
        scratch_shapes=[pltpu.VMEM((tm // LANES, dm, LANES), BF16),
                        pltpu.VMEM((tm // LANES, dm, LANES), F32)])
    return pl.pallas_call(
        _experts_kernel,
        grid_spec=grid_spec,
        out_shape=jax.ShapeDtypeStruct((dm, nt * tm), F32),
        compiler_params=_cparams("parallel", "arbitrary", "arbitrary"),
        name="moe_experts",
    )(seg.reshape(-1), ut, col, row, wgu_t, wgu_t, wd_t)


def _ple_kernel(h_ref, g_ref, gw_ref, p_ref, pw_ref, *rest):
    o_ref = rest[-1]
    h = h_ref[...]
    if len(rest) == 2:
        h = h + rest[0][...].T
    gate = _sigmoid(_wdot(_rms_rows(h, g_ref[...]), gw_ref[...]))
    o_ref[...] = h + _wdot(p_ref[...], pw_ref[...]) * gate


def _ple(h, gain, gate_w, p, ple_w, *, tm, y_t=None):
    m, dm = h.shape
    p, layer = p
    pd = p.shape[-1]
    in_specs = [pl.BlockSpec((tm, dm), lambda i: (i, 0)),
                pl.BlockSpec((1, dm), lambda i: (0, 0)),
                pl.BlockSpec((dm, dm), lambda i: (0, 0)),
                pl.BlockSpec((None, tm, pd), lambda i: (layer, i, 0)),
                pl.BlockSpec((pd, dm), lambda i: (0, 0))]
    args = [h, gain.reshape(1, dm), gate_w, p, ple_w]
    if y_t is not None:
        in_specs.append(pl.BlockSpec((dm, tm), lambda i: (0, i)))
        args.append(y_t)
    return pl.pallas_call(
        _ple_kernel,
        grid=(m // tm,),
        in_specs=in_specs,
        out_specs=pl.BlockSpec((tm, dm), lambda i: (i, 0)),
        out_shape=jax.ShapeDtypeStruct((m, dm), F32),
        compiler_params=_cparams("parallel"),
        name="ple",
    )(*args)


def _band_attn_kernel(q_ref, kp_ref, kc_ref, vp_ref, vc_ref, o_ref, l_ref, *, span, tq):
    j = pl.program_id(2)
    q = q_ref[...] * (HD_B ** -0.5)
    kk = jnp.concatenate([kp_ref[...], kc_ref[...]], axis=0)
    vv = jnp.concatenate([vp_ref[...], vc_ref[...]], axis=0)
    qi = lax.broadcasted_iota(jnp.int32, (span, 2 * span), 0)
    ki = lax.broadcasted_iota(jnp.int32, (span, 2 * span), 1)
    dist = qi + span - ki
    band = (dist >= 0) & (dist <= span)
    lane = lax.broadcasted_iota(jnp.int32, (span, LANES), 1)
    for sb in range(tq // span):
        r0 = sb * span
        mask = band & (ki >= jnp.where(j > 0, 0, span)) if sb == 0 else band
        heads = [slice(h * HD_B, (h + 1) * HD_B) for h in range(HG)]
        ss = [jnp.where(mask, _dot_nt(q[r0:r0 + span, hs], kk[r0:r0 + 2 * span, hs]), NEG)
              for hs in heads]
        ms = [jnp.max(s, -1, keepdims=True) for s in ss]
        es = [jnp.exp(s - m).astype(BF16) for s, m in zip(ss, ms)]
        ones = jnp.ones((2 * span, LANES), BF16)
        dens = [_dot(e, ones) for e in es]
        outs = [_dot(e, vv[r0:r0 + 2 * span, hs]) * (1.0 / den[:, :HD_B])
                for e, den, hs in zip(es, dens, heads)]
        lse_tile = jnp.zeros((span, LANES), F32)
        for h in range(HG):
            lse_tile = jnp.where(lane == h, ms[h] + jnp.log(dens[h]), lse_tile)
        o_ref[r0:r0 + span, :] = jnp.concatenate(outs, axis=1)
        l_ref[r0:r0 + span, :] = lse_tile


def _band_attn(q, k, v, gi, *, batch, seq):
    win, dil = GROUPS[gi]
    span = win // dil
    n = seq // dil
    tq = min(4 * span, n)
    nb = n // tq
    sub = tq // span
    cur = lambda b, r, j: (r, b * nb + j, 0)
    prev = lambda b, r, j: (r, b * nb * sub + jnp.maximum(j * sub - 1, 0), 0)
    return pl.pallas_call(
        functools.partial(_band_attn_kernel, span=span, tq=tq),
        grid=(batch, dil, nb),
        in_specs=[pl.BlockSpec((None, tq, GW), cur),
                  pl.BlockSpec((None, span, GW), prev),
                  pl.BlockSpec((None, tq, GW), cur),
                  pl.BlockSpec((None, span, GW), prev),
                  pl.BlockSpec((None, tq, GW), cur)],
        out_specs=[pl.BlockSpec((None, tq, GW), cur),
                   pl.BlockSpec((None, tq, LANES), cur)],
        out_shape=[jax.ShapeDtypeStruct((dil, batch * n, GW), F32),
                   jax.ShapeDtypeStruct((dil, batch * n, LANES), F32)],
        compiler_params=_cparams("parallel", "parallel", "arbitrary"),
        name=f"band_attn_g{gi}",
    )(q, k, k, v, v)


def _merge_out_kernel(o0_ref, o1_ref, o2_ref, l0_ref, l1_ref, l2_ref, w_ref, r_ref, o_ref,
                      o_scr, l_scr, *, tm):
    for gi, (o_ph, l_ph) in enumerate(((o1_ref, l1_ref), (o2_ref, l2_ref))):
        d = o_ph.shape[0]
        for r in range(d):
            rows = pl.ds(r, tm // d, stride=d)
            l_scr[gi, rows, :] = l_ph[r]
            for c in range(GW // LANES):
                o_scr[gi, c, rows, :] = o_ph[r, :, c * LANES:(c + 1) * LANES]
    ls = [l0_ref[0], l_scr[0], l_scr[1]]

    def o_cols(g, c):
        if g == 0:
            return o0_ref[0, :, c * LANES:(c + 1) * LANES]
        return o_scr[g - 1, c]

    m = jnp.maximum(jnp.maximum(ls[0], ls[1]), ls[2])
    es = [jnp.exp(l - m) for l in ls]
    inv = 1.0 / (es[0] + es[1] + es[2])
    t = ls[0].shape[0]
    lo = lax.broadcasted_iota(jnp.int32, (t, LANES), 1) < HD_B
    cols = []
    for c in range(GW // LANES):
        acc = None
        for g in range(N_GROUPS):
            wt = es[g] * inv
            wexp = jnp.where(lo, wt[:, 2 * c:2 * c + 1], wt[:, 2 * c + 1:2 * c + 2])
            term = wexp * o_cols(g, c)
            acc = term if acc is None else acc + term
        cols.append(acc)
    o = jnp.concatenate(cols, axis=1).astype(BF16)
    o_ref[...] = r_ref[...] + jnp.dot(o, w_ref[...], preferred_element_type=F32)


def _merge_out(outs, lses, w, res, *, tm):
    m, dm = res.shape
    ph_spec = lambda a: pl.BlockSpec((a.shape[0], tm // a.shape[0], a.shape[2]),
                                     lambda i: (0, i, 0))
    return pl.pallas_call(
        functools.partial(_merge_out_kernel, tm=tm),
        grid=(m // tm,),
        in_specs=[ph_spec(a) for a in outs] + [ph_spec(a) for a in lses] +
                 [pl.BlockSpec((GW, dm), lambda i: (0, 0)),
                  pl.BlockSpec((tm, dm), lambda i: (i, 0))],
        out_specs=pl.BlockSpec((tm, dm), lambda i: (i, 0)),
        out_shape=jax.ShapeDtypeStruct((m, dm), F32),
        scratch_shapes=[pltpu.VMEM((N_GROUPS - 1, GW // LANES, tm, LANES), F32),
                        pltpu.VMEM((N_GROUPS - 1, tm, LANES), F32)],
        compiler_params=_cparams("parallel"),
        name="merge_out",
    )(*outs, *lses, w, res)


def _gather_attn_kernel(q_ref, kvn_ref, c0_ref, c1_ref, c2_ref, o_ref):
    caches = [c0_ref, c1_ref, c2_ref]
    b = pl.program_id(0)

    def column(ref):
        lane = lax.broadcasted_iota(jnp.int32, ref.shape, 1)
        col = jnp.sum(jnp.where(lane == b, ref[...], 0.0), -1, keepdims=True)
        return col.reshape(ref.shape[0] // HD_B, HD_B, 1)

    q_all, kv_all = column(q_ref), column(kvn_ref)
    nh = N_GROUPS * HG
    outs, lses = [], []
    for g, (_, dil) in enumerate(GROUPS):
        q = q_all[g * HG:(g + 1) * HG] * (HD_B ** -0.5)
        kn = kv_all[g * HG:(g + 1) * HG]
        vn = kv_all[nh + g * HG:nh + (g + 1) * HG]
        kc, vc = caches[g][0], caches[g][1]
        rows = kc.shape[-1]
        s = jnp.sum(kc * q, 1, keepdims=True)
        row = lax.broadcasted_iota(jnp.int32, (1, 1, rows), 2)
        s = jnp.where((row & (dil - 1)) == 0, s, NEG)
        s_new = jnp.sum(kn * q, 1, keepdims=True)
        m = jnp.maximum(jnp.max(s, 2, keepdims=True), s_new)
        e = jnp.exp(s - m)
        e_new = jnp.exp(s_new - m)
        den = jnp.sum(e, 2, keepdims=True) + e_new
        outs.append((jnp.sum(e * vc, 2, keepdims=True) + e_new * vn) / den)
        lses.append(m + jnp.log(den))
    m = jnp.maximum(jnp.maximum(lses[0], lses[1]), lses[2])
    es = [jnp.exp(l - m) for l in lses]
    o_ref[...] = (es[0] * outs[0] + es[1] * outs[1] + es[2] * outs[2]) / (es[0] + es[1] + es[2])


def _gather_attn(q, kv_new, caches):
    nb = caches[0].shape[0]
    span = GROUPS[0][0] // GROUPS[0][1]
    views = []
    for (win, dil), c in zip(GROUPS, caches):
        lb = c.shape[1]
        assert lb == win and lb // dil == span, "window buffer must hold the full window"
        assert dil & (dil - 1) == 0, "dilations are powers of two"
        views.append(jnp.transpose(c, (0, 2, 3, 4, 1)))
    return pl.pallas_call(
        _gather_attn_kernel,
        grid=(nb,),
        in_specs=[pl.BlockSpec(q.shape, lambda b: (0, 0)),
                  pl.BlockSpec(kv_new.shape, lambda b: (0, 0))] +
                 [pl.BlockSpec((None, 2, HG, HD_B, v.shape[-1]), lambda b: (b, 0, 0, 0, 0))
                  for v in views],
        out_specs=pl.BlockSpec((None, HG, HD_B, 1), lambda b: (b, 0, 0, 0)),
        out_shape=jax.ShapeDtypeStruct((nb, HG, HD_B, 1), F32),
        compiler_params=_cparams("parallel"),
        name="gather_attn",
    )(q, kv_new, *views)


def _transpose_cast_kernel(x_ref, o_ref):
    o_ref[...] = x_ref[...].T.astype(BF16)


def _transpose_cast(w, *, tk, tn):
    ne, k, n = w.shape
    return pl.pallas_call(
        _transpose_cast_kernel,
        grid=(ne, k // tk, n // tn),
        in_specs=[pl.BlockSpec((None, tk, tn), lambda e, i, j: (e, i, j))],
        out_specs=pl.BlockSpec((None, tn, tk), lambda e, i, j: (e, j, i)),
        out_shape=jax.ShapeDtypeStruct((ne, n, k), BF16),
        compiler_params=_cparams("parallel", "parallel", "parallel"),
        name="transpose_cast",
    )(w)


def _prep_weights(a_w_in, a_A_log, a_dt_bias, a_w_out, w_kv, b_w_q, b_w_out, dense_w_gu,
                  dense_w_down, moe_router, moe_w_gu, moe_w_down, ple_w, ple_gate_w, k_norm,
                  b_q_norm):
    d_model, a_in = a_w_in.shape[1:]
    a_in_pad = -(-a_in // LANES) * LANES
    wf = {
        'a_w_in': jnp.pad(a_w_in[0], ((0, 0), (0, a_in_pad - a_in))),
        'a_w_out': a_w_out[0], 'w_kv': w_kv, 'b_w_q': b_w_q[0], 'b_w_out': b_w_out[0],
        'dense_w_gu': dense_w_gu[0], 'dense_w_down': dense_w_down[0],
        'router': jnp.pad(moe_router[0], ((0, 0), (0, LANES - moe_router.shape[2]))),
        'ple_w': ple_w, 'ple_gate_w': ple_gate_w,
    }
    shared = {}
    hp = jnp.stack([a_A_log[0], a_dt_bias[0]])
    shared['a_hp'] = jnp.pad(hp, ((0, 0), (H_A, LANES - 2 * H_A)))
    shared['moe_wgu_t'] = _transpose_cast(moe_w_gu[0], tk=d_model, tn=1024)
    shared['moe_wd_t'] = _transpose_cast(moe_w_down[0], tk=896, tn=d_model)
    shared['k_gain'] = jnp.tile(k_norm, HG).reshape(1, GW)
    shared['q_gain'] = jnp.tile(b_q_norm[0], HG).reshape(1, GW)
    w_prompt = dict(shared, **{k: v.astype(BF16) for k, v in wf.items()})
    w_sample = dict(shared, **wf)
    return w_prompt, w_sample


def _layer0(x, p0, w, P, *, tm, in_tn, mixer):
    proj = _norm_mm(x, P['a_norm'][0], w['a_w_in'], tm=min(tm, 512), tn=in_tn)
    og, s_new = mixer(proj)
    h = _mm_res(og, w['a_w_out'], x, tm=tm)
    h = _ffn(h, P['ffn_norm'][0], w['dense_w_gu'], w['dense_w_down'], tm=tm, tf=896)
    h = _ple(h, P['ple_norm'][0], w['ple_gate_w'][0], (p0, 0), w['ple_w'][0], tm=tm)
    return h, proj, s_new


def _layer1_tail(h, p1, w, P, *, tm):
    m = h.shape[0]
    tmr = next((t for t in (2048, 1024, 512, 256, LANES) if m % t == 0), LANES)
    hp = jnp.pad(h, ((0, -m % tmr), (0, 0)))
    ut, col, row, seg = _route(hp, P['ffn_norm'][1], w['router'], tm=tmr, n_valid=m)
    y_t = _experts(ut, col, row, seg, w['moe_wgu_t'], w['moe_wd_t'], tf=896)
    if tm % LANES:
        h, p1, tm = hp, jnp.pad(p1, ((0, 0), (0, -m % tmr), (0, 0))), tmr
    return _ple(h, P['ple_norm'][1], w['ple_gate_w'][1], (p1, 1), w['ple_w'][1], tm=tm,
                y_t=y_t)[:m]


def kernel(x_prompt, x_sample, p_prompt, p_sample, state_conv, state_delta, cache_kv_w128, cache_kv_w512, cache_kv_w2048, a_norm, a_w_in, a_conv_w, a_A_log, a_dt_bias, a_out_norm, a_w_out, kv_norm, w_kv, k_norm, b_norm, b_w_q, b_q_norm, b_w_out, ffn_norm, dense_w_gu, dense_w_down, moe_router, moe_w_gu, moe_w_down, ple_w, ple_norm, ple_gate_w):
    assert a_w_in.shape[0] == 1 and b_w_q.shape[0] == 1, "one mixer of each kind"
    bp, sp, dm = x_prompt.shape
    bs, ls, _ = x_sample.shape
    assert ls == 1, "sample group decodes one token per sequence"
    qkv_w = 3 * H_A * DK_A
    P = dict(a_norm=a_norm, ffn_norm=ffn_norm, ple_norm=ple_norm)
    w, ws = _prep_weights(a_w_in, a_A_log, a_dt_bias, a_w_out, w_kv, b_w_q, b_w_out, dense_w_gu,
                          dense_w_down, moe_router, moe_w_gu, moe_w_down, ple_w, ple_gate_w,
                          k_norm, b_q_norm)
    a_in_pad = w['a_w_in'].shape[1]
    in_tn = a_in_pad // 3 if a_in_pad % (3 * LANES) == 0 else LANES
    conv_w = a_conv_w[0]
    out_norm = a_out_norm[0].reshape(1, DK_A)

    mp = bp * sp
    tm = min(1024, sp)
    xp = x_prompt.reshape(mp, dm)
    conv0 = jnp.zeros((bp, CONV_W - 1, qkv_w), F32)
    s0 = jnp.zeros((bp, H_A, DK_A, DK_A), F32)
    h, proj, delta_p = _layer0(
        xp, p_prompt.reshape(p_prompt.shape[0], mp, -1), w, P, tm=tm, in_tn=in_tn,
        mixer=lambda pr: _gdn_prompt(pr, conv_w, w['a_hp'], out_norm, conv0, s0, batch=bp, seq=sp))
    conv_p = proj.reshape(bp, sp, -1)[:, sp - (CONV_W - 1):, :qkv_w][None]

    cos, sin = _rope_tables(jnp.arange(sp, dtype=jnp.int32))
    dils = [d for _, d in GROUPS]
    tmp = min(512, sp)
    kv, *kv_ph = _proj_rope(h, kv_norm, w['w_kv'], w['k_gain'], cos, sin, tm=tmp, n_rope=N_GROUPS,
                            natural=True, dils=dils + dils)
    q_ph = _proj_rope(h, b_norm[0], w['b_w_q'], w['q_gain'], cos, sin, tm=tmp, n_rope=N_GROUPS,
                      natural=False, dils=dils)
    outs, lses = [], []
    for gi in range(N_GROUPS):
        o, lse = _band_attn(q_ph[gi], kv_ph[gi], kv_ph[N_GROUPS + gi], gi, batch=bp, seq=sp)
        outs.append(o)
        lses.append(lse)
    h = _merge_out(outs, lses, w['b_w_out'], h, tm=tm)
    y_prompt = _layer1_tail(h, p_prompt.reshape(p_prompt.shape[0], mp, -1), w, P, tm=tm).reshape(bp, sp, dm)
    kv3 = kv.reshape(bp, sp, 2 * N_GROUPS * GW)
    kv_p = []
    for gi, (win, _) in enumerate(GROUPS):
        rows = kv3[:, sp - min(win, sp):]
        k_g = rows[:, :, gi * GW:(gi + 1) * GW]
        v_g = rows[:, :, (N_GROUPS + gi) * GW:(N_GROUPS + gi + 1) * GW]
        kv_p.append(jnp.stack([k_g, v_g], axis=2).reshape(bp, -1, 2, HG, HD_B))

    xs = x_sample.reshape(bs, dm)
    hs, proj_s, delta_s = _layer0(
        xs, p_sample.reshape(p_sample.shape[0], bs, -1), ws, P, tm=bs, in_tn=in_tn,
        mixer=lambda pr: _gdn_step(pr, state_conv[0], conv_w, ws['a_hp'], out_norm, state_delta[0]))
    conv_s = jnp.concatenate([state_conv[0][:, 1:], proj_s[:, None, :qkv_w]], axis=1)[None]
    bs_pad = -(-bs // LANES) * LANES
    hs_pad = jnp.pad(hs, ((0, bs_pad - bs), (0, 0)))
    cos_s, sin_s = _rope_tables(jnp.full((bs_pad,), PAST_LEN, jnp.int32))
    kv_s, kv_cols = _proj_rope(hs_pad, kv_norm, ws['w_kv'], ws['k_gain'], cos_s, sin_s, tm=bs_pad,
                               n_rope=N_GROUPS, natural=True, columns=True, dils=())
    q_cols, = _proj_rope(hs_pad, b_norm[0], ws['b_w_q'], ws['q_gain'], cos_s, sin_s, tm=bs_pad,
                         n_rope=N_GROUPS, natural=False, columns=True, dils=())
    o_s = _gather_attn(q_cols, kv_cols, (cache_kv_w128, cache_kv_w512, cache_kv_w2048))
    hs = _mm_res(o_s.reshape(bs, GW), ws['b_w_out'], hs, tm=bs)
    y_sample = _layer1_tail(hs, p_sample.reshape(p_sample.shape[0], bs, -1), ws, P, tm=bs).reshape(bs, 1, dm)
    kvs5 = kv_s[:bs].reshape(bs, 1, 2, N_GROUPS, HG, HD_B)
    kv_sn = [kvs5[:, :, :, gi] for gi in range(N_GROUPS)]

    return (y_prompt, y_sample, conv_p, conv_s, delta_p[None], delta_s[None],
            kv_p[0], kv_sn[0], kv_p[1], kv_sn[1], kv_p[2], kv_sn[2])
```

```python
import functools

import jax
import jax.numpy as jnp
from jax import lax
from jax.experimental import pallas as pl
from jax.experimental.pallas import tpu as pltpu

F32 = jnp.float32
BF16 = jnp.bfloat16

EPS = 1e-6
PAST_LEN = 16384
GROUPS = ((128, 1), (512, 4), (2048, 16))
N_GROUPS = len(GROUPS)
HG = 8
HD_B = 64
ROT_DIM = HD_B // 4
ROPE_THETA = 500000.0
GW = HG * HD_B
H_A = 8
DK_A = 128
CONV_W = 4
CHUNK = 64
LANES = 128
VMEM_LIMIT = 57 * 1024 * 1024
NEG = -1e30


def _cparams(*sem):
    return pltpu.CompilerParams(dimension_semantics=sem, vmem_limit_bytes=VMEM_LIMIT)


def _rms_rows(x, gain):
    return x * lax.rsqrt(jnp.mean(x * x, -1, keepdims=True) + EPS) * gain


def _silu(x):
    return x * (1.0 / (1.0 + jnp.exp(-x)))


def _sigmoid(x):
    return 1.0 / (1.0 + jnp.exp(-x))


def _dot(a, b):
    return jnp.dot(a.astype(BF16), b.astype(BF16), preferred_element_type=F32)


def _wdot(a, w):
    dot = functools.partial(jnp.dot, preferred_element_type=F32)
    if w.dtype == BF16:
        return dot(a.astype(BF16), w)
    a = a.astype(F32)
    a_hi = a.astype(BF16)
    a_lo = (a - a_hi.astype(F32)).astype(BF16)
    w_hi = w.astype(BF16)
    w_lo = (w - w_hi.astype(F32)).astype(BF16)
    return dot(a_hi, w_hi) + (dot(a_lo, w_hi) + dot(a_hi, w_lo))


def _act_dtype(w):
    return BF16 if w.dtype == BF16 else F32


def _dot_nt(a, b):
    return lax.dot_general(a.astype(BF16), b.astype(BF16), (((1,), (1,)), ((), ())),
                           preferred_element_type=F32)


def _dot_tn(a, b):
    return lax.dot_general(a.astype(BF16), b.astype(BF16), (((0,), (0,)), ((), ())),
                           preferred_element_type=F32)


def _head_norm_rope(x, hgain, cos, sin):
    t = x.shape[0]
    lane = lax.broadcasted_iota(jnp.int32, (t, LANES), 1)
    d = lane & (HD_B - 1)
    r = lax.broadcasted_iota(jnp.int32, (LANES, LANES), 0)
    c_ = lax.broadcasted_iota(jnp.int32, (LANES, LANES), 1)
    same_head = jnp.where((r < HD_B) == (c_ < HD_B), 1.0, 0.0).astype(BF16)
    dot = functools.partial(jnp.dot, preferred_element_type=F32)
    outs = []
    for c in range(GW // LANES):
        sl = slice(c * LANES, (c + 1) * LANES)
        xb = x[:, sl]
        sq = xb * xb
        sq_hi = sq.astype(BF16)
        sq_lo = (sq - sq_hi.astype(F32)).astype(BF16)
        ssq = dot(sq_hi, same_head) + dot(sq_lo, same_head)
        yb = xb * lax.rsqrt(ssq * (1.0 / HD_B) + EPS) * hgain[:, sl]
        half = ROT_DIM // 2
        rot = jnp.where(d < half, pltpu.roll(yb, LANES - half, 1), pltpu.roll(yb, half, 1))
        outs.append(yb * cos[:, sl] + rot * sin[:, sl])
    return jnp.concatenate(outs, axis=1)


def _norm_mm_kernel(x_ref, g_ref, w_ref, o_ref, *, tn):
    u = _rms_rows(x_ref[...], g_ref[...]).astype(_act_dtype(w_ref))
    for j in range(w_ref.shape[1] // tn):
        cols = slice(j * tn, (j + 1) * tn)
        o_ref[:, cols] = _wdot(u, w_ref[:, cols])


def _norm_mm(x, gain, w, *, tm, tn):
    m, k = x.shape
    n = w.shape[1]
    return pl.pallas_call(
        functools.partial(_norm_mm_kernel, tn=tn),
        grid=(m // tm,),
        in_specs=[pl.BlockSpec((tm, k), lambda i: (i, 0)),
                  pl.BlockSpec((1, k), lambda i: (0, 0)),
                  pl.BlockSpec((k, n), lambda i: (0, 0))],
        out_specs=pl.BlockSpec((tm, n), lambda i: (i, 0)),
        out_shape=jax.ShapeDtypeStruct((m, n), F32),
        compiler_params=_cparams("parallel"),
        name="norm_mm",
    )(x, gain.reshape(1, k), w)


def _proj_rope_kernel(x_ref, g_ref, w_ref, hg_ref, cos_ref, sin_ref, *rest,
                      n_rope, natural, columns, dils, tm):
    n_out = int(natural) + int(columns) + len(dils)
    outs, (slab_ref,) = rest[:n_out], rest[n_out:]
    nat_ref = outs[0] if natural else None
    col_ref = outs[int(natural)] if columns else None
    ph_refs = outs[int(natural) + int(columns):]
    u = _rms_rows(x_ref[...], g_ref[...]).astype(_act_dtype(w_ref))
    slab = 0
    for jj in range(w_ref.shape[1] // GW):
        cols = slice(jj * GW, (jj + 1) * GW)
        y = _wdot(u, w_ref[:, cols])
        if jj < n_rope:
            y = _head_norm_rope(y, hg_ref[...], cos_ref[...], sin_ref[...])
        if natural:
            nat_ref[:, cols] = y
        if columns:
            col_ref[cols, :] = y.T
        if jj >= len(dils):
            continue
        d = dils[jj]
        if d == 1:
            ph_refs[jj][0] = y.astype(BF16)
            continue
        for c in range(GW // LANES):
            slab_ref[slab, c] = y[:, c * LANES:(c + 1) * LANES]
        for r in range(d):
            for c in range(GW // LANES):
                ph_refs[jj][r, :, c * LANES:(c + 1) * LANES] = (
                    slab_ref[slab, c, pl.ds(r, tm // d, stride=d), :].astype(BF16))
        slab += 1


def _proj_rope(x, gain, w, hgain, cos, sin, *, tm, n_rope, natural, dils, columns=False):
    m, k = x.shape
    n = w.shape[1]
    pos_blocks = cos.shape[0] // tm
    out_specs, out_shape = [], []
    if natural:
        out_specs.append(pl.BlockSpec((tm, n), lambda i: (i, 0)))
        out_shape.append(jax.ShapeDtypeStruct((m, n), F32))
    if columns:
        out_specs.append(pl.BlockSpec((n, tm), lambda i: (0, i)))
        out_shape.append(jax.ShapeDtypeStruct((n, m), F32))
    for d in dils:
        out_specs.append(pl.BlockSpec((d, tm // d, GW), lambda i: (0, i, 0)))
        out_shape.append(jax.ShapeDtypeStruct((d, m // d, GW), BF16))
    n_slabs = max(1, sum(d > 1 for d in dils))
    return pl.pallas_call(
        functools.partial(_proj_rope_kernel, n_rope=n_rope, natural=natural, columns=columns,
                          dils=tuple(dils), tm=tm),
        grid=(m // tm,),
        in_specs=[pl.BlockSpec((tm, k), lambda i: (i, 0)),
                  pl.BlockSpec((1, k), lambda i: (0, 0)),
                  pl.BlockSpec((k, n), lambda i: (0, 0)),
                  pl.BlockSpec((1, GW), lambda i: (0, 0)),
                  pl.BlockSpec((tm, GW), lambda i: (i % pos_blocks, 0)),
                  pl.BlockSpec((tm, GW), lambda i: (i % pos_blocks, 0))],
        out_specs=out_specs,
        out_shape=out_shape,
        scratch_shapes=[pltpu.VMEM((n_slabs, GW // LANES, tm, LANES), F32)],
        compiler_params=_cparams("parallel"),
        name="proj_rope",
    )(x, gain.reshape(1, k), w, hgain, cos, sin)


def _rope_tables(pos):
    half = ROT_DIM // 2
    inv = ROPE_THETA ** (-jnp.arange(half, dtype=F32) * 2.0 / ROT_DIM)
    ang = pos.astype(F32)[:, None] * inv[None]
    c, s = jnp.cos(ang), jnp.sin(ang)
    n = pos.shape[0]
    cos_h = jnp.concatenate([c, c, jnp.ones((n, HD_B - ROT_DIM), F32)], 1)
    sin_h = jnp.concatenate([-s, s, jnp.zeros((n, HD_B - ROT_DIM), F32)], 1)
    return jnp.tile(cos_h, (1, HG)), jnp.tile(sin_h, (1, HG))


def _mm_res_kernel(x_ref, w_ref, r_ref, o_ref):
    o_ref[...] = r_ref[...] + _wdot(x_ref[...], w_ref[...])


def _mm_res(x, w, res, *, tm):
    m, k = x.shape
    n = w.shape[1]
    return pl.pallas_call(
        _mm_res_kernel,
        grid=(m // tm,),
        in_specs=[pl.BlockSpec((tm, k), lambda i: (i, 0)),
                  pl.BlockSpec((k, n), lambda i: (0, 0)),
                  pl.BlockSpec((tm, n), lambda i: (i, 0))],
        out_specs=pl.BlockSpec((tm, n), lambda i: (i, 0)),
        out_shape=jax.ShapeDtypeStruct((m, n), F32),
        compiler_params=_cparams("parallel"),
        name="mm_res",
    )(x, w, res)


def _unit_lower_inverses(mats):
    c = mats[0].shape[0]
    row = lax.broadcasted_iota(jnp.int32, (c, c), 0)
    col = lax.broadcasted_iota(jnp.int32, (c, c), 1)
    eye = jnp.where(row == col, 1.0, 0.0).astype(F32)
    ts = None
    b = 1
    while b < c:
        sel = ((row ^ col) < 2 * b) & ((row & b) != 0) & ((col & b) == 0)
        lows = [jnp.where(sel, a, 0.0) for a in mats]
        if ts is None:
            ts = [eye - low for low in lows]
        else:
            tl = [_dot(t, low) for t, low in zip(ts, lows)]
            ts = [t - _dot(x, t) for t, x in zip(ts, tl)]
        b *= 2
    return ts


def _gdn_head_params(ba, hp):
    beta = _sigmoid(ba)
    x = ba + hp[1:2, :]
    softplus = jnp.maximum(x, 0.0) + jnp.log(1.0 + jnp.exp(-jnp.abs(x)))
    g = -jnp.exp(hp[0:1, :]) * softplus
    return beta, g


def _gdn_kernel(qkv_ref, z_ref, ba_ref, cw_ref, hp_ref, on_ref, conv0_ref, s0_ref,
                og_ref, sout_ref, xbuf, s_scr, *, C, nch):
    n = pl.program_id(1)
    R = nch * C
    pad = 8

    @pl.when(n == 0)
    def _():
        xbuf[pad - (CONV_W - 1):pad, :] = conv0_ref[...]
        s_scr[...] = s0_ref[...]

    xbuf[pad:pad + R, :] = qkv_ref[...]

    def conv_cols(c0):
        acc = None
        for j in range(CONV_W):
            r0 = pad - (CONV_W - 1) + j
            term = xbuf[r0:r0 + R, c0:c0 + DK_A] * cw_ref[j:j + 1, c0:c0 + DK_A]
            acc = term if acc is None else acc + term
        return _silu(acc)

    beta, g = _gdn_head_params(ba_ref[...], hp_ref[...])
    rr = lax.broadcasted_iota(jnp.int32, (R, R), 0)
    rc = lax.broadcasted_iota(jnp.int32, (R, R), 1)
    blocktri = ((rr >= rc) & ((rr ^ rc) < C)).astype(F32)
    gcum = jnp.dot(blocktri, g, preferred_element_type=F32, precision=lax.Precision.HIGHEST)
    gcum_t = gcum.T
    row = lax.broadcasted_iota(jnp.int32, (C, C), 0)
    col = lax.broadcasted_iota(jnp.int32, (C, C), 1)
    incl = row >= col
    strict = row > col

    units = [(c, h) for c in range(nch) for h in range(H_A)]
    qs, ks, vs = {}, {}, {}
    for h in range(H_A):
        q = conv_cols(h * DK_A)
        k = conv_cols((H_A + h) * DK_A)
        v = conv_cols((2 * H_A + h) * DK_A)
        q = q * lax.rsqrt(jnp.sum(q * q, -1, keepdims=True) + EPS) * (DK_A ** -0.5)
        k = k * lax.rsqrt(jnp.sum(k * k, -1, keepdims=True) + EPS)
        for c in range(nch):
            rs = slice(c * C, (c + 1) * C)
            qs[c, h], ks[c, h], vs[c, h] = q[rs], k[rs], v[rs]

    bcs, gcs, decays, kbs = {}, {}, {}, {}
    for c, h in units:
        rs = slice(c * C, (c + 1) * C)
        bcs[c, h] = beta[rs, h:h + 1]
        gcs[c, h] = gcum[rs, H_A + h:H_A + h + 1]
        gr = gcum_t[H_A + h:H_A + h + 1, rs]
        decays[c, h] = jnp.exp(jnp.where(incl, gcs[c, h] - gr, NEG))
        kbs[c, h] = ks[c, h] * bcs[c, h]
    grams = {u: _dot_nt(jnp.concatenate([kbs[u], qs[u]], axis=0), ks[u]) for u in units}
    a_mats = [jnp.where(strict, grams[u][:C] * decays[u], 0.0) for u in units]
    aqks = {u: grams[u][C:] * decays[u] for u in units}
    t_mats = dict(zip(units, _unit_lower_inverses(a_mats)))
    egs = {u: jnp.exp(gcs[u]) for u in units}
    sols = {u: _dot(t_mats[u], jnp.concatenate([vs[u] * bcs[u], kbs[u] * egs[u]], axis=1))
            for u in units}

    states = [s_scr[h] for h in range(H_A)]
    for c in range(nch):
        rs = slice(c * C, (c + 1) * C)
        for h in range(H_A):
            u = (c, h)
            g_last = gcs[u][C - 1:C, :]
            ws = _dot(jnp.concatenate([sols[u][:, DK_A:], qs[u] * egs[u]], axis=0), states[h])
            v_new = sols[u][:, :DK_A] - ws[:C]
            o = ws[C:] + _dot(aqks[u], v_new)
            kd = ks[u] * jnp.exp(g_last - gcs[u])
            states[h] = states[h] * jnp.exp(g_last) + _dot_tn(kd, v_new)
            o = _rms_rows(o, on_ref[...]) * _silu(z_ref[rs, h * DK_A:(h + 1) * DK_A])
            og_ref[rs, h * DK_A:(h + 1) * DK_A] = o
    for h in range(H_A):
        s_scr[h] = states[h]

    xbuf[pad - (CONV_W - 1):pad, :] = xbuf[pad + R - (CONV_W - 1):pad + R, :]

    @pl.when(n == pl.num_programs(1) - 1)
    def _():
        sout_ref[...] = s_scr[...]


def _gdn_prompt(proj, conv_w, hp, out_norm, conv0, s0, *, batch, seq):
    C = min(CHUNK, seq)
    nch = next(n for n in (4, 2, 1) if seq % (n * C) == 0)
    R = nch * C
    nc = seq // R
    qkv_w = 3 * H_A * DK_A
    z_w = H_A * DK_A
    return pl.pallas_call(
        functools.partial(_gdn_kernel, C=C, nch=nch),
        grid=(batch, nc),
        in_specs=[pl.BlockSpec((R, qkv_w), lambda b, n: (b * nc + n, 0)),
                  pl.BlockSpec((R, z_w), lambda b, n: (b * nc + n, qkv_w // z_w)),
                  pl.BlockSpec((R, LANES), lambda b, n: (b * nc + n, (qkv_w + z_w) // LANES)),
                  pl.BlockSpec((CONV_W, qkv_w), lambda b, n: (0, 0)),
                  pl.BlockSpec((2, LANES), lambda b, n: (0, 0)),
                  pl.BlockSpec((1, DK_A), lambda b, n: (0, 0)),
                  pl.BlockSpec((None, CONV_W - 1, qkv_w), lambda b, n: (b, 0, 0)),
                  pl.BlockSpec((None, H_A, DK_A, DK_A), lambda b, n: (b, 0, 0, 0))],
        out_specs=[pl.BlockSpec((R, z_w), lambda b, n: (b * nc + n, 0)),
                   pl.BlockSpec((None, H_A, DK_A, DK_A), lambda b, n: (b, 0, 0, 0))],
        out_shape=[jax.ShapeDtypeStruct((batch * seq, z_w), F32),
                   jax.ShapeDtypeStruct((batch, H_A, DK_A, DK_A), F32)],
        scratch_shapes=[pltpu.VMEM((R + 8, qkv_w), F32),
                        pltpu.VMEM((H_A, DK_A, DK_A), F32)],
        compiler_params=_cparams("parallel", "arbitrary"),
        name="gdn_chunked",
    )(proj, proj, proj, conv_w, hp, out_norm, conv0, s0)


def _gdn_step_kernel(proj_ref, conv_ref, cw_ref, hp_ref, on_ref, s0_ref, og_ref, sout_ref, qk_scr):
    qkv_w = 3 * H_A * DK_A
    z_w = H_A * DK_A

    def conv_cols(c0):
        sl = slice(c0, c0 + DK_A)
        acc = proj_ref[:, sl] * cw_ref[CONV_W - 1:CONV_W, sl]
        for j in range(CONV_W - 1):
            acc = acc + conv_ref[j:j + 1, sl] * cw_ref[j:j + 1, sl]
        return _silu(acc)

    beta, g = _gdn_head_params(proj_ref[:, qkv_w + z_w:qkv_w + z_w + LANES], hp_ref[...])
    qk_scr[...] = jnp.zeros_like(qk_scr)
    vs = []
    for h in range(H_A):
        q = conv_cols(h * DK_A)
        k = conv_cols((H_A + h) * DK_A)
        vs.append(conv_cols((2 * H_A + h) * DK_A))
        qk_scr[H_A + h:H_A + h + 1, :] = (
            q * lax.rsqrt(jnp.sum(q * q, -1, keepdims=True) + EPS) * (DK_A ** -0.5))
        qk_scr[h:h + 1, :] = k * lax.rsqrt(jnp.sum(k * k, -1, keepdims=True) + EPS)
    qk = qk_scr[...]
    qk_t = qk.T
    for h in range(H_A):
        k_row = qk[h:h + 1, :]
        q_row = qk[H_A + h:H_A + h + 1, :]
        k_col = qk_t[:, h:h + 1]
        q_col = qk_t[:, H_A + h:H_A + h + 1]
        bh = beta[:, h:h + 1]
        eg = jnp.exp(g[:, H_A + h:H_A + h + 1])
        s = s0_ref[h]
        k_s = jnp.sum(s * k_col, 0, keepdims=True)
        q_s = jnp.sum(s * q_col, 0, keepdims=True)
        v_new = bh * (vs[h] - eg * k_s)
        o = eg * q_s + jnp.sum(q_row * k_row, -1, keepdims=True) * v_new
        sout_ref[h] = s * eg + k_col * v_new
        o = _rms_rows(o, on_ref[...]) * _silu(proj_ref[:, qkv_w + h * DK_A:qkv_w + (h + 1) * DK_A])
        og_ref[:, h * DK_A:(h + 1) * DK_A] = o


def _gdn_step(proj, conv_state, conv_w, hp, out_norm, s0):
    nb, pw = proj.shape
    qkv_w = 3 * H_A * DK_A
    z_w = H_A * DK_A
    og, s_new = pl.pallas_call(
        _gdn_step_kernel,
        grid=(nb,),
        in_specs=[pl.BlockSpec((None, 1, pw), lambda b: (b, 0, 0)),
                  pl.BlockSpec((None, CONV_W - 1, qkv_w), lambda b: (b, 0, 0)),
                  pl.BlockSpec((CONV_W, qkv_w), lambda b: (0, 0)),
                  pl.BlockSpec((2, LANES), lambda b: (0, 0)),
                  pl.BlockSpec((1, DK_A), lambda b: (0, 0)),
                  pl.BlockSpec((None, H_A, DK_A, DK_A), lambda b: (b, 0, 0, 0))],
        out_specs=[pl.BlockSpec((None, 1, z_w), lambda b: (b, 0, 0)),
                   pl.BlockSpec((None, H_A, DK_A, DK_A), lambda b: (b, 0, 0, 0))],
        out_shape=[jax.ShapeDtypeStruct((nb, 1, z_w), F32),
                   jax.ShapeDtypeStruct((nb, H_A, DK_A, DK_A), F32)],
        scratch_shapes=[pltpu.VMEM((LANES, DK_A), F32)],
        compiler_params=_cparams("parallel"),
        name="gdn_step",
    )(proj.reshape(nb, 1, pw), conv_state, conv_w, hp, out_norm, s0)
    return og.reshape(nb, z_w), s_new


def _ffn_kernel(h_ref, g_ref, wg_ref, wu_ref, wd_ref, o_ref, u_ref, acc_ref):
    f = pl.program_id(1)

    @pl.when(f == 0)
    def _():
        u_ref[...] = _rms_rows(h_ref[...], g_ref[...]).astype(u_ref.dtype)
        acc_ref[...] = jnp.zeros_like(acc_ref)

    u = u_ref[...]
    gate = _wdot(u, wg_ref[...])
    up = _wdot(u, wu_ref[...])
    acc_ref[...] += _wdot(_silu(gate) * up, wd_ref[...])

    @pl.when(f == pl.num_programs(1) - 1)
    def _():
        o_ref[...] = h_ref[...] + acc_ref[...]


def _ffn(h, gain, w_gu, w_down, *, tm, tf):
    m, dm = h.shape
    ff = w_down.shape[0]
    nf = ff // tf
    return pl.pallas_call(
        _ffn_kernel,
        grid=(m // tm, nf),
        in_specs=[pl.BlockSpec((tm, dm), lambda i, f: (i, 0)),
                  pl.BlockSpec((1, dm), lambda i, f: (0, 0)),
                  pl.BlockSpec((dm, tf), lambda i, f: (0, f)),
                  pl.BlockSpec((dm, tf), lambda i, f: (0, nf + f)),
                  pl.BlockSpec((tf, dm), lambda i, f: (f, 0))],
        out_specs=pl.BlockSpec((tm, dm), lambda i, f: (i, 0)),
        out_shape=jax.ShapeDtypeStruct((m, dm), F32),
        scratch_shapes=[pltpu.VMEM((tm, dm), _act_dtype(w_gu)),
                        pltpu.VMEM((tm, dm), F32)],
        compiler_params=_cparams("parallel", "arbitrary"),
        name="ffn_dense",
    )(h, gain.reshape(1, dm), w_gu, w_gu, w_down)


N_EXPERTS = 8
SEG_ALIGN = LANES


def _top2(logits):
    t = logits.shape[0]
    lane = lax.broadcasted_iota(jnp.int32, (t, LANES), 1)
    valid = lane < N_EXPERTS
    lg = jnp.where(valid, logits, NEG)
    mx = jnp.max(lg, -1, keepdims=True)
    e = jnp.where(valid, jnp.exp(lg - mx), 0.0)
    probs = e / jnp.sum(e, -1, keepdims=True)
    p1 = jnp.max(probs, -1, keepdims=True)
    i1 = jnp.min(jnp.where((probs == p1) & valid, lane, LANES), -1, keepdims=True)
    rest = jnp.where((lane == i1) | ~valid, -1.0, probs)
    p2 = jnp.max(rest, -1, keepdims=True)
    i2 = jnp.min(jnp.where(rest == p2, lane, LANES), -1, keepdims=True)
    tot = p1 + p2
    return i1, i2, p1 / tot, p2 / tot


def _router_kernel(h_ref, g_ref, r_ref, tri_ref, upper_ref, ut_ref, col_ref, row_ref, seg_ref,
                   *, n_valid):
    t = h_ref.shape[0]
    sub = tri_ref.shape[0]
    u = _rms_rows(h_ref[...], g_ref[...])
    ut_ref[...] = u.T.astype(BF16)
    i1, i2, g1, g2 = _top2(_wdot(u, r_ref[...]))
    lane = lax.broadcasted_iota(jnp.int32, (t, LANES), 1)
    tok = pl.program_id(0) * t + lax.broadcasted_iota(jnp.int32, (t, 1), 0)
    valid = tok < n_valid
    sel = jnp.where(valid & ((lane == i1) | (lane == i2)), 1.0, 0.0)
    counts = jnp.zeros((1, LANES), F32)
    pos_parts = []
    for k in range(t // sub):
        sel_k = sel[k * sub:(k + 1) * sub]
        pos_parts.append(counts + jnp.dot(tri_ref[...], sel_k.astype(BF16),
                                          preferred_element_type=F32))
        counts = counts + jnp.sum(sel_k, 0, keepdims=True)
    pos = jnp.concatenate(pos_parts, axis=0)
    nblk = jnp.floor((counts + (SEG_ALIGN - 1)) * (1.0 / SEG_ALIGN))
    nblk8 = jnp.broadcast_to(nblk, (8, LANES))
    start8 = jnp.dot(nblk8.astype(BF16), upper_ref[...], preferred_element_type=F32)
    dest = start8[0:1] * SEG_ALIGN + pos
    d1 = jnp.where(valid, jnp.sum(jnp.where(lane == i1, dest, 0.0), -1, keepdims=True), -1.0)
    d2 = jnp.where(valid, jnp.sum(jnp.where(lane == i2, dest, 0.0), -1, keepdims=True), -1.0)
    col = jnp.where(lane == 0, d1, jnp.where(lane == 1, d2,
                    jnp.where(lane == 2, g1, jnp.where(lane == 3, g2, 0.0))))
    col_ref[...] = col
    row_ref[...] = col.T[0:8, :]
    lane8 = lane[0:8]
    seg = jnp.where(lane8 < N_EXPERTS, start8,
                    jnp.where(lane8 < 2 * N_EXPERTS, pltpu.roll(nblk8, N_EXPERTS, 1), 0.0))
    seg_ref[...] = seg[0:1].astype(jnp.int32)


def _route(h, gain, router, *, tm, n_valid):
    m, dm = h.shape
    nt = m // tm
    sub = next(s for s in (1024, 640, 512, 256, LANES) if tm % s == 0)
    tri = jnp.tril(jnp.ones((sub, sub), F32), -1).astype(BF16)
    upper = jnp.triu(jnp.ones((LANES, LANES), F32), 1).astype(BF16)
    return pl.pallas_call(
        functools.partial(_router_kernel, n_valid=n_valid),
        grid=(nt,),
        in_specs=[pl.BlockSpec((tm, dm), lambda i: (i, 0)),
                  pl.BlockSpec((1, dm), lambda i: (0, 0)),
                  pl.BlockSpec((dm, LANES), lambda i: (0, 0)),
                  pl.BlockSpec((sub, sub), lambda i: (0, 0)),
                  pl.BlockSpec((LANES, LANES), lambda i: (0, 0))],
        out_specs=[pl.BlockSpec((None, dm, tm), lambda i: (i, 0, 0)),
                   pl.BlockSpec((tm, LANES), lambda i: (i, 0)),
                   pl.BlockSpec((None, 8, tm), lambda i: (i, 0, 0)),
                   pl.BlockSpec((None, 1, LANES), lambda i: (i, 0, 0))],
        out_shape=[jax.ShapeDtypeStruct((nt, dm, tm), BF16),
                   jax.ShapeDtypeStruct((m, LANES), F32),
                   jax.ShapeDtypeStruct((nt, 8, tm), F32),
                   jax.ShapeDtypeStruct((nt, 1, LANES), jnp.int32)],
        compiler_params=_cparams("parallel"),
        name="moe_route",
    )(h, gain.reshape(1, dm), router, tri, upper)


def _one_hot_rows(row0, n, d1_row, d2_row):
    ridx = (lax.broadcasted_iota(jnp.int32, (n, d1_row.shape[1]), 0) + row0).astype(F32)
    return jnp.where((ridx == d1_row) | (ridx == d2_row), 1.0, 0.0).astype(BF16)


def _experts_kernel(seg_ref, ut_ref, col_ref, row_ref, wg_ref, wu_ref, wd_ref, o_ref,
                    xs_scr, acc_scr):
    i, e, f = pl.program_id(0), pl.program_id(1), pl.program_id(2)
    last_f = pl.num_programs(2) - 1
    t = ut_ref.shape[1]
    start = seg_ref[i * LANES + e]
    nblk = seg_ref[i * LANES + N_EXPERTS + e]

    @pl.when((e == 0) & (f == 0))
    def _():
        o_ref[...] = jnp.zeros_like(o_ref)

    def gather(lb, nb):
        w = nb * LANES
        col = col_ref[...]
        ridx = (lax.broadcasted_iota(jnp.int32, (t, w), 1) + (start + lb) * LANES).astype(F32)
        p_t = jnp.where((ridx == col[:, 0:1]) | (ridx == col[:, 1:2]), 1.0, 0.0).astype(BF16)
        x_t = jnp.dot(ut_ref[...], p_t, preferred_element_type=F32).astype(BF16)
        for k in range(nb):
            xs_scr[lb + k] = x_t[:, k * LANES:(k + 1) * LANES]
            acc_scr[lb + k] = jnp.zeros(acc_scr.shape[1:], F32)

    def swiglu(lb, nb):
        x_t = jnp.concatenate([xs_scr[lb + k] for k in range(nb)], axis=1)
        gate_t = jnp.dot(wg_ref[...], x_t, preferred_element_type=F32)
        up_t = jnp.dot(wu_ref[...], x_t, preferred_element_type=F32)
        act_t = (_silu(gate_t) * up_t).astype(BF16)
        down_t = jnp.dot(wd_ref[...], act_t, preferred_element_type=F32)
        for k in range(nb):
            acc_scr[lb + k] += down_t[:, k * LANES:(k + 1) * LANES]

    def combine(lb, nb):
        w = nb * LANES
        col = col_ref[...]
        row0 = (start + lb) * LANES
        ridx = (lax.broadcasted_iota(jnp.int32, (t, w), 1) + row0).astype(F32)
        gs = jnp.sum(jnp.where(ridx == col[:, 0:1], col[:, 2:3], 0.0) +
                     jnp.where(ridx == col[:, 1:2], col[:, 3:4], 0.0), 0, keepdims=True)
        acc_t = jnp.concatenate([acc_scr[lb + k] for k in range(nb)], axis=1)
        p = _one_hot_rows(row0, w, row_ref[0:1, :], row_ref[1:2, :])
        o_ref[...] += jnp.dot((acc_t * gs).astype(BF16), p, preferred_element_type=F32)

    def for_blocks(*stages):
        def run(lb, nb):
            for stage in stages:
                stage(lb, nb)

        n4 = nblk // 4

        def body(j, carry):
            run(4 * j, 4)
            return carry

        lax.fori_loop(0, n4, body, 0)
        rem = nblk - 4 * n4

        @pl.when((rem & 2) != 0)
        def _():
            run(4 * n4, 2)

        @pl.when((rem & 1) != 0)
        def _():
            run(4 * n4 + (rem & 2), 1)

    @pl.when(f == 0)
    def _():
        for_blocks(gather, swiglu)

    @pl.when((f > 0) & (f < last_f))
    def _():
        for_blocks(swiglu)

    @pl.when(f == last_f)
    def _():
        for_blocks(swiglu, combine)


def _experts(ut, col, row, seg, wgu_t, wd_t, *, tf):
    nt, dm, tm = ut.shape
    ne, _, ff = wd_t.shape
    nf = ff // tf
    assert nf >= 2, "first and last F block are distinct steps"
    grid_spec = pltpu.PrefetchScalarGridSpec(
        num_scalar_prefetch=1,
        grid=(nt, ne, nf),
        in_specs=[pl.BlockSpec((None, dm, tm), lambda i, e, f, s: (i, 0, 0),
                               pipeline_mode=pl.Buffered(1)),
                  pl.BlockSpec((tm, LANES), lambda i, e, f, s: (i, 0),
                               pipeline_mode=pl.Buffered(1)),
                  pl.BlockSpec((None, 8, tm), lambda i, e, f, s: (i, 0, 0)),
                  pl.BlockSpec((None, tf, dm), lambda i, e, f, s: (e, f, 0)),
                  pl.BlockSpec((None, tf, dm), lambda i, e, f, s: (e, nf + f, 0)),
                  pl.BlockSpec((None, dm, tf), lambda i, e, f, s: (e, 0, f))],
        out_specs=pl.BlockSpec((dm, tm), lambda i, e, f, s: (0, i)),
        scratch_shapes=[pltpu.VMEM((tm // LANES, dm, LANES), BF16),
                        pltpu.VMEM((tm // LANES, dm, LANES), F32)])
    return pl.pallas_call(
        _experts_kernel,
        grid_spec=grid_spec,
        out_shape=jax.ShapeDtypeStruct((dm, nt * tm), F32),
        compiler_params=_cparams("parallel", "arbitrary", "arbitrary"),
        name="moe_experts",
    )(seg.reshape(-1), ut, col, row, wgu_t, wgu_t, wd_t)


def _ple_kernel(h_ref, g_ref, gw_ref, p_ref, pw_ref, *rest):
    o_ref = rest[-1]
    h = h_ref[...]
    if len(rest) == 2:
        h = h + rest[0][...].T
    gate = _sigmoid(_wdot(_rms_rows(h, g_ref[...]), gw_ref[...]))
    o_ref[...] = h + _wdot(p_ref[...], pw_ref[...]) * gate


def _ple(h, gain, gate_w, p, ple_w, *, tm, y_t=None):
    m, dm = h.shape
    p, layer = p
    pd = p.shape[-1]
    in_specs = [pl.BlockSpec((tm, dm), lambda i: (i, 0)),
                pl.BlockSpec((1, dm), lambda i: (0, 0)),
                pl.BlockSpec((dm, dm), lambda i: (0, 0)),
                pl.BlockSpec((None, tm, pd), lambda i: (layer, i, 0)),
                pl.BlockSpec((pd, dm), lambda i: (0, 0))]
    args = [h, gain.reshape(1, dm), gate_w, p, ple_w]
    if y_t is not None:
        in_specs.append(pl.BlockSpec((dm, tm), lambda i: (0, i)))
        args.append(y_t)
    return pl.pallas_call(
        _ple_kernel,
        grid=(m // tm,),
        in_specs=in_specs,
        out_specs=pl.BlockSpec((tm, dm), lambda i: (i, 0)),
        out_shape=jax.ShapeDtypeStruct((m, dm), F32),
        compiler_params=_cparams("parallel"),
        name="ple",
    )(*args)


def _band_attn_kernel(q_ref, kp_ref, kc_ref, vp_ref, vc_ref, o_ref, l_ref, *, span, tq):
    j = pl.program_id(2)
    q = q_ref[...] * (HD_B ** -0.5)
    kk = jnp.concatenate([kp_ref[...], kc_ref[...]], axis=0)
    vv = jnp.concatenate([vp_ref[...], vc_ref[...]], axis=0)
    qi = lax.broadcasted_iota(jnp.int32, (span, 2 * span), 0)
    ki = lax.broadcasted_iota(jnp.int32, (span, 2 * span), 1)
    dist = qi + span - ki
    band = (dist >= 0) & (dist <= span)
    lane = lax.broadcasted_iota(jnp.int32, (span, LANES), 1)
    for sb in range(tq // span):
        r0 = sb * span
        mask = band & (ki >= jnp.where(j > 0, 0, span)) if sb == 0 else band
        heads = [slice(h * HD_B, (h + 1) * HD_B) for h in range(HG)]
        ss = [jnp.where(mask, _dot_nt(q[r0:r0 + span, hs], kk[r0:r0 + 2 * span, hs]), NEG)
              for hs in heads]
        ms = [jnp.max(s, -1, keepdims=True) for s in ss]
        es = [jnp.exp(s - m).astype(BF16) for s, m in zip(ss, ms)]
        ones = jnp.ones((2 * span, LANES), BF16)
        dens = [_dot(e, ones) for e in es]
        outs = [_dot(e, vv[r0:r0 + 2 * span, hs]) * (1.0 / den[:, :HD_B])
                for e, den, hs in zip(es, dens, heads)]
        lse_tile = jnp.zeros((span, LANES), F32)
        for h in range(HG):
            lse_tile = jnp.where(lane == h, ms[h] + jnp.log(dens[h]), lse_tile)
        o_ref[r0:r0 + span, :] = jnp.concatenate(outs, axis=1)
        l_ref[r0:r0 + span, :] = lse_tile


def _band_attn(q, k, v, gi, *, batch, seq):
    win, dil = GROUPS[gi]
    span = win // dil
    n = seq // dil
    tq = min(4 * span, n)
    nb = n // tq
    sub = tq // span
    cur = lambda b, r, j: (r, b * nb + j, 0)
    prev = lambda b, r, j: (r, b * nb * sub + jnp.maximum(j * sub - 1, 0), 0)
    return pl.pallas_call(
        functools.partial(_band_attn_kernel, span=span, tq=tq),
        grid=(batch, dil, nb),
        in_specs=[pl.BlockSpec((None, tq, GW), cur),
                  pl.BlockSpec((None, span, GW), prev),
                  pl.BlockSpec((None, tq, GW), cur),
                  pl.BlockSpec((None, span, GW), prev),
                  pl.BlockSpec((None, tq, GW), cur)],
        out_specs=[pl.BlockSpec((None, tq, GW), cur),
                   pl.BlockSpec((None, tq, LANES), cur)],
        out_shape=[jax.ShapeDtypeStruct((dil, batch * n, GW), F32),
                   jax.ShapeDtypeStruct((dil, batch * n, LANES), F32)],
        compiler_params=_cparams("parallel", "parallel", "arbitrary"),
        name=f"band_attn_g{gi}",
    )(q, k, k, v, v)


def _merge_out_kernel(o0_ref, o1_ref, o2_ref, l0_ref, l1_ref, l2_ref, w_ref, r_ref, o_ref,
                      o_scr, l_scr, *, tm):
    for gi, (o_ph, l_ph) in enumerate(((o1_ref, l1_ref), (o2_ref, l2_ref))):
        d = o_ph.shape[0]
        for r in range(d):
            rows = pl.ds(r, tm // d, stride=d)
            l_scr[gi, rows, :] = l_ph[r]
            for c in range(GW // LANES):
                o_scr[gi, c, rows, :] = o_ph[r, :, c * LANES:(c + 1) * LANES]
    ls = [l0_ref[0], l_scr[0], l_scr[1]]

    def o_cols(g, c):
        if g == 0:
            return o0_ref[0, :, c * LANES:(c + 1) * LANES]
        return o_scr[g - 1, c]

    m = jnp.maximum(jnp.maximum(ls[0], ls[1]), ls[2])
    es = [jnp.exp(l - m) for l in ls]
    inv = 1.0 / (es[0] + es[1] + es[2])
    t = ls[0].shape[0]
    lo = lax.broadcasted_iota(jnp.int32, (t, LANES), 1) < HD_B
    cols = []
    for c in range(GW // LANES):
        acc = None
        for g in range(N_GROUPS):
            wt = es[g] * inv
            wexp = jnp.where(lo, wt[:, 2 * c:2 * c + 1], wt[:, 2 * c + 1:2 * c + 2])
            term = wexp * o_cols(g, c)
            acc = term if acc is None else acc + term
        cols.append(acc)
    o = jnp.concatenate(cols, axis=1).astype(BF16)
    o_ref[...] = r_ref[...] + jnp.dot(o, w_ref[...], preferred_element_type=F32)


def _merge_out(outs, lses, w, res, *, tm):
    m, dm = res.shape
    ph_spec = lambda a: pl.BlockSpec((a.shape[0], tm // a.shape[0], a.shape[2]),
                                     lambda i: (0, i, 0))
    return pl.pallas_call(
        functools.partial(_merge_out_kernel, tm=tm),
        grid=(m // tm,),
        in_specs=[ph_spec(a) for a in outs] + [ph_spec(a) for a in lses] +
                 [pl.BlockSpec((GW, dm), lambda i: (0, 0)),
                  pl.BlockSpec((tm, dm), lambda i: (i, 0))],
        out_specs=pl.BlockSpec((tm, dm), lambda i: (i, 0)),
        out_shape=jax.ShapeDtypeStruct((m, dm), F32),
        scratch_shapes=[pltpu.VMEM((N_GROUPS - 1, GW // LANES, tm, LANES), F32),
                        pltpu.VMEM((N_GROUPS - 1, tm, LANES), F32)],
        compiler_params=_cparams("parallel"),
        name="merge_out",
    )(*outs, *lses, w, res)


def _gather_attn_kernel(q_ref, kvn_ref, c0_ref, c1_ref, c2_ref, o_ref):
    caches = [c0_ref, c1_ref, c2_ref]
    b = pl.program_id(0)

    def column(ref):
        lane = lax.broadcasted_iota(jnp.int32, ref.shape, 1)
        col = jnp.sum(jnp.where(lane == b, ref[...], 0.0), -1, keepdims=True)
        return col.reshape(ref.shape[0] // HD_B, HD_B, 1)

    q_all, kv_all = column(q_ref), column(kvn_ref)
    nh = N_GROUPS * HG
    outs, lses = [], []
    for g, (_, dil) in enumerate(GROUPS):
        q = q_all[g * HG:(g + 1) * HG] * (HD_B ** -0.5)
        kn = kv_all[g * HG:(g + 1) * HG]
        vn = kv_all[nh + g * HG:nh + (g + 1) * HG]
        kc, vc = caches[g][0], caches[g][1]
        rows = kc.shape[-1]
        s = jnp.sum(kc * q, 1, keepdims=True)
        row = lax.broadcasted_iota(jnp.int32, (1, 1, rows), 2)
        s = jnp.where((row & (dil - 1)) == 0, s, NEG)
        s_new = jnp.sum(kn * q, 1, keepdims=True)
        m = jnp.maximum(jnp.max(s, 2, keepdims=True), s_new)
        e = jnp.exp(s - m)
        e_new = jnp.exp(s_new - m)
        den = jnp.sum(e, 2, keepdims=True) + e_new
        outs.append((jnp.sum(e * vc, 2, keepdims=True) + e_new * vn) / den)
        lses.append(m + jnp.log(den))
    m = jnp.maximum(jnp.maximum(lses[0], lses[1]), lses[2])
    es = [jnp.exp(l - m) for l in lses]
    o_ref[...] = (es[0] * outs[0] + es[1] * outs[1] + es[2] * outs[2]) / (es[0] + es[1] + es[2])


def _gather_attn(q, kv_new, caches):
    nb = caches[0].shape[0]
    span = GROUPS[0][0] // GROUPS[0][1]
    views = []
    for (win, dil), c in zip(GROUPS, caches):
        lb = c.shape[1]
        assert lb == win and lb // dil == span, "window buffer must hold the full window"
        assert dil & (dil - 1) == 0, "dilations are powers of two"
        views.append(jnp.transpose(c, (0, 2, 3, 4, 1)))
    return pl.pallas_call(
        _gather_attn_kernel,
        grid=(nb,),
        in_specs=[pl.BlockSpec(q.shape, lambda b: (0, 0)),
                  pl.BlockSpec(kv_new.shape, lambda b: (0, 0))] +
                 [pl.BlockSpec((None, 2, HG, HD_B, v.shape[-1]), lambda b: (b, 0, 0, 0, 0))
                  for v in views],
        out_specs=pl.BlockSpec((None, HG, HD_B, 1), lambda b: (b, 0, 0, 0)),
        out_shape=jax.ShapeDtypeStruct((nb, HG, HD_B, 1), F32),
        compiler_params=_cparams("parallel"),
        name="gather_attn",
    )(q, kv_new, *views)


def _transpose_cast_kernel(x_ref, o_ref):
    o_ref[...] = x_ref[...].T.astype(BF16)


def _transpose_cast(w, *, tk, tn):
    ne, k, n = w.shape
    return pl.pallas_call(
        _transpose_cast_kernel,
        grid=(ne, k // tk, n // tn),
        in_specs=[pl.BlockSpec((None, tk, tn), lambda e, i, j: (e, i, j))],
        out_specs=pl.BlockSpec((None, tn, tk), lambda e, i, j: (e, j, i)),
        out_shape=jax.ShapeDtypeStruct((ne, n, k), BF16),
        compiler_params=_cparams("parallel", "parallel", "parallel"),
        name="transpose_cast",
    )(w)


def _prep_weights(a_w_in, a_A_log, a_dt_bias, a_w_out, w_kv, b_w_q, b_w_out, dense_w_gu,
                  dense_w_down, moe_router, moe_w_gu, moe_w_down, ple_w, ple_gate_w, k_norm,
                  b_q_norm):
    d_model, a_in = a_w_in.shape[1:]
    a_in_pad = -(-a_in // LANES) * LANES
    wf = {
        'a_w_in': jnp.pad(a_w_in[0], ((0, 0), (0, a_in_pad - a_in))),
        'a_w_out': a_w_out[0], 'w_kv': w_kv, 'b_w_q': b_w_q[0], 'b_w_out': b_w_out[0],
        'dense_w_gu': dense_w_gu[0], 'dense_w_down': dense_w_down[0],
        'router': jnp.pad(moe_router[0], ((0, 0), (0, LANES - moe_router.shape[2]))),
        'ple_w': ple_w, 'ple_gate_w': ple_gate_w,
    }
    shared = {}
    hp = jnp.stack([a_A_log[0], a_dt_bias[0]])
    shared['a_hp'] = jnp.pad(hp, ((0, 0), (H_A, LANES - 2 * H_A)))
    shared['moe_wgu_t'] = _transpose_cast(moe_w_gu[0], tk=d_model, tn=1024)
    shared['moe_wd_t'] = _transpose_cast(moe_w_down[0], tk=896, tn=d_model)
    shared['k_gain'] = jnp.tile(k_norm, HG).reshape(1, GW)
    shared['q_gain'] = jnp.tile(b_q_norm[0], HG).reshape(1, GW)
    w_prompt = dict(shared, **{k: v.astype(BF16) for k, v in wf.items()})
    w_sample = dict(shared, **wf)
    return w_prompt, w_sample


def _layer0(x, p0, w, P, *, tm, in_tn, mixer):
    proj = _norm_mm(x, P['a_norm'][0], w['a_w_in'], tm=min(tm, 512), tn=in_tn)
    og, s_new = mixer(proj)
    h = _mm_res(og, w['a_w_out'], x, tm=tm)
    h = _ffn(h, P['ffn_norm'][0], w['dense_w_gu'], w['dense_w_down'], tm=tm, tf=512)
    h = _ple(h, P['ple_norm'][0], w['ple_gate_w'][0], (p0, 0), w['ple_w'][0], tm=tm)
    return h, proj, s_new


def _layer1_tail(h, p1, w, P, *, tm):
    m = h.shape[0]
    tmr = next((t for t in (2048, 1024, 512, 256, LANES) if m % t == 0), LANES)
    hp = jnp.pad(h, ((0, -m % tmr), (0, 0)))
    ut, col, row, seg = _route(hp, P['ffn_norm'][1], w['router'], tm=tmr, n_valid=m)
    y_t = _experts(ut, col, row, seg, w['moe_wgu_t'], w['moe_wd_t'], tf=896)
    if tm % LANES:
        h, p1, tm = hp, jnp.pad(p1, ((0, 0), (0, -m % tmr), (0, 0))), tmr
    return _ple(h, P['ple_norm'][1], w['ple_gate_w'][1], (p1, 1), w['ple_w'][1], tm=tm,
                y_t=y_t)[:m]


def kernel(x_prompt, x_sample, p_prompt, p_sample, state_conv, state_delta, cache_kv_w128, cache_kv_w512, cache_kv_w2048, a_norm, a_w_in, a_conv_w, a_A_log, a_dt_bias, a_out_norm, a_w_out, kv_norm, w_kv, k_norm, b_norm, b_w_q, b_q_norm, b_w_out, ffn_norm, dense_w_gu, dense_w_down, moe_router, moe_w_gu, moe_w_down, ple_w, ple_norm, ple_gate_w):
    assert a_w_in.shape[0] == 1 and b_w_q.shape[0] == 1, "one mixer of each kind"
    bp, sp, dm = x_prompt.shape
    bs, ls, _ = x_sample.shape
    assert ls == 1, "sample group decodes one token per sequence"
    qkv_w = 3 * H_A * DK_A
    P = dict(a_norm=a_norm, ffn_norm=ffn_norm, ple_norm=ple_norm)
    w, ws = _prep_weights(a_w_in, a_A_log, a_dt_bias, a_w_out, w_kv, b_w_q, b_w_out, dense_w_gu,
                          dense_w_down, moe_router, moe_w_gu, moe_w_down, ple_w, ple_gate_w,
                          k_norm, b_q_norm)
    a_in_pad = w['a_w_in'].shape[1]
    in_tn = a_in_pad // 3 if a_in_pad % (3 * LANES) == 0 else LANES
    conv_w = a_conv_w[0]
    out_norm = a_out_norm[0].reshape(1, DK_A)

    mp = bp * sp
    tm = min(1024, sp)
    xp = x_prompt.reshape(mp, dm)
    conv0 = jnp.zeros((bp, CONV_W - 1, qkv_w), F32)
    s0 = jnp.zeros((bp, H_A, DK_A, DK_A), F32)
    h, proj, delta_p = _layer0(
        xp, p_prompt.reshape(p_prompt.shape[0], mp, -1), w, P, tm=tm, in_tn=in_tn,
        mixer=lambda pr: _gdn_prompt(pr, conv_w, w['a_hp'], out_norm, conv0, s0, batch=bp, seq=sp))
    conv_p = proj.reshape(bp, sp, -1)[:, sp - (CONV_W - 1):, :qkv_w][None]

    cos, sin = _rope_tables(jnp.arange(sp, dtype=jnp.int32))
    dils = [d for _, d in GROUPS]
    tmp = min(512, sp)
    kv, *kv_ph = _proj_rope(h, kv_norm, w['w_kv'], w['k_gain'], cos, sin, tm=tmp, n_rope=N_GROUPS,
                            natural=True, dils=dils + dils)
    q_ph = _proj_rope(h, b_norm[0], w['b_w_q'], w['q_gain'], cos, sin, tm=tmp, n_rope=N_GROUPS,
                      natural=False, dils=dils)
    outs, lses = [], []
    for gi in range(N_GROUPS):
        o, lse = _band_attn(q_ph[gi], kv_ph[gi], kv_ph[N_GROUPS + gi], gi, batch=bp, seq=sp)
        outs.append(o)
        lses.append(lse)
    h = _merge_out(outs, lses, w['b_w_out'], h, tm=tm)
    y_prompt = _layer1_tail(h, p_prompt.reshape(p_prompt.shape[0], mp, -1), w, P, tm=tm).reshape(bp, sp, dm)
    kv3 = kv.reshape(bp, sp, 2 * N_GROUPS * GW)
    kv_p = []
    for gi, (win, _) in enumerate(GROUPS):
        rows = kv3[:, sp - min(win, sp):]
        k_g = rows[:, :, gi * GW:(gi + 1) * GW]
        v_g = rows[:, :, (N_GROUPS + gi) * GW:(N_GROUPS + gi + 1) * GW]
        kv_p.append(jnp.stack([k_g, v_g], axis=2).reshape(bp, -1, 2, HG, HD_B))

    xs = x_sample.reshape(bs, dm)
    hs, proj_s, delta_s = _layer0(
        xs, p_sample.reshape(p_sample.shape[0], bs, -1), ws, P, tm=bs, in_tn=in_tn,
        mixer=lambda pr: _gdn_step(pr, state_conv[0], conv_w, ws['a_hp'], out_norm, state_delta[0]))
    conv_s = jnp.concatenate([state_conv[0][:, 1:], proj_s[:, None, :qkv_w]], axis=1)[None]
    bs_pad = -(-bs // LANES) * LANES
    hs_pad = jnp.pad(hs, ((0, bs_pad - bs), (0, 0)))
    cos_s, sin_s = _rope_tables(jnp.full((bs_pad,), PAST_LEN, jnp.int32))
    kv_s, kv_cols = _proj_rope(hs_pad, kv_norm, ws['w_kv'], ws['k_gain'], cos_s, sin_s, tm=bs_pad,
                               n_rope=N_GROUPS, natural=True, columns=True, dils=())
    q_cols, = _proj_rope(hs_pad, b_norm[0], ws['b_w_q'], ws['q_gain'], cos_s, sin_s, tm=bs_pad,
                         n_rope=N_GROUPS, natural=False, columns=True, dils=())
    o_s = _gather_attn(q_cols, kv_cols, (cache_kv_w128, cache_kv_w512, cache_kv_w2048))
    hs = _mm_res(o_s.reshape(bs, GW), ws['b_w_out'], hs, tm=bs)
    y_sample = _layer1_tail(hs, p_sample.reshape(p_sample.shape[0], bs, -1), ws, P, tm=bs).reshape(bs, 1, dm)
    kvs5 = kv_s[:bs].reshape(bs, 1, 2, N_GROUPS, HG, HD_B)
    kv_sn = [kvs5[:, :, :, gi] for gi in range(N_GROUPS)]

    return (y_prompt, y_sample, conv_p, conv_s, delta_p[None], delta_s[None],
            kv_p[0], kv_sn[0], kv_p[1], kv_sn[1], kv_p[2], kv_sn[2])
```

```python
import functools

import jax
import jax.numpy as jnp
from jax import lax
from jax.experimental import pallas as pl
from jax.experimental.pallas import tpu as pltpu

F32 = jnp.float32
BF16 = jnp.bfloat16

EPS = 1e-6
PAST_LEN = 16384
GROUPS = ((128, 1), (512, 4), (2048, 16))
N_GROUPS = len(GROUPS)
HG = 8
HD_B = 64
ROT_DIM = HD_B // 4
ROPE_THETA = 500000.0
GW = HG * HD_B
H_A = 8
DK_A = 128
CONV_W = 4
CHUNK = 64
LANES = 128
VMEM_LIMIT = 57 * 1024 * 1024
NEG = -1e30


def _cparams(*sem):
    return pltpu.CompilerParams(dimension_semantics=sem, vmem_limit_bytes=VMEM_LIMIT)


def _rms_rows(x, gain):
    return x * lax.rsqrt(jnp.mean(x * x, -1, keepdims=True) + EPS) * gain


def _silu(x):
    return x * (1.0 / (1.0 + jnp.exp(-x)))


def _sigmoid(x):
    return 1.0 / (1.0 + jnp.exp(-x))


def _dot(a, b):
    return jnp.dot(a.astype(BF16), b.astype(BF16), preferred_element_type=F32)


def _wdot(a, w):
    dot = functools.partial(jnp.dot, preferred_element_type=F32)
    if w.dtype == BF16:
        return dot(a.astype(BF16), w)
    a = a.astype(F32)
    a_hi = a.astype(BF16)
    a_lo = (a - a_hi.astype(F32)).astype(BF16)
    w_hi = w.astype(BF16)
    w_lo = (w - w_hi.astype(F32)).astype(BF16)
    return dot(a_hi, w_hi) + (dot(a_lo, w_hi) + dot(a_hi, w_lo))


def _act_dtype(w):
    return BF16 if w.dtype == BF16 else F32


def _dot_nt(a, b):
    return lax.dot_general(a.astype(BF16), b.astype(BF16), (((1,), (1,)), ((), ())),
                           preferred_element_type=F32)


def _dot_tn(a, b):
    return lax.dot_general(a.astype(BF16), b.astype(BF16), (((0,), (0,)), ((), ())),
                           preferred_element_type=F32)


def _head_norm_rope(x, hgain, cos, sin):
    t = x.shape[0]
    lane = lax.broadcasted_iota(jnp.int32, (t, LANES), 1)
    d = lane & (HD_B - 1)
    r = lax.broadcasted_iota(jnp.int32, (LANES, LANES), 0)
    c_ = lax.broadcasted_iota(jnp.int32, (LANES, LANES), 1)
    same_head = jnp.where((r < HD_B) == (c_ < HD_B), 1.0, 0.0).astype(BF16)
    dot = functools.partial(jnp.dot, preferred_element_type=F32)
    outs = []
    for c in range(GW // LANES):
        sl = slice(c * LANES, (c + 1) * LANES)
        xb = x[:, sl]
        sq = xb * xb
        sq_hi = sq.astype(BF16)
        sq_lo = (sq - sq_hi.astype(F32)).astype(BF16)
        ssq = dot(sq_hi, same_head) + dot(sq_lo, same_head)
        yb = xb * lax.rsqrt(ssq * (1.0 / HD_B) + EPS) * hgain[:, sl]
        half = ROT_DIM // 2
        rot = jnp.where(d < half, pltpu.roll(yb, LANES - half, 1), pltpu.roll(yb, half, 1))
        outs.append(yb * cos[:, sl] + rot * sin[:, sl])
    return jnp.concatenate(outs, axis=1)


def _norm_mm_kernel(x_ref, g_ref, w_ref, o_ref, *, tn):
    u = _rms_rows(x_ref[...], g_ref[...]).astype(_act_dtype(w_ref))
    for j in range(w_ref.shape[1] // tn):
        cols = slice(j * tn, (j + 1) * tn)
        o_ref[:, cols] = _wdot(u, w_ref[:, cols])


def _norm_mm(x, gain, w, *, tm, tn):
    m, k = x.shape
    n = w.shape[1]
    return pl.pallas_call(
        functools.partial(_norm_mm_kernel, tn=tn),
        grid=(m // tm,),
        in_specs=[pl.BlockSpec((tm, k), lambda i: (i, 0)),
                  pl.BlockSpec((1, k), lambda i: (0, 0)),
                  pl.BlockSpec((k, n), lambda i: (0, 0))],
        out_specs=pl.BlockSpec((tm, n), lambda i: (i, 0)),
        out_shape=jax.ShapeDtypeStruct((m, n), F32),
        compiler_params=_cparams("parallel"),
        name="norm_mm",
    )(x, gain.reshape(1, k), w)


def _proj_rope_kernel(x_ref, g_ref, w_ref, hg_ref, cos_ref, sin_ref, *rest,
                      n_rope, natural, columns, dils, tm):
    n_out = int(natural) + int(columns) + len(dils)
    outs, (slab_ref,) = rest[:n_out], rest[n_out:]
    nat_ref = outs[0] if natural else None
    col_ref = outs[int(natural)] if columns else None
    ph_refs = outs[int(natural) + int(columns):]
    u = _rms_rows(x_ref[...], g_ref[...]).astype(_act_dtype(w_ref))
    slab = 0
    for jj in range(w_ref.shape[1] // GW):
        cols = slice(jj * GW, (jj + 1) * GW)
        y = _wdot(u, w_ref[:, cols])
        if jj < n_rope:
            y = _head_norm_rope(y, hg_ref[...], cos_ref[...], sin_ref[...])
        if natural:
            nat_ref[:, cols] = y
        if columns:
            col_ref[cols, :] = y.T
        if jj >= len(dils):
            continue
        d = dils[jj]
        if d == 1:
            ph_refs[jj][0] = y.astype(BF16)
            continue
        for c in range(GW // LANES):
            slab_ref[slab, c] = y[:, c * LANES:(c + 1) * LANES]
        for r in range(d):
            for c in range(GW // LANES):
                ph_refs[jj][r, :, c * LANES:(c + 1) * LANES] = (
                    slab_ref[slab, c, pl.ds(r, tm // d, stride=d), :].astype(BF16))
        slab += 1


def _proj_rope(x, gain, w, hgain, cos, sin, *, tm, n_rope, natural, dils, columns=False):
    m, k = x.shape
    n = w.shape[1]
    pos_blocks = cos.shape[0] // tm
    out_specs, out_shape = [], []
    if natural:
        out_specs.append(pl.BlockSpec((tm, n), lambda i: (i, 0)))
        out_shape.append(jax.ShapeDtypeStruct((m, n), F32))
    if columns:
        out_specs.append(pl.BlockSpec((n, tm), lambda i: (0, i)))
        out_shape.append(jax.ShapeDtypeStruct((n, m), F32))
    for d in dils:
        out_specs.append(pl.BlockSpec((d, tm // d, GW), lambda i: (0, i, 0)))
        out_shape.append(jax.ShapeDtypeStruct((d, m // d, GW), BF16))
    n_slabs = max(1, sum(d > 1 for d in dils))
    return pl.pallas_call(
        functools.partial(_proj_rope_kernel, n_rope=n_rope, natural=natural, columns=columns,
                          dils=tuple(dils), tm=tm),
        grid=(m // tm,),
        in_specs=[pl.BlockSpec((tm, k), lambda i: (i, 0)),
                  pl.BlockSpec((1, k), lambda i: (0, 0)),
                  pl.BlockSpec((k, n), lambda i: (0, 0)),
                  pl.BlockSpec((1, GW), lambda i: (0, 0)),
                  pl.BlockSpec((tm, GW), lambda i: (i % pos_blocks, 0)),
                  pl.BlockSpec((tm, GW), lambda i: (i % pos_blocks, 0))],
        out_specs=out_specs,
        out_shape=out_shape,
        scratch_shapes=[pltpu.VMEM((n_slabs, GW // LANES, tm, LANES), F32)],
        compiler_params=_cparams("parallel"),
        name="proj_rope",
    )(x, gain.reshape(1, k), w, hgain, cos, sin)


def _rope_tables(pos):
    half = ROT_DIM // 2
    inv = ROPE_THETA ** (-jnp.arange(half, dtype=F32) * 2.0 / ROT_DIM)
    ang = pos.astype(F32)[:, None] * inv[None]
    c, s = jnp.cos(ang), jnp.sin(ang)
    n = pos.shape[0]
    cos_h = jnp.concatenate([c, c, jnp.ones((n, HD_B - ROT_DIM), F32)], 1)
    sin_h = jnp.concatenate([-s, s, jnp.zeros((n, HD_B - ROT_DIM), F32)], 1)
    return jnp.tile(cos_h, (1, HG)), jnp.tile(sin_h, (1, HG))


def _mm_res_kernel(x_ref, w_ref, r_ref, o_ref):
    o_ref[...] = r_ref[...] + _wdot(x_ref[...], w_ref[...])


def _mm_res(x, w, res, *, tm):
    m, k = x.shape
    n = w.shape[1]
    return pl.pallas_call(
        _mm_res_kernel,
        grid=(m // tm,),
        in_specs=[pl.BlockSpec((tm, k), lambda i: (i, 0)),
                  pl.BlockSpec((k, n), lambda i: (0, 0)),
                  pl.BlockSpec((tm, n), lambda i: (i, 0))],
        out_specs=pl.BlockSpec((tm, n), lambda i: (i, 0)),
        out_shape=jax.ShapeDtypeStruct((m, n), F32),
        compiler_params=_cparams("parallel"),
        name="mm_res",
    )(x, w, res)


def _unit_lower_inverses(mats):
    c = mats[0].shape[0]
    row = lax.broadcasted_iota(jnp.int32, (c, c), 0)
    col = lax.broadcasted_iota(jnp.int32, (c, c), 1)
    eye = jnp.where(row == col, 1.0, 0.0).astype(F32)
    ts = None
    b = 1
    while b < c:
        sel = ((row ^ col) < 2 * b) & ((row & b) != 0) & ((col & b) == 0)
        lows = [jnp.where(sel, a, 0.0) for a in mats]
        if ts is None:
            ts = [eye - low for low in lows]
        else:
            tl = [_dot(t, low) for t, low in zip(ts, lows)]
            ts = [t - _dot(x, t) for t, x in zip(ts, tl)]
        b *= 2
    return ts


def _gdn_head_params(ba, hp):
    beta = _sigmoid(ba)
    x = ba + hp[1:2, :]
    softplus = jnp.maximum(x, 0.0) + jnp.log(1.0 + jnp.exp(-jnp.abs(x)))
    g = -jnp.exp(hp[0:1, :]) * softplus
    return beta, g


def _gdn_kernel(qkv_ref, z_ref, ba_ref, cw_ref, hp_ref, on_ref, conv0_ref, s0_ref,
                og_ref, sout_ref, xbuf, s_scr, *, C, nch):
    n = pl.program_id(1)
    R = nch * C
    pad = 8

    @pl.when(n == 0)
    def _():
        xbuf[pad - (CONV_W - 1):pad, :] = conv0_ref[...]
        s_scr[...] = s0_ref[...]

    xbuf[pad:pad + R, :] = qkv_ref[...]

    def conv_cols(c0):
        acc = None
        for j in range(CONV_W):
            r0 = pad - (CONV_W - 1) + j
            term = xbuf[r0:r0 + R, c0:c0 + DK_A] * cw_ref[j:j + 1, c0:c0 + DK_A]
            acc = term if acc is None else acc + term
        return _silu(acc)

    beta, g = _gdn_head_params(ba_ref[...], hp_ref[...])
    rr = lax.broadcasted_iota(jnp.int32, (R, R), 0)
    rc = lax.broadcasted_iota(jnp.int32, (R, R), 1)
    blocktri = ((rr >= rc) & ((rr ^ rc) < C)).astype(F32)
    gcum = jnp.dot(blocktri, g, preferred_element_type=F32, precision=lax.Precision.HIGHEST)
    gcum_t = gcum.T
    row = lax.broadcasted_iota(jnp.int32, (C, C), 0)
    col = lax.broadcasted_iota(jnp.int32, (C, C), 1)
    incl = row >= col
    strict = row > col

    units = [(c, h) for c in range(nch) for h in range(H_A)]
    qs, ks, vs = {}, {}, {}
    for h in range(H_A):
        q = conv_cols(h * DK_A)
        k = conv_cols((H_A + h) * DK_A)
        v = conv_cols((2 * H_A + h) * DK_A)
        q = q * lax.rsqrt(jnp.sum(q * q, -1, keepdims=True) + EPS) * (DK_A ** -0.5)
        k = k * lax.rsqrt(jnp.sum(k * k, -1, keepdims=True) + EPS)
        for c in range(nch):
            rs = slice(c * C, (c + 1) * C)
            qs[c, h], ks[c, h], vs[c, h] = q[rs], k[rs], v[rs]

    bcs, gcs, decays, kbs = {}, {}, {}, {}
    for c, h in units:
        rs = slice(c * C, (c + 1) * C)
        bcs[c, h] = beta[rs, h:h + 1]
        gcs[c, h] = gcum[rs, H_A + h:H_A + h + 1]
        gr = gcum_t[H_A + h:H_A + h + 1, rs]
        decays[c, h] = jnp.exp(jnp.where(incl, gcs[c, h] - gr, NEG))
        kbs[c, h] = ks[c, h] * bcs[c, h]
    grams = {u: _dot_nt(jnp.concatenate([kbs[u], qs[u]], axis=0), ks[u]) for u in units}
    a_mats = [jnp.where(strict, grams[u][:C] * decays[u], 0.0) for u in units]
    aqks = {u: grams[u][C:] * decays[u] for u in units}
    t_mats = dict(zip(units, _unit_lower_inverses(a_mats)))
    egs = {u: jnp.exp(gcs[u]) for u in units}
    sols = {u: _dot(t_mats[u], jnp.concatenate([vs[u] * bcs[u], kbs[u] * egs[u]], axis=1))
            for u in units}

    states = [s_scr[h] for h in range(H_A)]
    for c in range(nch):
        rs = slice(c * C, (c + 1) * C)
        for h in range(H_A):
            u = (c, h)
            g_last = gcs[u][C - 1:C, :]
            ws = _dot(jnp.concatenate([sols[u][:, DK_A:], qs[u] * egs[u]], axis=0), states[h])
            v_new = sols[u][:, :DK_A] - ws[:C]
            o = ws[C:] + _dot(aqks[u], v_new)
            kd = ks[u] * jnp.exp(g_last - gcs[u])
            states[h] = states[h] * jnp.exp(g_last) + _dot_tn(kd, v_new)
            o = _rms_rows(o, on_ref[...]) * _silu(z_ref[rs, h * DK_A:(h + 1) * DK_A])
            og_ref[rs, h * DK_A:(h + 1) * DK_A] = o
    for h in range(H_A):
        s_scr[h] = states[h]

    xbuf[pad - (CONV_W - 1):pad, :] = xbuf[pad + R - (CONV_W - 1):pad + R, :]

    @pl.when(n == pl.num_programs(1) - 1)
    def _():
        sout_ref[...] = s_scr[...]


def _gdn_prompt(proj, conv_w, hp, out_norm, conv0, s0, *, batch, seq):
    C = min(CHUNK, seq)
    nch = next(n for n in (4, 2, 1) if seq % (n * C) == 0)
    R = nch * C
    nc = seq // R
    qkv_w = 3 * H_A * DK_A
    z_w = H_A * DK_A
    return pl.pallas_call(
        functools.partial(_gdn_kernel, C=C, nch=nch),
        grid=(batch, nc),
        in_specs=[pl.BlockSpec((R, qkv_w), lambda b, n: (b * nc + n, 0)),
                  pl.BlockSpec((R, z_w), lambda b, n: (b * nc + n, qkv_w // z_w)),
                  pl.BlockSpec((R, LANES), lambda b, n: (b * nc + n, (qkv_w + z_w) // LANES)),
                  pl.BlockSpec((CONV_W, qkv_w), lambda b, n: (0, 0)),
                  pl.BlockSpec((2, LANES), lambda b, n: (0, 0)),
                  pl.BlockSpec((1, DK_A), lambda b, n: (0, 0)),
                  pl.BlockSpec((None, CONV_W - 1, qkv_w), lambda b, n: (b, 0, 0)),
                  pl.BlockSpec((None, H_A, DK_A, DK_A), lambda b, n: (b, 0, 0, 0))],
        out_specs=[pl.BlockSpec((R, z_w), lambda b, n: (b * nc + n, 0)),
                   pl.BlockSpec((None, H_A, DK_A, DK_A), lambda b, n: (b, 0, 0, 0))],
        out_shape=[jax.ShapeDtypeStruct((batch * seq, z_w), F32),
                   jax.ShapeDtypeStruct((batch, H_A, DK_A, DK_A), F32)],
        scratch_shapes=[pltpu.VMEM((R + 8, qkv_w), F32),
                        pltpu.VMEM((H_A, DK_A, DK_A), F32)],
        compiler_params=_cparams("parallel", "arbitrary"),
        name="gdn_chunked",
    )(proj, proj, proj, conv_w, hp, out_norm, conv0, s0)


def _gdn_step_kernel(proj_ref, conv_ref, cw_ref, hp_ref, on_ref, s0_ref, og_ref, sout_ref, qk_scr):
    for r in range(proj_ref.shape[0]):
        _gdn_step_row(proj_ref.at[r], conv_ref.at[r], cw_ref, hp_ref, on_ref, s0_ref.at[r],
                      og_ref.at[r], sout_ref.at[r], qk_scr.at[r])


def _gdn_step_row(proj_ref, conv_ref, cw_ref, hp_ref, on_ref, s0_ref, og_ref, sout_ref, qk_scr):
    qkv_w = 3 * H_A * DK_A
    z_w = H_A * DK_A

    def conv_cols(c0):
        sl = slice(c0, c0 + DK_A)
        acc = proj_ref[:, sl] * cw_ref[CONV_W - 1:CONV_W, sl]
        for j in range(CONV_W - 1):
            acc = acc + conv_ref[j:j + 1, sl] * cw_ref[j:j + 1, sl]
        return _silu(acc)

    beta, g = _gdn_head_params(proj_ref[:, qkv_w + z_w:qkv_w + z_w + LANES], hp_ref[...])
    qk_scr[...] = jnp.zeros_like(qk_scr)
    vs = []
    for h in range(H_A):
        q = conv_cols(h * DK_A)
        k = conv_cols((H_A + h) * DK_A)
        vs.append(conv_cols((2 * H_A + h) * DK_A))
        qk_scr[H_A + h:H_A + h + 1, :] = (
            q * lax.rsqrt(jnp.sum(q * q, -1, keepdims=True) + EPS) * (DK_A ** -0.5))
        qk_scr[h:h + 1, :] = k * lax.rsqrt(jnp.sum(k * k, -1, keepdims=True) + EPS)
    qk = qk_scr[...]
    qk_t = qk.T
    for h in range(H_A):
        k_row = qk[h:h + 1, :]
        q_row = qk[H_A + h:H_A + h + 1, :]
        k_col = qk_t[:, h:h + 1]
        q_col = qk_t[:, H_A + h:H_A + h + 1]
        bh = beta[:, h:h + 1]
        eg = jnp.exp(g[:, H_A + h:H_A + h + 1])
        s = s0_ref[h]
        k_s = jnp.sum(s * k_col, 0, keepdims=True)
        q_s = jnp.sum(s * q_col, 0, keepdims=True)
        v_new = bh * (vs[h] - eg * k_s)
        o = eg * q_s + jnp.sum(q_row * k_row, -1, keepdims=True) * v_new
        sout_ref[h] = s * eg + k_col * v_new
        o = _rms_rows(o, on_ref[...]) * _silu(proj_ref[:, qkv_w + h * DK_A:qkv_w + (h + 1) * DK_A])
        og_ref[:, h * DK_A:(h + 1) * DK_A] = o


def _gdn_step(proj, conv_state, conv_w, hp, out_norm, s0):
    nb, pw = proj.shape
    qkv_w = 3 * H_A * DK_A
    z_w = H_A * DK_A
    rows = next(r for r in (4, 2, 1) if nb % r == 0)
    og, s_new = pl.pallas_call(
        _gdn_step_kernel,
        grid=(nb // rows,),
        in_specs=[pl.BlockSpec((rows, 1, pw), lambda b: (b, 0, 0)),
                  pl.BlockSpec((rows, CONV_W - 1, qkv_w), lambda b: (b, 0, 0)),
                  pl.BlockSpec((CONV_W, qkv_w), lambda b: (0, 0)),
                  pl.BlockSpec((2, LANES), lambda b: (0, 0)),
                  pl.BlockSpec((1, DK_A), lambda b: (0, 0)),
                  pl.BlockSpec((rows, H_A, DK_A, DK_A), lambda b: (b, 0, 0, 0))],
        out_specs=[pl.BlockSpec((rows, 1, z_w), lambda b: (b, 0, 0)),
                   pl.BlockSpec((rows, H_A, DK_A, DK_A), lambda b: (b, 0, 0, 0))],
        out_shape=[jax.ShapeDtypeStruct((nb, 1, z_w), F32),
                   jax.ShapeDtypeStruct((nb, H_A, DK_A, DK_A), F32)],
        scratch_shapes=[pltpu.VMEM((rows, LANES, DK_A), F32)],
        compiler_params=_cparams("parallel"),
        name="gdn_step",
    )(proj.reshape(nb, 1, pw), conv_state, conv_w, hp, out_norm, s0)
    return og.reshape(nb, z_w), s_new


def _ffn_kernel(h_ref, g_ref, wg_ref, wu_ref, wd_ref, o_ref, u_ref, acc_ref):
    f = pl.program_id(1)

    @pl.when(f == 0)
    def _():
        u_ref[...] = _rms_rows(h_ref[...], g_ref[...]).astype(u_ref.dtype)
        acc_ref[...] = jnp.zeros_like(acc_ref)

    u = u_ref[...]
    gate = _wdot(u, wg_ref[...])
    up = _wdot(u, wu_ref[...])
    acc_ref[...] += _wdot(_silu(gate) * up, wd_ref[...])

    @pl.when(f == pl.num_programs(1) - 1)
    def _():
        o_ref[...] = h_ref[...] + acc_ref[...]


def _ffn(h, gain, w_gu, w_down, *, tm, tf):
    m, dm = h.shape
    ff = w_down.shape[0]
    nf = ff // tf
    return pl.pallas_call(
        _ffn_kernel,
        grid=(m // tm, nf),
        in_specs=[pl.BlockSpec((tm, dm), lambda i, f: (i, 0)),
                  pl.BlockSpec((1, dm), lambda i, f: (0, 0)),
                  pl.BlockSpec((dm, tf), lambda i, f: (0, f)),
                  pl.BlockSpec((dm, tf), lambda i, f: (0, nf + f)),
                  pl.BlockSpec((tf, dm), lambda i, f: (f, 0))],
        out_specs=pl.BlockSpec((tm, dm), lambda i, f: (i, 0)),
        out_shape=jax.ShapeDtypeStruct((m, dm), F32),
        scratch_shapes=[pltpu.VMEM((tm, dm), _act_dtype(w_gu)),
                        pltpu.VMEM((tm, dm), F32)],
        compiler_params=_cparams("parallel", "arbitrary"),
        name="ffn_dense",
    )(h, gain.reshape(1, dm), w_gu, w_gu, w_down)


N_EXPERTS = 8
SEG_ALIGN = LANES


def _top2(logits):
    t = logits.shape[0]
    lane = lax.broadcasted_iota(jnp.int32, (t, LANES), 1)
    valid = lane < N_EXPERTS
    lg = jnp.where(valid, logits, NEG)
    mx = jnp.max(lg, -1, keepdims=True)
    e = jnp.where(valid, jnp.exp(lg - mx), 0.0)
    probs = e / jnp.sum(e, -1, keepdims=True)
    p1 = jnp.max(probs, -1, keepdims=True)
    i1 = jnp.min(jnp.where((probs == p1) & valid, lane, LANES), -1, keepdims=True)
    rest = jnp.where((lane == i1) | ~valid, -1.0, probs)
    p2 = jnp.max(rest, -1, keepdims=True)
    i2 = jnp.min(jnp.where(rest == p2, lane, LANES), -1, keepdims=True)
    tot = p1 + p2
    return i1, i2, p1 / tot, p2 / tot


def _router_kernel(h_ref, g_ref, r_ref, tri_ref, upper_ref, ut_ref, col_ref, row_ref, seg_ref,
                   *, n_valid):
    t = h_ref.shape[0]
    sub = tri_ref.shape[0]
    u = _rms_rows(h_ref[...], g_ref[...])
    ut_ref[...] = u.T.astype(BF16)
    i1, i2, g1, g2 = _top2(_wdot(u, r_ref[...]))
    lane = lax.broadcasted_iota(jnp.int32, (t, LANES), 1)
    tok = pl.program_id(0) * t + lax.broadcasted_iota(jnp.int32, (t, 1), 0)
    valid = tok < n_valid
    sel = jnp.where(valid & ((lane == i1) | (lane == i2)), 1.0, 0.0)
    counts = jnp.zeros((1, LANES), F32)
    pos_parts = []
    for k in range(t // sub):
        sel_k = sel[k * sub:(k + 1) * sub]
        pos_parts.append(counts + jnp.dot(tri_ref[...], sel_k.astype(BF16),
                                          preferred_element_type=F32))
        counts = counts + jnp.sum(sel_k, 0, keepdims=True)
    pos = jnp.concatenate(pos_parts, axis=0)
    nblk = jnp.floor((counts + (SEG_ALIGN - 1)) * (1.0 / SEG_ALIGN))
    nblk8 = jnp.broadcast_to(nblk, (8, LANES))
    start8 = jnp.dot(nblk8.astype(BF16), upper_ref[...], preferred_element_type=F32)
    dest = start8[0:1] * SEG_ALIGN + pos
    d1 = jnp.where(valid, jnp.sum(jnp.where(lane == i1, dest, 0.0), -1, keepdims=True), -1.0)
    d2 = jnp.where(valid, jnp.sum(jnp.where(lane == i2, dest, 0.0), -1, keepdims=True), -1.0)
    col = jnp.where(lane == 0, d1, jnp.where(lane == 1, d2,
                    jnp.where(lane == 2, g1, jnp.where(lane == 3, g2, 0.0))))
    col_ref[...] = col
    row_ref[...] = col.T[0:8, :]
    lane8 = lane[0:8]
    seg = jnp.where(lane8 < N_EXPERTS, start8,
                    jnp.where(lane8 < 2 * N_EXPERTS, pltpu.roll(nblk8, N_EXPERTS, 1), 0.0))
    seg_ref[...] = seg[0:1].astype(jnp.int32)


def _route(h, gain, router, *, tm, n_valid):
    m, dm = h.shape
    nt = m // tm
    sub = next(s for s in (1024, 640, 512, 256, LANES) if tm % s == 0)
    tri = jnp.tril(jnp.ones((sub, sub), F32), -1).astype(BF16)
    upper = jnp.triu(jnp.ones((LANES, LANES), F32), 1).astype(BF16)
    return pl.pallas_call(
        functools.partial(_router_kernel, n_valid=n_valid),
        grid=(nt,),
        in_specs=[pl.BlockSpec((tm, dm), lambda i: (i, 0)),
                  pl.BlockSpec((1, dm), lambda i: (0, 0)),
                  pl.BlockSpec((dm, LANES), lambda i: (0, 0)),
                  pl.BlockSpec((sub, sub), lambda i: (0, 0)),
                  pl.BlockSpec((LANES, LANES), lambda i: (0, 0))],
        out_specs=[pl.BlockSpec((None, dm, tm), lambda i: (i, 0, 0)),
                   pl.BlockSpec((tm, LANES), lambda i: (i, 0)),
                   pl.BlockSpec((None, 8, tm), lambda i: (i, 0, 0)),
                   pl.BlockSpec((None, 1, LANES), lambda i: (i, 0, 0))],
        out_shape=[jax.ShapeDtypeStruct((nt, dm, tm), BF16),
                   jax.ShapeDtypeStruct((m, LANES), F32),
                   jax.ShapeDtypeStruct((nt, 8, tm), F32),
                   jax.ShapeDtypeStruct((nt, 1, LANES), jnp.int32)],
        compiler_params=_cparams("parallel"),
        name="moe_route",
    )(h, gain.reshape(1, dm), router, tri, upper)


def _one_hot_rows(row0, n, d1_row, d2_row):
    ridx = (lax.broadcasted_iota(jnp.int32, (n, d1_row.shape[1]), 0) + row0).astype(F32)
    return jnp.where((ridx == d1_row) | (ridx == d2_row), 1.0, 0.0).astype(BF16)


def _experts_kernel(seg_ref, ut_ref, col_ref, row_ref, wg_ref, wu_ref, wd_ref, o_ref,
                    xs_scr, acc_scr):
    i, e, f = pl.program_id(0), pl.program_id(1), pl.program_id(2)
    last_f = pl.num_programs(2) - 1
    t = ut_ref.shape[1]
    start = seg_ref[i * LANES + e]
    nblk = seg_ref[i * LANES + N_EXPERTS + e]

    @pl.when((e == 0) & (f == 0))
    def _():
        o_ref[...] = jnp.zeros_like(o_ref)

    def gather(lb, nb):
        w = nb * LANES
        col = col_ref[...]
        ridx = (lax.broadcasted_iota(jnp.int32, (t, w), 1) + (start + lb) * LANES).astype(F32)
        p_t = jnp.where((ridx == col[:, 0:1]) | (ridx == col[:, 1:2]), 1.0, 0.0).astype(BF16)
        x_t = jnp.dot(ut_ref[...], p_t, preferred_element_type=F32).astype(BF16)
        for k in range(nb):
            xs_scr[lb + k] = x_t[:, k * LANES:(k + 1) * LANES]
            acc_scr[lb + k] = jnp.zeros(acc_scr.shape[1:], F32)

    def swiglu(lb, nb):
        x_t = jnp.concatenate([xs_scr[lb + k] for k in range(nb)], axis=1)
        gate_t = jnp.dot(wg_ref[...], x_t, preferred_element_type=F32)
        up_t = jnp.dot(wu_ref[...], x_t, preferred_element_type=F32)
        act_t = (_silu(gate_t) * up_t).astype(BF16)
        down_t = jnp.dot(wd_ref[...], act_t, preferred_element_type=F32)
        for k in range(nb):
            acc_scr[lb + k] += down_t[:, k * LANES:(k + 1) * LANES]

    def combine(lb, nb):
        w = nb * LANES
        col = col_ref[...]
        row0 = (start + lb) * LANES
        ridx = (lax.broadcasted_iota(jnp.int32, (t, w), 1) + row0).astype(F32)
        gs = jnp.sum(jnp.where(ridx == col[:, 0:1], col[:, 2:3], 0.0) +
                     jnp.where(ridx == col[:, 1:2], col[:, 3:4], 0.0), 0, keepdims=True)
        acc_t = jnp.concatenate([acc_scr[lb + k] for k in range(nb)], axis=1)
        p = _one_hot_rows(row0, w, row_ref[0:1, :], row_ref[1:2, :])
        o_ref[...] += jnp.dot((acc_t * gs).astype(BF16), p, preferred_element_type=F32)

    def for_blocks(*stages):
        def run(lb, nb):
            for stage in stages:
                stage(lb, nb)

        n4 = nblk // 4

        def body(j, carry):
            run(4 * j, 4)
            return carry

        lax.fori_loop(0, n4, body, 0)
        rem = nblk - 4 * n4

        @pl.when((rem & 2) != 0)
        def _():
            run(4 * n4, 2)

        @pl.when((rem & 1) != 0)
        def _():
            run(4 * n4 + (rem & 2), 1)

    @pl.when(f == 0)
    def _():
        for_blocks(gather, swiglu)

    @pl.when((f > 0) & (f < last_f))
    def _():
        for_blocks(swiglu)

    @pl.when(f == last_f)
    def _():
        for_blocks(swiglu, combine)


def _experts(ut, col, row, seg, wgu_t, wd_t, *, tf):
    nt, dm, tm = ut.shape
    ne, _, ff = wd_t.shape
    nf = ff // tf
    assert nf >= 2, "first and last F block are distinct steps"
    grid_spec = pltpu.PrefetchScalarGridSpec(
        num_scalar_prefetch=1,
        grid=(nt, ne, nf),
        in_specs=[pl.BlockSpec((None, dm, tm), lambda i, e, f, s: (i, 0, 0),
                               pipeline_mode=pl.Buffered(1)),
                  pl.BlockSpec((tm, LANES), lambda i, e, f, s: (i, 0),
                               pipeline_mode=pl.Buffered(1)),
                  pl.BlockSpec((None, 8, tm), lambda i, e, f, s: (i, 0, 0)),
                  pl.BlockSpec((None, tf, dm), lambda i, e, f, s: (e, f, 0)),
                  pl.BlockSpec((None, tf, dm), lambda i, e, f, s: (e, nf + f, 0)),
                  pl.BlockSpec((None, dm, tf), lambda i, e, f, s: (e, 0, f))],
        out_specs=pl.BlockSpec((dm, tm), lambda i, e, f, s: (0, i)),
        scratch_shapes=[pltpu.VMEM((tm // LANES, dm, LANES), BF16),
                        pltpu.VMEM((tm // LANES, dm, LANES), F32)])
    return pl.pallas_call(
        _experts_kernel,
        grid_spec=grid_spec,
        out_shape=jax.ShapeDtypeStruct((dm, nt * tm), F32),
        compiler_params=_cparams("parallel", "arbitrary", "arbitrary"),
        name="moe_experts",
    )(seg.reshape(-1), ut, col, row, wgu_t, wgu_t, wd_t)


def _ple_kernel(h_ref, g_ref, gw_ref, p_ref, pw_ref, *rest):
    o_ref = rest[-1]
    h = h_ref[...]
    if len(rest) == 2:
        h = h + rest[0][...].T
    gate = _sigmoid(_wdot(_rms_rows(h, g_ref[...]), gw_ref[...]))
    o_ref[...] = h + _wdot(p_ref[...], pw_ref[...]) * gate


def _ple(h, gain, gate_w, p, ple_w, *, tm, y_t=None):
    m, dm = h.shape
    p, layer = p
    pd = p.shape[-1]
    in_specs = [pl.BlockSpec((tm, dm), lambda i: (i, 0)),
                pl.BlockSpec((1, dm), lambda i: (0, 0)),
                pl.BlockSpec((dm, dm), lambda i: (0, 0)),
                pl.BlockSpec((None, tm, pd), lambda i: (layer, i, 0)),
                pl.BlockSpec((pd, dm), lambda i: (0, 0))]
    args = [h, gain.reshape(1, dm), gate_w, p, ple_w]
    if y_t is not None:
        in_specs.append(pl.BlockSpec((dm, tm), lambda i: (0, i)))
        args.append(y_t)
    return pl.pallas_call(
        _ple_kernel,
        grid=(m // tm,),
        in_specs=in_specs,
        out_specs=pl.BlockSpec((tm, dm), lambda i: (i, 0)),
        out_shape=jax.ShapeDtypeStruct((m, dm), F32),
        compiler_params=_cparams("parallel"),
        name="ple",
    )(*args)


def _band_attn_kernel(q_ref, kp_ref, kc_ref, vp_ref, vc_ref, o_ref, l_ref, *, span, tq):
    j = pl.program_id(2)
    q = q_ref[...] * (HD_B ** -0.5)
    kk = jnp.concatenate([kp_ref[...], kc_ref[...]], axis=0)
    vv = jnp.concatenate([vp_ref[...], vc_ref[...]], axis=0)
    qi = lax.broadcasted_iota(jnp.int32, (span, 2 * span), 0)
    ki = lax.broadcasted_iota(jnp.int32, (span, 2 * span), 1)
    dist = qi + span - ki
    band = (dist >= 0) & (dist <= span)
    lane = lax.broadcasted_iota(jnp.int32, (span, LANES), 1)
    for sb in range(tq // span):
        r0 = sb * span
        mask = band & (ki >= jnp.where(j > 0, 0, span)) if sb == 0 else band
        heads = [slice(h * HD_B, (h + 1) * HD_B) for h in range(HG)]
        ss = [jnp.where(mask, _dot_nt(q[r0:r0 + span, hs], kk[r0:r0 + 2 * span, hs]), NEG)
              for hs in heads]
        ms = [jnp.max(s, -1, keepdims=True) for s in ss]
        es = [jnp.exp(s - m).astype(BF16) for s, m in zip(ss, ms)]
        ones = jnp.ones((2 * span, LANES), BF16)
        dens = [_dot(e, ones) for e in es]
        outs = [_dot(e, vv[r0:r0 + 2 * span, hs]) * (1.0 / den[:, :HD_B])
                for e, den, hs in zip(es, dens, heads)]
        lse_tile = jnp.zeros((span, LANES), F32)
        for h in range(HG):
            lse_tile = jnp.where(lane == h, ms[h] + jnp.log(dens[h]), lse_tile)
        o_ref[r0:r0 + span, :] = jnp.concatenate(outs, axis=1)
        l_ref[r0:r0 + span, :] = lse_tile


def _band_attn(q, k, v, gi, *, batch, seq):
    win, dil = GROUPS[gi]
    span = win // dil
    n = seq // dil
    tq = min(4 * span, n)
    nb = n // tq
    sub = tq // span
    cur = lambda b, r, j: (r, b * nb + j, 0)
    prev = lambda b, r, j: (r, b * nb * sub + jnp.maximum(j * sub - 1, 0), 0)
    return pl.pallas_call(
        functools.partial(_band_attn_kernel, span=span, tq=tq),
        grid=(batch, dil, nb),
        in_specs=[pl.BlockSpec((None, tq, GW), cur),
                  pl.BlockSpec((None, span, GW), prev),
                  pl.BlockSpec((None, tq, GW), cur),
                  pl.BlockSpec((None, span, GW), prev),
                  pl.BlockSpec((None, tq, GW), cur)],
        out_specs=[pl.BlockSpec((None, tq, GW), cur),
                   pl.BlockSpec((None, tq, LANES), cur)],
        out_shape=[jax.ShapeDtypeStruct((dil, batch * n, GW), F32),
                   jax.ShapeDtypeStruct((dil, batch * n, LANES), F32)],
        compiler_params=_cparams("parallel", "parallel", "arbitrary"),
        name=f"band_attn_g{gi}",
    )(q, k, k, v, v)


def _merge_out_kernel(o0_ref, o1_ref, o2_ref, l0_ref, l1_ref, l2_ref, w_ref, r_ref, o_ref,
                      o_scr, l_scr, *, tm):
    for gi, (o_ph, l_ph) in enumerate(((o1_ref, l1_ref), (o2_ref, l2_ref))):
        d = o_ph.shape[0]
        for r in range(d):
            rows = pl.ds(r, tm // d, stride=d)
            l_scr[gi, rows, :] = l_ph[r]
            for c in range(GW // LANES):
                o_scr[gi, c, rows, :] = o_ph[r, :, c * LANES:(c + 1) * LANES]
    ls = [l0_ref[0], l_scr[0], l_scr[1]]

    def o_cols(g, c):
        if g == 0:
            return o0_ref[0, :, c * LANES:(c + 1) * LANES]
        return o_scr[g - 1, c]

    m = jnp.maximum(jnp.maximum(ls[0], ls[1]), ls[2])
    es = [jnp.exp(l - m) for l in ls]
    inv = 1.0 / (es[0] + es[1] + es[2])
    t = ls[0].shape[0]
    lo = lax.broadcasted_iota(jnp.int32, (t, LANES), 1) < HD_B
    cols = []
    for c in range(GW // LANES):
        acc = None
        for g in range(N_GROUPS):
            wt = es[g] * inv
            wexp = jnp.where(lo, wt[:, 2 * c:2 * c + 1], wt[:, 2 * c + 1:2 * c + 2])
            term = wexp * o_cols(g, c)
            acc = term if acc is None else acc + term
        cols.append(acc)
    o = jnp.concatenate(cols, axis=1).astype(BF16)
    o_ref[...] = r_ref[...] + jnp.dot(o, w_ref[...], preferred_element_type=F32)


def _merge_out(outs, lses, w, res, *, tm):
    m, dm = res.shape
    ph_spec = lambda a: pl.BlockSpec((a.shape[0], tm // a.shape[0], a.shape[2]),
                                     lambda i: (0, i, 0))
    return pl.pallas_call(
        functools.partial(_merge_out_kernel, tm=tm),
        grid=(m // tm,),
        in_specs=[ph_spec(a) for a in outs] + [ph_spec(a) for a in lses] +
                 [pl.BlockSpec((GW, dm), lambda i: (0, 0)),
                  pl.BlockSpec((tm, dm), lambda i: (i, 0))],
        out_specs=pl.BlockSpec((tm, dm), lambda i: (i, 0)),
        out_shape=jax.ShapeDtypeStruct((m, dm), F32),
        scratch_shapes=[pltpu.VMEM((N_GROUPS - 1, GW // LANES, tm, LANES), F32),
                        pltpu.VMEM((N_GROUPS - 1, tm, LANES), F32)],
        compiler_params=_cparams("parallel"),
        name="merge_out",
    )(*outs, *lses, w, res)


def _gather_attn_kernel(q_ref, kvn_ref, c0_ref, c1_ref, c2_ref, o_ref):
    caches = [c0_ref, c1_ref, c2_ref]
    b = pl.program_id(0)

    def column(ref):
        lane = lax.broadcasted_iota(jnp.int32, ref.shape, 1)
        col = jnp.sum(jnp.where(lane == b, ref[...], 0.0), -1, keepdims=True)
        return col.reshape(ref.shape[0] // HD_B, HD_B, 1)

    q_all, kv_all = column(q_ref), column(kvn_ref)
    nh = N_GROUPS * HG
    outs, lses = [], []
    for g, (_, dil) in enumerate(GROUPS):
        q = q_all[g * HG:(g + 1) * HG] * (HD_B ** -0.5)
        kn = kv_all[g * HG:(g + 1) * HG]
        vn = kv_all[nh + g * HG:nh + (g + 1) * HG]
        kc, vc = caches[g][0], caches[g][1]
        rows = kc.shape[-1]
        s = jnp.sum(kc * q, 1, keepdims=True)
        row = lax.broadcasted_iota(jnp.int32, (1, 1, rows), 2)
        s = jnp.where((row & (dil - 1)) == 0, s, NEG)
        s_new = jnp.sum(kn * q, 1, keepdims=True)
        m = jnp.maximum(jnp.max(s, 2, keepdims=True), s_new)
        e = jnp.exp(s - m)
        e_new = jnp.exp(s_new - m)
        den = jnp.sum(e, 2, keepdims=True) + e_new
        outs.append((jnp.sum(e * vc, 2, keepdims=True) + e_new * vn) / den)
        lses.append(m + jnp.log(den))
    m = jnp.maximum(jnp.maximum(lses[0], lses[1]), lses[2])
    es = [jnp.exp(l - m) for l in lses]
    o_ref[...] = (es[0] * outs[0] + es[1] * outs[1] + es[2] * outs[2]) / (es[0] + es[1] + es[2])


def _gather_attn(q, kv_new, caches):
    nb = caches[0].shape[0]
    span = GROUPS[0][0] // GROUPS[0][1]
    views = []
    for (win, dil), c in zip(GROUPS, caches):
        lb = c.shape[1]
        assert lb == win and lb // dil == span, "window buffer must hold the full window"
        assert dil & (dil - 1) == 0, "dilations are powers of two"
        views.append(jnp.transpose(c, (0, 2, 3, 4, 1)))
    return pl.pallas_call(
        _gather_attn_kernel,
        grid=(nb,),
        in_specs=[pl.BlockSpec(q.shape, lambda b: (0, 0)),
                  pl.BlockSpec(kv_new.shape, lambda b: (0, 0))] +
                 [pl.BlockSpec((None, 2, HG, HD_B, v.shape[-1]), lambda b: (b, 0, 0, 0, 0))
                  for v in views],
        out_specs=pl.BlockSpec((None, HG, HD_B, 1), lambda b: (b, 0, 0, 0)),
        out_shape=jax.ShapeDtypeStruct((nb, HG, HD_B, 1), F32),
        compiler_params=_cparams("parallel"),
        name="gather_attn",
    )(q, kv_new, *views)


def _transpose_cast_kernel(x_ref, o_ref):
    o_ref[...] = x_ref[...].T.astype(BF16)


def _transpose_cast(w, *, tk, tn):
    ne, k, n = w.shape
    return pl.pallas_call(
        _transpose_cast_kernel,
        grid=(ne, k // tk, n // tn),
        in_specs=[pl.BlockSpec((None, tk, tn), lambda e, i, j: (e, i, j))],
        out_specs=pl.BlockSpec((None, tn, tk), lambda e, i, j: (e, j, i)),
        out_shape=jax.ShapeDtypeStruct((ne, n, k), BF16),
        compiler_params=_cparams("parallel", "parallel", "parallel"),
        name="transpose_cast",
    )(w)


def _prep_weights(a_w_in, a_A_log, a_dt_bias, a_w_out, w_kv, b_w_q, b_w_out, dense_w_gu,
                  dense_w_down, moe_router, moe_w_gu, moe_w_down, ple_w, ple_gate_w, k_norm,
                  b_q_norm):
    d_model, a_in = a_w_in.shape[1:]
    a_in_pad = -(-a_in // LANES) * LANES
    wf = {
        'a_w_in': jnp.pad(a_w_in[0], ((0, 0), (0, a_in_pad - a_in))),
        'a_w_out': a_w_out[0], 'w_kv': w_kv, 'b_w_q': b_w_q[0], 'b_w_out': b_w_out[0],
        'dense_w_gu': dense_w_gu[0], 'dense_w_down': dense_w_down[0],
        'router': jnp.pad(moe_router[0], ((0, 0), (0, LANES - moe_router.shape[2]))),
        'ple_w': ple_w, 'ple_gate_w': ple_gate_w,
    }
    shared = {}
    hp = jnp.stack([a_A_log[0], a_dt_bias[0]])
    shared['a_hp'] = jnp.pad(hp, ((0, 0), (H_A, LANES - 2 * H_A)))
    shared['moe_wgu_t'] = _transpose_cast(moe_w_gu[0], tk=d_model, tn=1024)
    shared['moe_wd_t'] = _transpose_cast(moe_w_down[0], tk=896, tn=d_model)
    shared['k_gain'] = jnp.tile(k_norm, HG).reshape(1, GW)
    shared['q_gain'] = jnp.tile(b_q_norm[0], HG).reshape(1, GW)
    w_prompt = dict(shared, **{k: v.astype(BF16) for k, v in wf.items()})
    w_sample = dict(shared, **wf)
    return w_prompt, w_sample


def _layer0(x, p0, w, P, *, tm, in_tn, mixer):
    proj = _norm_mm(x, P['a_norm'][0], w['a_w_in'], tm=min(tm, 512), tn=in_tn)
    og, s_new = mixer(proj)
    h = _mm_res(og, w['a_w_out'], x, tm=tm)
    h = _ffn(h, P['ffn_norm'][0], w['dense_w_gu'], w['dense_w_down'], tm=tm, tf=512)
    h = _ple(h, P['ple_norm'][0], w['ple_gate_w'][0], (p0, 0), w['ple_w'][0], tm=tm)
    return h, proj, s_new


def _layer1_tail(h, p1, w, P, *, tm):
    m = h.shape[0]
    tmr = next((t for t in (2048, 1024, 512, 256, LANES) if m % t == 0), LANES)
    hp = jnp.pad(h, ((0, -m % tmr), (0, 0)))
    ut, col, row, seg = _route(hp, P['ffn_norm'][1], w['router'], tm=tmr, n_valid=m)
    y_t = _experts(ut, col, row, seg, w['moe_wgu_t'], w['moe_wd_t'], tf=896)
    if tm % LANES:
        h, p1, tm = hp, jnp.pad(p1, ((0, 0), (0, -m % tmr), (0, 0))), tmr
    return _ple(h, P['ple_norm'][1], w['ple_gate_w'][1], (p1, 1), w['ple_w'][1], tm=tm,
                y_t=y_t)[:m]


def kernel(x_prompt, x_sample, p_prompt, p_sample, state_conv, state_delta, cache_kv_w128, cache_kv_w512, cache_kv_w2048, a_norm, a_w_in, a_conv_w, a_A_log, a_dt_bias, a_out_norm, a_w_out, kv_norm, w_kv, k_norm, b_norm, b_w_q, b_q_norm, b_w_out, ffn_norm, dense_w_gu, dense_w_down, moe_router, moe_w_gu, moe_w_down, ple_w, ple_norm, ple_gate_w):
    assert a_w_in.shape[0] == 1 and b_w_q.shape[0] == 1, "one mixer of each kind"
    bp, sp, dm = x_prompt.shape
    bs, ls, _ = x_sample.shape
    assert ls == 1, "sample group decodes one token per sequence"
    qkv_w = 3 * H_A * DK_A
    P = dict(a_norm=a_norm, ffn_norm=ffn_norm, ple_norm=ple_norm)
    w, ws = _prep_weights(a_w_in, a_A_log, a_dt_bias, a_w_out, w_kv, b_w_q, b_w_out, dense_w_gu,
                          dense_w_down, moe_router, moe_w_gu, moe_w_down, ple_w, ple_gate_w,
                          k_norm, b_q_norm)
    a_in_pad = w['a_w_in'].shape[1]
    in_tn = a_in_pad // 3 if a_in_pad % (3 * LANES) == 0 else LANES
    conv_w = a_conv_w[0]
    out_norm = a_out_norm[0].reshape(1, DK_A)

    mp = bp * sp
    tm = min(1024, sp)
    xp = x_prompt.reshape(mp, dm)
    conv0 = jnp.zeros((bp, CONV_W - 1, qkv_w), F32)
    s0 = jnp.zeros((bp, H_A, DK_A, DK_A), F32)
    h, proj, delta_p = _layer0(
        xp, p_prompt.reshape(p_prompt.shape[0], mp, -1), w, P, tm=tm, in_tn=in_tn,
        mixer=lambda pr: _gdn_prompt(pr, conv_w, w['a_hp'], out_norm, conv0, s0, batch=bp, seq=sp))
    conv_p = proj.reshape(bp, sp, -1)[:, sp - (CONV_W - 1):, :qkv_w][None]

    cos, sin = _rope_tables(jnp.arange(sp, dtype=jnp.int32))
    dils = [d for _, d in GROUPS]
    tmp = min(512, sp)
    kv, *kv_ph = _proj_rope(h, kv_norm, w['w_kv'], w['k_gain'], cos, sin, tm=tmp, n_rope=N_GROUPS,
                            natural=True, dils=dils + dils)
    q_ph = _proj_rope(h, b_norm[0], w['b_w_q'], w['q_gain'], cos, sin, tm=tmp, n_rope=N_GROUPS,
                      natural=False, dils=dils)
    outs, lses = [], []
    for gi in range(N_GROUPS):
        o, lse = _band_attn(q_ph[gi], kv_ph[gi], kv_ph[N_GROUPS + gi], gi, batch=bp, seq=sp)
        outs.append(o)
        lses.append(lse)
    h = _merge_out(outs, lses, w['b_w_out'], h, tm=tm)
    y_prompt = _layer1_tail(h, p_prompt.reshape(p_prompt.shape[0], mp, -1), w, P, tm=tm).reshape(bp, sp, dm)
    kv3 = kv.reshape(bp, sp, 2 * N_GROUPS * GW)
    kv_p = []
    for gi, (win, _) in enumerate(GROUPS):
        rows = kv3[:, sp - min(win, sp):]
        k_g = rows[:, :, gi * GW:(gi + 1) * GW]
        v_g = rows[:, :, (N_GROUPS + gi) * GW:(N_GROUPS + gi + 1) * GW]
        kv_p.append(jnp.stack([k_g, v_g], axis=2).reshape(bp, -1, 2, HG, HD_B))

    xs = x_sample.reshape(bs, dm)
    hs, proj_s, delta_s = _layer0(
        xs, p_sample.reshape(p_sample.shape[0], bs, -1), ws, P, tm=bs, in_tn=in_tn,
        mixer=lambda pr: _gdn_step(pr, state_conv[0], conv_w, ws['a_hp'], out_norm, state_delta[0]))
    conv_s = jnp.concatenate([state_conv[0][:, 1:], proj_s[:, None, :qkv_w]], axis=1)[None]
    bs_pad = -(-bs // LANES) * LANES
    hs_pad = jnp.pad(hs, ((0, bs_pad - bs), (0, 0)))
    cos_s, sin_s = _rope_tables(jnp.full((bs_pad,), PAST_LEN, jnp.int32))
    kv_s, kv_cols = _proj_rope(hs_pad, kv_norm, ws['w_kv'], ws['k_gain'], cos_s, sin_s, tm=bs_pad,
                               n_rope=N_GROUPS, natural=True, columns=True, dils=())
    q_cols, = _proj_rope(hs_pad, b_norm[0], ws['b_w_q'], ws['q_gain'], cos_s, sin_s, tm=bs_pad,
                         n_rope=N_GROUPS, natural=False, columns=True, dils=())
    o_s = _gather_attn(q_cols, kv_cols, (cache_kv_w128, cache_kv_w512, cache_kv_w2048))
    hs = _mm_res(o_s.reshape(bs, GW), ws['b_w_out'], hs, tm=bs)
    y_sample = _layer1_tail(hs, p_sample.reshape(p_sample.shape[0], bs, -1), ws, P, tm=bs).reshape(bs, 1, dm)
    kvs5 = kv_s[:bs].reshape(bs, 1, 2, N_GROUPS, HG, HD_B)
    kv_sn = [kvs5[:, :, :, gi] for gi in range(N_GROUPS)]

    return (y_prompt, y_sample, conv_p, conv_s, delta_p[None], delta_s[None],
            kv_p[0], kv_sn[0], kv_p[1], kv_sn[1], kv_p[2], kv_sn[2])
```
